```python
import jax, jax.numpy as jnp
from jax import lax
import numpy as np

D_MODEL = 1024
BATCH = 2
SEQ = 8192
DEPTH = 1

HG_HEADS = 4
HG_DIM = 128
HG_WIDTH = HG_HEADS * HG_DIM
HG_CHUNK = 64
ATT_HEADS = 8
ATT_DIM = 64
ATT_WIDTH = ATT_HEADS * ATT_DIM
DILATED_PATTERNS = ((128, 1), (512, 4), (2048, 16))
ATT_BLOCK = 128
ROPE_THETA = 10000.0
MIX_WIDTH = HG_WIDTH + ATT_WIDTH
N_IN_SPLITS = 8
IN_COLS = 5 * HG_WIDTH + 3 * ATT_WIDTH
N_EXPERTS = 32
TOP_K = 4
D_EXPERT = D_MODEL
SWIGLU_LIMIT = 7.0
SWIGLU_ALPHA = 1.702
MOE_BLOCK = 128
EPS = 1e-6
NEG = -1e30

kernel_name = "hybrid_hgrn2_dilated_attn_moe_encoder"


def rms_norm(x, g):
    xf = x.astype(jnp.float32)
    y = xf * lax.rsqrt(jnp.mean(xf * xf, axis=-1, keepdims=True) + EPS)
    return (y * g.astype(jnp.float32)).astype(x.dtype)


def rope(x, positions):
    half = x.shape[-1] // 2
    inv = 1.0 / (ROPE_THETA ** (jnp.arange(half, dtype=jnp.float32) / half))
    ang = positions.astype(jnp.float32)[..., None] * inv
    cos = jnp.cos(ang)[:, :, None, :]
    sin = jnp.sin(ang)[:, :, None, :]
    x1, x2 = x[..., :half], x[..., half:]
    return jnp.concatenate([x1 * cos - x2 * sin, x2 * cos + x1 * sin], axis=-1)


def banded_attention(q, k, v, half):
    B, H, L, Dh = q.shape
    nb = -(-L // ATT_BLOCK)
    Lp = nb * ATT_BLOCK
    W = ATT_BLOCK + 2 * half
    qb = jnp.pad(q, ((0, 0), (0, 0), (0, Lp - L), (0, 0))).reshape(B, H, nb, ATT_BLOCK, Dh)
    kpad = ((0, 0), (0, 0), (half, Lp - L + half), (0, 0))
    key_idx = jnp.arange(nb)[:, None] * ATT_BLOCK + jnp.arange(W)[None, :]
    kb = jnp.pad(k, kpad)[:, :, key_idx]
    vb = jnp.pad(v, kpad)[:, :, key_idx]
    s = jnp.einsum('bhnqd,bhnkd->bhnqk', qb, kb) * (Dh ** -0.5)
    qpos = jnp.arange(nb)[:, None] * ATT_BLOCK + jnp.arange(ATT_BLOCK)[None, :]
    kpos = key_idx - half
    rel = kpos[:, None, :] - qpos[:, :, None]
    valid = (jnp.abs(rel) <= half) & ((kpos >= 0) & (kpos < L))[:, None, :]
    s = jnp.where(valid, s, NEG)
    m = jnp.max(s, axis=-1, keepdims=True)
    p = jnp.exp(s - m)
    den = jnp.sum(p, axis=-1)
    o = jnp.einsum('bhnqk,bhnkd->bhnqd', p, vb) / den[..., None]
    lse = m[..., 0] + jnp.log(den)
    return o.reshape(B, H, Lp, Dh)[:, :, :L], lse.reshape(B, H, Lp)[:, :, :L]


def dilated_attention(q, k, v):
    B, S, H, Dh = q.shape
    outs, lses = [], []
    for window, dil in DILATED_PATTERNS:
        half = window // (2 * dil)
        L = S // dil
        def to_strided(t):
            return t.reshape(B, L, dil, H, Dh).transpose(0, 2, 3, 1, 4).reshape(B, dil * H, L, Dh)
        o, lse = banded_attention(to_strided(q), to_strided(k), to_strided(v), half)
        outs.append(o.reshape(B, dil, H, L, Dh).transpose(0, 3, 1, 2, 4).reshape(B, S, H, Dh))
        lses.append(lse.reshape(B, dil, H, L).transpose(0, 3, 1, 2).reshape(B, S, H))
    w = jax.nn.softmax(jnp.stack(lses, axis=0), axis=0)
    return jnp.sum(w[..., None] * jnp.stack(outs, axis=0), axis=0)


def gla_chunk_scan(q, k, v, log_f):
    B, H, S, dk = q.shape
    dv = v.shape[-1]
    N = S // HG_CHUNK
    C = HG_CHUNK
    q = q.reshape(B, H, N, C, dk)
    k = k.reshape(B, H, N, C, dk)
    v = v.reshape(B, H, N, C, dv)
    b = jnp.cumsum(log_f.reshape(B, H, N, C, dk), axis=3)
    b_last = b[:, :, :, -1:, :]
    qt = q * jnp.exp(b)
    kt = k * jnp.exp(-b)
    A = jnp.einsum('bhncd,bhnsd->bhncs', qt, kt)
    A = jnp.where(jnp.tril(jnp.ones((C, C), dtype=bool)), A, 0.0)
    o_intra = jnp.einsum('bhncs,bhnse->bhnce', A, v)
    U = jnp.einsum('bhncd,bhnce->bhnde', k * jnp.exp(b_last - b), v)
    a = jnp.exp(b_last[:, :, :, 0, :])

    def step(state, inp):
        a_n, U_n = inp
        return a_n[..., None] * state + U_n, state

    _, s_prev = lax.scan(step, jnp.zeros((B, H, dk, dv), q.dtype),
                         (jnp.moveaxis(a, 2, 0), jnp.moveaxis(U, 2, 0)))
    o_inter = jnp.einsum('bhncd,bhnde->bhnce', qt, jnp.moveaxis(s_prev, 0, 2))
    return (o_intra + o_inter).reshape(B, H, S, dv)


def hgrn2_mixer(hq, hf_fwd, hf_bwd, hi, hg, lb, onorm_g):
    B, S, _ = hq.shape

    def heads(t):
        return t.astype(jnp.float32).reshape(B, S, HG_HEADS, HG_DIM).transpose(0, 2, 1, 3)

    def gate(z, lbd):
        z = z.astype(jnp.float32)
        f = lbd + (1.0 - lbd) * jax.nn.sigmoid(z)
        k = (1.0 - lbd) * jax.nn.sigmoid(-z)
        return heads(k), heads(jnp.log(f))

    q = heads(jax.nn.silu(hq.astype(jnp.float32)))
    v = heads(hi)
    k_f, g_f = gate(hf_fwd, lb[0])
    k_b, g_b = gate(hf_bwd, lb[1])
    fwd = gla_chunk_scan(q, k_f, v, g_f)
    flip = lambda t: jnp.flip(t, axis=2)
    bwd = flip(gla_chunk_scan(flip(q), flip(k_b), flip(v), flip(g_b)))
    o = (fwd + bwd).transpose(0, 2, 1, 3)
    o = rms_norm(o, onorm_g) * jax.nn.silu(hg.astype(jnp.float32).reshape(B, S, HG_HEADS, HG_DIM))
    return o.reshape(B, S, HG_WIDTH)


def clamped_swiglu(h):
    glu = jnp.minimum(h[..., ::2], SWIGLU_LIMIT)
    lin = jnp.clip(h[..., 1::2], -SWIGLU_LIMIT, SWIGLU_LIMIT)
    return glu * jax.nn.sigmoid(SWIGLU_ALPHA * glu) * (lin + 1.0)


def moe_ffn(h, w_router, b_router, w_up, b_up, w_down, b_down):
    B, S, D = h.shape
    N = B * S
    xt = h.reshape(N, D)
    logits = (xt @ w_router + b_router).astype(jnp.float32)
    top_val, top_idx = lax.top_k(logits, TOP_K)
    gates = jax.nn.softmax(top_val, axis=-1)
    A = N * TOP_K
    e_flat = top_idx.reshape(A)
    t_flat = jnp.arange(A, dtype=jnp.int32) // TOP_K
    g_flat = gates.reshape(A)
    order = jnp.argsort(e_flat, stable=True)
    e_s, t_s, g_s = e_flat[order], t_flat[order], g_flat[order]
    counts = jnp.bincount(e_flat, length=N_EXPERTS)
    padded = ((counts + MOE_BLOCK - 1) // MOE_BLOCK) * MOE_BLOCK
    pad_end = jnp.cumsum(padded)
    pad_start = pad_end - padded
    raw_start = jnp.cumsum(counts) - counts
    dest = pad_start[e_s] + (jnp.arange(A) - raw_start[e_s])
    P = (-(-A // MOE_BLOCK)) * MOE_BLOCK + N_EXPERTS * MOE_BLOCK
    nb = P // MOE_BLOCK
    row_tok = jnp.zeros((P,), jnp.int32).at[dest].set(t_s)
    row_gate = jnp.zeros((P,), jnp.float32).at[dest].set(g_s)
    block_exp = jnp.minimum(jnp.searchsorted(pad_end, jnp.arange(nb) * MOE_BLOCK, side='right'),
                            N_EXPERTS - 1)

    def expert_block(args):
        tok, gate, e = args
        xb = xt[tok]
        hu = xb @ w_up[e] + b_up[e]
        y = clamped_swiglu(hu) @ w_down[e] + b_down[e]
        return y.astype(jnp.float32) * gate[:, None]

    y = lax.map(expert_block, (row_tok.reshape(nb, MOE_BLOCK), row_gate.reshape(nb, MOE_BLOCK), block_exp))
    out = jax.ops.segment_sum(y.reshape(P, D), row_tok, num_segments=N)
    return out.reshape(B, S, D).astype(h.dtype)


def setup_inputs(seed: int = 0) -> dict:
    key = jax.random.key(seed)
    ks = jax.random.split(key, 16)
    f32 = jnp.float32
    nrm = lambda k, shape, s: jax.random.normal(k, shape, f32) * s
    return {
        "x": nrm(ks[0], (BATCH, SEQ, D_MODEL), 1.0),
        "positions": jnp.broadcast_to(jnp.arange(SEQ, dtype=jnp.int32), (BATCH, SEQ)),
        "norm1_g": 1.0 + nrm(ks[1], (DEPTH, D_MODEL), 0.02),
        "w_in": nrm(ks[2], (DEPTH, D_MODEL, IN_COLS), D_MODEL ** -0.5),
        "q_norm_g": 1.0 + nrm(ks[3], (DEPTH, ATT_DIM), 0.02),
        "k_norm_g": 1.0 + nrm(ks[4], (DEPTH, ATT_DIM), 0.02),
        "hgrn_lower_bounds": nrm(ks[5], (DEPTH + 1, 2, HG_WIDTH), 0.1),
        "hgrn_onorm_g": 1.0 + nrm(ks[6], (DEPTH, HG_DIM), 0.02),
        "w_out": nrm(ks[7], (DEPTH, MIX_WIDTH, D_MODEL), MIX_WIDTH ** -0.5),
        "norm2_g": 1.0 + nrm(ks[8], (DEPTH, D_MODEL), 0.02),
        "w_router": nrm(ks[9], (DEPTH, D_MODEL, N_EXPERTS), D_MODEL ** -0.5),
        "b_router": nrm(ks[10], (DEPTH, N_EXPERTS), 0.01),
        "w_up": nrm(ks[11], (DEPTH, N_EXPERTS, D_MODEL, 2 * D_EXPERT), D_MODEL ** -0.5),
        "b_up": nrm(ks[12], (DEPTH, N_EXPERTS, 2 * D_EXPERT), 0.01),
        "w_down": nrm(ks[13], (DEPTH, N_EXPERTS, D_EXPERT, D_MODEL), D_EXPERT ** -0.5),
        "b_down": nrm(ks[14], (DEPTH, N_EXPERTS, D_MODEL), 0.01),
    }


def reference(x, positions, norm1_g, w_in, q_norm_g, k_norm_g, hgrn_lower_bounds, hgrn_onorm_g,
              w_out, norm2_g, w_router, b_router, w_up, b_up, w_down, b_down):
    B, S, _ = x.shape
    lbs = jnp.cumsum(jax.nn.softmax(hgrn_lower_bounds.astype(jnp.float32), axis=0), axis=0)
    for l in range(DEPTH):
        h = rms_norm(x, norm1_g[l])
        proj = h @ w_in[l]
        hq, hf_fwd, hf_bwd, hi, hg, aq, ak, av = jnp.split(proj, N_IN_SPLITS, axis=-1)
        o_hg = hgrn2_mixer(hq, hf_fwd, hf_bwd, hi, hg, lbs[l], hgrn_onorm_g[l])
        to_heads = lambda t: t.astype(jnp.float32).reshape(B, S, ATT_HEADS, ATT_DIM)
        q = rope(rms_norm(to_heads(aq), q_norm_g[l]), positions)
        k = rope(rms_norm(to_heads(ak), k_norm_g[l]), positions)
        o_att = dilated_attention(q, k, to_heads(av)).reshape(B, S, ATT_WIDTH)
        mix = jnp.concatenate([o_hg.astype(x.dtype), o_att.astype(x.dtype)], axis=-1)
        x = x + mix @ w_out[l]
        x = x + moe_ffn(rms_norm(x, norm2_g[l]), w_router[l], b_router[l], w_up[l], b_up[l],
                        w_down[l], b_down[l])
    return x
```

```python
import functools

import jax
import jax.numpy as jnp
from jax import lax
from jax.experimental import pallas as pl
from jax.experimental.pallas import tpu as pltpu

F32 = jnp.float32
BF16 = jnp.bfloat16

D_MODEL = 1024
HG_HEADS = 4
HG_DIM = 128
HG_WIDTH = HG_HEADS * HG_DIM
HG_CHUNK = 64
ATT_HEADS = 8
ATT_DIM = 64
ATT_WIDTH = ATT_HEADS * ATT_DIM
DILATED_PATTERNS = ((128, 1), (512, 4), (2048, 16))
ATT_HALF = 64
ATT_QBLOCK = 128
ROPE_THETA = 10000.0
IN_COLS = 5 * HG_WIDTH + 3 * ATT_WIDTH
N_EXPERTS = 32
TOP_K = 4
D_EXPERT = D_MODEL
SWIGLU_LIMIT = 7.0
SWIGLU_ALPHA = 1.702
EPS = 1e-6
NEG = -1e30

COL_HQ, COL_HF_FWD, COL_HF_BWD, COL_HI, COL_HG, COL_AQ, COL_AK, COL_AV = range(8)

TOKEN_TILE = 512
MOE_BLOCK_ROWS = 512
LANES = 128
VMEM_LIMIT = 56 * 1024 * 1024


def _dot(a, b):
    return jnp.dot(a, b, preferred_element_type=F32)


def _dot_nt(a, b):
    return lax.dot_general(a, b, (((1,), (1,)), ((), ())), preferred_element_type=F32)


def _dot_tn(a, b):
    return lax.dot_general(a, b, (((0,), (0,)), ((), ())), preferred_element_type=F32)


def _sigmoid(x):
    return 1.0 / (1.0 + jnp.exp(-x))


def _head_norm_rope(p, gain, cos, sin_signed, scale):
    lane = lax.broadcasted_iota(jnp.int32, (p.shape[0], LANES), 1)
    low = lane < ATT_DIM
    first_half = (lane % ATT_DIM) < (ATT_DIM // 2)
    outs = []
    for t in range(ATT_WIDTH // LANES):
        blk = p[:, t * LANES:(t + 1) * LANES]
        sq = blk * blk
        s_low = jnp.sum(jnp.where(low, sq, 0.0), axis=-1, keepdims=True)
        s_high = jnp.sum(jnp.where(low, 0.0, sq), axis=-1, keepdims=True)
        r = jnp.where(low, lax.rsqrt(s_low * (1.0 / ATT_DIM) + EPS),
                      lax.rsqrt(s_high * (1.0 / ATT_DIM) + EPS))
        y = blk * r * gain
        partner = jnp.where(first_half, pltpu.roll(y, LANES - ATT_DIM // 2, axis=1),
                            pltpu.roll(y, ATT_DIM // 2, axis=1))
        outs.append((y * cos + partner * sin_signed) * scale)
    return jnp.concatenate(outs, axis=1)


def _inproj_kernel(x_ref, pos_ref, inv_ref, g1_ref, w_ref, qg_ref, kg_ref, out_ref):
    x = x_ref[...]
    ms = jnp.mean(x * x, axis=-1, keepdims=True)
    h = (x * lax.rsqrt(ms + EPS) * g1_ref[...]).astype(BF16)
    ang = pos_ref[...].astype(F32) * inv_ref[...]
    lane = lax.broadcasted_iota(jnp.int32, ang.shape, 1)
    cos = jnp.cos(ang)
    sin_signed = jnp.where((lane % ATT_DIM) < (ATT_DIM // 2), -jnp.sin(ang), jnp.sin(ang))
    for j in range(IN_COLS // 512):
        p = _dot(h, w_ref[:, j * 512:(j + 1) * 512])
        if j == COL_AQ:
            p = _head_norm_rope(p, qg_ref[...], cos, sin_signed, ATT_DIM ** -0.5)
        elif j == COL_AK:
            p = _head_norm_rope(p, kg_ref[...], cos, sin_signed, 1.0)
        out_ref[:, j * 512:(j + 1) * 512] = p.astype(BF16)


def _inproj(x2d, pos_col, inv_tab, g1, w_in_bf16, qg, kg):
    n = x2d.shape[0]
    t = TOKEN_TILE
    const = lambda i: (0, 0)
    return pl.pallas_call(
        _inproj_kernel,
        grid=(n // t,),
        in_specs=[
            pl.BlockSpec((t, D_MODEL), lambda i: (i, 0)),
            pl.BlockSpec((t, 1), lambda i: (i, 0)),
            pl.BlockSpec((1, LANES), const),
            pl.BlockSpec((1, D_MODEL), const),
            pl.BlockSpec((D_MODEL, IN_COLS), const),
            pl.BlockSpec((1, LANES), const),
            pl.BlockSpec((1, LANES), const),
        ],
        out_specs=pl.BlockSpec((t, IN_COLS), lambda i: (i, 0)),
        out_shape=jax.ShapeDtypeStruct((n, IN_COLS), BF16),
        compiler_params=pltpu.CompilerParams(
            dimension_semantics=("parallel",), vmem_limit_bytes=VMEM_LIMIT),
        name="inproj",
    )(x2d, pos_col, inv_tab, g1, w_in_bf16, qg, kg)


def _hgrn_chunk(q_raw, z, v, lb, tri, mask, state_t, last_row):
    z = z.astype(F32)
    q = q_raw.astype(F32)
    sg = _sigmoid(z)
    f = lb + (1.0 - lb) * sg
    k = (1.0 - lb) * (1.0 - sg)
    lf = jnp.log(f)
    lf_hi = lf.astype(BF16)
    lf_lo = (lf - lf_hi.astype(F32)).astype(BF16)
    b = _dot(tri, lf_hi) + _dot(tri, lf_lo)
    b_last = b[last_row:last_row + 1, :]
    qt = (q * _sigmoid(q) * jnp.exp(b)).astype(BF16)
    kt = (k * jnp.exp(-b)).astype(BF16)
    kd = (k * jnp.exp(b_last - b)).astype(BF16)
    a = jnp.where(mask, _dot_nt(qt, kt), 0.0)
    o = _dot(a.astype(BF16), v) + _dot_nt(qt, state_t.astype(BF16))
    new_state = state_t * jnp.exp(b_last) + _dot_tn(v, kd)
    return o, new_state


def _hgrn_kernel(qf_ref, zf_ref, vf_ref, qb_ref, zb_ref, vb_ref, lbf_ref, lbb_ref,
                 of_ref, ob_ref, sf_ref, sb_ref):
    @pl.when(pl.program_id(2) == 0)
    def _():
        sf_ref[...] = jnp.zeros_like(sf_ref)
        sb_ref[...] = jnp.zeros_like(sb_ref)

    c = HG_CHUNK
    row = lax.broadcasted_iota(jnp.int32, (c, c), 0)
    col = lax.broadcasted_iota(jnp.int32, (c, c), 1)
    lower = row >= col
    upper = row <= col
    tri_f = jnp.where(lower, 1.0, 0.0).astype(BF16)
    tri_b = jnp.where(upper, 1.0, 0.0).astype(BF16)
    lbf = lbf_ref[0]
    lbb = lbb_ref[0]
    n_chunks = qf_ref.shape[0] // c

    sf = sf_ref[...]
    sb = sb_ref[...]
    for j in range(n_chunks):
        rf = slice(j * c, (j + 1) * c)
        o, sf = _hgrn_chunk(qf_ref[rf, :], zf_ref[rf, :], vf_ref[rf, :], lbf, tri_f, lower, sf, c - 1)
        of_ref[rf, :] = o.astype(of_ref.dtype)
        jb = n_chunks - 1 - j
        rb = slice(jb * c, (jb + 1) * c)
        o, sb = _hgrn_chunk(qb_ref[rb, :], zb_ref[rb, :], vb_ref[rb, :], lbb, tri_b, upper, sb, 0)
        ob_ref[rb, :] = o.astype(ob_ref.dtype)
    sf_ref[...] = sf
    sb_ref[...] = sb


def _hgrn(proj, lbs, batch, seq):
    n = proj.shape[0]
    t = TOKEN_TILE
    nblk = seq // t

    def fwd(colblk):
        return pl.BlockSpec((t, HG_DIM), lambda b, h, i: (b * nblk + i, colblk * HG_HEADS + h))

    def bwd(colblk):
        return pl.BlockSpec((t, HG_DIM), lambda b, h, i: (b * nblk + nblk - 1 - i, colblk * HG_HEADS + h))

    out_f = pl.BlockSpec((t, HG_DIM), lambda b, h, i: (b * nblk + i, h))
    out_b = pl.BlockSpec((t, HG_DIM), lambda b, h, i: (b * nblk + nblk - 1 - i, h))
    return pl.pallas_call(
        _hgrn_kernel,
        grid=(batch, HG_HEADS, nblk),
        in_specs=[
            fwd(COL_HQ), fwd(COL_HF_FWD), fwd(COL_HI),
            bwd(COL_HQ), bwd(COL_HF_BWD), bwd(COL_HI),
            pl.BlockSpec((1, 1, HG_DIM), lambda b, h, i: (h, 0, 0)),
            pl.BlockSpec((1, 1, HG_DIM), lambda b, h, i: (HG_HEADS + h, 0, 0)),
        ],
        out_specs=[out_f, out_b],
        out_shape=[jax.ShapeDtypeStruct((n, HG_WIDTH), BF16)] * 2,
        scratch_shapes=[pltpu.VMEM((HG_DIM, HG_DIM), F32)] * 2,
        compiler_params=pltpu.CompilerParams(
            dimension_semantics=("parallel", "parallel", "arbitrary"), vmem_limit_bytes=VMEM_LIMIT),
        name="hgrn",
    )(proj, proj, proj, proj, proj, proj, lbs, lbs)


def _attn_kernel(q_ref, kc_ref, kp_ref, kn_ref, vc_ref, vp_ref, vn_ref, o_ref, l_ref,
                 kw_ref, vw_ref, *, tq, length):
    n = pl.program_id(2)
    half = ATT_HALF
    kw_ref[0:half, :] = kp_ref[0]
    kw_ref[half:half + tq, :] = kc_ref[0]
    kw_ref[half + tq:, :] = kn_ref[0]
    vw_ref[0:half, :] = vp_ref[0]
    vw_ref[half:half + tq, :] = vc_ref[0]
    vw_ref[half + tq:, :] = vn_ref[0]

    qb_rows = ATT_QBLOCK
    win = qb_rows + 2 * half
    i_idx = lax.broadcasted_iota(jnp.int32, (qb_rows, win), 0)
    j_idx = lax.broadcasted_iota(jnp.int32, (qb_rows, win), 1)
    band = (j_idx >= i_idx) & (j_idx <= i_idx + 2 * half)

    def body(qb, carry):
        r0 = pl.multiple_of(qb * qb_rows, qb_rows)
        base = n * tq + r0 - half
        valid = band & (j_idx >= -base) & (j_idx < length - base)
        q = q_ref[0, pl.ds(r0, qb_rows), :]
        kw = kw_ref[pl.ds(r0, win), :]
        vw = vw_ref[pl.ds(r0, win), :]
        outs, lses = [], []
        for h in range(ATT_HEADS):
            cs = slice(h * ATT_DIM, (h + 1) * ATT_DIM)
            s = jnp.where(valid, _dot_nt(q[:, cs], kw[:, cs]), NEG)
            m = jnp.max(s, axis=-1, keepdims=True)
            p = jnp.exp(s - m)
            den = jnp.sum(p, axis=-1, keepdims=True)
            outs.append(_dot(p.astype(BF16), vw[:, cs]) / den)
            lses.append(jnp.broadcast_to(m + jnp.log(den), (qb_rows, ATT_DIM)))
        o_ref[0, pl.ds(r0, qb_rows), :] = jnp.concatenate(outs, axis=1).astype(o_ref.dtype)
        l_ref[0, pl.ds(r0, qb_rows), :] = jnp.concatenate(lses, axis=1)
        return carry

    lax.fori_loop(0, tq // qb_rows, body, 0)


def _attention(proj3, dil):
    batch, seq, _ = proj3.shape
    length = seq // dil
    tq = min(TOKEN_TILE, length)
    nq = length // tq
    hb = tq // ATT_HALF
    n_hblk = length // ATT_HALF
    ncol = IN_COLS // ATT_WIDTH
    pv = proj3.reshape(batch, length, dil * IN_COLS)

    def cur(col):
        return pl.BlockSpec((1, tq, ATT_WIDTH), lambda b, r, n: (b, n, r * ncol + col))

    def prev(col):
        return pl.BlockSpec((1, ATT_HALF, ATT_WIDTH),
                            lambda b, r, n: (b, jnp.maximum(n * hb - 1, 0), r * ncol + col))

    def nxt(col):
        return pl.BlockSpec((1, ATT_HALF, ATT_WIDTH),
                            lambda b, r, n: (b, jnp.minimum((n + 1) * hb, n_hblk - 1), r * ncol + col))

    out_spec = pl.BlockSpec((1, tq, ATT_WIDTH), lambda b, r, n: (b, n, r))
    o, lse = pl.pallas_call(
        functools.partial(_attn_kernel, tq=tq, length=length),
        grid=(batch, dil, nq),
        in_specs=[cur(COL_AQ), cur(COL_AK), prev(COL_AK), nxt(COL_AK),
                  cur(COL_AV), prev(COL_AV), nxt(COL_AV)],
        out_specs=[out_spec, out_spec],
        out_shape=[jax.ShapeDtypeStruct((batch, length, dil * ATT_WIDTH), BF16),
                   jax.ShapeDtypeStruct((batch, length, dil * ATT_WIDTH), F32)],
        scratch_shapes=[pltpu.VMEM((tq + 2 * ATT_HALF, ATT_WIDTH), BF16)] * 2,
        compiler_params=pltpu.CompilerParams(
            dimension_semantics=("parallel", "parallel", "parallel"), vmem_limit_bytes=VMEM_LIMIT),
        name=f"attn_d{dil}",
    )(pv, pv, pv, pv, pv, pv, pv)
    return o.reshape(batch * seq, ATT_WIDTH), lse.reshape(batch * seq, ATT_WIDTH)


def _outproj_kernel(of_ref, ob_ref, hg_ref, o1_ref, o2_ref, o3_ref, l1_ref, l2_ref, l3_ref,
                    x_ref, og_ref, w_ref, g2_ref, wrh_ref, wrl_ref, br_ref,
                    x2_ref, xn_ref, gate_ref, idx_ref):
    o = of_ref[...].astype(F32) + ob_ref[...].astype(F32)
    hg = hg_ref[...].astype(F32)
    parts = []
    for h in range(HG_HEADS):
        blk = o[:, h * HG_DIM:(h + 1) * HG_DIM]
        ms = jnp.mean(blk * blk, axis=-1, keepdims=True)
        parts.append(blk * lax.rsqrt(ms + EPS) * og_ref[...])
    o_hg = jnp.concatenate(parts, axis=1) * (hg * _sigmoid(hg))

    l1, l2, l3 = l1_ref[...], l2_ref[...], l3_ref[...]
    mx = jnp.maximum(jnp.maximum(l1, l2), l3)
    e1, e2, e3 = jnp.exp(l1 - mx), jnp.exp(l2 - mx), jnp.exp(l3 - mx)
    o_att = (e1 * o1_ref[...].astype(F32) + e2 * o2_ref[...].astype(F32)
             + e3 * o3_ref[...].astype(F32)) / (e1 + e2 + e3)

    y = _dot(o_hg.astype(BF16), w_ref[0:HG_WIDTH, :]) + _dot(o_att.astype(BF16), w_ref[HG_WIDTH:, :])
    x2 = x_ref[...] + y
    x2_ref[...] = x2

    ms = jnp.mean(x2 * x2, axis=-1, keepdims=True)
    xn = x2 * lax.rsqrt(ms + EPS) * g2_ref[...]
    xn_hi = xn.astype(BF16)
    xn_ref[...] = xn_hi
    xn_lo = (xn - xn_hi.astype(F32)).astype(BF16)
    logits = (_dot(xn_hi, wrh_ref[...]) + _dot(xn_lo, wrh_ref[...]) + _dot(xn_hi, wrl_ref[...])
              + br_ref[...])

    lane = lax.broadcasted_iota(jnp.int32, logits.shape, 1)
    lane_f = lane.astype(F32)
    work = jnp.where(lane < N_EXPERTS, logits, -jnp.inf)
    vals, idxs = [], []
    for _ in range(TOP_K):
        m = jnp.max(work, axis=-1, keepdims=True)
        idx = jnp.min(jnp.where(work == m, lane_f, float(LANES)), axis=-1, keepdims=True)
        vals.append(m)
        idxs.append(idx)
        work = jnp.where(lane_f == idx, -jnp.inf, work)
    es = [jnp.exp(v - vals[0]) for v in vals]
    den = es[0] + es[1] + es[2] + es[3]
    gate_out = jnp.zeros(logits.shape, F32)
    idx_out = jnp.zeros(logits.shape, F32)
    for k in range(TOP_K):
        gate_out = jnp.where(lane == k, es[k] / den, gate_out)
        idx_out = jnp.where(lane == k, idxs[k], idx_out)
    gate_ref[...] = gate_out
    idx_ref[...] = idx_out.astype(jnp.int32)


def _outproj(o_f, o_b, proj, atts, lses, x2d, og, w_out_bf16, g2, wr_hi, wr_lo, br):
    n = x2d.shape[0]
    t = TOKEN_TILE
    row = lambda i: (i, 0)
    const = lambda i: (0, 0)
    half = pl.BlockSpec((t, 512), row)
    return pl.pallas_call(
        _outproj_kernel,
        grid=(n // t,),
        in_specs=[
            half, half, pl.BlockSpec((t, 512), lambda i: (i, COL_HG)),
            half, half, half, half, half, half,
            pl.BlockSpec((t, D_MODEL), row),
            pl.BlockSpec((1, HG_DIM), const),
            pl.BlockSpec((D_MODEL, D_MODEL), const),
            pl.BlockSpec((1, D_MODEL), const),
            pl.BlockSpec((D_MODEL, LANES), const),
            pl.BlockSpec((D_MODEL, LANES), const),
            pl.BlockSpec((1, LANES), const),
        ],
        out_specs=[pl.BlockSpec((t, D_MODEL), row), pl.BlockSpec((t, D_MODEL), row),
                   pl.BlockSpec((t, LANES), row), pl.BlockSpec((t, LANES), row)],
        out_shape=[jax.ShapeDtypeStruct((n, D_MODEL), F32), jax.ShapeDtypeStruct((n, D_MODEL), BF16),
                   jax.ShapeDtypeStruct((n, LANES), F32), jax.ShapeDtypeStruct((n, LANES), jnp.int32)],
        compiler_params=pltpu.CompilerParams(
            dimension_semantics=("parallel",), vmem_limit_bytes=VMEM_LIMIT),
        name="outproj",
    )(o_f, o_b, proj, *atts, *lses, x2d, og, w_out_bf16, g2, wr_hi, wr_lo, br)


def _moe_kernel(bexp_ref, bfirst_ref, nused_ref, xs_ref, wu_ref, bg_ref, bl_ref, wd_ref, bd_ref,
                y_ref, wg_s, wl_s, wd_s):
    i = pl.program_id(0)

    @pl.when(bfirst_ref[i] == 1)
    def _():
        r = lax.broadcasted_iota(jnp.int32, (2 * LANES, 2 * LANES), 0)
        c = lax.broadcasted_iota(jnp.int32, (2 * LANES, 2 * LANES), 1)
        src = jnp.where(c < LANES, 2 * c, 2 * (c - LANES) + 1)
        perm = jnp.where(r == src, 1.0, 0.0).astype(BF16)
        rows = 256
        for rb in range(D_MODEL // rows):
            rs = slice(rb * rows, (rb + 1) * rows)
            for cb in range(D_EXPERT // LANES):
                w = wu_ref[0, rs, cb * 2 * LANES:(cb + 1) * 2 * LANES].astype(BF16)
                split = _dot(w, perm).astype(BF16)
                wg_s[rs, cb * LANES:(cb + 1) * LANES] = split[:, :LANES]
                wl_s[rs, cb * LANES:(cb + 1) * LANES] = split[:, LANES:]
        wd_s[...] = wd_ref[0].astype(BF16)

    @pl.when(i < nused_ref[0])
    def _():
        x = xs_ref[...]
        hglu = _dot(x, wg_s[...]) + bg_ref[0]
        hlin = _dot(x, wl_s[...]) + bl_ref[0]
        glu = jnp.minimum(hglu, SWIGLU_LIMIT)
        lin = jnp.clip(hlin, -SWIGLU_LIMIT, SWIGLU_LIMIT)
        act = glu * _sigmoid(SWIGLU_ALPHA * glu) * (lin + 1.0)
        y_ref[...] = (_dot(act.astype(BF16), wd_s[...]) + bd_ref[0]).astype(y_ref.dtype)

    @pl.when(i >= nused_ref[0])
    def _():
        y_ref[...] = jnp.zeros_like(y_ref)


def _moe(block_exp, block_first, n_used, xs, w_up, b_glu, b_lin, w_down, b_down):
    p_rows = xs.shape[0]
    bm = MOE_BLOCK_ROWS
    nb = p_rows // bm
    exp3 = lambda i, be, bf, nu: (be[i], 0, 0)
    grid_spec = pltpu.PrefetchScalarGridSpec(
        num_scalar_prefetch=3,
        grid=(nb,),
        in_specs=[
            pl.BlockSpec((bm, D_MODEL), lambda i, be, bf, nu: (jnp.minimum(i, nu[0] - 1), 0)),
            pl.BlockSpec((1, D_MODEL, 2 * D_EXPERT), exp3),
            pl.BlockSpec((1, 1, D_EXPERT), exp3),
            pl.BlockSpec((1, 1, D_EXPERT), exp3),
            pl.BlockSpec((1, D_EXPERT, D_MODEL), exp3),
            pl.BlockSpec((1, 1, D_MODEL), exp3),
        ],
        out_specs=pl.BlockSpec((bm, D_MODEL), lambda i, be, bf, nu: (jnp.where(i < nu[0], i, nb - 1), 0)),
        scratch_shapes=[pltpu.VMEM((D_MODEL, D_EXPERT), BF16), pltpu.VMEM((D_MODEL, D_EXPERT), BF16),
                        pltpu.VMEM((D_EXPERT, D_MODEL), BF16)],
    )
    return pl.pallas_call(
        _moe_kernel,
        grid_spec=grid_spec,
        out_shape=jax.ShapeDtypeStruct((p_rows, D_MODEL), BF16),
        compiler_params=pltpu.CompilerParams(
            dimension_semantics=("arbitrary",), vmem_limit_bytes=VMEM_LIMIT),
        name="moe",
    )(block_exp, block_first, n_used, xs, w_up, b_glu, b_lin, w_down, b_down)


def _route(top_idx):
    n = top_idx.shape[0]
    a = n * TOP_K
    bm = MOE_BLOCK_ROWS
    nb = a // bm + N_EXPERTS
    e_flat = top_idx.reshape(a)
    onehot = (e_flat[:, None] == jnp.arange(N_EXPERTS, dtype=jnp.int32)[None, :]).astype(jnp.int32)
    csum = jnp.cumsum(onehot, axis=0)
    rank = jnp.take_along_axis(csum, e_flat[:, None], axis=1)[:, 0] - 1
    counts = csum[-1]
    padded = ((counts + bm - 1) // bm) * bm
    pad_end = jnp.cumsum(padded)
    pad_start = pad_end - padded
    dest = pad_start[e_flat] + rank
    n_used = (pad_end[-1] // bm).astype(jnp.int32)
    blk = jnp.arange(nb, dtype=jnp.int32)
    bexp = jnp.sum((pad_end[None, :] <= (blk * bm)[:, None]).astype(jnp.int32), axis=1)
    bexp = jnp.minimum(bexp, N_EXPERTS - 1)
    bexp = jnp.where(blk < n_used, bexp, bexp[jnp.maximum(n_used - 1, 0)])
    bfirst = jnp.concatenate([jnp.ones((1,), jnp.int32), (bexp[1:] != bexp[:-1]).astype(jnp.int32)])
    row_tok = jnp.zeros((nb * bm,), jnp.int32).at[dest].set(jnp.arange(a, dtype=jnp.int32) // TOP_K)
    return dest, row_tok, bexp, bfirst, n_used.reshape(1)


def kernel(x, positions, norm1_g, w_in, q_norm_g, k_norm_g, hgrn_lower_bounds, hgrn_onorm_g,
           w_out, norm2_g, w_router, b_router, w_up, b_up, w_down, b_down):
    batch, seq, d = x.shape
    n = batch * seq
    depth = norm1_g.shape[0]
    lbs_all = jnp.cumsum(jax.nn.softmax(hgrn_lower_bounds.astype(F32), axis=0), axis=0)
    half = ATT_DIM // 2
    inv = 1.0 / (ROPE_THETA ** (jnp.arange(half, dtype=F32) / half))
    inv_tab = jnp.tile(inv, LANES // half).reshape(1, LANES)
    pos_col = positions.reshape(n, 1)

    x2d = x.reshape(n, d)
    for l in range(depth):
        lbs = lbs_all[l].reshape(2 * HG_HEADS, 1, HG_DIM)
        proj = _inproj(x2d, pos_col, inv_tab, norm1_g[l].reshape(1, d), w_in[l].astype(BF16),
                       jnp.tile(q_norm_g[l], LANES // ATT_DIM).reshape(1, LANES),
                       jnp.tile(k_norm_g[l], LANES // ATT_DIM).reshape(1, LANES))
        o_f, o_b = _hgrn(proj, lbs, batch, seq)
        proj3 = proj.reshape(batch, seq, IN_COLS)
        atts, lses = [], []
        for _, dil in DILATED_PATTERNS:
            o, lse = _attention(proj3, dil)
            atts.append(o)
            lses.append(lse)

        wr = jnp.pad(w_router[l], ((0, 0), (0, LANES - N_EXPERTS)))
        wr_hi = wr.astype(BF16)
        wr_lo = (wr - wr_hi.astype(F32)).astype(BF16)
        br = jnp.pad(b_router[l], (0, LANES - N_EXPERTS)).reshape(1, LANES)
        x2, xn, gates, top_idx = _outproj(
            o_f, o_b, proj, atts, lses, x2d, hgrn_onorm_g[l].reshape(1, HG_DIM),
            w_out[l].astype(BF16), norm2_g[l].reshape(1, d), wr_hi, wr_lo, br)
        gates = gates[:, :TOP_K]
        top_idx = top_idx[:, :TOP_K]

        dest, row_tok, bexp, bfirst, n_used = _route(top_idx)
        xs = xn[row_tok]
        y = _moe(bexp, bfirst, n_used, xs, w_up[l],
                 b_up[l][:, 0::2].reshape(N_EXPERTS, 1, D_EXPERT),
                 b_up[l][:, 1::2].reshape(N_EXPERTS, 1, D_EXPERT),
                 w_down[l], b_down[l].reshape(N_EXPERTS, 1, D_MODEL))
        yk = y[dest].reshape(n, TOP_K, d).astype(F32)
        x2d = x2 + jnp.sum(yk * gates[:, :, None], axis=1)
    return x2d.reshape(batch, seq, d)
```

```python
import functools

import jax
import jax.numpy as jnp
from jax import lax
from jax.experimental import pallas as pl
from jax.experimental.pallas import tpu as pltpu

F32 = jnp.float32
BF16 = jnp.bfloat16

D_MODEL = 1024
HG_HEADS = 4
HG_DIM = 128
HG_WIDTH = HG_HEADS * HG_DIM
HG_CHUNK = 64
ATT_HEADS = 8
ATT_DIM = 64
ATT_WIDTH = ATT_HEADS * ATT_DIM
DILATED_PATTERNS = ((128, 1), (512, 4), (2048, 16))
ATT_HALF = 64
ATT_QBLOCK = 128
ROPE_THETA = 10000.0
IN_COLS = 5 * HG_WIDTH + 3 * ATT_WIDTH
N_EXPERTS = 32
TOP_K = 4
D_EXPERT = D_MODEL
SWIGLU_LIMIT = 7.0
SWIGLU_ALPHA = 1.702
EPS = 1e-6
NEG = -1e30

COL_HQ, COL_HF_FWD, COL_HF_BWD, COL_HI, COL_HG, COL_AQ, COL_AK, COL_AV = range(8)

TOKEN_TILE = 512
MOE_BLOCK_ROWS = 512
LANES = 128
VMEM_LIMIT = 56 * 1024 * 1024


def _dot(a, b):
    return jnp.dot(a, b, preferred_element_type=F32)


def _dot_nt(a, b):
    return lax.dot_general(a, b, (((1,), (1,)), ((), ())), preferred_element_type=F32)


def _dot_tn(a, b):
    return lax.dot_general(a, b, (((0,), (0,)), ((), ())), preferred_element_type=F32)


def _sigmoid(x):
    return 1.0 / (1.0 + jnp.exp(-x))


def _head_norm_rope(p, gain, cos, sin_signed, scale):
    lane = lax.broadcasted_iota(jnp.int32, (p.shape[0], LANES), 1)
    low = lane < ATT_DIM
    first_half = (lane % ATT_DIM) < (ATT_DIM // 2)
    outs = []
    for t in range(ATT_WIDTH // LANES):
        blk = p[:, t * LANES:(t + 1) * LANES]
        sq = blk * blk
        s_low = jnp.sum(jnp.where(low, sq, 0.0), axis=-1, keepdims=True)
        s_high = jnp.sum(jnp.where(low, 0.0, sq), axis=-1, keepdims=True)
        r = jnp.where(low, lax.rsqrt(s_low * (1.0 / ATT_DIM) + EPS),
                      lax.rsqrt(s_high * (1.0 / ATT_DIM) + EPS))
        y = blk * r * gain
        partner = jnp.where(first_half, pltpu.roll(y, LANES - ATT_DIM // 2, axis=1),
                            pltpu.roll(y, ATT_DIM // 2, axis=1))
        outs.append((y * cos + partner * sin_signed) * scale)
    return jnp.concatenate(outs, axis=1)


def _inproj_kernel(x_ref, pos_ref, inv_ref, g1_ref, w_ref, qg_ref, kg_ref,
                   out_ref, d4_ref, d16_ref, stage_ref):
    x = x_ref[...]
    ms = jnp.mean(x * x, axis=-1, keepdims=True)
    h = (x * lax.rsqrt(ms + EPS) * g1_ref[...]).astype(BF16)
    ang = pos_ref[...].astype(F32) * inv_ref[...]
    lane = lax.broadcasted_iota(jnp.int32, ang.shape, 1)
    cos = jnp.cos(ang)
    sin_signed = jnp.where((lane % ATT_DIM) < (ATT_DIM // 2), -jnp.sin(ang), jnp.sin(ang))
    for j in range(IN_COLS // 512):
        p = _dot(h, w_ref[:, j * 512:(j + 1) * 512])
        if j == COL_AQ:
            p = _head_norm_rope(p, qg_ref[...], cos, sin_signed, ATT_DIM ** -0.5)
        elif j == COL_AK:
            p = _head_norm_rope(p, kg_ref[...], cos, sin_signed, 1.0)
        out_ref[:, j * 512:(j + 1) * 512] = p.astype(BF16)
        if j >= COL_AQ:
            for c in range(ATT_WIDTH // LANES):
                stage_ref[c] = p[:, c * LANES:(c + 1) * LANES]
            c0 = (j - COL_AQ) * ATT_WIDTH
            for dil, ref in ((4, d4_ref), (16, d16_ref)):
                rows = x.shape[0] // dil
                for r in range(dil):
                    for c in range(ATT_WIDTH // LANES):
                        ref[0, r, :, c0 + c * LANES:c0 + (c + 1) * LANES] = (
                            stage_ref[c, pl.ds(r, rows, stride=dil), :].astype(BF16))


def _inproj(x2d, pos_col, inv_tab, g1, w_in_bf16, qg, kg, batch, seq):
    n = x2d.shape[0]
    t = TOKEN_TILE
    nt = seq // t
    const = lambda i: (0, 0)
    qkv = 3 * ATT_WIDTH

    def residue_major(dil):
        spec = pl.BlockSpec((1, dil, t // dil, qkv), lambda i: (i // nt, 0, i % nt, 0))
        return spec, jax.ShapeDtypeStruct((batch, dil, seq // dil, qkv), BF16)

    spec4, shape4 = residue_major(4)
    spec16, shape16 = residue_major(16)
    return pl.pallas_call(
        _inproj_kernel,
        grid=(n // t,),
        in_specs=[
            pl.BlockSpec((t, D_MODEL), lambda i: (i, 0)),
            pl.BlockSpec((t, 1), lambda i: (i, 0)),
            pl.BlockSpec((1, LANES), const),
            pl.BlockSpec((1, D_MODEL), const),
            pl.BlockSpec((D_MODEL, IN_COLS), const),
            pl.BlockSpec((1, LANES), const),
            pl.BlockSpec((1, LANES), const),
        ],
        out_specs=[pl.BlockSpec((t, IN_COLS), lambda i: (i, 0)), spec4, spec16],
        out_shape=[jax.ShapeDtypeStruct((n, IN_COLS), BF16), shape4, shape16],
        scratch_shapes=[pltpu.VMEM((ATT_WIDTH // LANES, t, LANES), F32)],
        compiler_params=pltpu.CompilerParams(
            dimension_semantics=("parallel",), vmem_limit_bytes=VMEM_LIMIT),
        name="inproj",
    )(x2d, pos_col, inv_tab, g1, w_in_bf16, qg, kg)


def _hgrn_chunk(q_raw, z, v, lb, tri, mask, state_t, last_row):
    z = z.astype(F32)
    q = q_raw.astype(F32)
    sg = _sigmoid(z)
    f = lb + (1.0 - lb) * sg
    k = (1.0 - lb) * (1.0 - sg)
    lf = jnp.log(f)
    lf_hi = lf.astype(BF16)
    lf_lo = (lf - lf_hi.astype(F32)).astype(BF16)
    b = _dot(tri, lf_hi) + _dot(tri, lf_lo)
    b_last = b[last_row:last_row + 1, :]
    qt = (q * _sigmoid(q) * jnp.exp(b)).astype(BF16)
    kt = (k * jnp.exp(-b)).astype(BF16)
    kd = (k * jnp.exp(b_last - b)).astype(BF16)
    a = jnp.where(mask, _dot_nt(qt, kt), 0.0)
    o = _dot(a.astype(BF16), v) + _dot_nt(qt, state_t.astype(BF16))
    new_state = state_t * jnp.exp(b_last) + _dot_tn(v, kd)
    return o, new_state


def _hgrn_kernel(qf_ref, zf_ref, vf_ref, qb_ref, zb_ref, vb_ref, lbf_ref, lbb_ref,
                 of_ref, ob_ref, sf_ref, sb_ref):
    @pl.when(pl.program_id(2) == 0)
    def _():
        sf_ref[...] = jnp.zeros_like(sf_ref)
        sb_ref[...] = jnp.zeros_like(sb_ref)

    c = HG_CHUNK
    row = lax.broadcasted_iota(jnp.int32, (c, c), 0)
    col = lax.broadcasted_iota(jnp.int32, (c, c), 1)
    lower = row >= col
    upper = row <= col
    tri_f = jnp.where(lower, 1.0, 0.0).astype(BF16)
    tri_b = jnp.where(upper, 1.0, 0.0).astype(BF16)
    lbf = lbf_ref[0]
    lbb = lbb_ref[0]
    n_chunks = qf_ref.shape[0] // c

    sf = sf_ref[...]
    sb = sb_ref[...]
    for j in range(n_chunks):
        rf = slice(j * c, (j + 1) * c)
        o, sf = _hgrn_chunk(qf_ref[rf, :], zf_ref[rf, :], vf_ref[rf, :], lbf, tri_f, lower, sf, c - 1)
        of_ref[rf, :] = o.astype(of_ref.dtype)
        jb = n_chunks - 1 - j
        rb = slice(jb * c, (jb + 1) * c)
        o, sb = _hgrn_chunk(qb_ref[rb, :], zb_ref[rb, :], vb_ref[rb, :], lbb, tri_b, upper, sb, 0)
        ob_ref[rb, :] = o.astype(ob_ref.dtype)
    sf_ref[...] = sf
    sb_ref[...] = sb


def _hgrn(proj, lbs, batch, seq):
    n = proj.shape[0]
    t = TOKEN_TILE
    nblk = seq // t

    def fwd(colblk):
        return pl.BlockSpec((t, HG_DIM), lambda b, h, i: (b * nblk + i, colblk * HG_HEADS + h))

    def bwd(colblk):
        return pl.BlockSpec((t, HG_DIM), lambda b, h, i: (b * nblk + nblk - 1 - i, colblk * HG_HEADS + h))

    out_f = pl.BlockSpec((t, HG_DIM), lambda b, h, i: (b * nblk + i, h))
    out_b = pl.BlockSpec((t, HG_DIM), lambda b, h, i: (b * nblk + nblk - 1 - i, h))
    return pl.pallas_call(
        _hgrn_kernel,
        grid=(batch, HG_HEADS, nblk),
        in_specs=[
            fwd(COL_HQ), fwd(COL_HF_FWD), fwd(COL_HI),
            bwd(COL_HQ), bwd(COL_HF_BWD), bwd(COL_HI),
            pl.BlockSpec((1, 1, HG_DIM), lambda b, h, i: (h, 0, 0)),
            pl.BlockSpec((1, 1, HG_DIM), lambda b, h, i: (HG_HEADS + h, 0, 0)),
        ],
        out_specs=[out_f, out_b],
        out_shape=[jax.ShapeDtypeStruct((n, HG_WIDTH), BF16)] * 2,
        scratch_shapes=[pltpu.VMEM((HG_DIM, HG_DIM), F32)] * 2,
        compiler_params=pltpu.CompilerParams(
            dimension_semantics=("parallel", "parallel", "arbitrary"), vmem_limit_bytes=VMEM_LIMIT),
        name="hgrn",
    )(proj, proj, proj, proj, proj, proj, lbs, lbs)


def _attn_kernel(q_ref, kc_ref, kp_ref, kn_ref, vc_ref, vp_ref, vn_ref, o_ref, l_ref,
                 kw_ref, vw_ref, *, tq, length):
    n = pl.program_id(2)
    half = ATT_HALF
    kw_ref[0:half, :] = kp_ref[...]
    kw_ref[half:half + tq, :] = kc_ref[...]
    kw_ref[half + tq:, :] = kn_ref[...]
    vw_ref[0:half, :] = vp_ref[...]
    vw_ref[half:half + tq, :] = vc_ref[...]
    vw_ref[half + tq:, :] = vn_ref[...]

    qb_rows = ATT_QBLOCK
    win = qb_rows + 2 * half
    i_idx = lax.broadcasted_iota(jnp.int32, (qb_rows, win), 0)
    j_idx = lax.broadcasted_iota(jnp.int32, (qb_rows, win), 1)
    band = (j_idx >= i_idx) & (j_idx <= i_idx + 2 * half)

    def body(qb, carry):
        r0 = pl.multiple_of(qb * qb_rows, qb_rows)
        base = n * tq + r0 - half
        valid = band & (j_idx >= -base) & (j_idx < length - base)
        q = q_ref[pl.ds(r0, qb_rows), :]
        kw = kw_ref[pl.ds(r0, win), :]
        vw = vw_ref[pl.ds(r0, win), :]
        outs, lses = [], []
        for h in range(ATT_HEADS):
            cs = slice(h * ATT_DIM, (h + 1) * ATT_DIM)
            s = jnp.where(valid, _dot_nt(q[:, cs], kw[:, cs]), NEG)
            m = jnp.max(s, axis=-1, keepdims=True)
            p = jnp.exp(s - m)
            den = jnp.sum(p, axis=-1, keepdims=True)
            outs.append(_dot(p.astype(BF16), vw[:, cs]) / den)
            lses.append(jnp.broadcast_to(m + jnp.log(den), (qb_rows, ATT_DIM)))
        o_ref[pl.ds(r0, qb_rows), :] = jnp.concatenate(outs, axis=1).astype(o_ref.dtype)
        l_ref[pl.ds(r0, qb_rows), :] = jnp.concatenate(lses, axis=1)
        return carry

    lax.fori_loop(0, tq // qb_rows, body, 0)


def _attention(qkv, col0):
    batch, dil, length, _ = qkv.shape
    tq = min(TOKEN_TILE, length)
    nq = length // tq
    hb = tq // ATT_HALF
    n_hblk = length // ATT_HALF

    def cur(col):
        return pl.BlockSpec((None, None, tq, ATT_WIDTH), lambda b, r, n: (b, r, n, col))

    def prev(col):
        return pl.BlockSpec((None, None, ATT_HALF, ATT_WIDTH),
                            lambda b, r, n: (b, r, jnp.maximum(n * hb - 1, 0), col))

    def nxt(col):
        return pl.BlockSpec((None, None, ATT_HALF, ATT_WIDTH),
                            lambda b, r, n: (b, r, jnp.minimum((n + 1) * hb, n_hblk - 1), col))

    out_spec = pl.BlockSpec((None, None, tq, ATT_WIDTH), lambda b, r, n: (b, r, n, 0))
    return pl.pallas_call(
        functools.partial(_attn_kernel, tq=tq, length=length),
        grid=(batch, dil, nq),
        in_specs=[cur(col0), cur(col0 + 1), prev(col0 + 1), nxt(col0 + 1),
                  cur(col0 + 2), prev(col0 + 2), nxt(col0 + 2)],
        out_specs=[out_spec, out_spec],
        out_shape=[jax.ShapeDtypeStruct((batch, dil, length, ATT_WIDTH), BF16),
                   jax.ShapeDtypeStruct((batch, dil, length, ATT_WIDTH), F32)],
        scratch_shapes=[pltpu.VMEM((tq + 2 * ATT_HALF, ATT_WIDTH), BF16)] * 2,
        compiler_params=pltpu.CompilerParams(
            dimension_semantics=("parallel", "parallel", "parallel"), vmem_limit_bytes=VMEM_LIMIT),
        name=f"attn_d{dil}",
    )(qkv, qkv, qkv, qkv, qkv, qkv, qkv)


def _token_major(src_ref, stage_ref):
    dil, rows = src_ref.shape[1], src_ref.shape[2]
    nc = src_ref.shape[3] // LANES
    for r in range(dil):
        for c in range(nc):
            stage_ref[c, pl.ds(r, rows, stride=dil), :] = (
                src_ref[0, r, :, c * LANES:(c + 1) * LANES].astype(F32))
    return jnp.concatenate([stage_ref[c] for c in range(nc)], axis=1)


def _outproj_kernel(of_ref, ob_ref, hg_ref, o1_ref, o2_ref, o3_ref, l1_ref, l2_ref, l3_ref,
                    x_ref, og_ref, w_ref, g2_ref, wrh_ref, wrl_ref, br_ref,
                    x2_ref, xn_ref, gate_ref, idx_ref, st_o2, st_o3, st_l2, st_l3):
    o = of_ref[...].astype(F32) + ob_ref[...].astype(F32)
    hg = hg_ref[...].astype(F32)
    parts = []
    for h in range(HG_HEADS):
        blk = o[:, h * HG_DIM:(h + 1) * HG_DIM]
        ms = jnp.mean(blk * blk, axis=-1, keepdims=True)
        parts.append(blk * lax.rsqrt(ms + EPS) * og_ref[...])
    o_hg = jnp.concatenate(parts, axis=1) * (hg * _sigmoid(hg))

    l1 = l1_ref[...]
    l2 = _token_major(l2_ref, st_l2)
    l3 = _token_major(l3_ref, st_l3)
    mx = jnp.maximum(jnp.maximum(l1, l2), l3)
    e1, e2, e3 = jnp.exp(l1 - mx), jnp.exp(l2 - mx), jnp.exp(l3 - mx)
    o_att = (e1 * o1_ref[...].astype(F32) + e2 * _token_major(o2_ref, st_o2)
             + e3 * _token_major(o3_ref, st_o3)) / (e1 + e2 + e3)

    y = _dot(o_hg.astype(BF16), w_ref[0:HG_WIDTH, :]) + _dot(o_att.astype(BF16), w_ref[HG_WIDTH:, :])
    x2 = x_ref[...] + y
    x2_ref[...] = x2

    ms = jnp.mean(x2 * x2, axis=-1, keepdims=True)
    xn = x2 * lax.rsqrt(ms + EPS) * g2_ref[...]
    xn_hi = xn.astype(BF16)
    xn_ref[...] = xn_hi
    xn_lo = (xn - xn_hi.astype(F32)).astype(BF16)
    logits = (_dot(xn_hi, wrh_ref[...]) + _dot(xn_lo, wrh_ref[...]) + _dot(xn_hi, wrl_ref[...])
              + br_ref[...])

    lane = lax.broadcasted_iota(jnp.int32, logits.shape, 1)
    lane_f = lane.astype(F32)
    work = jnp.where(lane < N_EXPERTS, logits, -jnp.inf)
    vals, idxs = [], []
    for _ in range(TOP_K):
        m = jnp.max(work, axis=-1, keepdims=True)
        idx = jnp.min(jnp.where(work == m, lane_f, float(LANES)), axis=-1, keepdims=True)
        vals.append(m)
        idxs.append(idx)
        work = jnp.where(lane_f == idx, -jnp.inf, work)
    es = [jnp.exp(v - vals[0]) for v in vals]
    den = es[0] + es[1] + es[2] + es[3]
    gate_out = jnp.zeros(logits.shape, F32)
    idx_out = jnp.zeros(logits.shape, F32)
    for k in range(TOP_K):
        gate_out = jnp.where(lane == k, es[k] / den, gate_out)
        idx_out = jnp.where(lane == k, idxs[k], idx_out)
    gate_ref[...] = gate_out
    idx_ref[...] = idx_out.astype(jnp.int32)


def _outproj(o_f, o_b, proj, atts, lses, x2d, og, w_out_bf16, g2, wr_hi, wr_lo, br, seq):
    n = x2d.shape[0]
    t = TOKEN_TILE
    nt = seq // t
    row = lambda i: (i, 0)
    const = lambda i: (0, 0)
    half = pl.BlockSpec((t, 512), row)

    def residue_major(dil):
        return pl.BlockSpec((1, dil, t // dil, ATT_WIDTH), lambda i: (i // nt, 0, i % nt, 0))

    rm4, rm16 = residue_major(4), residue_major(16)
    return pl.pallas_call(
        _outproj_kernel,
        grid=(n // t,),
        in_specs=[
            half, half, pl.BlockSpec((t, 512), lambda i: (i, COL_HG)),
            half, rm4, rm16, half, rm4, rm16,
            pl.BlockSpec((t, D_MODEL), row),
            pl.BlockSpec((1, HG_DIM), const),
            pl.BlockSpec((D_MODEL, D_MODEL), const),
            pl.BlockSpec((1, D_MODEL), const),
            pl.BlockSpec((D_MODEL, LANES), const),
            pl.BlockSpec((D_MODEL, LANES), const),
            pl.BlockSpec((1, LANES), const),
        ],
        out_specs=[pl.BlockSpec((t, D_MODEL), row), pl.BlockSpec((t, D_MODEL), row),
                   pl.BlockSpec((t, LANES), row), pl.BlockSpec((t, LANES), row)],
        out_shape=[jax.ShapeDtypeStruct((n, D_MODEL), F32), jax.ShapeDtypeStruct((n, D_MODEL), BF16),
                   jax.ShapeDtypeStruct((n, LANES), F32), jax.ShapeDtypeStruct((n, LANES), jnp.int32)],
        scratch_shapes=[pltpu.VMEM((ATT_WIDTH // LANES, t, LANES), F32)] * 4,
        compiler_params=pltpu.CompilerParams(
            dimension_semantics=("parallel",), vmem_limit_bytes=VMEM_LIMIT),
        name="outproj",
    )(o_f, o_b, proj, *atts, *lses, x2d, og, w_out_bf16, g2, wr_hi, wr_lo, br)


def _moe_kernel(bexp_ref, bfirst_ref, nused_ref, xs_ref, wu_ref, bg_ref, bl_ref, wd_ref, bd_ref,
                y_ref, wg_s, wl_s, wd_s):
    i = pl.program_id(0)

    @pl.when(bfirst_ref[i] == 1)
    def _():
        r = lax.broadcasted_iota(jnp.int32, (2 * LANES, 2 * LANES), 0)
        c = lax.broadcasted_iota(jnp.int32, (2 * LANES, 2 * LANES), 1)
        src = jnp.where(c < LANES, 2 * c, 2 * (c - LANES) + 1)
        perm = jnp.where(r == src, 1.0, 0.0).astype(BF16)
        rows = 256
        for rb in range(D_MODEL // rows):
            rs = slice(rb * rows, (rb + 1) * rows)
            for cb in range(D_EXPERT // LANES):
                w = wu_ref[0, rs, cb * 2 * LANES:(cb + 1) * 2 * LANES].astype(BF16)
                split = _dot(w, perm).astype(BF16)
                wg_s[rs, cb * LANES:(cb + 1) * LANES] = split[:, :LANES]
                wl_s[rs, cb * LANES:(cb + 1) * LANES] = split[:, LANES:]
        wd_s[...] = wd_ref[0].astype(BF16)

    @pl.when(i < nused_ref[0])
    def _():
        x = xs_ref[...]
        hglu = _dot(x, wg_s[...]) + bg_ref[0]
        hlin = _dot(x, wl_s[...]) + bl_ref[0]
        glu = jnp.minimum(hglu, SWIGLU_LIMIT)
        lin = jnp.clip(hlin, -SWIGLU_LIMIT, SWIGLU_LIMIT)
        act = glu * _sigmoid(SWIGLU_ALPHA * glu) * (lin + 1.0)
        y_ref[...] = (_dot(act.astype(BF16), wd_s[...]) + bd_ref[0]).astype(y_ref.dtype)

    @pl.when(i >= nused_ref[0])
    def _():
        y_ref[...] = jnp.zeros_like(y_ref)


def _moe(block_exp, block_first, n_used, xs, w_up, b_glu, b_lin, w_down, b_down):
    p_rows = xs.shape[0]
    bm = MOE_BLOCK_ROWS
    nb = p_rows // bm
    exp3 = lambda i, be, bf, nu: (be[i], 0, 0)
    grid_spec = pltpu.PrefetchScalarGridSpec(
        num_scalar_prefetch=3,
        grid=(nb,),
        in_specs=[
            pl.BlockSpec((bm, D_MODEL), lambda i, be, bf, nu: (jnp.minimum(i, nu[0] - 1), 0)),
            pl.BlockSpec((1, D_MODEL, 2 * D_EXPERT), exp3),
            pl.BlockSpec((1, 1, D_EXPERT), exp3),
            pl.BlockSpec((1, 1, D_EXPERT), exp3),
            pl.BlockSpec((1, D_EXPERT, D_MODEL), exp3),
            pl.BlockSpec((1, 1, D_MODEL), exp3),
        ],
        out_specs=pl.BlockSpec((bm, D_MODEL), lambda i, be, bf, nu: (jnp.where(i < nu[0], i, nb - 1), 0)),
        scratch_shapes=[pltpu.VMEM((D_MODEL, D_EXPERT), BF16), pltpu.VMEM((D_MODEL, D_EXPERT), BF16),
                        pltpu.VMEM((D_EXPERT, D_MODEL), BF16)],
    )
    return pl.pallas_call(
        _moe_kernel,
        grid_spec=grid_spec,
        out_shape=jax.ShapeDtypeStruct((p_rows, D_MODEL), BF16),
        compiler_params=pltpu.CompilerParams(
            dimension_semantics=("arbitrary",), vmem_limit_bytes=VMEM_LIMIT),
        name="moe",
    )(block_exp, block_first, n_used, xs, w_up, b_glu, b_lin, w_down, b_down)


def _route(top_idx):
    n = top_idx.shape[0]
    a = n * TOP_K
    bm = MOE_BLOCK_ROWS
    nb = a // bm + N_EXPERTS
    e_flat = top_idx.reshape(a)
    onehot = (e_flat[:, None] == jnp.arange(N_EXPERTS, dtype=jnp.int32)[None, :]).astype(jnp.int32)
    csum = jnp.cumsum(onehot, axis=0)
    counts = csum[-1]
    padded = ((counts + bm - 1) // bm) * bm
    pad_end = jnp.cumsum(padded)
    pad_start = pad_end - padded
    dest = jnp.sum(onehot * (csum - 1 + pad_start[None, :]), axis=1)
    n_used = (pad_end[-1] // bm).astype(jnp.int32)
    blk = jnp.arange(nb, dtype=jnp.int32)
    bexp = jnp.sum((pad_end[None, :] <= (blk * bm)[:, None]).astype(jnp.int32), axis=1)
    bexp = jnp.minimum(bexp, N_EXPERTS - 1)
    bexp = jnp.where(blk < n_used, bexp, bexp[jnp.maximum(n_used - 1, 0)])
    bfirst = jnp.concatenate([jnp.ones((1,), jnp.int32), (bexp[1:] != bexp[:-1]).astype(jnp.int32)])
    row_tok = jnp.zeros((nb * bm,), jnp.int32).at[dest].set(jnp.arange(a, dtype=jnp.int32) // TOP_K)
    return dest, row_tok, bexp, bfirst, n_used.reshape(1)


def kernel(x, positions, norm1_g, w_in, q_norm_g, k_norm_g, hgrn_lower_bounds, hgrn_onorm_g,
           w_out, norm2_g, w_router, b_router, w_up, b_up, w_down, b_down):
    batch, seq, d = x.shape
    n = batch * seq
    depth = norm1_g.shape[0]
    lbs_all = jnp.cumsum(jax.nn.softmax(hgrn_lower_bounds.astype(F32), axis=0), axis=0)
    half = ATT_DIM // 2
    inv = 1.0 / (ROPE_THETA ** (jnp.arange(half, dtype=F32) / half))
    inv_tab = jnp.tile(inv, LANES // half).reshape(1, LANES)
    pos_col = positions.reshape(n, 1)

    x2d = x.reshape(n, d)
    for l in range(depth):
        lbs = lbs_all[l].reshape(2 * HG_HEADS, 1, HG_DIM)
        proj, qkv4, qkv16 = _inproj(
            x2d, pos_col, inv_tab, norm1_g[l].reshape(1, d), w_in[l].astype(BF16),
            jnp.tile(q_norm_g[l], LANES // ATT_DIM).reshape(1, LANES),
            jnp.tile(k_norm_g[l], LANES // ATT_DIM).reshape(1, LANES), batch, seq)
        o_f, o_b = _hgrn(proj, lbs, batch, seq)
        o1, l1 = _attention(proj.reshape(batch, 1, seq, IN_COLS), COL_AQ)
        o4, l4 = _attention(qkv4, 0)
        o16, l16 = _attention(qkv16, 0)
        atts = [o1.reshape(n, ATT_WIDTH), o4, o16]
        lses = [l1.reshape(n, ATT_WIDTH), l4, l16]

        wr = jnp.pad(w_router[l], ((0, 0), (0, LANES - N_EXPERTS)))
        wr_hi = wr.astype(BF16)
        wr_lo = (wr - wr_hi.astype(F32)).astype(BF16)
        br = jnp.pad(b_router[l], (0, LANES - N_EXPERTS)).reshape(1, LANES)
        x2, xn, gates, top_idx = _outproj(
            o_f, o_b, proj, atts, lses, x2d, hgrn_onorm_g[l].reshape(1, HG_DIM),
            w_out[l].astype(BF16), norm2_g[l].reshape(1, d), wr_hi, wr_lo, br, seq)
        gates = gates[:, :TOP_K]
        top_idx = top_idx[:, :TOP_K]

        dest, row_tok, bexp, bfirst, n_used = _route(top_idx)
        xs = xn[row_tok]
        y = _moe(bexp, bfirst, n_used, xs, w_up[l],
                 b_up[l][:, 0::2].reshape(N_EXPERTS, 1, D_EXPERT),
                 b_up[l][:, 1::2].reshape(N_EXPERTS, 1, D_EXPERT),
                 w_down[l], b_down[l].reshape(N_EXPERTS, 1, D_MODEL))
        yk = y[dest].reshape(n, TOP_K, d).astype(F32)
        x2d = x2 + jnp.sum(yk * gates[:, :, None], axis=1)
    return x2d.reshape(batch, seq, d)
```

```python
import functools

import jax
import jax.numpy as jnp
from jax import lax
from jax.experimental import pallas as pl
from jax.experimental.pallas import tpu as pltpu

F32 = jnp.float32
BF16 = jnp.bfloat16

D_MODEL = 1024
HG_HEADS = 4
HG_DIM = 128
HG_WIDTH = HG_HEADS * HG_DIM
HG_CHUNK = 64
ATT_HEADS = 8
ATT_DIM = 64
ATT_WIDTH = ATT_HEADS * ATT_DIM
DILATED_PATTERNS = ((128, 1), (512, 4), (2048, 16))
ATT_HALF = 64
ATT_QBLOCK = 128
ROPE_THETA = 10000.0
IN_COLS = 5 * HG_WIDTH + 3 * ATT_WIDTH
N_EXPERTS = 32
TOP_K = 4
D_EXPERT = D_MODEL
SWIGLU_LIMIT = 7.0
SWIGLU_ALPHA = 1.702
EPS = 1e-6
NEG = -1e30

COL_HQ, COL_HF_FWD, COL_HF_BWD, COL_HI, COL_HG, COL_AQ, COL_AK, COL_AV = range(8)

TOKEN_TILE = 512
MOE_BLOCK_ROWS = 512
DISPATCH_TOKENS = 2048
COMBINE_TOKENS = 256
LANES = 128
VMEM_LIMIT = 56 * 1024 * 1024


def _dot(a, b):
    return jnp.dot(a, b, preferred_element_type=F32)


def _dot_nt(a, b):
    return lax.dot_general(a, b, (((1,), (1,)), ((), ())), preferred_element_type=F32)


def _dot_tn(a, b):
    return lax.dot_general(a, b, (((0,), (0,)), ((), ())), preferred_element_type=F32)


def _sigmoid(x):
    return 1.0 / (1.0 + jnp.exp(-x))


def _head_norm_rope(p, gain, cos, sin_signed, scale):
    lane = lax.broadcasted_iota(jnp.int32, (p.shape[0], LANES), 1)
    low = lane < ATT_DIM
    first_half = (lane % ATT_DIM) < (ATT_DIM // 2)
    outs = []
    for t in range(ATT_WIDTH // LANES):
        blk = p[:, t * LANES:(t + 1) * LANES]
        sq = blk * blk
        s_low = jnp.sum(jnp.where(low, sq, 0.0), axis=-1, keepdims=True)
        s_high = jnp.sum(jnp.where(low, 0.0, sq), axis=-1, keepdims=True)
        r = jnp.where(low, lax.rsqrt(s_low * (1.0 / ATT_DIM) + EPS),
                      lax.rsqrt(s_high * (1.0 / ATT_DIM) + EPS))
        y = blk * r * gain
        partner = jnp.where(first_half, pltpu.roll(y, LANES - ATT_DIM // 2, axis=1),
                            pltpu.roll(y, ATT_DIM // 2, axis=1))
        outs.append((y * cos + partner * sin_signed) * scale)
    return jnp.concatenate(outs, axis=1)


def _inproj_kernel(x_ref, pos_ref, inv_ref, g1_ref, w_ref, qg_ref, kg_ref,
                   out_ref, d4_ref, d16_ref, stage_ref):
    x = x_ref[...]
    ms = jnp.mean(x * x, axis=-1, keepdims=True)
    h = (x * lax.rsqrt(ms + EPS) * g1_ref[...]).astype(BF16)
    ang = pos_ref[...].astype(F32) * inv_ref[...]
    lane = lax.broadcasted_iota(jnp.int32, ang.shape, 1)
    cos = jnp.cos(ang)
    sin_signed = jnp.where((lane % ATT_DIM) < (ATT_DIM // 2), -jnp.sin(ang), jnp.sin(ang))
    for j in range(IN_COLS // 512):
        p = _dot(h, w_ref[:, j * 512:(j + 1) * 512])
        if j == COL_AQ:
            p = _head_norm_rope(p, qg_ref[...], cos, sin_signed, ATT_DIM ** -0.5)
        elif j == COL_AK:
            p = _head_norm_rope(p, kg_ref[...], cos, sin_signed, 1.0)
        out_ref[:, j * 512:(j + 1) * 512] = p.astype(BF16)
        if j >= COL_AQ:
            for c in range(ATT_WIDTH // LANES):
                stage_ref[c] = p[:, c * LANES:(c + 1) * LANES]
            c0 = (j - COL_AQ) * ATT_WIDTH
            for dil, ref in ((4, d4_ref), (16, d16_ref)):
                rows = x.shape[0] // dil
                for r in range(dil):
                    for c in range(ATT_WIDTH // LANES):
                        ref[0, r, :, c0 + c * LANES:c0 + (c + 1) * LANES] = (
                            stage_ref[c, pl.ds(r, rows, stride=dil), :].astype(BF16))


def _inproj(x2d, pos_col, inv_tab, g1, w_in_bf16, qg, kg, batch, seq):
    n = x2d.shape[0]
    t = TOKEN_TILE
    nt = seq // t
    const = lambda i: (0, 0)
    qkv = 3 * ATT_WIDTH

    def residue_major(dil):
        spec = pl.BlockSpec((1, dil, t // dil, qkv), lambda i: (i // nt, 0, i % nt, 0))
        return spec, jax.ShapeDtypeStruct((batch, dil, seq // dil, qkv), BF16)

    spec4, shape4 = residue_major(4)
    spec16, shape16 = residue_major(16)
    return pl.pallas_call(
        _inproj_kernel,
        grid=(n // t,),
        in_specs=[
            pl.BlockSpec((t, D_MODEL), lambda i: (i, 0)),
            pl.BlockSpec((t, 1), lambda i: (i, 0)),
            pl.BlockSpec((1, LANES), const),
            pl.BlockSpec((1, D_MODEL), const),
            pl.BlockSpec((D_MODEL, IN_COLS), const),
            pl.BlockSpec((1, LANES), const),
            pl.BlockSpec((1, LANES), const),
        ],
        out_specs=[pl.BlockSpec((t, IN_COLS), lambda i: (i, 0)), spec4, spec16],
        out_shape=[jax.ShapeDtypeStruct((n, IN_COLS), BF16), shape4, shape16],
        scratch_shapes=[pltpu.VMEM((ATT_WIDTH // LANES, t, LANES), F32)],
        compiler_params=pltpu.CompilerParams(
            dimension_semantics=("parallel",), vmem_limit_bytes=VMEM_LIMIT),
        name="inproj",
    )(x2d, pos_col, inv_tab, g1, w_in_bf16, qg, kg)


def _hgrn_chunk(q_raw, z, v, lb, tri, mask, state_t, last_row):
    z = z.astype(F32)
    q = q_raw.astype(F32)
    sg = _sigmoid(z)
    f = lb + (1.0 - lb) * sg
    k = (1.0 - lb) * (1.0 - sg)
    lf = jnp.log(f)
    lf_hi = lf.astype(BF16)
    lf_lo = (lf - lf_hi.astype(F32)).astype(BF16)
    b = _dot(tri, lf_hi) + _dot(tri, lf_lo)
    b_last = b[last_row:last_row + 1, :]
    qt = (q * _sigmoid(q) * jnp.exp(b)).astype(BF16)
    kt = (k * jnp.exp(-b)).astype(BF16)
    kd = (k * jnp.exp(b_last - b)).astype(BF16)
    a = jnp.where(mask, _dot_nt(qt, kt), 0.0)
    o = _dot(a.astype(BF16), v) + _dot_nt(qt, state_t.astype(BF16))
    new_state = state_t * jnp.exp(b_last) + _dot_tn(v, kd)
    return o, new_state


def _hgrn_kernel(qf_ref, zf_ref, vf_ref, qb_ref, zb_ref, vb_ref, lbf_ref, lbb_ref,
                 of_ref, ob_ref, sf_ref, sb_ref):
    @pl.when(pl.program_id(2) == 0)
    def _():
        sf_ref[...] = jnp.zeros_like(sf_ref)
        sb_ref[...] = jnp.zeros_like(sb_ref)

    c = HG_CHUNK
    row = lax.broadcasted_iota(jnp.int32, (c, c), 0)
    col = lax.broadcasted_iota(jnp.int32, (c, c), 1)
    lower = row >= col
    upper = row <= col
    tri_f = jnp.where(lower, 1.0, 0.0).astype(BF16)
    tri_b = jnp.where(upper, 1.0, 0.0).astype(BF16)
    lbf = lbf_ref[0]
    lbb = lbb_ref[0]
    n_chunks = qf_ref.shape[0] // c

    sf = sf_ref[...]
    sb = sb_ref[...]
    for j in range(n_chunks):
        rf = slice(j * c, (j + 1) * c)
        o, sf = _hgrn_chunk(qf_ref[rf, :], zf_ref[rf, :], vf_ref[rf, :], lbf, tri_f, lower, sf, c - 1)
        of_ref[rf, :] = o.astype(of_ref.dtype)
        jb = n_chunks - 1 - j
        rb = slice(jb * c, (jb + 1) * c)
        o, sb = _hgrn_chunk(qb_ref[rb, :], zb_ref[rb, :], vb_ref[rb, :], lbb, tri_b, upper, sb, 0)
        ob_ref[rb, :] = o.astype(ob_ref.dtype)
    sf_ref[...] = sf
    sb_ref[...] = sb


def _hgrn(proj, lbs, batch, seq):
    n = proj.shape[0]
    t = TOKEN_TILE
    nblk = seq // t

    def fwd(colblk):
        return pl.BlockSpec((t, HG_DIM), lambda b, h, i: (b * nblk + i, colblk * HG_HEADS + h))

    def bwd(colblk):
        return pl.BlockSpec((t, HG_DIM), lambda b, h, i: (b * nblk + nblk - 1 - i, colblk * HG_HEADS + h))

    out_f = pl.BlockSpec((t, HG_DIM), lambda b, h, i: (b * nblk + i, h))
    out_b = pl.BlockSpec((t, HG_DIM), lambda b, h, i: (b * nblk + nblk - 1 - i, h))
    return pl.pallas_call(
        _hgrn_kernel,
        grid=(batch, HG_HEADS, nblk),
        in_specs=[
            fwd(COL_HQ), fwd(COL_HF_FWD), fwd(COL_HI),
            bwd(COL_HQ), bwd(COL_HF_BWD), bwd(COL_HI),
            pl.BlockSpec((1, 1, HG_DIM), lambda b, h, i: (h, 0, 0)),
            pl.BlockSpec((1, 1, HG_DIM), lambda b, h, i: (HG_HEADS + h, 0, 0)),
        ],
        out_specs=[out_f, out_b],
        out_shape=[jax.ShapeDtypeStruct((n, HG_WIDTH), BF16)] * 2,
        scratch_shapes=[pltpu.VMEM((HG_DIM, HG_DIM), F32)] * 2,
        compiler_params=pltpu.CompilerParams(
            dimension_semantics=("parallel", "parallel", "arbitrary"), vmem_limit_bytes=VMEM_LIMIT),
        name="hgrn",
    )(proj, proj, proj, proj, proj, proj, lbs, lbs)


def _attn_kernel(q_ref, kc_ref, kp_ref, kn_ref, vc_ref, vp_ref, vn_ref, o_ref, l_ref,
                 kw_ref, vw_ref, *, tq, length):
    n = pl.program_id(2)
    half = ATT_HALF
    kw_ref[0:half, :] = kp_ref[...]
    kw_ref[half:half + tq, :] = kc_ref[...]
    kw_ref[half + tq:, :] = kn_ref[...]
    vw_ref[0:half, :] = vp_ref[...]
    vw_ref[half:half + tq, :] = vc_ref[...]
    vw_ref[half + tq:, :] = vn_ref[...]

    qb_rows = ATT_QBLOCK
    win = qb_rows + 2 * half
    i_idx = lax.broadcasted_iota(jnp.int32, (qb_rows, win), 0)
    j_idx = lax.broadcasted_iota(jnp.int32, (qb_rows, win), 1)
    band = (j_idx >= i_idx) & (j_idx <= i_idx + 2 * half)

    def body(qb, carry):
        r0 = pl.multiple_of(qb * qb_rows, qb_rows)
        base = n * tq + r0 - half
        valid = band & (j_idx >= -base) & (j_idx < length - base)
        q = q_ref[pl.ds(r0, qb_rows), :]
        kw = kw_ref[pl.ds(r0, win), :]
        vw = vw_ref[pl.ds(r0, win), :]
        outs, lses = [], []
        for h in range(ATT_HEADS):
            cs = slice(h * ATT_DIM, (h + 1) * ATT_DIM)
            s = jnp.where(valid, _dot_nt(q[:, cs], kw[:, cs]), NEG)
            m = jnp.max(s, axis=-1, keepdims=True)
            p = jnp.exp(s - m)
            den = jnp.sum(p, axis=-1, keepdims=True)
            outs.append(_dot(p.astype(BF16), vw[:, cs]) / den)
            lses.append(jnp.broadcast_to(m + jnp.log(den), (qb_rows, ATT_DIM)))
        o_ref[pl.ds(r0, qb_rows), :] = jnp.concatenate(outs, axis=1).astype(o_ref.dtype)
        l_ref[pl.ds(r0, qb_rows), :] = jnp.concatenate(lses, axis=1)
        return carry

    lax.fori_loop(0, tq // qb_rows, body, 0)


def _attention(qkv, col0):
    batch, dil, length, _ = qkv.shape
    tq = min(TOKEN_TILE, length)
    nq = length // tq
    hb = tq // ATT_HALF
    n_hblk = length // ATT_HALF

    def cur(col):
        return pl.BlockSpec((None, None, tq, ATT_WIDTH), lambda b, r, n: (b, r, n, col))

    def prev(col):
        return pl.BlockSpec((None, None, ATT_HALF, ATT_WIDTH),
                            lambda b, r, n: (b, r, jnp.maximum(n * hb - 1, 0), col))

    def nxt(col):
        return pl.BlockSpec((None, None, ATT_HALF, ATT_WIDTH),
                            lambda b, r, n: (b, r, jnp.minimum((n + 1) * hb, n_hblk - 1), col))

    out_spec = pl.BlockSpec((None, None, tq, ATT_WIDTH), lambda b, r, n: (b, r, n, 0))
    return pl.pallas_call(
        functools.partial(_attn_kernel, tq=tq, length=length),
        grid=(batch, dil, nq),
        in_specs=[cur(col0), cur(col0 + 1), prev(col0 + 1), nxt(col0 + 1),
                  cur(col0 + 2), prev(col0 + 2), nxt(col0 + 2)],
        out_specs=[out_spec, out_spec],
        out_shape=[jax.ShapeDtypeStruct((batch, dil, length, ATT_WIDTH), BF16),
                   jax.ShapeDtypeStruct((batch, dil, length, ATT_WIDTH), F32)],
        scratch_shapes=[pltpu.VMEM((tq + 2 * ATT_HALF, ATT_WIDTH), BF16)] * 2,
        compiler_params=pltpu.CompilerParams(
            dimension_semantics=("parallel", "parallel", "parallel"), vmem_limit_bytes=VMEM_LIMIT),
        name=f"attn_d{dil}",
    )(qkv, qkv, qkv, qkv, qkv, qkv, qkv)


def _token_major(src_ref, stage_ref):
    dil, rows = src_ref.shape[1], src_ref.shape[2]
    nc = src_ref.shape[3] // LANES
    for r in range(dil):
        for c in range(nc):
            stage_ref[c, pl.ds(r, rows, stride=dil), :] = (
                src_ref[0, r, :, c * LANES:(c + 1) * LANES].astype(F32))
    return jnp.concatenate([stage_ref[c] for c in range(nc)], axis=1)


def _outproj_kernel(of_ref, ob_ref, hg_ref, o1_ref, o2_ref, o3_ref, l1_ref, l2_ref, l3_ref,
                    x_ref, og_ref, w_ref, g2_ref, wrh_ref, wrl_ref, br_ref,
                    x2_ref, xn_ref, gate_ref, idx_ref, st_o2, st_o3, st_l2, st_l3):
    o = of_ref[...].astype(F32) + ob_ref[...].astype(F32)
    hg = hg_ref[...].astype(F32)
    parts = []
    for h in range(HG_HEADS):
        blk = o[:, h * HG_DIM:(h + 1) * HG_DIM]
        ms = jnp.mean(blk * blk, axis=-1, keepdims=True)
        parts.append(blk * lax.rsqrt(ms + EPS) * og_ref[...])
    o_hg = jnp.concatenate(parts, axis=1) * (hg * _sigmoid(hg))

    l1 = l1_ref[...]
    l2 = _token_major(l2_ref, st_l2)
    l3 = _token_major(l3_ref, st_l3)
    mx = jnp.maximum(jnp.maximum(l1, l2), l3)
    e1, e2, e3 = jnp.exp(l1 - mx), jnp.exp(l2 - mx), jnp.exp(l3 - mx)
    o_att = (e1 * o1_ref[...].astype(F32) + e2 * _token_major(o2_ref, st_o2)
             + e3 * _token_major(o3_ref, st_o3)) / (e1 + e2 + e3)

    y = _dot(o_hg.astype(BF16), w_ref[0:HG_WIDTH, :]) + _dot(o_att.astype(BF16), w_ref[HG_WIDTH:, :])
    x2 = x_ref[...] + y
    x2_ref[...] = x2

    ms = jnp.mean(x2 * x2, axis=-1, keepdims=True)
    xn = x2 * lax.rsqrt(ms + EPS) * g2_ref[...]
    xn_ref[...] = xn
    xn_hi = xn.astype(BF16)
    xn_lo = (xn - xn_hi.astype(F32)).astype(BF16)
    logits = (_dot(xn_hi, wrh_ref[...]) + _dot(xn_lo, wrh_ref[...]) + _dot(xn_hi, wrl_ref[...])
              + br_ref[...])

    lane = lax.broadcasted_iota(jnp.int32, logits.shape, 1)
    lane_f = lane.astype(F32)
    work = jnp.where(lane < N_EXPERTS, logits, -jnp.inf)
    vals, idxs = [], []
    for _ in range(TOP_K):
        m = jnp.max(work, axis=-1, keepdims=True)
        idx = jnp.min(jnp.where(work == m, lane_f, float(LANES)), axis=-1, keepdims=True)
        vals.append(m)
        idxs.append(idx)
        work = jnp.where(lane_f == idx, -jnp.inf, work)
    es = [jnp.exp(v - vals[0]) for v in vals]
    den = es[0] + es[1] + es[2] + es[3]
    gate_out = jnp.zeros(logits.shape, F32)
    idx_out = jnp.zeros(logits.shape, F32)
    for k in range(TOP_K):
        gate_out = jnp.where(lane == k, es[k] / den, gate_out)
        idx_out = jnp.where(lane == k, idxs[k], idx_out)
    gate_ref[...] = gate_out
    idx_ref[...] = idx_out.astype(jnp.int32)


def _outproj(o_f, o_b, proj, atts, lses, x2d, og, w_out_bf16, g2, wr_hi, wr_lo, br, seq):
    n = x2d.shape[0]
    t = TOKEN_TILE
    nt = seq // t
    row = lambda i: (i, 0)
    const = lambda i: (0, 0)
    half = pl.BlockSpec((t, 512), row)

    def residue_major(dil):
        return pl.BlockSpec((1, dil, t // dil, ATT_WIDTH), lambda i: (i // nt, 0, i % nt, 0))

    rm4, rm16 = residue_major(4), residue_major(16)
    return pl.pallas_call(
        _outproj_kernel,
        grid=(n // t,),
        in_specs=[
            half, half, pl.BlockSpec((t, 512), lambda i: (i, COL_HG)),
            half, rm4, rm16, half, rm4, rm16,
            pl.BlockSpec((t, D_MODEL), row),
            pl.BlockSpec((1, HG_DIM), const),
            pl.BlockSpec((D_MODEL, D_MODEL), const),
            pl.BlockSpec((1, D_MODEL), const),
            pl.BlockSpec((D_MODEL, LANES), const),
            pl.BlockSpec((D_MODEL, LANES), const),
            pl.BlockSpec((1, LANES), const),
        ],
        out_specs=[pl.BlockSpec((t, D_MODEL), row), pl.BlockSpec((t, D_MODEL), row),
                   pl.BlockSpec((t, LANES), row), pl.BlockSpec((t, LANES), row)],
        out_shape=[jax.ShapeDtypeStruct((n, D_MODEL), F32), jax.ShapeDtypeStruct((n, D_MODEL), F32),
                   jax.ShapeDtypeStruct((n, LANES), F32), jax.ShapeDtypeStruct((n, LANES), jnp.int32)],
        scratch_shapes=[pltpu.VMEM((ATT_WIDTH // LANES, t, LANES), F32)] * 4,
        compiler_params=pltpu.CompilerParams(
            dimension_semantics=("parallel",), vmem_limit_bytes=VMEM_LIMIT),
        name="outproj",
    )(o_f, o_b, proj, *atts, *lses, x2d, og, w_out_bf16, g2, wr_hi, wr_lo, br)


def _moe_kernel(bexp_ref, bfirst_ref, nused_ref, xs_ref, wu_ref, bg_ref, bl_ref, wd_ref, bd_ref,
                y_ref, wg_s, wl_s, wd_s):
    i = pl.program_id(0)

    @pl.when(bfirst_ref[i] == 1)
    def _():
        r = lax.broadcasted_iota(jnp.int32, (2 * LANES, 2 * LANES), 0)
        c = lax.broadcasted_iota(jnp.int32, (2 * LANES, 2 * LANES), 1)
        src = jnp.where(c < LANES, 2 * c, 2 * (c - LANES) + 1)
        perm = jnp.where(r == src, 1.0, 0.0).astype(BF16)
        rows = 256
        for rb in range(D_MODEL // rows):
            rs = slice(rb * rows, (rb + 1) * rows)
            for cb in range(D_EXPERT // LANES):
                w = wu_ref[0, rs, cb * 2 * LANES:(cb + 1) * 2 * LANES].astype(BF16)
                split = _dot(w, perm).astype(BF16)
                wg_s[rs, cb * LANES:(cb + 1) * LANES] = split[:, :LANES]
                wl_s[rs, cb * LANES:(cb + 1) * LANES] = split[:, LANES:]
        wd_s[...] = wd_ref[0].astype(BF16)

    @pl.when(i < nused_ref[0])
    def _():
        x = xs_ref[...].astype(BF16)
        hglu = _dot(x, wg_s[...]) + bg_ref[0]
        hlin = _dot(x, wl_s[...]) + bl_ref[0]
        glu = jnp.minimum(hglu, SWIGLU_LIMIT)
        lin = jnp.clip(hlin, -SWIGLU_LIMIT, SWIGLU_LIMIT)
        act = glu * _sigmoid(SWIGLU_ALPHA * glu) * (lin + 1.0)
        y_ref[...] = (_dot(act.astype(BF16), wd_s[...]) + bd_ref[0]).astype(y_ref.dtype)

    @pl.when(i >= nused_ref[0])
    def _():
        y_ref[...] = jnp.zeros_like(y_ref)


def _moe(block_exp, block_first, n_used, xs, w_up, b_glu, b_lin, w_down, b_down):
    p_rows = xs.shape[0]
    bm = MOE_BLOCK_ROWS
    nb = p_rows // bm
    exp3 = lambda i, be, bf, nu: (be[i], 0, 0)
    grid_spec = pltpu.PrefetchScalarGridSpec(
        num_scalar_prefetch=3,
        grid=(nb,),
        in_specs=[
            pl.BlockSpec((bm, D_MODEL), lambda i, be, bf, nu: (jnp.minimum(i, nu[0] - 1), 0)),
            pl.BlockSpec((1, D_MODEL, 2 * D_EXPERT), exp3),
            pl.BlockSpec((1, 1, D_EXPERT), exp3),
            pl.BlockSpec((1, 1, D_EXPERT), exp3),
            pl.BlockSpec((1, D_EXPERT, D_MODEL), exp3),
            pl.BlockSpec((1, 1, D_MODEL), exp3),
        ],
        out_specs=pl.BlockSpec((bm, D_MODEL), lambda i, be, bf, nu: (i, 0)),
        scratch_shapes=[pltpu.VMEM((D_MODEL, D_EXPERT), BF16), pltpu.VMEM((D_MODEL, D_EXPERT), BF16),
                        pltpu.VMEM((D_EXPERT, D_MODEL), BF16)],
    )
    return pl.pallas_call(
        _moe_kernel,
        grid_spec=grid_spec,
        out_shape=jax.ShapeDtypeStruct((p_rows, D_MODEL), F32),
        compiler_params=pltpu.CompilerParams(
            dimension_semantics=("arbitrary",), vmem_limit_bytes=VMEM_LIMIT),
        name="moe",
    )(block_exp, block_first, n_used, xs, w_up, b_glu, b_lin, w_down, b_down)


def _dispatch_kernel(dest_ref, zstart_ref, xn_hbm, xs_hbm, zero_ref, sem, zsem):
    i = pl.program_id(0)
    bm = MOE_BLOCK_ROWS

    def zero_copy(e):
        start = pl.multiple_of(zstart_ref[e], bm)
        return pltpu.make_async_copy(zero_ref, xs_hbm.at[pl.ds(start, bm)], zsem)

    @pl.when(i == 0)
    def _():
        zero_ref[...] = jnp.zeros_like(zero_ref)
        for e in range(N_EXPERTS):
            @pl.when(zstart_ref[e] >= 0)
            def _():
                zero_copy(e).start()
        for e in range(N_EXPERTS):
            @pl.when(zstart_ref[e] >= 0)
            def _():
                zero_copy(e).wait()

    t0 = i * DISPATCH_TOKENS

    def body(j, carry):
        t = t0 + j
        for k in range(TOP_K):
            d = dest_ref[t * TOP_K + k]
            pltpu.make_async_copy(xn_hbm.at[pl.ds(t, 1)], xs_hbm.at[pl.ds(d, 1)], sem).start()
        return carry

    lax.fori_loop(0, DISPATCH_TOKENS, body, 0, unroll=8)
    rows = DISPATCH_TOKENS * TOP_K
    pltpu.make_async_copy(xn_hbm.at[pl.ds(0, rows)], xs_hbm.at[pl.ds(0, rows)], sem).wait()


def _dispatch(dest, zstart, xn, p_rows):
    n = xn.shape[0]
    grid_spec = pltpu.PrefetchScalarGridSpec(
        num_scalar_prefetch=2,
        grid=(n // DISPATCH_TOKENS,),
        in_specs=[pl.BlockSpec(memory_space=pl.ANY)],
        out_specs=pl.BlockSpec(memory_space=pl.ANY),
        scratch_shapes=[pltpu.VMEM((MOE_BLOCK_ROWS, D_MODEL), F32),
                        pltpu.SemaphoreType.DMA, pltpu.SemaphoreType.DMA],
    )
    return pl.pallas_call(
        _dispatch_kernel,
        grid_spec=grid_spec,
        out_shape=jax.ShapeDtypeStruct((p_rows, D_MODEL), F32),
        compiler_params=pltpu.CompilerParams(
            dimension_semantics=("arbitrary",), vmem_limit_bytes=VMEM_LIMIT),
        name="dispatch",
    )(dest, zstart, xn)


def _combine_kernel(dest_ref, y_hbm, x2_ref, gate_ref, out_ref, buf_ref, sems):
    i = pl.program_id(0)
    tc = COMBINE_TOKENS

    def issue(step, slot):
        t0 = step * tc

        def body(j, carry):
            for k in range(TOP_K):
                d = dest_ref[(t0 + j) * TOP_K + k]
                pltpu.make_async_copy(y_hbm.at[pl.ds(d, 1)], buf_ref.at[slot, pl.ds(k * tc + j, 1)],
                                      sems.at[slot]).start()
            return carry

        lax.fori_loop(0, tc, body, 0, unroll=8)

    @pl.when(i == 0)
    def _():
        issue(0, 0)

    @pl.when(i + 1 < pl.num_programs(0))
    def _():
        issue(i + 1, (i + 1) % 2)

    slot = i % 2
    pltpu.make_async_copy(y_hbm.at[pl.ds(0, TOP_K * tc)], buf_ref.at[slot], sems.at[slot]).wait()
    acc = x2_ref[...]
    gate = gate_ref[...]
    for k in range(TOP_K):
        acc = acc + gate[:, k:k + 1] * buf_ref[slot, k * tc:(k + 1) * tc, :]
    out_ref[...] = acc


def _combine(dest, y, x2, gates):
    n = x2.shape[0]
    tc = COMBINE_TOKENS
    grid_spec = pltpu.PrefetchScalarGridSpec(
        num_scalar_prefetch=1,
        grid=(n // tc,),
        in_specs=[pl.BlockSpec(memory_space=pl.ANY),
                  pl.BlockSpec((tc, D_MODEL), lambda i, d: (i, 0)),
                  pl.BlockSpec((tc, LANES), lambda i, d: (i, 0))],
        out_specs=pl.BlockSpec((tc, D_MODEL), lambda i, d: (i, 0)),
        scratch_shapes=[pltpu.VMEM((2, TOP_K * tc, D_MODEL), F32), pltpu.SemaphoreType.DMA((2,))],
    )
    return pl.pallas_call(
        _combine_kernel,
        grid_spec=grid_spec,
        out_shape=jax.ShapeDtypeStruct((n, D_MODEL), F32),
        compiler_params=pltpu.CompilerParams(
            dimension_semantics=("arbitrary",), vmem_limit_bytes=VMEM_LIMIT),
        name="combine",
    )(dest, y, x2, gates)


def _route(top_idx):
    n = top_idx.shape[0]
    a = n * TOP_K
    bm = MOE_BLOCK_ROWS
    nb = a // bm + N_EXPERTS
    e_flat = top_idx.reshape(a)
    onehot = (e_flat[:, None] == jnp.arange(N_EXPERTS, dtype=jnp.int32)[None, :]).astype(jnp.int32)
    csum = jnp.cumsum(onehot, axis=0)
    counts = csum[-1]
    padded = ((counts + bm - 1) // bm) * bm
    pad_end = jnp.cumsum(padded)
    pad_start = pad_end - padded
    dest = jnp.sum(onehot * (csum - 1 + pad_start[None, :]), axis=1)
    n_used = (pad_end[-1] // bm).astype(jnp.int32)
    blk = jnp.arange(nb, dtype=jnp.int32)
    bexp = jnp.sum((pad_end[None, :] <= (blk * bm)[:, None]).astype(jnp.int32), axis=1)
    bexp = jnp.minimum(bexp, N_EXPERTS - 1)
    bexp = jnp.where(blk < n_used, bexp, bexp[jnp.maximum(n_used - 1, 0)])
    bfirst = jnp.concatenate([jnp.ones((1,), jnp.int32), (bexp[1:] != bexp[:-1]).astype(jnp.int32)])
    zstart = jnp.where(counts > 0, pad_end - bm, -1).astype(jnp.int32)
    return dest.astype(jnp.int32), zstart, bexp, bfirst, n_used.reshape(1)


def kernel(x, positions, norm1_g, w_in, q_norm_g, k_norm_g, hgrn_lower_bounds, hgrn_onorm_g,
           w_out, norm2_g, w_router, b_router, w_up, b_up, w_down, b_down):
    batch, seq, d = x.shape
    n = batch * seq
    depth = norm1_g.shape[0]
    lbs_all = jnp.cumsum(jax.nn.softmax(hgrn_lower_bounds.astype(F32), axis=0), axis=0)
    half = ATT_DIM // 2
    inv = 1.0 / (ROPE_THETA ** (jnp.arange(half, dtype=F32) / half))
    inv_tab = jnp.tile(inv, LANES // half).reshape(1, LANES)
    pos_col = positions.reshape(n, 1)

    x2d = x.reshape(n, d)
    for l in range(depth):
        lbs = lbs_all[l].reshape(2 * HG_HEADS, 1, HG_DIM)
        proj, qkv4, qkv16 = _inproj(
            x2d, pos_col, inv_tab, norm1_g[l].reshape(1, d), w_in[l].astype(BF16),
            jnp.tile(q_norm_g[l], LANES // ATT_DIM).reshape(1, LANES),
            jnp.tile(k_norm_g[l], LANES // ATT_DIM).reshape(1, LANES), batch, seq)
        o_f, o_b = _hgrn(proj, lbs, batch, seq)
        o1, l1 = _attention(proj.reshape(batch, 1, seq, IN_COLS), COL_AQ)
        o4, l4 = _attention(qkv4, 0)
        o16, l16 = _attention(qkv16, 0)
        atts = [o1.reshape(n, ATT_WIDTH), o4, o16]
        lses = [l1.reshape(n, ATT_WIDTH), l4, l16]

        wr = jnp.pad(w_router[l], ((0, 0), (0, LANES - N_EXPERTS)))
        wr_hi = wr.astype(BF16)
        wr_lo = (wr - wr_hi.astype(F32)).astype(BF16)
        br = jnp.pad(b_router[l], (0, LANES - N_EXPERTS)).reshape(1, LANES)
        x2, xn, gates, top_idx = _outproj(
            o_f, o_b, proj, atts, lses, x2d, hgrn_onorm_g[l].reshape(1, HG_DIM),
            w_out[l].astype(BF16), norm2_g[l].reshape(1, d), wr_hi, wr_lo, br, seq)
        dest, zstart, bexp, bfirst, n_used = _route(top_idx[:, :TOP_K])
        xs = _dispatch(dest, zstart, xn, bexp.shape[0] * MOE_BLOCK_ROWS)
        y = _moe(bexp, bfirst, n_used, xs, w_up[l],
                 b_up[l][:, 0::2].reshape(N_EXPERTS, 1, D_EXPERT),
                 b_up[l][:, 1::2].reshape(N_EXPERTS, 1, D_EXPERT),
                 w_down[l], b_down[l].reshape(N_EXPERTS, 1, D_MODEL))
        x2d = _combine(dest, y, x2, gates)
    return x2d.reshape(batch, seq, d)
```

```python
import functools

import jax
import jax.numpy as jnp
from jax import lax
from jax.experimental import pallas as pl
from jax.experimental.pallas import tpu as pltpu

F32 = jnp.float32
BF16 = jnp.bfloat16

D_MODEL = 1024
HG_HEADS = 4
HG_DIM = 128
HG_WIDTH = HG_HEADS * HG_DIM
HG_CHUNK = 64
ATT_HEADS = 8
ATT_DIM = 64
ATT_WIDTH = ATT_HEADS * ATT_DIM
DILATED_PATTERNS = ((128, 1), (512, 4), (2048, 16))
ATT_HALF = 64
ATT_QBLOCK = 128
ROPE_THETA = 10000.0
IN_COLS = 5 * HG_WIDTH + 3 * ATT_WIDTH
N_EXPERTS = 32
TOP_K = 4
D_EXPERT = D_MODEL
SWIGLU_LIMIT = 7.0
SWIGLU_ALPHA = 1.702
EPS = 1e-6
NEG = -1e30

COL_HQ, COL_HF_FWD, COL_HF_BWD, COL_HI, COL_HG, COL_AQ, COL_AK, COL_AV = range(8)

TOKEN_TILE = 512
MOE_BLOCK_ROWS = 512
DISPATCH_TOKENS = 512
COMBINE_TOKENS = 256
LANES = 128
VMEM_LIMIT = 56 * 1024 * 1024


def _dot(a, b):
    return jnp.dot(a, b, preferred_element_type=F32)


def _dot_nt(a, b):
    return lax.dot_general(a, b, (((1,), (1,)), ((), ())), preferred_element_type=F32)


def _dot_tn(a, b):
    return lax.dot_general(a, b, (((0,), (0,)), ((), ())), preferred_element_type=F32)


def _sigmoid(x):
    return 1.0 / (1.0 + jnp.exp(-x))


def _head_norm_rope(p, gain, cos, sin_signed, scale):
    lane = lax.broadcasted_iota(jnp.int32, (p.shape[0], LANES), 1)
    low = lane < ATT_DIM
    first_half = (lane % ATT_DIM) < (ATT_DIM // 2)
    outs = []
    for t in range(ATT_WIDTH // LANES):
        blk = p[:, t * LANES:(t + 1) * LANES]
        sq = blk * blk
        s_low = jnp.sum(jnp.where(low, sq, 0.0), axis=-1, keepdims=True)
        s_high = jnp.sum(jnp.where(low, 0.0, sq), axis=-1, keepdims=True)
        r = jnp.where(low, lax.rsqrt(s_low * (1.0 / ATT_DIM) + EPS),
                      lax.rsqrt(s_high * (1.0 / ATT_DIM) + EPS))
        y = blk * r * gain
        partner = jnp.where(first_half, pltpu.roll(y, LANES - ATT_DIM // 2, axis=1),
                            pltpu.roll(y, ATT_DIM // 2, axis=1))
        outs.append((y * cos + partner * sin_signed) * scale)
    return jnp.concatenate(outs, axis=1)


def _inproj_kernel(x_ref, pos_ref, inv_ref, g1_ref, w_ref, qg_ref, kg_ref,
                   out_ref, d4_ref, d16_ref, stage_ref):
    x = x_ref[...]
    ms = jnp.mean(x * x, axis=-1, keepdims=True)
    h = (x * lax.rsqrt(ms + EPS) * g1_ref[...]).astype(BF16)
    ang = pos_ref[...].astype(F32) * inv_ref[...]
    lane = lax.broadcasted_iota(jnp.int32, ang.shape, 1)
    cos = jnp.cos(ang)
    sin_signed = jnp.where((lane % ATT_DIM) < (ATT_DIM // 2), -jnp.sin(ang), jnp.sin(ang))
    for j in range(IN_COLS // 512):
        p = _dot(h, w_ref[:, j * 512:(j + 1) * 512])
        if j == COL_AQ:
            p = _head_norm_rope(p, qg_ref[...], cos, sin_signed, ATT_DIM ** -0.5)
        elif j == COL_AK:
            p = _head_norm_rope(p, kg_ref[...], cos, sin_signed, 1.0)
        out_ref[:, j * 512:(j + 1) * 512] = p.astype(BF16)
        if j >= COL_AQ:
            for c in range(ATT_WIDTH // LANES):
                stage_ref[c] = p[:, c * LANES:(c + 1) * LANES]
            c0 = (j - COL_AQ) * ATT_WIDTH
            for dil, ref in ((4, d4_ref), (16, d16_ref)):
                rows = x.shape[0] // dil
                for r in range(dil):
                    for c in range(ATT_WIDTH // LANES):
                        ref[0, r, :, c0 + c * LANES:c0 + (c + 1) * LANES] = (
                            stage_ref[c, pl.ds(r, rows, stride=dil), :].astype(BF16))


def _inproj(x2d, pos_col, inv_tab, g1, w_in_bf16, qg, kg, batch, seq):
    n = x2d.shape[0]
    t = TOKEN_TILE
    nt = seq // t
    const = lambda i: (0, 0)
    qkv = 3 * ATT_WIDTH

    def residue_major(dil):
        spec = pl.BlockSpec((1, dil, t // dil, qkv), lambda i: (i // nt, 0, i % nt, 0))
        return spec, jax.ShapeDtypeStruct((batch, dil, seq // dil, qkv), BF16)

    spec4, shape4 = residue_major(4)
    spec16, shape16 = residue_major(16)
    return pl.pallas_call(
        _inproj_kernel,
        grid=(n // t,),
        in_specs=[
            pl.BlockSpec((t, D_MODEL), lambda i: (i, 0)),
            pl.BlockSpec((t, 1), lambda i: (i, 0)),
            pl.BlockSpec((1, LANES), const),
            pl.BlockSpec((1, D_MODEL), const),
            pl.BlockSpec((D_MODEL, IN_COLS), const),
            pl.BlockSpec((1, LANES), const),
            pl.BlockSpec((1, LANES), const),
        ],
        out_specs=[pl.BlockSpec((t, IN_COLS), lambda i: (i, 0)), spec4, spec16],
        out_shape=[jax.ShapeDtypeStruct((n, IN_COLS), BF16), shape4, shape16],
        scratch_shapes=[pltpu.VMEM((ATT_WIDTH // LANES, t, LANES), F32)],
        compiler_params=pltpu.CompilerParams(
            dimension_semantics=("parallel",), vmem_limit_bytes=VMEM_LIMIT),
        name="inproj",
    )(x2d, pos_col, inv_tab, g1, w_in_bf16, qg, kg)


def _hgrn_chunk(q_raw, z, v, lb, tri, mask, state_t, last_row):
    z = z.astype(F32)
    q = q_raw.astype(F32)
    sg = _sigmoid(z)
    f = lb + (1.0 - lb) * sg
    k = (1.0 - lb) * (1.0 - sg)
    lf = jnp.log(f)
    lf_hi = lf.astype(BF16)
    lf_lo = (lf - lf_hi.astype(F32)).astype(BF16)
    b = _dot(tri, lf_hi) + _dot(tri, lf_lo)
    b_last = b[last_row:last_row + 1, :]
    qt = (q * _sigmoid(q) * jnp.exp(b)).astype(BF16)
    kt = (k * jnp.exp(-b)).astype(BF16)
    kd = (k * jnp.exp(b_last - b)).astype(BF16)
    a = jnp.where(mask, _dot_nt(qt, kt), 0.0)
    o = _dot(a.astype(BF16), v) + _dot_nt(qt, state_t.astype(BF16))
    new_state = state_t * jnp.exp(b_last) + _dot_tn(v, kd)
    return o, new_state


def _hgrn_kernel(qf_ref, zf_ref, vf_ref, qb_ref, zb_ref, vb_ref, lbf_ref, lbb_ref,
                 of_ref, ob_ref, sf_ref, sb_ref):
    @pl.when(pl.program_id(2) == 0)
    def _():
        sf_ref[...] = jnp.zeros_like(sf_ref)
        sb_ref[...] = jnp.zeros_like(sb_ref)

    c = HG_CHUNK
    row = lax.broadcasted_iota(jnp.int32, (c, c), 0)
    col = lax.broadcasted_iota(jnp.int32, (c, c), 1)
    lower = row >= col
    upper = row <= col
    tri_f = jnp.where(lower, 1.0, 0.0).astype(BF16)
    tri_b = jnp.where(upper, 1.0, 0.0).astype(BF16)
    lbf = lbf_ref[0]
    lbb = lbb_ref[0]
    n_chunks = qf_ref.shape[0] // c

    sf = sf_ref[...]
    sb = sb_ref[...]
    for j in range(n_chunks):
        rf = slice(j * c, (j + 1) * c)
        o, sf = _hgrn_chunk(qf_ref[rf, :], zf_ref[rf, :], vf_ref[rf, :], lbf, tri_f, lower, sf, c - 1)
        of_ref[rf, :] = o.astype(of_ref.dtype)
        jb = n_chunks - 1 - j
        rb = slice(jb * c, (jb + 1) * c)
        o, sb = _hgrn_chunk(qb_ref[rb, :], zb_ref[rb, :], vb_ref[rb, :], lbb, tri_b, upper, sb, 0)
        ob_ref[rb, :] = o.astype(ob_ref.dtype)
    sf_ref[...] = sf
    sb_ref[...] = sb


def _hgrn(proj, lbs, batch, seq):
    n = proj.shape[0]
    t = TOKEN_TILE
    nblk = seq // t

    def fwd(colblk):
        return pl.BlockSpec((t, HG_DIM), lambda b, h, i: (b * nblk + i, colblk * HG_HEADS + h))

    def bwd(colblk):
        return pl.BlockSpec((t, HG_DIM), lambda b, h, i: (b * nblk + nblk - 1 - i, colblk * HG_HEADS + h))

    out_f = pl.BlockSpec((t, HG_DIM), lambda b, h, i: (b * nblk + i, h))
    out_b = pl.BlockSpec((t, HG_DIM), lambda b, h, i: (b * nblk + nblk - 1 - i, h))
    return pl.pallas_call(
        _hgrn_kernel,
        grid=(batch, HG_HEADS, nblk),
        in_specs=[
            fwd(COL_HQ), fwd(COL_HF_FWD), fwd(COL_HI),
            bwd(COL_HQ), bwd(COL_HF_BWD), bwd(COL_HI),
            pl.BlockSpec((1, 1, HG_DIM), lambda b, h, i: (h, 0, 0)),
            pl.BlockSpec((1, 1, HG_DIM), lambda b, h, i: (HG_HEADS + h, 0, 0)),
        ],
        out_specs=[out_f, out_b],
        out_shape=[jax.ShapeDtypeStruct((n, HG_WIDTH), BF16)] * 2,
        scratch_shapes=[pltpu.VMEM((HG_DIM, HG_DIM), F32)] * 2,
        compiler_params=pltpu.CompilerParams(
            dimension_semantics=("parallel", "parallel", "arbitrary"), vmem_limit_bytes=VMEM_LIMIT),
        name="hgrn",
    )(proj, proj, proj, proj, proj, proj, lbs, lbs)


def _attn_kernel(q_ref, kc_ref, kp_ref, kn_ref, vc_ref, vp_ref, vn_ref, o_ref, l_ref,
                 kw_ref, vw_ref, *, tq, length):
    n = pl.program_id(2)
    half = ATT_HALF
    kw_ref[0:half, :] = kp_ref[...]
    kw_ref[half:half + tq, :] = kc_ref[...]
    kw_ref[half + tq:, :] = kn_ref[...]
    vw_ref[0:half, :] = vp_ref[...]
    vw_ref[half:half + tq, :] = vc_ref[...]
    vw_ref[half + tq:, :] = vn_ref[...]

    qb_rows = ATT_QBLOCK
    win = qb_rows + 2 * half
    i_idx = lax.broadcasted_iota(jnp.int32, (qb_rows, win), 0)
    j_idx = lax.broadcasted_iota(jnp.int32, (qb_rows, win), 1)
    band = (j_idx >= i_idx) & (j_idx <= i_idx + 2 * half)

    def body(qb, carry):
        r0 = pl.multiple_of(qb * qb_rows, qb_rows)
        base = n * tq + r0 - half
        valid = band & (j_idx >= -base) & (j_idx < length - base)
        q = q_ref[pl.ds(r0, qb_rows), :]
        kw = kw_ref[pl.ds(r0, win), :]
        vw = vw_ref[pl.ds(r0, win), :]
        outs, lses = [], []
        for h in range(ATT_HEADS):
            cs = slice(h * ATT_DIM, (h + 1) * ATT_DIM)
            s = jnp.where(valid, _dot_nt(q[:, cs], kw[:, cs]), NEG)
            m = jnp.max(s, axis=-1, keepdims=True)
            p = jnp.exp(s - m)
            den = jnp.sum(p, axis=-1, keepdims=True)
            outs.append(_dot(p.astype(BF16), vw[:, cs]) / den)
            lses.append(jnp.broadcast_to(m + jnp.log(den), (qb_rows, ATT_DIM)))
        o_ref[pl.ds(r0, qb_rows), :] = jnp.concatenate(outs, axis=1).astype(o_ref.dtype)
        l_ref[pl.ds(r0, qb_rows), :] = jnp.concatenate(lses, axis=1)
        return carry

    lax.fori_loop(0, tq // qb_rows, body, 0)


def _attention(qkv, col0):
    batch, dil, length, _ = qkv.shape
    tq = min(TOKEN_TILE, length)
    nq = length // tq
    hb = tq // ATT_HALF
    n_hblk = length // ATT_HALF

    def cur(col):
        return pl.BlockSpec((None, None, tq, ATT_WIDTH), lambda b, r, n: (b, r, n, col))

    def prev(col):
        return pl.BlockSpec((None, None, ATT_HALF, ATT_WIDTH),
                            lambda b, r, n: (b, r, jnp.maximum(n * hb - 1, 0), col))

    def nxt(col):
        return pl.BlockSpec((None, None, ATT_HALF, ATT_WIDTH),
                            lambda b, r, n: (b, r, jnp.minimum((n + 1) * hb, n_hblk - 1), col))

    out_spec = pl.BlockSpec((None, None, tq, ATT_WIDTH), lambda b, r, n: (b, r, n, 0))
    return pl.pallas_call(
        functools.partial(_attn_kernel, tq=tq, length=length),
        grid=(batch, dil, nq),
        in_specs=[cur(col0), cur(col0 + 1), prev(col0 + 1), nxt(col0 + 1),
                  cur(col0 + 2), prev(col0 + 2), nxt(col0 + 2)],
        out_specs=[out_spec, out_spec],
        out_shape=[jax.ShapeDtypeStruct((batch, dil, length, ATT_WIDTH), BF16),
                   jax.ShapeDtypeStruct((batch, dil, length, ATT_WIDTH), F32)],
        scratch_shapes=[pltpu.VMEM((tq + 2 * ATT_HALF, ATT_WIDTH), BF16)] * 2,
        compiler_params=pltpu.CompilerParams(
            dimension_semantics=("parallel", "parallel", "parallel"), vmem_limit_bytes=VMEM_LIMIT),
        name=f"attn_d{dil}",
    )(qkv, qkv, qkv, qkv, qkv, qkv, qkv)


def _token_major(src_ref, stage_ref):
    dil, rows = src_ref.shape[1], src_ref.shape[2]
    nc = src_ref.shape[3] // LANES
    for r in range(dil):
        for c in range(nc):
            stage_ref[c, pl.ds(r, rows, stride=dil), :] = (
                src_ref[0, r, :, c * LANES:(c + 1) * LANES].astype(F32))
    return jnp.concatenate([stage_ref[c] for c in range(nc)], axis=1)


def _outproj_kernel(of_ref, ob_ref, hg_ref, o1_ref, o2_ref, o3_ref, l1_ref, l2_ref, l3_ref,
                    x_ref, og_ref, w_ref, g2_ref, wrh_ref, wrl_ref, br_ref,
                    x2_ref, xn_ref, gate_ref, idx_ref, st_o2, st_o3, st_l2, st_l3):
    o = of_ref[...].astype(F32) + ob_ref[...].astype(F32)
    hg = hg_ref[...].astype(F32)
    parts = []
    for h in range(HG_HEADS):
        blk = o[:, h * HG_DIM:(h + 1) * HG_DIM]
        ms = jnp.mean(blk * blk, axis=-1, keepdims=True)
        parts.append(blk * lax.rsqrt(ms + EPS) * og_ref[...])
    o_hg = jnp.concatenate(parts, axis=1) * (hg * _sigmoid(hg))

    l1 = l1_ref[...]
    l2 = _token_major(l2_ref, st_l2)
    l3 = _token_major(l3_ref, st_l3)
    mx = jnp.maximum(jnp.maximum(l1, l2), l3)
    e1, e2, e3 = jnp.exp(l1 - mx), jnp.exp(l2 - mx), jnp.exp(l3 - mx)
    o_att = (e1 * o1_ref[...].astype(F32) + e2 * _token_major(o2_ref, st_o2)
             + e3 * _token_major(o3_ref, st_o3)) / (e1 + e2 + e3)

    y = _dot(o_hg.astype(BF16), w_ref[0:HG_WIDTH, :]) + _dot(o_att.astype(BF16), w_ref[HG_WIDTH:, :])
    x2 = x_ref[...] + y
    x2_ref[...] = x2

    ms = jnp.mean(x2 * x2, axis=-1, keepdims=True)
    xn = x2 * lax.rsqrt(ms + EPS) * g2_ref[...]
    xn_ref[...] = xn
    xn_hi = xn.astype(BF16)
    xn_lo = (xn - xn_hi.astype(F32)).astype(BF16)
    logits = (_dot(xn_hi, wrh_ref[...]) + _dot(xn_lo, wrh_ref[...]) + _dot(xn_hi, wrl_ref[...])
              + br_ref[...])

    lane = lax.broadcasted_iota(jnp.int32, logits.shape, 1)
    lane_f = lane.astype(F32)
    work = jnp.where(lane < N_EXPERTS, logits, -jnp.inf)
    vals, idxs = [], []
    for _ in range(TOP_K):
        m = jnp.max(work, axis=-1, keepdims=True)
        idx = jnp.min(jnp.where(work == m, lane_f, float(LANES)), axis=-1, keepdims=True)
        vals.append(m)
        idxs.append(idx)
        work = jnp.where(lane_f == idx, -jnp.inf, work)
    es = [jnp.exp(v - vals[0]) for v in vals]
    den = es[0] + es[1] + es[2] + es[3]
    gate_out = jnp.zeros(logits.shape, F32)
    idx_out = jnp.zeros(logits.shape, F32)
    for k in range(TOP_K):
        gate_out = jnp.where(lane == k, es[k] / den, gate_out)
        idx_out = jnp.where(lane == k, idxs[k], idx_out)
    gate_ref[...] = gate_out
    idx_ref[...] = idx_out.astype(jnp.int32)


def _outproj(o_f, o_b, proj, atts, lses, x2d, og, w_out_bf16, g2, wr_hi, wr_lo, br, seq):
    n = x2d.shape[0]
    t = TOKEN_TILE
    nt = seq // t
    row = lambda i: (i, 0)
    const = lambda i: (0, 0)
    half = pl.BlockSpec((t, 512), row)

    def residue_major(dil):
        return pl.BlockSpec((1, dil, t // dil, ATT_WIDTH), lambda i: (i // nt, 0, i % nt, 0))

    rm4, rm16 = residue_major(4), residue_major(16)
    return pl.pallas_call(
        _outproj_kernel,
        grid=(n // t,),
        in_specs=[
            half, half, pl.BlockSpec((t, 512), lambda i: (i, COL_HG)),
            half, rm4, rm16, half, rm4, rm16,
            pl.BlockSpec((t, D_MODEL), row),
            pl.BlockSpec((1, HG_DIM), const),
            pl.BlockSpec((D_MODEL, D_MODEL), const),
            pl.BlockSpec((1, D_MODEL), const),
            pl.BlockSpec((D_MODEL, LANES), const),
            pl.BlockSpec((D_MODEL, LANES), const),
            pl.BlockSpec((1, LANES), const),
        ],
        out_specs=[pl.BlockSpec((t, D_MODEL), row), pl.BlockSpec((t, D_MODEL), row),
                   pl.BlockSpec((t, LANES), row), pl.BlockSpec((t, LANES), row)],
        out_shape=[jax.ShapeDtypeStruct((n, D_MODEL), F32), jax.ShapeDtypeStruct((n, D_MODEL), F32),
                   jax.ShapeDtypeStruct((n, LANES), F32), jax.ShapeDtypeStruct((n, LANES), jnp.int32)],
        scratch_shapes=[pltpu.VMEM((ATT_WIDTH // LANES, t, LANES), F32)] * 4,
        compiler_params=pltpu.CompilerParams(
            dimension_semantics=("parallel",), vmem_limit_bytes=VMEM_LIMIT),
        name="outproj",
    )(o_f, o_b, proj, *atts, *lses, x2d, og, w_out_bf16, g2, wr_hi, wr_lo, br)


def _moe_kernel(bexp_ref, bfirst_ref, nused_ref, xs_ref, wu_ref, bg_ref, bl_ref, wd_ref, bd_ref,
                y_ref, wg_s, wl_s, wd_s):
    i = pl.program_id(0)

    @pl.when(bfirst_ref[i] == 1)
    def _():
        r = lax.broadcasted_iota(jnp.int32, (2 * LANES, 2 * LANES), 0)
        c = lax.broadcasted_iota(jnp.int32, (2 * LANES, 2 * LANES), 1)
        src = jnp.where(c < LANES, 2 * c, 2 * (c - LANES) + 1)
        perm = jnp.where(r == src, 1.0, 0.0).astype(BF16)
        rows = 256
        for rb in range(D_MODEL // rows):
            rs = slice(rb * rows, (rb + 1) * rows)
            for cb in range(D_EXPERT // LANES):
                w = wu_ref[0, rs, cb * 2 * LANES:(cb + 1) * 2 * LANES].astype(BF16)
                split = _dot(w, perm).astype(BF16)
                wg_s[rs, cb * LANES:(cb + 1) * LANES] = split[:, :LANES]
                wl_s[rs, cb * LANES:(cb + 1) * LANES] = split[:, LANES:]
        wd_s[...] = wd_ref[0].astype(BF16)

    @pl.when(i < nused_ref[0])
    def _():
        x = xs_ref[...].astype(BF16)
        hglu = _dot(x, wg_s[...]) + bg_ref[0]
        hlin = _dot(x, wl_s[...]) + bl_ref[0]
        glu = jnp.minimum(hglu, SWIGLU_LIMIT)
        lin = jnp.clip(hlin, -SWIGLU_LIMIT, SWIGLU_LIMIT)
        act = glu * _sigmoid(SWIGLU_ALPHA * glu) * (lin + 1.0)
        y_ref[...] = (_dot(act.astype(BF16), wd_s[...]) + bd_ref[0]).astype(y_ref.dtype)

    @pl.when(i >= nused_ref[0])
    def _():
        y_ref[...] = jnp.zeros_like(y_ref)


def _moe(block_exp, block_first, n_used, xs, w_up, b_glu, b_lin, w_down, b_down):
    p_rows = xs.shape[0]
    bm = MOE_BLOCK_ROWS
    nb = p_rows // bm
    exp3 = lambda i, be, bf, nu: (be[i], 0, 0)
    grid_spec = pltpu.PrefetchScalarGridSpec(
        num_scalar_prefetch=3,
        grid=(nb,),
        in_specs=[
            pl.BlockSpec((bm, D_MODEL), lambda i, be, bf, nu: (jnp.minimum(i, nu[0] - 1), 0)),
            pl.BlockSpec((1, D_MODEL, 2 * D_EXPERT), exp3),
            pl.BlockSpec((1, 1, D_EXPERT), exp3),
            pl.BlockSpec((1, 1, D_EXPERT), exp3),
            pl.BlockSpec((1, D_EXPERT, D_MODEL), exp3),
            pl.BlockSpec((1, 1, D_MODEL), exp3),
        ],
        out_specs=pl.BlockSpec((bm, D_MODEL), lambda i, be, bf, nu: (i, 0)),
        scratch_shapes=[pltpu.VMEM((D_MODEL, D_EXPERT), BF16), pltpu.VMEM((D_MODEL, D_EXPERT), BF16),
                        pltpu.VMEM((D_EXPERT, D_MODEL), BF16)],
    )
    return pl.pallas_call(
        _moe_kernel,
        grid_spec=grid_spec,
        out_shape=jax.ShapeDtypeStruct((p_rows, D_MODEL), F32),
        compiler_params=pltpu.CompilerParams(
            dimension_semantics=("arbitrary",), vmem_limit_bytes=VMEM_LIMIT),
        name="moe",
    )(block_exp, block_first, n_used, xs, w_up, b_glu, b_lin, w_down, b_down)


def _dispatch_kernel(dest_ref, zstart_ref, xn_ref, xs_hbm, zero_ref, sem, zsem):
    i = pl.program_id(0)
    bm = MOE_BLOCK_ROWS
    tokens = xn_ref.shape[0]

    def zero_copy(e):
        start = pl.multiple_of(zstart_ref[e], bm)
        return pltpu.make_async_copy(zero_ref, xs_hbm.at[pl.ds(start, bm)], zsem)

    @pl.when(i == 0)
    def _():
        zero_ref[...] = jnp.zeros_like(zero_ref)
        for e in range(N_EXPERTS):
            @pl.when(zstart_ref[e] >= 0)
            def _():
                zero_copy(e).start()
        for e in range(N_EXPERTS):
            @pl.when(zstart_ref[e] >= 0)
            def _():
                zero_copy(e).wait()

    t0 = i * tokens

    def body(j, carry):
        for k in range(TOP_K):
            d = dest_ref[(t0 + j) * TOP_K + k]
            pltpu.make_async_copy(xn_ref.at[pl.ds(j, 1)], xs_hbm.at[pl.ds(d, 1)], sem).start()
        return carry

    lax.fori_loop(0, tokens, body, 0, unroll=8)
    for k in range(TOP_K):
        pltpu.make_async_copy(xn_ref, xs_hbm.at[pl.ds(0, tokens)], sem).wait()


def _dispatch(dest, zstart, xn, p_rows):
    n = xn.shape[0]
    t = DISPATCH_TOKENS
    grid_spec = pltpu.PrefetchScalarGridSpec(
        num_scalar_prefetch=2,
        grid=(n // t,),
        in_specs=[pl.BlockSpec((t, D_MODEL), lambda i, d, z: (i, 0))],
        out_specs=pl.BlockSpec(memory_space=pl.ANY),
        scratch_shapes=[pltpu.VMEM((MOE_BLOCK_ROWS, D_MODEL), F32),
                        pltpu.SemaphoreType.DMA, pltpu.SemaphoreType.DMA],
    )
    return pl.pallas_call(
        _dispatch_kernel,
        grid_spec=grid_spec,
        out_shape=jax.ShapeDtypeStruct((p_rows, D_MODEL), F32),
        compiler_params=pltpu.CompilerParams(
            dimension_semantics=("arbitrary",), vmem_limit_bytes=VMEM_LIMIT),
        name="dispatch",
    )(dest, zstart, xn)


def _combine_kernel(dest_ref, y_hbm, x2_ref, gate_ref, out_ref, buf_ref, sems):
    i = pl.program_id(0)
    tc = COMBINE_TOKENS

    def issue(step, slot):
        t0 = step * tc

        def body(j, carry):
            for k in range(TOP_K):
                d = dest_ref[(t0 + j) * TOP_K + k]
                pltpu.make_async_copy(y_hbm.at[pl.ds(d, 1)], buf_ref.at[slot, pl.ds(k * tc + j, 1)],
                                      sems.at[slot]).start()
            return carry

        lax.fori_loop(0, tc, body, 0, unroll=8)

    @pl.when(i == 0)
    def _():
        issue(0, 0)

    @pl.when(i + 1 < pl.num_programs(0))
    def _():
        issue(i + 1, (i + 1) % 2)

    slot = i % 2
    pltpu.make_async_copy(y_hbm.at[pl.ds(0, TOP_K * tc)], buf_ref.at[slot], sems.at[slot]).wait()
    acc = x2_ref[...]
    gate = gate_ref[...]
    for k in range(TOP_K):
        acc = acc + gate[:, k:k + 1] * buf_ref[slot, k * tc:(k + 1) * tc, :]
    out_ref[...] = acc


def _combine(dest, y, x2, gates):
    n = x2.shape[0]
    tc = COMBINE_TOKENS
    grid_spec = pltpu.PrefetchScalarGridSpec(
        num_scalar_prefetch=1,
        grid=(n // tc,),
        in_specs=[pl.BlockSpec(memory_space=pl.ANY),
                  pl.BlockSpec((tc, D_MODEL), lambda i, d: (i, 0)),
                  pl.BlockSpec((tc, LANES), lambda i, d: (i, 0))],
        out_specs=pl.BlockSpec((tc, D_MODEL), lambda i, d: (i, 0)),
        scratch_shapes=[pltpu.VMEM((2, TOP_K * tc, D_MODEL), F32), pltpu.SemaphoreType.DMA((2,))],
    )
    return pl.pallas_call(
        _combine_kernel,
        grid_spec=grid_spec,
        out_shape=jax.ShapeDtypeStruct((n, D_MODEL), F32),
        compiler_params=pltpu.CompilerParams(
            dimension_semantics=("arbitrary",), vmem_limit_bytes=VMEM_LIMIT),
        name="combine",
    )(dest, y, x2, gates)


def _route(top_idx):
    n = top_idx.shape[0]
    a = n * TOP_K
    bm = MOE_BLOCK_ROWS
    nb = a // bm + N_EXPERTS
    e_flat = top_idx.reshape(a)
    onehot = (e_flat[:, None] == jnp.arange(N_EXPERTS, dtype=jnp.int32)[None, :]).astype(jnp.int32)
    csum = jnp.cumsum(onehot, axis=0)
    counts = csum[-1]
    padded = ((counts + bm - 1) // bm) * bm
    pad_end = jnp.cumsum(padded)
    pad_start = pad_end - padded
    dest = jnp.sum(onehot * (csum - 1 + pad_start[None, :]), axis=1)
    n_used = (pad_end[-1] // bm).astype(jnp.int32)
    blk = jnp.arange(nb, dtype=jnp.int32)
    bexp = jnp.sum((pad_end[None, :] <= (blk * bm)[:, None]).astype(jnp.int32), axis=1)
    bexp = jnp.minimum(bexp, N_EXPERTS - 1)
    bexp = jnp.where(blk < n_used, bexp, bexp[jnp.maximum(n_used - 1, 0)])
    bfirst = jnp.concatenate([jnp.ones((1,), jnp.int32), (bexp[1:] != bexp[:-1]).astype(jnp.int32)])
    zstart = jnp.where(counts > 0, pad_end - bm, -1).astype(jnp.int32)
    return dest.astype(jnp.int32), zstart, bexp, bfirst, n_used.reshape(1)


def kernel(x, positions, norm1_g, w_in, q_norm_g, k_norm_g, hgrn_lower_bounds, hgrn_onorm_g,
           w_out, norm2_g, w_router, b_router, w_up, b_up, w_down, b_down):
    batch, seq, d = x.shape
    n = batch * seq
    depth = norm1_g.shape[0]
    lbs_all = jnp.cumsum(jax.nn.softmax(hgrn_lower_bounds.astype(F32), axis=0), axis=0)
    half = ATT_DIM // 2
    inv = 1.0 / (ROPE_THETA ** (jnp.arange(half, dtype=F32) / half))
    inv_tab = jnp.tile(inv, LANES // half).reshape(1, LANES)
    pos_col = positions.reshape(n, 1)

    x2d = x.reshape(n, d)
    for l in range(depth):
        lbs = lbs_all[l].reshape(2 * HG_HEADS, 1, HG_DIM)
        proj, qkv4, qkv16 = _inproj(
            x2d, pos_col, inv_tab, norm1_g[l].reshape(1, d), w_in[l].astype(BF16),
            jnp.tile(q_norm_g[l], LANES // ATT_DIM).reshape(1, LANES),
            jnp.tile(k_norm_g[l], LANES // ATT_DIM).reshape(1, LANES), batch, seq)
        o_f, o_b = _hgrn(proj, lbs, batch, seq)
        o1, l1 = _attention(proj.reshape(batch, 1, seq, IN_COLS), COL_AQ)
        o4, l4 = _attention(qkv4, 0)
        o16, l16 = _attention(qkv16, 0)
        atts = [o1.reshape(n, ATT_WIDTH), o4, o16]
        lses = [l1.reshape(n, ATT_WIDTH), l4, l16]

        wr = jnp.pad(w_router[l], ((0, 0), (0, LANES - N_EXPERTS)))
        wr_hi = wr.astype(BF16)
        wr_lo = (wr - wr_hi.astype(F32)).astype(BF16)
        br = jnp.pad(b_router[l], (0, LANES - N_EXPERTS)).reshape(1, LANES)
        x2, xn, gates, top_idx = _outproj(
            o_f, o_b, proj, atts, lses, x2d, hgrn_onorm_g[l].reshape(1, HG_DIM),
            w_out[l].astype(BF16), norm2_g[l].reshape(1, d), wr_hi, wr_lo, br, seq)
        dest, zstart, bexp, bfirst, n_used = _route(top_idx[:, :TOP_K])
        xs = _dispatch(dest, zstart, xn, bexp.shape[0] * MOE_BLOCK_ROWS)
        y = _moe(bexp, bfirst, n_used, xs, w_up[l],
                 b_up[l][:, 0::2].reshape(N_EXPERTS, 1, D_EXPERT),
                 b_up[l][:, 1::2].reshape(N_EXPERTS, 1, D_EXPERT),
                 w_down[l], b_down[l].reshape(N_EXPERTS, 1, D_MODEL))
        x2d = _combine(dest, y, x2, gates)
    return x2d.reshape(batch, seq, d)
```

```python
import functools

import jax
import jax.numpy as jnp
from jax import lax
from jax.experimental import pallas as pl
from jax.experimental.pallas import tpu as pltpu

F32 = jnp.float32
BF16 = jnp.bfloat16

D_MODEL = 1024
HG_HEADS = 4
HG_DIM = 128
HG_WIDTH = HG_HEADS * HG_DIM
HG_CHUNK = 64
ATT_HEADS = 8
ATT_DIM = 64
ATT_WIDTH = ATT_HEADS * ATT_DIM
DILATED_PATTERNS = ((128, 1), (512, 4), (2048, 16))
ATT_HALF = 64
ATT_QBLOCK = 128
ROPE_THETA = 10000.0
IN_COLS = 5 * HG_WIDTH + 3 * ATT_WIDTH
N_EXPERTS = 32
TOP_K = 4
D_EXPERT = D_MODEL
SWIGLU_LIMIT = 7.0
SWIGLU_ALPHA = 1.702
EPS = 1e-6
NEG = -1e30

COL_HQ, COL_HF_FWD, COL_HF_BWD, COL_HI, COL_HG, COL_AQ, COL_AK, COL_AV = range(8)

TOKEN_TILE = 512
MOE_BLOCK_ROWS = 512
DISPATCH_TOKENS = 512
COMBINE_TOKENS = 256
LANES = 128
VMEM_LIMIT = 56 * 1024 * 1024


def _dot(a, b):
    return jnp.dot(a, b, preferred_element_type=F32)


def _dot_nt(a, b):
    return lax.dot_general(a, b, (((1,), (1,)), ((), ())), preferred_element_type=F32)


def _dot_tn(a, b):
    return lax.dot_general(a, b, (((0,), (0,)), ((), ())), preferred_element_type=F32)


def _sigmoid(x):
    return 1.0 / (1.0 + jnp.exp(-x))


def _head_norm_rope(p, gain, cos, sin_signed, scale):
    lane = lax.broadcasted_iota(jnp.int32, (p.shape[0], LANES), 1)
    low = lane < ATT_DIM
    first_half = (lane % ATT_DIM) < (ATT_DIM // 2)
    outs = []
    for t in range(ATT_WIDTH // LANES):
        blk = p[:, t * LANES:(t + 1) * LANES]
        sq = blk * blk
        s_low = jnp.sum(jnp.where(low, sq, 0.0), axis=-1, keepdims=True)
        s_high = jnp.sum(jnp.where(low, 0.0, sq), axis=-1, keepdims=True)
        r = jnp.where(low, lax.rsqrt(s_low * (1.0 / ATT_DIM) + EPS),
                      lax.rsqrt(s_high * (1.0 / ATT_DIM) + EPS))
        y = blk * r * gain
        partner = jnp.where(first_half, pltpu.roll(y, LANES - ATT_DIM // 2, axis=1),
                            pltpu.roll(y, ATT_DIM // 2, axis=1))
        outs.append((y * cos + partner * sin_signed) * scale)
    return jnp.concatenate(outs, axis=1)


def _inproj_kernel(x_ref, pos_ref, inv_ref, g1_ref, w_ref, qg_ref, kg_ref,
                   out_ref, d4_ref, d16_ref, stage_ref):
    x = x_ref[...]
    ms = jnp.mean(x * x, axis=-1, keepdims=True)
    h = (x * lax.rsqrt(ms + EPS) * g1_ref[...]).astype(BF16)
    ang = pos_ref[...].astype(F32) * inv_ref[...]
    lane = lax.broadcasted_iota(jnp.int32, ang.shape, 1)
    cos = jnp.cos(ang)
    sin_signed = jnp.where((lane % ATT_DIM) < (ATT_DIM // 2), -jnp.sin(ang), jnp.sin(ang))
    for j in range(IN_COLS // 512):
        p = _dot(h, w_ref[:, j * 512:(j + 1) * 512])
        if j == COL_AQ:
            p = _head_norm_rope(p, qg_ref[...], cos, sin_signed, ATT_DIM ** -0.5)
        elif j == COL_AK:
            p = _head_norm_rope(p, kg_ref[...], cos, sin_signed, 1.0)
        out_ref[:, j * 512:(j + 1) * 512] = p.astype(BF16)
        if j >= COL_AQ:
            for c in range(ATT_WIDTH // LANES):
                stage_ref[c] = p[:, c * LANES:(c + 1) * LANES]
            c0 = (j - COL_AQ) * ATT_WIDTH
            for dil, ref in ((4, d4_ref), (16, d16_ref)):
                rows = x.shape[0] // dil
                for r in range(dil):
                    for c in range(ATT_WIDTH // LANES):
                        ref[0, r, :, c0 + c * LANES:c0 + (c + 1) * LANES] = (
                            stage_ref[c, pl.ds(r, rows, stride=dil), :].astype(BF16))


def _inproj(x2d, pos_col, inv_tab, g1, w_in_bf16, qg, kg, batch, seq):
    n = x2d.shape[0]
    t = TOKEN_TILE
    nt = seq // t
    const = lambda i: (0, 0)
    qkv = 3 * ATT_WIDTH

    def residue_major(dil):
        spec = pl.BlockSpec((1, dil, t // dil, qkv), lambda i: (i // nt, 0, i % nt, 0))
        return spec, jax.ShapeDtypeStruct((batch, dil, seq // dil, qkv), BF16)

    spec4, shape4 = residue_major(4)
    spec16, shape16 = residue_major(16)
    return pl.pallas_call(
        _inproj_kernel,
        grid=(n // t,),
        in_specs=[
            pl.BlockSpec((t, D_MODEL), lambda i: (i, 0)),
            pl.BlockSpec((t, 1), lambda i: (i, 0)),
            pl.BlockSpec((1, LANES), const),
            pl.BlockSpec((1, D_MODEL), const),
            pl.BlockSpec((D_MODEL, IN_COLS), const),
            pl.BlockSpec((1, LANES), const),
            pl.BlockSpec((1, LANES), const),
        ],
        out_specs=[pl.BlockSpec((t, IN_COLS), lambda i: (i, 0)), spec4, spec16],
        out_shape=[jax.ShapeDtypeStruct((n, IN_COLS), BF16), shape4, shape16],
        scratch_shapes=[pltpu.VMEM((ATT_WIDTH // LANES, t, LANES), F32)],
        compiler_params=pltpu.CompilerParams(
            dimension_semantics=("parallel",), vmem_limit_bytes=VMEM_LIMIT),
        name="inproj",
    )(x2d, pos_col, inv_tab, g1, w_in_bf16, qg, kg)


def _hgrn_direction(q_ref, z_ref, v_ref, lb, state_t, reverse):
    c = HG_CHUNK
    t = q_ref.shape[0]
    n = t // c
    row = lax.broadcasted_iota(jnp.int32, (c, c), 0)
    col = lax.broadcasted_iota(jnp.int32, (c, c), 1)
    mask = (row <= col) if reverse else (row >= col)
    tri = jnp.where(mask, 1.0, 0.0).astype(BF16)
    last_row = 0 if reverse else c - 1

    z = z_ref[...].astype(F32)
    q = q_ref[...].astype(F32)
    v = v_ref[...]
    sg = _sigmoid(z)
    f = lb + (1.0 - lb) * sg
    k = (1.0 - lb) * (1.0 - sg)
    lf = jnp.log(f)
    lf_hi = lf.astype(BF16)
    lf_lo = (lf - lf_hi.astype(F32)).astype(BF16)
    chunks = [slice(j * c, (j + 1) * c) for j in range(n)]
    b = jnp.concatenate([_dot(tri, lf_hi[rs]) + _dot(tri, lf_lo[rs]) for rs in chunks], axis=0)
    b_last = b.reshape(n, c, HG_DIM)[:, last_row:last_row + 1, :]
    decay = jnp.exp(b_last)
    qt = (q * _sigmoid(q) * jnp.exp(b)).astype(BF16)
    kt_f32 = k * jnp.exp(-b)
    kt = kt_f32.astype(BF16)
    kd = (kt_f32.reshape(n, c, HG_DIM) * decay).reshape(t, HG_DIM).astype(BF16)

    outs, updates = [], []
    for rs in chunks:
        a = jnp.where(mask, _dot_nt(qt[rs], kt[rs]), 0.0)
        outs.append(_dot(a.astype(BF16), v[rs]))
        updates.append(_dot_tn(v[rs], kd[rs]))
    for j in (reversed(range(n)) if reverse else range(n)):
        outs[j] = outs[j] + _dot_nt(qt[chunks[j]], state_t.astype(BF16))
        state_t = state_t * decay[j] + updates[j]
    return jnp.concatenate(outs, axis=0), state_t


def _hgrn_kernel(qf_ref, zf_ref, vf_ref, qb_ref, zb_ref, vb_ref, lbf_ref, lbb_ref,
                 of_ref, ob_ref, sf_ref, sb_ref):
    @pl.when(pl.program_id(2) == 0)
    def _():
        sf_ref[...] = jnp.zeros_like(sf_ref)
        sb_ref[...] = jnp.zeros_like(sb_ref)

    o, sf = _hgrn_direction(qf_ref, zf_ref, vf_ref, lbf_ref[0], sf_ref[...], False)
    of_ref[...] = o.astype(of_ref.dtype)
    sf_ref[...] = sf
    o, sb = _hgrn_direction(qb_ref, zb_ref, vb_ref, lbb_ref[0], sb_ref[...], True)
    ob_ref[...] = o.astype(ob_ref.dtype)
    sb_ref[...] = sb


def _hgrn(proj, lbs, batch, seq):
    n = proj.shape[0]
    t = TOKEN_TILE
    nblk = seq // t

    def fwd(colblk):
        return pl.BlockSpec((t, HG_DIM), lambda b, h, i: (b * nblk + i, colblk * HG_HEADS + h))

    def bwd(colblk):
        return pl.BlockSpec((t, HG_DIM), lambda b, h, i: (b * nblk + nblk - 1 - i, colblk * HG_HEADS + h))

    out_f = pl.BlockSpec((t, HG_DIM), lambda b, h, i: (b * nblk + i, h))
    out_b = pl.BlockSpec((t, HG_DIM), lambda b, h, i: (b * nblk + nblk - 1 - i, h))
    return pl.pallas_call(
        _hgrn_kernel,
        grid=(batch, HG_HEADS, nblk),
        in_specs=[
            fwd(COL_HQ), fwd(COL_HF_FWD), fwd(COL_HI),
            bwd(COL_HQ), bwd(COL_HF_BWD), bwd(COL_HI),
            pl.BlockSpec((1, 1, HG_DIM), lambda b, h, i: (h, 0, 0)),
            pl.BlockSpec((1, 1, HG_DIM), lambda b, h, i: (HG_HEADS + h, 0, 0)),
        ],
        out_specs=[out_f, out_b],
        out_shape=[jax.ShapeDtypeStruct((n, HG_WIDTH), BF16)] * 2,
        scratch_shapes=[pltpu.VMEM((HG_DIM, HG_DIM), F32)] * 2,
        compiler_params=pltpu.CompilerParams(
            dimension_semantics=("parallel", "parallel", "arbitrary"), vmem_limit_bytes=VMEM_LIMIT),
        name="hgrn",
    )(proj, proj, proj, proj, proj, proj, lbs, lbs)


def _attn_kernel(q_ref, kc_ref, kp_ref, kn_ref, vc_ref, vp_ref, vn_ref, o_ref, l_ref,
                 kw_ref, vw_ref, *, tq, length):
    n = pl.program_id(2)
    half = ATT_HALF
    kw_ref[0:half, :] = kp_ref[...]
    kw_ref[half:half + tq, :] = kc_ref[...]
    kw_ref[half + tq:, :] = kn_ref[...]
    vw_ref[0:half, :] = vp_ref[...]
    vw_ref[half:half + tq, :] = vc_ref[...]
    vw_ref[half + tq:, :] = vn_ref[...]

    qb_rows = ATT_QBLOCK
    win = qb_rows + 2 * half
    i_idx = lax.broadcasted_iota(jnp.int32, (qb_rows, win), 0)
    j_idx = lax.broadcasted_iota(jnp.int32, (qb_rows, win), 1)
    band = (j_idx >= i_idx) & (j_idx <= i_idx + 2 * half)

    def body(qb, carry):
        r0 = pl.multiple_of(qb * qb_rows, qb_rows)
        base = n * tq + r0 - half
        valid = band & (j_idx >= -base) & (j_idx < length - base)
        q = q_ref[pl.ds(r0, qb_rows), :]
        kw = kw_ref[pl.ds(r0, win), :]
        vw = vw_ref[pl.ds(r0, win), :]
        heads = [slice(h * ATT_DIM, (h + 1) * ATT_DIM) for h in range(ATT_HEADS)]
        scores = [jnp.where(valid, _dot_nt(q[:, cs], kw[:, cs]), NEG) for cs in heads]
        maxes = [jnp.max(s, axis=-1, keepdims=True) for s in scores]
        probs = [jnp.exp(s - m) for s, m in zip(scores, maxes)]
        dens = [jnp.sum(p, axis=-1, keepdims=True) for p in probs]
        outs = [_dot(p.astype(BF16), vw[:, cs]) / den for p, cs, den in zip(probs, heads, dens)]
        lses = [jnp.broadcast_to(m + jnp.log(den), (qb_rows, ATT_DIM)) for m, den in zip(maxes, dens)]
        o_ref[pl.ds(r0, qb_rows), :] = jnp.concatenate(outs, axis=1).astype(o_ref.dtype)
        l_ref[pl.ds(r0, qb_rows), :] = jnp.concatenate(lses, axis=1)
        return carry

    lax.fori_loop(0, tq // qb_rows, body, 0)


def _attention(qkv, col0):
    batch, dil, length, _ = qkv.shape
    tq = min(TOKEN_TILE, length)
    nq = length // tq
    hb = tq // ATT_HALF
    n_hblk = length // ATT_HALF

    def cur(col):
        return pl.BlockSpec((None, None, tq, ATT_WIDTH), lambda b, r, n: (b, r, n, col))

    def prev(col):
        return pl.BlockSpec((None, None, ATT_HALF, ATT_WIDTH),
                            lambda b, r, n: (b, r, jnp.maximum(n * hb - 1, 0), col))

    def nxt(col):
        return pl.BlockSpec((None, None, ATT_HALF, ATT_WIDTH),
                            lambda b, r, n: (b, r, jnp.minimum((n + 1) * hb, n_hblk - 1), col))

    out_spec = pl.BlockSpec((None, None, tq, ATT_WIDTH), lambda b, r, n: (b, r, n, 0))
    return pl.pallas_call(
        functools.partial(_attn_kernel, tq=tq, length=length),
        grid=(batch, dil, nq),
        in_specs=[cur(col0), cur(col0 + 1), prev(col0 + 1), nxt(col0 + 1),
                  cur(col0 + 2), prev(col0 + 2), nxt(col0 + 2)],
        out_specs=[out_spec, out_spec],
        out_shape=[jax.ShapeDtypeStruct((batch, dil, length, ATT_WIDTH), BF16),
                   jax.ShapeDtypeStruct((batch, dil, length, ATT_WIDTH), F32)],
        scratch_shapes=[pltpu.VMEM((tq + 2 * ATT_HALF, ATT_WIDTH), BF16)] * 2,
        compiler_params=pltpu.CompilerParams(
            dimension_semantics=("parallel", "parallel", "parallel"), vmem_limit_bytes=VMEM_LIMIT),
        name=f"attn_d{dil}",
    )(qkv, qkv, qkv, qkv, qkv, qkv, qkv)


def _token_major(src_ref, stage_ref):
    dil, rows = src_ref.shape[1], src_ref.shape[2]
    nc = src_ref.shape[3] // LANES
    for r in range(dil):
        for c in range(nc):
            stage_ref[c, pl.ds(r, rows, stride=dil), :] = (
                src_ref[0, r, :, c * LANES:(c + 1) * LANES].astype(F32))
    return jnp.concatenate([stage_ref[c] for c in range(nc)], axis=1)


def _outproj_kernel(of_ref, ob_ref, hg_ref, o1_ref, o2_ref, o3_ref, l1_ref, l2_ref, l3_ref,
                    x_ref, og_ref, w_ref, g2_ref, wrh_ref, wrl_ref, br_ref,
                    x2_ref, xn_ref, gate_ref, idx_ref, st_o2, st_o3, st_l2, st_l3):
    o = of_ref[...].astype(F32) + ob_ref[...].astype(F32)
    hg = hg_ref[...].astype(F32)
    parts = []
    for h in range(HG_HEADS):
        blk = o[:, h * HG_DIM:(h + 1) * HG_DIM]
        ms = jnp.mean(blk * blk, axis=-1, keepdims=True)
        parts.append(blk * lax.rsqrt(ms + EPS) * og_ref[...])
    o_hg = jnp.concatenate(parts, axis=1) * (hg * _sigmoid(hg))

    l1 = l1_ref[...]
    l2 = _token_major(l2_ref, st_l2)
    l3 = _token_major(l3_ref, st_l3)
    mx = jnp.maximum(jnp.maximum(l1, l2), l3)
    e1, e2, e3 = jnp.exp(l1 - mx), jnp.exp(l2 - mx), jnp.exp(l3 - mx)
    o_att = (e1 * o1_ref[...].astype(F32) + e2 * _token_major(o2_ref, st_o2)
             + e3 * _token_major(o3_ref, st_o3)) / (e1 + e2 + e3)

    y = _dot(o_hg.astype(BF16), w_ref[0:HG_WIDTH, :]) + _dot(o_att.astype(BF16), w_ref[HG_WIDTH:, :])
    x2 = x_ref[...] + y
    x2_ref[...] = x2

    ms = jnp.mean(x2 * x2, axis=-1, keepdims=True)
    xn = x2 * lax.rsqrt(ms + EPS) * g2_ref[...]
    xn_ref[...] = xn
    xn_hi = xn.astype(BF16)
    xn_lo = (xn - xn_hi.astype(F32)).astype(BF16)
    logits = (_dot(xn_hi, wrh_ref[...]) + _dot(xn_lo, wrh_ref[...]) + _dot(xn_hi, wrl_ref[...])
              + br_ref[...])

    lane = lax.broadcasted_iota(jnp.int32, logits.shape, 1)
    lane_f = lane.astype(F32)
    work = jnp.where(lane < N_EXPERTS, logits, -jnp.inf)
    vals, idxs = [], []
    for _ in range(TOP_K):
        m = jnp.max(work, axis=-1, keepdims=True)
        idx = jnp.min(jnp.where(work == m, lane_f, float(LANES)), axis=-1, keepdims=True)
        vals.append(m)
        idxs.append(idx)
        work = jnp.where(lane_f == idx, -jnp.inf, work)
    es = [jnp.exp(v - vals[0]) for v in vals]
    den = es[0] + es[1] + es[2] + es[3]
    gate_out = jnp.zeros(logits.shape, F32)
    idx_out = jnp.zeros(logits.shape, F32)
    for k in range(TOP_K):
        gate_out = jnp.where(lane == k, es[k] / den, gate_out)
        idx_out = jnp.where(lane == k, idxs[k], idx_out)
    gate_ref[...] = gate_out
    idx_ref[...] = idx_out.astype(jnp.int32)


def _outproj(o_f, o_b, proj, atts, lses, x2d, og, w_out_bf16, g2, wr_hi, wr_lo, br, seq):
    n = x2d.shape[0]
    t = TOKEN_TILE
    nt = seq // t
    row = lambda i: (i, 0)
    const = lambda i: (0, 0)
    half = pl.BlockSpec((t, 512), row)

    def residue_major(dil):
        return pl.BlockSpec((1, dil, t // dil, ATT_WIDTH), lambda i: (i // nt, 0, i % nt, 0))

    rm4, rm16 = residue_major(4), residue_major(16)
    return pl.pallas_call(
        _outproj_kernel,
        grid=(n // t,),
        in_specs=[
            half, half, pl.BlockSpec((t, 512), lambda i: (i, COL_HG)),
            half, rm4, rm16, half, rm4, rm16,
            pl.BlockSpec((t, D_MODEL), row),
            pl.BlockSpec((1, HG_DIM), const),
            pl.BlockSpec((D_MODEL, D_MODEL), const),
            pl.BlockSpec((1, D_MODEL), const),
            pl.BlockSpec((D_MODEL, LANES), const),
            pl.BlockSpec((D_MODEL, LANES), const),
            pl.BlockSpec((1, LANES), const),
        ],
        out_specs=[pl.BlockSpec((t, D_MODEL), row), pl.BlockSpec((t, D_MODEL), row),
                   pl.BlockSpec((t, LANES), row), pl.BlockSpec((t, LANES), row)],
        out_shape=[jax.ShapeDtypeStruct((n, D_MODEL), F32), jax.ShapeDtypeStruct((n, D_MODEL), F32),
                   jax.ShapeDtypeStruct((n, LANES), F32), jax.ShapeDtypeStruct((n, LANES), jnp.int32)],
        scratch_shapes=[pltpu.VMEM((ATT_WIDTH // LANES, t, LANES), F32)] * 4,
        compiler_params=pltpu.CompilerParams(
            dimension_semantics=("parallel",), vmem_limit_bytes=VMEM_LIMIT),
        name="outproj",
    )(o_f, o_b, proj, *atts, *lses, x2d, og, w_out_bf16, g2, wr_hi, wr_lo, br)


def _moe_kernel(bexp_ref, bfirst_ref, nused_ref, xs_ref, wu_ref, bg_ref, bl_ref, wd_ref, bd_ref,
                y_ref, wg_s, wl_s, wd_s):
    i = pl.program_id(0)

    @pl.when(bfirst_ref[i] == 1)
    def _():
        r = lax.broadcasted_iota(jnp.int32, (2 * LANES, 2 * LANES), 0)
        c = lax.broadcasted_iota(jnp.int32, (2 * LANES, 2 * LANES), 1)
        src = jnp.where(c < LANES, 2 * c, 2 * (c - LANES) + 1)
        perm = jnp.where(r == src, 1.0, 0.0).astype(BF16)
        rows = 256
        for rb in range(D_MODEL // rows):
            rs = slice(rb * rows, (rb + 1) * rows)
            for cb in range(D_EXPERT // LANES):
                w = wu_ref[0, rs, cb * 2 * LANES:(cb + 1) * 2 * LANES].astype(BF16)
                split = _dot(w, perm).astype(BF16)
                wg_s[rs, cb * LANES:(cb + 1) * LANES] = split[:, :LANES]
                wl_s[rs, cb * LANES:(cb + 1) * LANES] = split[:, LANES:]
        wd_s[...] = wd_ref[0].astype(BF16)

    @pl.when(i < nused_ref[0])
    def _():
        x = xs_ref[...].astype(BF16)
        hglu = _dot(x, wg_s[...]) + bg_ref[0]
        hlin = _dot(x, wl_s[...]) + bl_ref[0]
        glu = jnp.minimum(hglu, SWIGLU_LIMIT)
        lin = jnp.clip(hlin, -SWIGLU_LIMIT, SWIGLU_LIMIT)
        act = glu * _sigmoid(SWIGLU_ALPHA * glu) * (lin + 1.0)
        y_ref[...] = (_dot(act.astype(BF16), wd_s[...]) + bd_ref[0]).astype(y_ref.dtype)

    @pl.when(i >= nused_ref[0])
    def _():
        y_ref[...] = jnp.zeros_like(y_ref)


def _moe(block_exp, block_first, n_used, xs, w_up, b_glu, b_lin, w_down, b_down):
    p_rows = xs.shape[0]
    bm = MOE_BLOCK_ROWS
    nb = p_rows // bm
    exp3 = lambda i, be, bf, nu: (be[i], 0, 0)
    grid_spec = pltpu.PrefetchScalarGridSpec(
        num_scalar_prefetch=3,
        grid=(nb,),
        in_specs=[
            pl.BlockSpec((bm, D_MODEL), lambda i, be, bf, nu: (jnp.minimum(i, nu[0] - 1), 0)),
            pl.BlockSpec((1, D_MODEL, 2 * D_EXPERT), exp3),
            pl.BlockSpec((1, 1, D_EXPERT), exp3),
            pl.BlockSpec((1, 1, D_EXPERT), exp3),
            pl.BlockSpec((1, D_EXPERT, D_MODEL), exp3),
            pl.BlockSpec((1, 1, D_MODEL), exp3),
        ],
        out_specs=pl.BlockSpec((bm, D_MODEL), lambda i, be, bf, nu: (i, 0)),
        scratch_shapes=[pltpu.VMEM((D_MODEL, D_EXPERT), BF16), pltpu.VMEM((D_MODEL, D_EXPERT), BF16),
                        pltpu.VMEM((D_EXPERT, D_MODEL), BF16)],
    )
    return pl.pallas_call(
        _moe_kernel,
        grid_spec=grid_spec,
        out_shape=jax.ShapeDtypeStruct((p_rows, D_MODEL), F32),
        compiler_params=pltpu.CompilerParams(
            dimension_semantics=("arbitrary",), vmem_limit_bytes=VMEM_LIMIT),
        name="moe",
    )(block_exp, block_first, n_used, xs, w_up, b_glu, b_lin, w_down, b_down)


def _dispatch_kernel(dest_ref, zstart_ref, xn_ref, xs_hbm, zero_ref, sem, zsem):
    i = pl.program_id(0)
    bm = MOE_BLOCK_ROWS
    tokens = xn_ref.shape[0]

    def zero_copy(e):
        start = pl.multiple_of(zstart_ref[e], bm)
        return pltpu.make_async_copy(zero_ref, xs_hbm.at[pl.ds(start, bm)], zsem)

    @pl.when(i == 0)
    def _():
        zero_ref[...] = jnp.zeros_like(zero_ref)
        for e in range(N_EXPERTS):
            @pl.when(zstart_ref[e] >= 0)
            def _():
                zero_copy(e).start()
        for e in range(N_EXPERTS):
            @pl.when(zstart_ref[e] >= 0)
            def _():
                zero_copy(e).wait()

    t0 = i * tokens

    def body(j, carry):
        for k in range(TOP_K):
            d = dest_ref[(t0 + j) * TOP_K + k]
            pltpu.make_async_copy(xn_ref.at[pl.ds(j, 1)], xs_hbm.at[pl.ds(d, 1)], sem).start()
        return carry

    lax.fori_loop(0, tokens, body, 0, unroll=8)
    for k in range(TOP_K):
        pltpu.make_async_copy(xn_ref, xs_hbm.at[pl.ds(0, tokens)], sem).wait()


def _dispatch(dest, zstart, xn, p_rows):
    n = xn.shape[0]
    t = DISPATCH_TOKENS
    grid_spec = pltpu.PrefetchScalarGridSpec(
        num_scalar_prefetch=2,
        grid=(n // t,),
        in_specs=[pl.BlockSpec((t, D_MODEL), lambda i, d, z: (i, 0))],
        out_specs=pl.BlockSpec(memory_space=pl.ANY),
        scratch_shapes=[pltpu.VMEM((MOE_BLOCK_ROWS, D_MODEL), F32),
                        pltpu.SemaphoreType.DMA, pltpu.SemaphoreType.DMA],
    )
    return pl.pallas_call(
        _dispatch_kernel,
        grid_spec=grid_spec,
        out_shape=jax.ShapeDtypeStruct((p_rows, D_MODEL), F32),
        compiler_params=pltpu.CompilerParams(
            dimension_semantics=("arbitrary",), vmem_limit_bytes=VMEM_LIMIT),
        name="dispatch",
    )(dest, zstart, xn)


def _combine_kernel(dest_ref, y_hbm, x2_ref, gate_ref, out_ref, buf_ref, sems):
    i = pl.program_id(0)
    tc = COMBINE_TOKENS

    def issue(step, slot):
        t0 = step * tc

        def body(j, carry):
            for k in range(TOP_K):
                d = dest_ref[(t0 + j) * TOP_K + k]
                pltpu.make_async_copy(y_hbm.at[pl.ds(d, 1)], buf_ref.at[slot, pl.ds(k * tc + j, 1)],
                                      sems.at[slot]).start()
            return carry

        lax.fori_loop(0, tc, body, 0, unroll=8)

    @pl.when(i == 0)
    def _():
        issue(0, 0)

    @pl.when(i + 1 < pl.num_programs(0))
    def _():
        issue(i + 1, (i + 1) % 2)

    slot = i % 2
    pltpu.make_async_copy(y_hbm.at[pl.ds(0, TOP_K * tc)], buf_ref.at[slot], sems.at[slot]).wait()
    acc = x2_ref[...]
    gate = gate_ref[...]
    for k in range(TOP_K):
        acc = acc + gate[:, k:k + 1] * buf_ref[slot, k * tc:(k + 1) * tc, :]
    out_ref[...] = acc


def _combine(dest, y, x2, gates):
    n = x2.shape[0]
    tc = COMBINE_TOKENS
    grid_spec = pltpu.PrefetchScalarGridSpec(
        num_scalar_prefetch=1,
        grid=(n // tc,),
        in_specs=[pl.BlockSpec(memory_space=pl.ANY),
                  pl.BlockSpec((tc, D_MODEL), lambda i, d: (i, 0)),
                  pl.BlockSpec((tc, LANES), lambda i, d: (i, 0))],
        out_specs=pl.BlockSpec((tc, D_MODEL), lambda i, d: (i, 0)),
        scratch_shapes=[pltpu.VMEM((2, TOP_K * tc, D_MODEL), F32), pltpu.SemaphoreType.DMA((2,))],
    )
    return pl.pallas_call(
        _combine_kernel,
        grid_spec=grid_spec,
        out_shape=jax.ShapeDtypeStruct((n, D_MODEL), F32),
        compiler_params=pltpu.CompilerParams(
            dimension_semantics=("arbitrary",), vmem_limit_bytes=VMEM_LIMIT),
        name="combine",
    )(dest, y, x2, gates)


def _route(top_idx):
    n = top_idx.shape[0]
    a = n * TOP_K
    bm = MOE_BLOCK_ROWS
    nb = a // bm + N_EXPERTS
    e_flat = top_idx.reshape(a)
    onehot = (e_flat[:, None] == jnp.arange(N_EXPERTS, dtype=jnp.int32)[None, :]).astype(jnp.int32)
    csum = jnp.cumsum(onehot, axis=0)
    counts = csum[-1]
    padded = ((counts + bm - 1) // bm) * bm
    pad_end = jnp.cumsum(padded)
    pad_start = pad_end - padded
    dest = jnp.sum(onehot * (csum - 1 + pad_start[None, :]), axis=1)
    n_used = (pad_end[-1] // bm).astype(jnp.int32)
    blk = jnp.arange(nb, dtype=jnp.int32)
    bexp = jnp.sum((pad_end[None, :] <= (blk * bm)[:, None]).astype(jnp.int32), axis=1)
    bexp = jnp.minimum(bexp, N_EXPERTS - 1)
    bexp = jnp.where(blk < n_used, bexp, bexp[jnp.maximum(n_used - 1, 0)])
    bfirst = jnp.concatenate([jnp.ones((1,), jnp.int32), (bexp[1:] != bexp[:-1]).astype(jnp.int32)])
    zstart = jnp.where(counts > 0, pad_end - bm, -1).astype(jnp.int32)
    return dest.astype(jnp.int32), zstart, bexp, bfirst, n_used.reshape(1)


def kernel(x, positions, norm1_g, w_in, q_norm_g, k_norm_g, hgrn_lower_bounds, hgrn_onorm_g,
           w_out, norm2_g, w_router, b_router, w_up, b_up, w_down, b_down):
    batch, seq, d = x.shape
    n = batch * seq
    depth = norm1_g.shape[0]
    lbs_all = jnp.cumsum(jax.nn.softmax(hgrn_lower_bounds.astype(F32), axis=0), axis=0)
    half = ATT_DIM // 2
    inv = 1.0 / (ROPE_THETA ** (jnp.arange(half, dtype=F32) / half))
    inv_tab = jnp.tile(inv, LANES // half).reshape(1, LANES)
    pos_col = positions.reshape(n, 1)

    x2d = x.reshape(n, d)
    for l in range(depth):
        lbs = lbs_all[l].reshape(2 * HG_HEADS, 1, HG_DIM)
        proj, qkv4, qkv16 = _inproj(
            x2d, pos_col, inv_tab, norm1_g[l].reshape(1, d), w_in[l].astype(BF16),
            jnp.tile(q_norm_g[l], LANES // ATT_DIM).reshape(1, LANES),
            jnp.tile(k_norm_g[l], LANES // ATT_DIM).reshape(1, LANES), batch, seq)
        o_f, o_b = _hgrn(proj, lbs, batch, seq)
        o1, l1 = _attention(proj.reshape(batch, 1, seq, IN_COLS), COL_AQ)
        o4, l4 = _attention(qkv4, 0)
        o16, l16 = _attention(qkv16, 0)
        atts = [o1.reshape(n, ATT_WIDTH), o4, o16]
        lses = [l1.reshape(n, ATT_WIDTH), l4, l16]

        wr = jnp.pad(w_router[l], ((0, 0), (0, LANES - N_EXPERTS)))
        wr_hi = wr.astype(BF16)
        wr_lo = (wr - wr_hi.astype(F32)).astype(BF16)
        br = jnp.pad(b_router[l], (0, LANES - N_EXPERTS)).reshape(1, LANES)
        x2, xn, gates, top_idx = _outproj(
            o_f, o_b, proj, atts, lses, x2d, hgrn_onorm_g[l].reshape(1, HG_DIM),
            w_out[l].astype(BF16), norm2_g[l].reshape(1, d), wr_hi, wr_lo, br, seq)
        dest, zstart, bexp, bfirst, n_used = _route(top_idx[:, :TOP_K])
        xs = _dispatch(dest, zstart, xn, bexp.shape[0] * MOE_BLOCK_ROWS)
        y = _moe(bexp, bfirst, n_used, xs, w_up[l],
                 b_up[l][:, 0::2].reshape(N_EXPERTS, 1, D_EXPERT),
                 b_up[l][:, 1::2].reshape(N_EXPERTS, 1, D_EXPERT),
                 w_down[l], b_down[l].reshape(N_EXPERTS, 1, D_MODEL))
        x2d = _combine(dest, y, x2, gates)
    return x2d.reshape(batch, seq, d)
```

```python
import functools

import jax
import jax.numpy as jnp
from jax import lax
from jax.experimental import pallas as pl
from jax.experimental.pallas import tpu as pltpu

F32 = jnp.float32
BF16 = jnp.bfloat16

D_MODEL = 1024
HG_HEADS = 4
HG_DIM = 128
HG_WIDTH = HG_HEADS * HG_DIM
HG_CHUNK = 64
ATT_HEADS = 8
ATT_DIM = 64
ATT_WIDTH = ATT_HEADS * ATT_DIM
DILATED_PATTERNS = ((128, 1), (512, 4), (2048, 16))
ATT_HALF = 64
ATT_QBLOCK = 128
ROPE_THETA = 10000.0
IN_COLS = 5 * HG_WIDTH + 3 * ATT_WIDTH
N_EXPERTS = 32
TOP_K = 4
D_EXPERT = D_MODEL
SWIGLU_LIMIT = 7.0
SWIGLU_ALPHA = 1.702
EPS = 1e-6
NEG = -1e30

COL_HQ, COL_HF_FWD, COL_HF_BWD, COL_HI, COL_HG, COL_AQ, COL_AK, COL_AV = range(8)

TOKEN_TILE = 512
MOE_BLOCK_ROWS = 512
DISPATCH_TOKENS = 512
COMBINE_TOKENS = 256
LANES = 128
ROW_TILES = D_MODEL // LANES
VMEM_LIMIT = 56 * 1024 * 1024


def _dot(a, b):
    return jnp.dot(a, b, preferred_element_type=F32)


def _dot_nt(a, b):
    return lax.dot_general(a, b, (((1,), (1,)), ((), ())), preferred_element_type=F32)


def _dot_tn(a, b):
    return lax.dot_general(a, b, (((0,), (0,)), ((), ())), preferred_element_type=F32)


def _sigmoid(x):
    return 1.0 / (1.0 + jnp.exp(-x))


def _rows_to_tiles(dst_ref, x):
    for c in range(ROW_TILES):
        dst_ref[pl.ds(c, x.shape[0], stride=ROW_TILES), :] = x[:, c * LANES:(c + 1) * LANES]


def _tiles_to_rows(src, rows, first_row=0):
    return [src[pl.ds(first_row * ROW_TILES + c, rows, stride=ROW_TILES), :] for c in range(ROW_TILES)]


def _head_norm_rope(p, gain, cos, sin_signed, scale):
    lane = lax.broadcasted_iota(jnp.int32, (p.shape[0], LANES), 1)
    low = lane < ATT_DIM
    first_half = (lane % ATT_DIM) < (ATT_DIM // 2)
    outs = []
    for t in range(ATT_WIDTH // LANES):
        blk = p[:, t * LANES:(t + 1) * LANES]
        sq = blk * blk
        s_low = jnp.sum(jnp.where(low, sq, 0.0), axis=-1, keepdims=True)
        s_high = jnp.sum(jnp.where(low, 0.0, sq), axis=-1, keepdims=True)
        r = jnp.where(low, lax.rsqrt(s_low * (1.0 / ATT_DIM) + EPS),
                      lax.rsqrt(s_high * (1.0 / ATT_DIM) + EPS))
        y = blk * r * gain
        partner = jnp.where(first_half, pltpu.roll(y, LANES - ATT_DIM // 2, axis=1),
                            pltpu.roll(y, ATT_DIM // 2, axis=1))
        outs.append((y * cos + partner * sin_signed) * scale)
    return jnp.concatenate(outs, axis=1)


def _inproj_kernel(x_ref, pos_ref, inv_ref, g1_ref, w_ref, qg_ref, kg_ref,
                   out_ref, d4_ref, d16_ref, stage_ref):
    x = x_ref[...]
    ms = jnp.mean(x * x, axis=-1, keepdims=True)
    h = (x * lax.rsqrt(ms + EPS) * g1_ref[...]).astype(BF16)
    ang = pos_ref[...].astype(F32) * inv_ref[...]
    lane = lax.broadcasted_iota(jnp.int32, ang.shape, 1)
    cos = jnp.cos(ang)
    sin_signed = jnp.where((lane % ATT_DIM) < (ATT_DIM // 2), -jnp.sin(ang), jnp.sin(ang))
    for j in range(IN_COLS // 512):
        p = _dot(h, w_ref[:, j * 512:(j + 1) * 512])
        if j == COL_AQ:
            p = _head_norm_rope(p, qg_ref[...], cos, sin_signed, ATT_DIM ** -0.5)
        elif j == COL_AK:
            p = _head_norm_rope(p, kg_ref[...], cos, sin_signed, 1.0)
        out_ref[:, j * 512:(j + 1) * 512] = p.astype(BF16)
        if j >= COL_AQ:
            for c in range(ATT_WIDTH // LANES):
                stage_ref[c] = p[:, c * LANES:(c + 1) * LANES]
            c0 = (j - COL_AQ) * ATT_WIDTH
            for dil, ref in ((4, d4_ref), (16, d16_ref)):
                rows = x.shape[0] // dil
                for r in range(dil):
                    for c in range(ATT_WIDTH // LANES):
                        ref[0, r, :, c0 + c * LANES:c0 + (c + 1) * LANES] = (
                            stage_ref[c, pl.ds(r, rows, stride=dil), :].astype(BF16))


def _inproj(x2d, pos_col, inv_tab, g1, w_in_bf16, qg, kg, batch, seq):
    n = x2d.shape[0]
    t = TOKEN_TILE
    nt = seq // t
    const = lambda i: (0, 0)
    qkv = 3 * ATT_WIDTH

    def residue_major(dil):
        spec = pl.BlockSpec((1, dil, t // dil, qkv), lambda i: (i // nt, 0, i % nt, 0))
        return spec, jax.ShapeDtypeStruct((batch, dil, seq // dil, qkv), BF16)

    spec4, shape4 = residue_major(4)
    spec16, shape16 = residue_major(16)
    return pl.pallas_call(
        _inproj_kernel,
        grid=(n // t,),
        in_specs=[
            pl.BlockSpec((t, D_MODEL), lambda i: (i, 0)),
            pl.BlockSpec((t, 1), lambda i: (i, 0)),
            pl.BlockSpec((1, LANES), const),
            pl.BlockSpec((1, D_MODEL), const),
            pl.BlockSpec((D_MODEL, IN_COLS), const),
            pl.BlockSpec((1, LANES), const),
            pl.BlockSpec((1, LANES), const),
        ],
        out_specs=[pl.BlockSpec((t, IN_COLS), lambda i: (i, 0)), spec4, spec16],
        out_shape=[jax.ShapeDtypeStruct((n, IN_COLS), BF16), shape4, shape16],
        scratch_shapes=[pltpu.VMEM((ATT_WIDTH // LANES, t, LANES), F32)],
        compiler_params=pltpu.CompilerParams(
            dimension_semantics=("parallel",), vmem_limit_bytes=VMEM_LIMIT),
        name="inproj",
    )(x2d, pos_col, inv_tab, g1, w_in_bf16, qg, kg)


def _hgrn_direction(q_ref, z_ref, v_ref, lb, state_t, reverse):
    c = HG_CHUNK
    t = q_ref.shape[0]
    n = t // c
    row = lax.broadcasted_iota(jnp.int32, (c, c), 0)
    col = lax.broadcasted_iota(jnp.int32, (c, c), 1)
    mask = (row <= col) if reverse else (row >= col)
    tri = jnp.where(mask, 1.0, 0.0).astype(BF16)
    last_row = 0 if reverse else c - 1

    z = z_ref[...].astype(F32)
    q = q_ref[...].astype(F32)
    v = v_ref[...]
    sg = _sigmoid(z)
    f = lb + (1.0 - lb) * sg
    k = (1.0 - lb) * (1.0 - sg)
    lf = jnp.log(f)
    lf_hi = lf.astype(BF16)
    lf_lo = (lf - lf_hi.astype(F32)).astype(BF16)
    chunks = [slice(j * c, (j + 1) * c) for j in range(n)]
    b = jnp.concatenate([_dot(tri, lf_hi[rs]) + _dot(tri, lf_lo[rs]) for rs in chunks], axis=0)
    b_last = b.reshape(n, c, HG_DIM)[:, last_row:last_row + 1, :]
    decay = jnp.exp(b_last)
    qt = (q * _sigmoid(q) * jnp.exp(b)).astype(BF16)
    kt_f32 = k * jnp.exp(-b)
    kt = kt_f32.astype(BF16)
    kd = (kt_f32.reshape(n, c, HG_DIM) * decay).reshape(t, HG_DIM).astype(BF16)

    outs, updates = [], []
    for rs in chunks:
        a = jnp.where(mask, _dot_nt(qt[rs], kt[rs]), 0.0)
        outs.append(_dot(a.astype(BF16), v[rs]))
        updates.append(_dot_tn(v[rs], kd[rs]))
    for j in (reversed(range(n)) if reverse else range(n)):
        outs[j] = outs[j] + _dot_nt(qt[chunks[j]], state_t.astype(BF16))
        state_t = state_t * decay[j] + updates[j]
    return jnp.concatenate(outs, axis=0), state_t


def _hgrn_kernel(qf_ref, zf_ref, vf_ref, qb_ref, zb_ref, vb_ref, lbf_ref, lbb_ref,
                 of_ref, ob_ref, sf_ref, sb_ref):
    @pl.when(pl.program_id(2) == 0)
    def _():
        sf_ref[...] = jnp.zeros_like(sf_ref)
        sb_ref[...] = jnp.zeros_like(sb_ref)

    o, sf = _hgrn_direction(qf_ref, zf_ref, vf_ref, lbf_ref[0], sf_ref[...], False)
    of_ref[...] = o.astype(of_ref.dtype)
    sf_ref[...] = sf
    o, sb = _hgrn_direction(qb_ref, zb_ref, vb_ref, lbb_ref[0], sb_ref[...], True)
    ob_ref[...] = o.astype(ob_ref.dtype)
    sb_ref[...] = sb


def _hgrn(proj, lbs, batch, seq):
    n = proj.shape[0]
    t = TOKEN_TILE
    nblk = seq // t

    def fwd(colblk):
        return pl.BlockSpec((t, HG_DIM), lambda b, h, i: (b * nblk + i, colblk * HG_HEADS + h))

    def bwd(colblk):
        return pl.BlockSpec((t, HG_DIM), lambda b, h, i: (b * nblk + nblk - 1 - i, colblk * HG_HEADS + h))

    out_f = pl.BlockSpec((t, HG_DIM), lambda b, h, i: (b * nblk + i, h))
    out_b = pl.BlockSpec((t, HG_DIM), lambda b, h, i: (b * nblk + nblk - 1 - i, h))
    return pl.pallas_call(
        _hgrn_kernel,
        grid=(batch, HG_HEADS, nblk),
        in_specs=[
            fwd(COL_HQ), fwd(COL_HF_FWD), fwd(COL_HI),
            bwd(COL_HQ), bwd(COL_HF_BWD), bwd(COL_HI),
            pl.BlockSpec((1, 1, HG_DIM), lambda b, h, i: (h, 0, 0)),
            pl.BlockSpec((1, 1, HG_DIM), lambda b, h, i: (HG_HEADS + h, 0, 0)),
        ],
        out_specs=[out_f, out_b],
        out_shape=[jax.ShapeDtypeStruct((n, HG_WIDTH), BF16)] * 2,
        scratch_shapes=[pltpu.VMEM((HG_DIM, HG_DIM), F32)] * 2,
        compiler_params=pltpu.CompilerParams(
            dimension_semantics=("parallel", "parallel", "arbitrary"), vmem_limit_bytes=VMEM_LIMIT),
        name="hgrn",
    )(proj, proj, proj, proj, proj, proj, lbs, lbs)


def _attn_kernel(q_ref, kc_ref, kp_ref, kn_ref, vc_ref, vp_ref, vn_ref, o_ref, l_ref,
                 kw_ref, vw_ref, *, tq, length):
    n = pl.program_id(2)
    half = ATT_HALF
    kw_ref[0:half, :] = kp_ref[...]
    kw_ref[half:half + tq, :] = kc_ref[...]
    kw_ref[half + tq:, :] = kn_ref[...]
    vw_ref[0:half, :] = vp_ref[...]
    vw_ref[half:half + tq, :] = vc_ref[...]
    vw_ref[half + tq:, :] = vn_ref[...]

    qb_rows = ATT_QBLOCK
    win = qb_rows + 2 * half
    i_idx = lax.broadcasted_iota(jnp.int32, (qb_rows, win), 0)
    j_idx = lax.broadcasted_iota(jnp.int32, (qb_rows, win), 1)
    band = (j_idx >= i_idx) & (j_idx <= i_idx + 2 * half)

    def body(qb, carry):
        r0 = pl.multiple_of(qb * qb_rows, qb_rows)
        base = n * tq + r0 - half
        valid = band & (j_idx >= -base) & (j_idx < length - base)
        q = q_ref[pl.ds(r0, qb_rows), :]
        kw = kw_ref[pl.ds(r0, win), :]
        vw = vw_ref[pl.ds(r0, win), :]
        heads = [slice(h * ATT_DIM, (h + 1) * ATT_DIM) for h in range(ATT_HEADS)]
        scores = [jnp.where(valid, _dot_nt(q[:, cs], kw[:, cs]), NEG) for cs in heads]
        maxes = [jnp.max(s, axis=-1, keepdims=True) for s in scores]
        probs = [jnp.exp(s - m) for s, m in zip(scores, maxes)]
        dens = [jnp.sum(p, axis=-1, keepdims=True) for p in probs]
        outs = [_dot(p.astype(BF16), vw[:, cs]) / den for p, cs, den in zip(probs, heads, dens)]
        lses = [jnp.broadcast_to(m + jnp.log(den), (qb_rows, ATT_DIM)) for m, den in zip(maxes, dens)]
        o_ref[pl.ds(r0, qb_rows), :] = jnp.concatenate(outs, axis=1).astype(o_ref.dtype)
        l_ref[pl.ds(r0, qb_rows), :] = jnp.concatenate(lses, axis=1)
        return carry

    lax.fori_loop(0, tq // qb_rows, body, 0)


def _attention(qkv, col0):
    batch, dil, length, _ = qkv.shape
    tq = min(TOKEN_TILE, length)
    nq = length // tq
    hb = tq // ATT_HALF
    n_hblk = length // ATT_HALF

    def cur(col):
        return pl.BlockSpec((None, None, tq, ATT_WIDTH), lambda b, r, n: (b, r, n, col))

    def prev(col):
        return pl.BlockSpec((None, None, ATT_HALF, ATT_WIDTH),
                            lambda b, r, n: (b, r, jnp.maximum(n * hb - 1, 0), col))

    def nxt(col):
        return pl.BlockSpec((None, None, ATT_HALF, ATT_WIDTH),
                            lambda b, r, n: (b, r, jnp.minimum((n + 1) * hb, n_hblk - 1), col))

    out_spec = pl.BlockSpec((None, None, tq, ATT_WIDTH), lambda b, r, n: (b, r, n, 0))
    return pl.pallas_call(
        functools.partial(_attn_kernel, tq=tq, length=length),
        grid=(batch, dil, nq),
        in_specs=[cur(col0), cur(col0 + 1), prev(col0 + 1), nxt(col0 + 1),
                  cur(col0 + 2), prev(col0 + 2), nxt(col0 + 2)],
        out_specs=[out_spec, out_spec],
        out_shape=[jax.ShapeDtypeStruct((batch, dil, length, ATT_WIDTH), BF16),
                   jax.ShapeDtypeStruct((batch, dil, length, ATT_WIDTH), F32)],
        scratch_shapes=[pltpu.VMEM((tq + 2 * ATT_HALF, ATT_WIDTH), BF16)] * 2,
        compiler_params=pltpu.CompilerParams(
            dimension_semantics=("parallel", "parallel", "parallel"), vmem_limit_bytes=VMEM_LIMIT),
        name=f"attn_d{dil}",
    )(qkv, qkv, qkv, qkv, qkv, qkv, qkv)


def _token_major(src_ref, stage_ref):
    dil, rows = src_ref.shape[1], src_ref.shape[2]
    nc = src_ref.shape[3] // LANES
    for r in range(dil):
        for c in range(nc):
            stage_ref[c, pl.ds(r, rows, stride=dil), :] = (
                src_ref[0, r, :, c * LANES:(c + 1) * LANES].astype(F32))
    return jnp.concatenate([stage_ref[c] for c in range(nc)], axis=1)


def _outproj_kernel(of_ref, ob_ref, hg_ref, o1_ref, o2_ref, o3_ref, l1_ref, l2_ref, l3_ref,
                    x_ref, og_ref, w_ref, g2_ref, wrh_ref, wrl_ref, br_ref,
                    x2_ref, xn_ref, gate_ref, idx_ref, st_o2, st_o3, st_l2, st_l3):
    o = of_ref[...].astype(F32) + ob_ref[...].astype(F32)
    hg = hg_ref[...].astype(F32)
    parts = []
    for h in range(HG_HEADS):
        blk = o[:, h * HG_DIM:(h + 1) * HG_DIM]
        ms = jnp.mean(blk * blk, axis=-1, keepdims=True)
        parts.append(blk * lax.rsqrt(ms + EPS) * og_ref[...])
    o_hg = jnp.concatenate(parts, axis=1) * (hg * _sigmoid(hg))

    l1 = l1_ref[...]
    l2 = _token_major(l2_ref, st_l2)
    l3 = _token_major(l3_ref, st_l3)
    mx = jnp.maximum(jnp.maximum(l1, l2), l3)
    e1, e2, e3 = jnp.exp(l1 - mx), jnp.exp(l2 - mx), jnp.exp(l3 - mx)
    o_att = (e1 * o1_ref[...].astype(F32) + e2 * _token_major(o2_ref, st_o2)
             + e3 * _token_major(o3_ref, st_o3)) / (e1 + e2 + e3)

    y = _dot(o_hg.astype(BF16), w_ref[0:HG_WIDTH, :]) + _dot(o_att.astype(BF16), w_ref[HG_WIDTH:, :])
    x2 = x_ref[...] + y
    x2_ref[...] = x2

    ms = jnp.mean(x2 * x2, axis=-1, keepdims=True)
    xn = x2 * lax.rsqrt(ms + EPS) * g2_ref[...]
    _rows_to_tiles(xn_ref, xn)
    xn_hi = xn.astype(BF16)
    xn_lo = (xn - xn_hi.astype(F32)).astype(BF16)
    logits = (_dot(xn_hi, wrh_ref[...]) + _dot(xn_lo, wrh_ref[...]) + _dot(xn_hi, wrl_ref[...])
              + br_ref[...])

    lane = lax.broadcasted_iota(jnp.int32, logits.shape, 1)
    lane_f = lane.astype(F32)
    work = jnp.where(lane < N_EXPERTS, logits, -jnp.inf)
    vals, idxs = [], []
    for _ in range(TOP_K):
        m = jnp.max(work, axis=-1, keepdims=True)
        idx = jnp.min(jnp.where(work == m, lane_f, float(LANES)), axis=-1, keepdims=True)
        vals.append(m)
        idxs.append(idx)
        work = jnp.where(lane_f == idx, -jnp.inf, work)
    es = [jnp.exp(v - vals[0]) for v in vals]
    den = es[0] + es[1] + es[2] + es[3]
    gate_out = jnp.zeros(logits.shape, F32)
    idx_out = jnp.zeros(logits.shape, F32)
    for k in range(TOP_K):
        gate_out = jnp.where(lane == k, es[k] / den, gate_out)
        idx_out = jnp.where(lane == k, idxs[k], idx_out)
    gate_ref[...] = gate_out
    idx_ref[...] = idx_out.astype(jnp.int32)


def _outproj(o_f, o_b, proj, atts, lses, x2d, og, w_out_bf16, g2, wr_hi, wr_lo, br, seq):
    n = x2d.shape[0]
    t = TOKEN_TILE
    nt = seq // t
    row = lambda i: (i, 0)
    const = lambda i: (0, 0)
    half = pl.BlockSpec((t, 512), row)

    def residue_major(dil):
        return pl.BlockSpec((1, dil, t // dil, ATT_WIDTH), lambda i: (i // nt, 0, i % nt, 0))

    rm4, rm16 = residue_major(4), residue_major(16)
    return pl.pallas_call(
        _outproj_kernel,
        grid=(n // t,),
        in_specs=[
            half, half, pl.BlockSpec((t, 512), lambda i: (i, COL_HG)),
            half, rm4, rm16, half, rm4, rm16,
            pl.BlockSpec((t, D_MODEL), row),
            pl.BlockSpec((1, HG_DIM), const),
            pl.BlockSpec((D_MODEL, D_MODEL), const),
            pl.BlockSpec((1, D_MODEL), const),
            pl.BlockSpec((D_MODEL, LANES), const),
            pl.BlockSpec((D_MODEL, LANES), const),
            pl.BlockSpec((1, LANES), const),
        ],
        out_specs=[pl.BlockSpec((t, D_MODEL), row), pl.BlockSpec((t * ROW_TILES, LANES), row),
                   pl.BlockSpec((t, LANES), row), pl.BlockSpec((t, LANES), row)],
        out_shape=[jax.ShapeDtypeStruct((n, D_MODEL), F32), jax.ShapeDtypeStruct((n * ROW_TILES, LANES), F32),
                   jax.ShapeDtypeStruct((n, LANES), F32), jax.ShapeDtypeStruct((n, LANES), jnp.int32)],
        scratch_shapes=[pltpu.VMEM((ATT_WIDTH // LANES, t, LANES), F32)] * 4,
        compiler_params=pltpu.CompilerParams(
            dimension_semantics=("parallel",), vmem_limit_bytes=VMEM_LIMIT),
        name="outproj",
    )(o_f, o_b, proj, *atts, *lses, x2d, og, w_out_bf16, g2, wr_hi, wr_lo, br)


def _moe_kernel(bexp_ref, bfirst_ref, nused_ref, xs_ref, wu_ref, bg_ref, bl_ref, wd_ref, bd_ref,
                y_ref, wg_s, wl_s, wd_s):
    i = pl.program_id(0)

    @pl.when(bfirst_ref[i] == 1)
    def _():
        r = lax.broadcasted_iota(jnp.int32, (2 * LANES, 2 * LANES), 0)
        c = lax.broadcasted_iota(jnp.int32, (2 * LANES, 2 * LANES), 1)
        src = jnp.where(c < LANES, 2 * c, 2 * (c - LANES) + 1)
        perm = jnp.where(r == src, 1.0, 0.0).astype(BF16)
        rows = 256
        for rb in range(D_MODEL // rows):
            rs = slice(rb * rows, (rb + 1) * rows)
            for cb in range(D_EXPERT // LANES):
                w = wu_ref[0, rs, cb * 2 * LANES:(cb + 1) * 2 * LANES].astype(BF16)
                split = _dot(w, perm).astype(BF16)
                wg_s[rs, cb * LANES:(cb + 1) * LANES] = split[:, :LANES]
                wl_s[rs, cb * LANES:(cb + 1) * LANES] = split[:, LANES:]
        wd_s[...] = wd_ref[0].astype(BF16)

    @pl.when(i < nused_ref[0])
    def _():
        x = jnp.concatenate(_tiles_to_rows(xs_ref, MOE_BLOCK_ROWS), axis=1).astype(BF16)
        hglu = _dot(x, wg_s[...]) + bg_ref[0]
        hlin = _dot(x, wl_s[...]) + bl_ref[0]
        glu = jnp.minimum(hglu, SWIGLU_LIMIT)
        lin = jnp.clip(hlin, -SWIGLU_LIMIT, SWIGLU_LIMIT)
        act = glu * _sigmoid(SWIGLU_ALPHA * glu) * (lin + 1.0)
        _rows_to_tiles(y_ref, _dot(act.astype(BF16), wd_s[...]) + bd_ref[0])

    @pl.when(i >= nused_ref[0])
    def _():
        y_ref[...] = jnp.zeros_like(y_ref)


def _moe(block_exp, block_first, n_used, xs, w_up, b_glu, b_lin, w_down, b_down):
    p_rows = xs.shape[0] // ROW_TILES
    bm = MOE_BLOCK_ROWS
    nb = p_rows // bm
    exp3 = lambda i, be, bf, nu: (be[i], 0, 0)
    grid_spec = pltpu.PrefetchScalarGridSpec(
        num_scalar_prefetch=3,
        grid=(nb,),
        in_specs=[
            pl.BlockSpec((bm * ROW_TILES, LANES), lambda i, be, bf, nu: (jnp.minimum(i, nu[0] - 1), 0)),
            pl.BlockSpec((1, D_MODEL, 2 * D_EXPERT), exp3),
            pl.BlockSpec((1, 1, D_EXPERT), exp3),
            pl.BlockSpec((1, 1, D_EXPERT), exp3),
            pl.BlockSpec((1, D_EXPERT, D_MODEL), exp3),
            pl.BlockSpec((1, 1, D_MODEL), exp3),
        ],
        out_specs=pl.BlockSpec((bm * ROW_TILES, LANES), lambda i, be, bf, nu: (i, 0)),
        scratch_shapes=[pltpu.VMEM((D_MODEL, D_EXPERT), BF16), pltpu.VMEM((D_MODEL, D_EXPERT), BF16),
                        pltpu.VMEM((D_EXPERT, D_MODEL), BF16)],
    )
    return pl.pallas_call(
        _moe_kernel,
        grid_spec=grid_spec,
        out_shape=jax.ShapeDtypeStruct((p_rows * ROW_TILES, LANES), F32),
        compiler_params=pltpu.CompilerParams(
            dimension_semantics=("arbitrary",), vmem_limit_bytes=VMEM_LIMIT),
        name="moe",
    )(block_exp, block_first, n_used, xs, w_up, b_glu, b_lin, w_down, b_down)


def _dispatch_kernel(dest_ref, zstart_ref, xn_ref, xs_hbm, zero_ref, sem, zsem):
    i = pl.program_id(0)
    bm = MOE_BLOCK_ROWS
    tokens = xn_ref.shape[0] // ROW_TILES

    def zero_copy(e):
        start = pl.multiple_of(zstart_ref[e] * ROW_TILES, bm * ROW_TILES)
        return pltpu.make_async_copy(zero_ref, xs_hbm.at[pl.ds(start, bm * ROW_TILES)], zsem)

    @pl.when(i == 0)
    def _():
        zero_ref[...] = jnp.zeros_like(zero_ref)
        for e in range(N_EXPERTS):
            @pl.when(zstart_ref[e] >= 0)
            def _():
                zero_copy(e).start()
        for e in range(N_EXPERTS):
            @pl.when(zstart_ref[e] >= 0)
            def _():
                zero_copy(e).wait()

    t0 = i * tokens

    def body(j, carry):
        src = xn_ref.at[pl.ds(pl.multiple_of(j * ROW_TILES, ROW_TILES), ROW_TILES)]
        for k in range(TOP_K):
            d = pl.multiple_of(dest_ref[(t0 + j) * TOP_K + k] * ROW_TILES, ROW_TILES)
            pltpu.make_async_copy(src, xs_hbm.at[pl.ds(d, ROW_TILES)], sem).start()
        return carry

    lax.fori_loop(0, tokens, body, 0, unroll=8)
    for k in range(TOP_K):
        pltpu.make_async_copy(xn_ref, xs_hbm.at[pl.ds(0, tokens * ROW_TILES)], sem).wait()


def _dispatch(dest, zstart, xn, p_rows):
    n = xn.shape[0] // ROW_TILES
    t = DISPATCH_TOKENS
    grid_spec = pltpu.PrefetchScalarGridSpec(
        num_scalar_prefetch=2,
        grid=(n // t,),
        in_specs=[pl.BlockSpec((t * ROW_TILES, LANES), lambda i, d, z: (i, 0))],
        out_specs=pl.BlockSpec(memory_space=pl.ANY),
        scratch_shapes=[pltpu.VMEM((MOE_BLOCK_ROWS * ROW_TILES, LANES), F32),
                        pltpu.SemaphoreType.DMA, pltpu.SemaphoreType.DMA],
    )
    return pl.pallas_call(
        _dispatch_kernel,
        grid_spec=grid_spec,
        out_shape=jax.ShapeDtypeStruct((p_rows * ROW_TILES, LANES), F32),
        compiler_params=pltpu.CompilerParams(
            dimension_semantics=("arbitrary",), vmem_limit_bytes=VMEM_LIMIT),
        name="dispatch",
    )(dest, zstart, xn)


def _combine_kernel(dest_ref, y_hbm, x2_ref, gate_ref, out_ref, buf_ref, sems):
    i = pl.program_id(0)
    tc = COMBINE_TOKENS

    def issue(step, slot):
        t0 = step * tc

        def body(j, carry):
            for k in range(TOP_K):
                d = pl.multiple_of(dest_ref[(t0 + j) * TOP_K + k] * ROW_TILES, ROW_TILES)
                r = pl.multiple_of((k * tc + j) * ROW_TILES, ROW_TILES)
                pltpu.make_async_copy(y_hbm.at[pl.ds(d, ROW_TILES)], buf_ref.at[slot, pl.ds(r, ROW_TILES)],
                                      sems.at[slot]).start()
            return carry

        lax.fori_loop(0, tc, body, 0, unroll=8)

    @pl.when(i == 0)
    def _():
        issue(0, 0)

    @pl.when(i + 1 < pl.num_programs(0))
    def _():
        issue(i + 1, (i + 1) % 2)

    slot = i % 2
    pltpu.make_async_copy(y_hbm.at[pl.ds(0, TOP_K * tc * ROW_TILES)], buf_ref.at[slot], sems.at[slot]).wait()
    rows = buf_ref.at[slot]
    sub = 64
    for r0 in range(0, tc, sub):
        gate = gate_ref[r0:r0 + sub, :]
        gates = [jnp.broadcast_to(gate[:, k:k + 1], (sub, LANES)) for k in range(TOP_K)]
        for c in range(ROW_TILES):
            acc = x2_ref[r0:r0 + sub, c * LANES:(c + 1) * LANES]
            for k in range(TOP_K):
                acc = acc + gates[k] * rows[pl.ds((k * tc + r0) * ROW_TILES + c, sub, stride=ROW_TILES), :]
            out_ref[r0:r0 + sub, c * LANES:(c + 1) * LANES] = acc


def _combine(dest, y, x2, gates):
    n = x2.shape[0]
    tc = COMBINE_TOKENS
    grid_spec = pltpu.PrefetchScalarGridSpec(
        num_scalar_prefetch=1,
        grid=(n // tc,),
        in_specs=[pl.BlockSpec(memory_space=pl.ANY),
                  pl.BlockSpec((tc, D_MODEL), lambda i, d: (i, 0)),
                  pl.BlockSpec((tc, LANES), lambda i, d: (i, 0))],
        out_specs=pl.BlockSpec((tc, D_MODEL), lambda i, d: (i, 0)),
        scratch_shapes=[pltpu.VMEM((2, TOP_K * tc * ROW_TILES, LANES), F32), pltpu.SemaphoreType.DMA((2,))],
    )
    return pl.pallas_call(
        _combine_kernel,
        grid_spec=grid_spec,
        out_shape=jax.ShapeDtypeStruct((n, D_MODEL), F32),
        compiler_params=pltpu.CompilerParams(
            dimension_semantics=("arbitrary",), vmem_limit_bytes=VMEM_LIMIT),
        name="combine",
    )(dest, y, x2, gates)


def _route(top_idx):
    n = top_idx.shape[0]
    a = n * TOP_K
    bm = MOE_BLOCK_ROWS
    nb = a // bm + N_EXPERTS
    e_flat = top_idx.reshape(a)
    onehot = (e_flat[:, None] == jnp.arange(N_EXPERTS, dtype=jnp.int32)[None, :]).astype(jnp.int32)
    csum = jnp.cumsum(onehot, axis=0)
    counts = csum[-1]
    padded = ((counts + bm - 1) // bm) * bm
    pad_end = jnp.cumsum(padded)
    pad_start = pad_end - padded
    dest = jnp.sum(onehot * (csum - 1 + pad_start[None, :]), axis=1)
    n_used = (pad_end[-1] // bm).astype(jnp.int32)
    blk = jnp.arange(nb, dtype=jnp.int32)
    bexp = jnp.sum((pad_end[None, :] <= (blk * bm)[:, None]).astype(jnp.int32), axis=1)
    bexp = jnp.minimum(bexp, N_EXPERTS - 1)
    bexp = jnp.where(blk < n_used, bexp, bexp[jnp.maximum(n_used - 1, 0)])
    bfirst = jnp.concatenate([jnp.ones((1,), jnp.int32), (bexp[1:] != bexp[:-1]).astype(jnp.int32)])
    zstart = jnp.where(counts > 0, pad_end - bm, -1).astype(jnp.int32)
    return dest.astype(jnp.int32), zstart, bexp, bfirst, n_used.reshape(1)


def kernel(x, positions, norm1_g, w_in, q_norm_g, k_norm_g, hgrn_lower_bounds, hgrn_onorm_g,
           w_out, norm2_g, w_router, b_router, w_up, b_up, w_down, b_down):
    batch, seq, d = x.shape
    n = batch * seq
    depth = norm1_g.shape[0]
    lbs_all = jnp.cumsum(jax.nn.softmax(hgrn_lower_bounds.astype(F32), axis=0), axis=0)
    half = ATT_DIM // 2
    inv = 1.0 / (ROPE_THETA ** (jnp.arange(half, dtype=F32) / half))
    inv_tab = jnp.tile(inv, LANES // half).reshape(1, LANES)
    pos_col = positions.reshape(n, 1)

    x2d = x.reshape(n, d)
    for l in range(depth):
        lbs = lbs_all[l].reshape(2 * HG_HEADS, 1, HG_DIM)
        proj, qkv4, qkv16 = _inproj(
            x2d, pos_col, inv_tab, norm1_g[l].reshape(1, d), w_in[l].astype(BF16),
            jnp.tile(q_norm_g[l], LANES // ATT_DIM).reshape(1, LANES),
            jnp.tile(k_norm_g[l], LANES // ATT_DIM).reshape(1, LANES), batch, seq)
        o_f, o_b = _hgrn(proj, lbs, batch, seq)
        o1, l1 = _attention(proj.reshape(batch, 1, seq, IN_COLS), COL_AQ)
        o4, l4 = _attention(qkv4, 0)
        o16, l16 = _attention(qkv16, 0)
        atts = [o1.reshape(n, ATT_WIDTH), o4, o16]
        lses = [l1.reshape(n, ATT_WIDTH), l4, l16]

        wr = jnp.pad(w_router[l], ((0, 0), (0, LANES - N_EXPERTS)))
        wr_hi = wr.astype(BF16)
        wr_lo = (wr - wr_hi.astype(F32)).astype(BF16)
        br = jnp.pad(b_router[l], (0, LANES - N_EXPERTS)).reshape(1, LANES)
        x2, xn, gates, top_idx = _outproj(
            o_f, o_b, proj, atts, lses, x2d, hgrn_onorm_g[l].reshape(1, HG_DIM),
            w_out[l].astype(BF16), norm2_g[l].reshape(1, d), wr_hi, wr_lo, br, seq)
        dest, zstart, bexp, bfirst, n_used = _route(top_idx[:, :TOP_K])
        xs = _dispatch(dest, zstart, xn, bexp.shape[0] * MOE_BLOCK_ROWS)
        y = _moe(bexp, bfirst, n_used, xs, w_up[l],
                 b_up[l][:, 0::2].reshape(N_EXPERTS, 1, D_EXPERT),
                 b_up[l][:, 1::2].reshape(N_EXPERTS, 1, D_EXPERT),
                 w_down[l], b_down[l].reshape(N_EXPERTS, 1, D_MODEL))
        x2d = _combine(dest, y, x2, gates)
    return x2d.reshape(batch, seq, d)
```

```python
import functools

import jax
import jax.numpy as jnp
from jax import lax
from jax.experimental import pallas as pl
from jax.experimental.pallas import tpu as pltpu

F32 = jnp.float32
BF16 = jnp.bfloat16

D_MODEL = 1024
HG_HEADS = 4
HG_DIM = 128
HG_WIDTH = HG_HEADS * HG_DIM
HG_CHUNK = 64
ATT_HEADS = 8
ATT_DIM = 64
ATT_WIDTH = ATT_HEADS * ATT_DIM
DILATED_PATTERNS = ((128, 1), (512, 4), (2048, 16))
ATT_HALF = 64
ATT_QBLOCK = 128
ROPE_THETA = 10000.0
IN_COLS = 5 * HG_WIDTH + 3 * ATT_WIDTH
N_EXPERTS = 32
TOP_K = 4
D_EXPERT = D_MODEL
SWIGLU_LIMIT = 7.0
SWIGLU_ALPHA = 1.702
EPS = 1e-6
NEG = -1e30

COL_HQ, COL_HF_FWD, COL_HF_BWD, COL_HI, COL_HG, COL_AQ, COL_AK, COL_AV = range(8)

TOKEN_TILE = 512
MOE_BLOCK_ROWS = 512
DISPATCH_TOKENS = 512
COMBINE_TOKENS = 256
LANES = 128
ROW_TILES = D_MODEL // LANES
VMEM_LIMIT = 56 * 1024 * 1024


def _dot(a, b):
    return jnp.dot(a, b, preferred_element_type=F32)


def _dot_nt(a, b):
    return lax.dot_general(a, b, (((1,), (1,)), ((), ())), preferred_element_type=F32)


def _dot_tn(a, b):
    return lax.dot_general(a, b, (((0,), (0,)), ((), ())), preferred_element_type=F32)


def _sigmoid(x):
    return 1.0 / (1.0 + jnp.exp(-x))


def _rows_to_tiles(dst_ref, x):
    for c in range(ROW_TILES):
        dst_ref[pl.ds(c, x.shape[0], stride=ROW_TILES), :] = x[:, c * LANES:(c + 1) * LANES]


def _tiles_to_rows(src, rows, first_row=0):
    return [src[pl.ds(first_row * ROW_TILES + c, rows, stride=ROW_TILES), :] for c in range(ROW_TILES)]


def _head_norm_rope(p, gain, cos, sin_signed, scale):
    lane = lax.broadcasted_iota(jnp.int32, (p.shape[0], LANES), 1)
    low = lane < ATT_DIM
    first_half = (lane % ATT_DIM) < (ATT_DIM // 2)
    outs = []
    for t in range(ATT_WIDTH // LANES):
        blk = p[:, t * LANES:(t + 1) * LANES]
        sq = blk * blk
        s_low = jnp.sum(jnp.where(low, sq, 0.0), axis=-1, keepdims=True)
        s_high = jnp.sum(jnp.where(low, 0.0, sq), axis=-1, keepdims=True)
        r = jnp.where(low, lax.rsqrt(s_low * (1.0 / ATT_DIM) + EPS),
                      lax.rsqrt(s_high * (1.0 / ATT_DIM) + EPS))
        y = blk * r * gain
        partner = jnp.where(first_half, pltpu.roll(y, LANES - ATT_DIM // 2, axis=1),
                            pltpu.roll(y, ATT_DIM // 2, axis=1))
        outs.append((y * cos + partner * sin_signed) * scale)
    return jnp.concatenate(outs, axis=1)


def _inproj_kernel(x_ref, pos_ref, inv_ref, g1_ref, w_ref, qg_ref, kg_ref,
                   out_ref, d4_ref, d16_ref, stage_ref, stage2_ref):
    x = x_ref[...]
    ms = jnp.mean(x * x, axis=-1, keepdims=True)
    h = (x * lax.rsqrt(ms + EPS) * g1_ref[...]).astype(BF16)
    ang = pos_ref[...].astype(F32) * inv_ref[...]
    lane = lax.broadcasted_iota(jnp.int32, ang.shape, 1)
    cos = jnp.cos(ang)
    sin_signed = jnp.where((lane % ATT_DIM) < (ATT_DIM // 2), -jnp.sin(ang), jnp.sin(ang))
    for j in range(IN_COLS // 512):
        p = _dot(h, w_ref[:, j * 512:(j + 1) * 512])
        if j == COL_AQ:
            p = _head_norm_rope(p, qg_ref[...], cos, sin_signed, ATT_DIM ** -0.5)
        elif j == COL_AK:
            p = _head_norm_rope(p, kg_ref[...], cos, sin_signed, 1.0)
        out_ref[:, j * 512:(j + 1) * 512] = p.astype(BF16)
        if j >= COL_AQ:
            rows4, rows16 = x.shape[0] // 4, x.shape[0] // 16
            for c in range(ATT_WIDTH // LANES):
                cols = slice((j - COL_AQ) * ATT_WIDTH + c * LANES, (j - COL_AQ) * ATT_WIDTH + (c + 1) * LANES)
                stage_ref[c] = p[:, c * LANES:(c + 1) * LANES]
                for r4 in range(4):
                    group = stage_ref[c, pl.ds(r4, rows4, stride=4), :]
                    d4_ref[0, r4, :, cols] = group.astype(BF16)
                    stage2_ref[c, r4 * rows4:(r4 + 1) * rows4, :] = group
                for r4 in range(4):
                    for m in range(4):
                        d16_ref[0, r4 + 4 * m, :, cols] = (
                            stage2_ref[c, pl.ds(r4 * rows4 + m, rows16, stride=4), :].astype(BF16))


def _inproj(x2d, pos_col, inv_tab, g1, w_in_bf16, qg, kg, batch, seq):
    n = x2d.shape[0]
    t = TOKEN_TILE
    nt = seq // t
    const = lambda i: (0, 0)
    qkv = 3 * ATT_WIDTH

    def residue_major(dil):
        spec = pl.BlockSpec((1, dil, t // dil, qkv), lambda i: (i // nt, 0, i % nt, 0))
        return spec, jax.ShapeDtypeStruct((batch, dil, seq // dil, qkv), BF16)

    spec4, shape4 = residue_major(4)
    spec16, shape16 = residue_major(16)
    return pl.pallas_call(
        _inproj_kernel,
        grid=(n // t,),
        in_specs=[
            pl.BlockSpec((t, D_MODEL), lambda i: (i, 0)),
            pl.BlockSpec((t, 1), lambda i: (i, 0)),
            pl.BlockSpec((1, LANES), const),
            pl.BlockSpec((1, D_MODEL), const),
            pl.BlockSpec((D_MODEL, IN_COLS), const),
            pl.BlockSpec((1, LANES), const),
            pl.BlockSpec((1, LANES), const),
        ],
        out_specs=[pl.BlockSpec((t, IN_COLS), lambda i: (i, 0)), spec4, spec16],
        out_shape=[jax.ShapeDtypeStruct((n, IN_COLS), BF16), shape4, shape16],
        scratch_shapes=[pltpu.VMEM((ATT_WIDTH // LANES, t, LANES), F32)] * 2,
        compiler_params=pltpu.CompilerParams(
            dimension_semantics=("parallel",), vmem_limit_bytes=VMEM_LIMIT),
        name="inproj",
    )(x2d, pos_col, inv_tab, g1, w_in_bf16, qg, kg)


def _hgrn_direction(q_ref, z_ref, v_ref, lb, state_t, reverse):
    c = HG_CHUNK
    t = q_ref.shape[0]
    n = t // c
    row = lax.broadcasted_iota(jnp.int32, (c, c), 0)
    col = lax.broadcasted_iota(jnp.int32, (c, c), 1)
    mask = (row <= col) if reverse else (row >= col)
    tri = jnp.where(mask, 1.0, 0.0).astype(BF16)
    last_row = 0 if reverse else c - 1

    z = z_ref[...].astype(F32)
    q = q_ref[...].astype(F32)
    v = v_ref[...]
    sg = _sigmoid(z)
    f = lb + (1.0 - lb) * sg
    k = (1.0 - lb) * (1.0 - sg)
    lf = jnp.log(f)
    lf_hi = lf.astype(BF16)
    lf_lo = (lf - lf_hi.astype(F32)).astype(BF16)
    chunks = [slice(j * c, (j + 1) * c) for j in range(n)]
    b = jnp.concatenate([_dot(tri, lf_hi[rs]) + _dot(tri, lf_lo[rs]) for rs in chunks], axis=0)
    b_last = b.reshape(n, c, HG_DIM)[:, last_row:last_row + 1, :]
    decay = jnp.exp(b_last)
    qt = (q * _sigmoid(q) * jnp.exp(b)).astype(BF16)
    kt_f32 = k * jnp.exp(-b)
    kt = kt_f32.astype(BF16)
    kd = (kt_f32.reshape(n, c, HG_DIM) * decay).reshape(t, HG_DIM).astype(BF16)

    outs, updates = [], []
    for rs in chunks:
        a = jnp.where(mask, _dot_nt(qt[rs], kt[rs]), 0.0)
        outs.append(_dot(a.astype(BF16), v[rs]))
        updates.append(_dot_tn(v[rs], kd[rs]))
    for j in (reversed(range(n)) if reverse else range(n)):
        outs[j] = outs[j] + _dot_nt(qt[chunks[j]], state_t.astype(BF16))
        state_t = state_t * decay[j] + updates[j]
    return jnp.concatenate(outs, axis=0), state_t


def _hgrn_kernel(qf_ref, zf_ref, vf_ref, qb_ref, zb_ref, vb_ref, lbf_ref, lbb_ref,
                 of_ref, ob_ref, sf_ref, sb_ref):
    @pl.when(pl.program_id(2) == 0)
    def _():
        sf_ref[...] = jnp.zeros_like(sf_ref)
        sb_ref[...] = jnp.zeros_like(sb_ref)

    o, sf = _hgrn_direction(qf_ref, zf_ref, vf_ref, lbf_ref[0], sf_ref[...], False)
    of_ref[...] = o.astype(of_ref.dtype)
    sf_ref[...] = sf
    o, sb = _hgrn_direction(qb_ref, zb_ref, vb_ref, lbb_ref[0], sb_ref[...], True)
    ob_ref[...] = o.astype(ob_ref.dtype)
    sb_ref[...] = sb


def _hgrn(proj, lbs, batch, seq):
    n = proj.shape[0]
    t = TOKEN_TILE
    nblk = seq // t

    def fwd(colblk):
        return pl.BlockSpec((t, HG_DIM), lambda b, h, i: (b * nblk + i, colblk * HG_HEADS + h))

    def bwd(colblk):
        return pl.BlockSpec((t, HG_DIM), lambda b, h, i: (b * nblk + nblk - 1 - i, colblk * HG_HEADS + h))

    out_f = pl.BlockSpec((t, HG_DIM), lambda b, h, i: (b * nblk + i, h))
    out_b = pl.BlockSpec((t, HG_DIM), lambda b, h, i: (b * nblk + nblk - 1 - i, h))
    return pl.pallas_call(
        _hgrn_kernel,
        grid=(batch, HG_HEADS, nblk),
        in_specs=[
            fwd(COL_HQ), fwd(COL_HF_FWD), fwd(COL_HI),
            bwd(COL_HQ), bwd(COL_HF_BWD), bwd(COL_HI),
            pl.BlockSpec((1, 1, HG_DIM), lambda b, h, i: (h, 0, 0)),
            pl.BlockSpec((1, 1, HG_DIM), lambda b, h, i: (HG_HEADS + h, 0, 0)),
        ],
        out_specs=[out_f, out_b],
        out_shape=[jax.ShapeDtypeStruct((n, HG_WIDTH), BF16)] * 2,
        scratch_shapes=[pltpu.VMEM((HG_DIM, HG_DIM), F32)] * 2,
        compiler_params=pltpu.CompilerParams(
            dimension_semantics=("parallel", "parallel", "arbitrary"), vmem_limit_bytes=VMEM_LIMIT),
        name="hgrn",
    )(proj, proj, proj, proj, proj, proj, lbs, lbs)


def _attn_kernel(q_ref, kc_ref, kp_ref, kn_ref, vc_ref, vp_ref, vn_ref, o_ref, l_ref,
                 kw_ref, vw_ref, *, tq, length):
    n = pl.program_id(2)
    half = ATT_HALF
    kw_ref[0:half, :] = kp_ref[...]
    kw_ref[half:half + tq, :] = kc_ref[...]
    kw_ref[half + tq:, :] = kn_ref[...]
    vw_ref[0:half, :] = vp_ref[...]
    vw_ref[half:half + tq, :] = vc_ref[...]
    vw_ref[half + tq:, :] = vn_ref[...]

    qb_rows = ATT_QBLOCK
    win = qb_rows + 2 * half
    i_idx = lax.broadcasted_iota(jnp.int32, (qb_rows, win), 0)
    j_idx = lax.broadcasted_iota(jnp.int32, (qb_rows, win), 1)
    band = (j_idx >= i_idx) & (j_idx <= i_idx + 2 * half)

    def body(qb, carry):
        r0 = pl.multiple_of(qb * qb_rows, qb_rows)
        base = n * tq + r0 - half
        valid = band & (j_idx >= -base) & (j_idx < length - base)
        q = q_ref[pl.ds(r0, qb_rows), :]
        kw = kw_ref[pl.ds(r0, win), :]
        vw = vw_ref[pl.ds(r0, win), :]
        heads = [slice(h * ATT_DIM, (h + 1) * ATT_DIM) for h in range(ATT_HEADS)]
        scores = [jnp.where(valid, _dot_nt(q[:, cs], kw[:, cs]), NEG) for cs in heads]
        maxes = [jnp.max(s, axis=-1, keepdims=True) for s in scores]
        probs = [jnp.exp(s - m) for s, m in zip(scores, maxes)]
        dens = [jnp.sum(p, axis=-1, keepdims=True) for p in probs]
        outs = [_dot(p.astype(BF16), vw[:, cs]) / den for p, cs, den in zip(probs, heads, dens)]
        lses = [jnp.broadcast_to(m + jnp.log(den), (qb_rows, ATT_DIM)) for m, den in zip(maxes, dens)]
        o_ref[pl.ds(r0, qb_rows), :] = jnp.concatenate(outs, axis=1).astype(o_ref.dtype)
        l_ref[pl.ds(r0, qb_rows), :] = jnp.concatenate(lses, axis=1)
        return carry

    lax.fori_loop(0, tq // qb_rows, body, 0)


def _attention(qkv, col0):
    batch, dil, length, _ = qkv.shape
    tq = min(TOKEN_TILE, length)
    nq = length // tq
    hb = tq // ATT_HALF
    n_hblk = length // ATT_HALF

    def cur(col):
        return pl.BlockSpec((None, None, tq, ATT_WIDTH), lambda b, r, n: (b, r, n, col))

    def prev(col):
        return pl.BlockSpec((None, None, ATT_HALF, ATT_WIDTH),
                            lambda b, r, n: (b, r, jnp.maximum(n * hb - 1, 0), col))

    def nxt(col):
        return pl.BlockSpec((None, None, ATT_HALF, ATT_WIDTH),
                            lambda b, r, n: (b, r, jnp.minimum((n + 1) * hb, n_hblk - 1), col))

    out_spec = pl.BlockSpec((None, None, tq, ATT_WIDTH), lambda b, r, n: (b, r, n, 0))
    return pl.pallas_call(
        functools.partial(_attn_kernel, tq=tq, length=length),
        grid=(batch, dil, nq),
        in_specs=[cur(col0), cur(col0 + 1), prev(col0 + 1), nxt(col0 + 1),
                  cur(col0 + 2), prev(col0 + 2), nxt(col0 + 2)],
        out_specs=[out_spec, out_spec],
        out_shape=[jax.ShapeDtypeStruct((batch, dil, length, ATT_WIDTH), BF16),
                   jax.ShapeDtypeStruct((batch, dil, length, ATT_WIDTH), F32)],
        scratch_shapes=[pltpu.VMEM((tq + 2 * ATT_HALF, ATT_WIDTH), BF16)] * 2,
        compiler_params=pltpu.CompilerParams(
            dimension_semantics=("parallel", "parallel", "parallel"), vmem_limit_bytes=VMEM_LIMIT),
        name=f"attn_d{dil}",
    )(qkv, qkv, qkv, qkv, qkv, qkv, qkv)


def _token_major(src_ref, stage_ref, tmp_ref):
    dil, rows = src_ref.shape[1], src_ref.shape[2]
    nc = src_ref.shape[3] // LANES
    for c in range(nc):
        cols = slice(c * LANES, (c + 1) * LANES)
        if dil == 4:
            for r in range(dil):
                stage_ref[c, pl.ds(r, rows, stride=dil), :] = src_ref[0, r, :, cols].astype(F32)
        else:
            group = 4 * rows
            for r4 in range(4):
                for m in range(4):
                    tmp_ref[c, pl.ds(r4 * group + m, rows, stride=4), :] = (
                        src_ref[0, r4 + 4 * m, :, cols].astype(F32))
            for r4 in range(4):
                stage_ref[c, pl.ds(r4, group, stride=4), :] = tmp_ref[c, r4 * group:(r4 + 1) * group, :]
    return jnp.concatenate([stage_ref[c] for c in range(nc)], axis=1)


def _outproj_kernel(of_ref, ob_ref, hg_ref, o1_ref, o2_ref, o3_ref, l1_ref, l2_ref, l3_ref,
                    x_ref, og_ref, w_ref, g2_ref, wrh_ref, wrl_ref, br_ref,
                    x2_ref, xn_ref, gate_ref, idx_ref, st_o2, st_o3, st_l2, st_l3, st_tmp_o, st_tmp_l):
    o = of_ref[...].astype(F32) + ob_ref[...].astype(F32)
    hg = hg_ref[...].astype(F32)
    parts = []
    for h in range(HG_HEADS):
        blk = o[:, h * HG_DIM:(h + 1) * HG_DIM]
        ms = jnp.mean(blk * blk, axis=-1, keepdims=True)
        parts.append(blk * lax.rsqrt(ms + EPS) * og_ref[...])
    o_hg = jnp.concatenate(parts, axis=1) * (hg * _sigmoid(hg))

    l1 = l1_ref[...]
    l2 = _token_major(l2_ref, st_l2, st_tmp_l)
    l3 = _token_major(l3_ref, st_l3, st_tmp_l)
    mx = jnp.maximum(jnp.maximum(l1, l2), l3)
    e1, e2, e3 = jnp.exp(l1 - mx), jnp.exp(l2 - mx), jnp.exp(l3 - mx)
    o_att = (e1 * o1_ref[...].astype(F32) + e2 * _token_major(o2_ref, st_o2, st_tmp_o)
             + e3 * _token_major(o3_ref, st_o3, st_tmp_o)) / (e1 + e2 + e3)

    y = _dot(o_hg.astype(BF16), w_ref[0:HG_WIDTH, :]) + _dot(o_att.astype(BF16), w_ref[HG_WIDTH:, :])
    x2 = x_ref[...] + y
    x2_ref[...] = x2

    ms = jnp.mean(x2 * x2, axis=-1, keepdims=True)
    xn = x2 * lax.rsqrt(ms + EPS) * g2_ref[...]
    _rows_to_tiles(xn_ref, xn)
    xn_hi = xn.astype(BF16)
    xn_lo = (xn - xn_hi.astype(F32)).astype(BF16)
    logits = (_dot(xn_hi, wrh_ref[...]) + _dot(xn_lo, wrh_ref[...]) + _dot(xn_hi, wrl_ref[...])
              + br_ref[...])

    lane = lax.broadcasted_iota(jnp.int32, logits.shape, 1)
    lane_f = lane.astype(F32)
    work = jnp.where(lane < N_EXPERTS, logits, -jnp.inf)
    vals, idxs = [], []
    for _ in range(TOP_K):
        m = jnp.max(work, axis=-1, keepdims=True)
        idx = jnp.min(jnp.where(work == m, lane_f, float(LANES)), axis=-1, keepdims=True)
        vals.append(m)
        idxs.append(idx)
        work = jnp.where(lane_f == idx, -jnp.inf, work)
    es = [jnp.exp(v - vals[0]) for v in vals]
    den = es[0] + es[1] + es[2] + es[3]
    gate_out = jnp.zeros(logits.shape, F32)
    idx_out = jnp.zeros(logits.shape, F32)
    for k in range(TOP_K):
        gate_out = jnp.where(lane == k, es[k] / den, gate_out)
        idx_out = jnp.where(lane == k, idxs[k], idx_out)
    gate_ref[...] = gate_out
    idx_ref[...] = idx_out.astype(jnp.int32)


def _outproj(o_f, o_b, proj, atts, lses, x2d, og, w_out_bf16, g2, wr_hi, wr_lo, br, seq):
    n = x2d.shape[0]
    t = TOKEN_TILE
    nt = seq // t
    row = lambda i: (i, 0)
    const = lambda i: (0, 0)
    half = pl.BlockSpec((t, 512), row)

    def residue_major(dil):
        return pl.BlockSpec((1, dil, t // dil, ATT_WIDTH), lambda i: (i // nt, 0, i % nt, 0))

    rm4, rm16 = residue_major(4), residue_major(16)
    return pl.pallas_call(
        _outproj_kernel,
        grid=(n // t,),
        in_specs=[
            half, half, pl.BlockSpec((t, 512), lambda i: (i, COL_HG)),
            half, rm4, rm16, half, rm4, rm16,
            pl.BlockSpec((t, D_MODEL), row),
            pl.BlockSpec((1, HG_DIM), const),
            pl.BlockSpec((D_MODEL, D_MODEL), const),
            pl.BlockSpec((1, D_MODEL), const),
            pl.BlockSpec((D_MODEL, LANES), const),
            pl.BlockSpec((D_MODEL, LANES), const),
            pl.BlockSpec((1, LANES), const),
        ],
        out_specs=[pl.BlockSpec((t, D_MODEL), row), pl.BlockSpec((t * ROW_TILES, LANES), row),
                   pl.BlockSpec((t, LANES), row), pl.BlockSpec((t, LANES), row)],
        out_shape=[jax.ShapeDtypeStruct((n, D_MODEL), F32), jax.ShapeDtypeStruct((n * ROW_TILES, LANES), F32),
                   jax.ShapeDtypeStruct((n, LANES), F32), jax.ShapeDtypeStruct((n, LANES), jnp.int32)],
        scratch_shapes=[pltpu.VMEM((ATT_WIDTH // LANES, t, LANES), F32)] * 6,
        compiler_params=pltpu.CompilerParams(
            dimension_semantics=("parallel",), vmem_limit_bytes=VMEM_LIMIT),
        name="outproj",
    )(o_f, o_b, proj, *atts, *lses, x2d, og, w_out_bf16, g2, wr_hi, wr_lo, br)


def _moe_kernel(bexp_ref, bfirst_ref, nused_ref, xs_ref, wu_ref, bg_ref, bl_ref, wd_ref, bd_ref,
                y_ref, wg_s, wl_s, wd_s):
    i = pl.program_id(0)

    @pl.when(bfirst_ref[i] == 1)
    def _():
        r = lax.broadcasted_iota(jnp.int32, (2 * LANES, 2 * LANES), 0)
        c = lax.broadcasted_iota(jnp.int32, (2 * LANES, 2 * LANES), 1)
        src = jnp.where(c < LANES, 2 * c, 2 * (c - LANES) + 1)
        perm = jnp.where(r == src, 1.0, 0.0).astype(BF16)
        rows = 256
        for rb in range(D_MODEL // rows):
            rs = slice(rb * rows, (rb + 1) * rows)
            for cb in range(D_EXPERT // LANES):
                w = wu_ref[0, rs, cb * 2 * LANES:(cb + 1) * 2 * LANES].astype(BF16)
                split = _dot(w, perm).astype(BF16)
                wg_s[rs, cb * LANES:(cb + 1) * LANES] = split[:, :LANES]
                wl_s[rs, cb * LANES:(cb + 1) * LANES] = split[:, LANES:]
        wd_s[...] = wd_ref[0].astype(BF16)

    @pl.when(i < nused_ref[0])
    def _():
        x = jnp.concatenate(_tiles_to_rows(xs_ref, MOE_BLOCK_ROWS), axis=1).astype(BF16)
        hglu = _dot(x, wg_s[...]) + bg_ref[0]
        hlin = _dot(x, wl_s[...]) + bl_ref[0]
        glu = jnp.minimum(hglu, SWIGLU_LIMIT)
        lin = jnp.clip(hlin, -SWIGLU_LIMIT, SWIGLU_LIMIT)
        act = glu * _sigmoid(SWIGLU_ALPHA * glu) * (lin + 1.0)
        _rows_to_tiles(y_ref, _dot(act.astype(BF16), wd_s[...]) + bd_ref[0])

    @pl.when(i >= nused_ref[0])
    def _():
        y_ref[...] = jnp.zeros_like(y_ref)


def _moe(block_exp, block_first, n_used, xs, w_up, b_glu, b_lin, w_down, b_down):
    p_rows = xs.shape[0] // ROW_TILES
    bm = MOE_BLOCK_ROWS
    nb = p_rows // bm
    exp3 = lambda i, be, bf, nu: (be[i], 0, 0)
    grid_spec = pltpu.PrefetchScalarGridSpec(
        num_scalar_prefetch=3,
        grid=(nb,),
        in_specs=[
            pl.BlockSpec((bm * ROW_TILES, LANES), lambda i, be, bf, nu: (jnp.minimum(i, nu[0] - 1), 0)),
            pl.BlockSpec((1, D_MODEL, 2 * D_EXPERT), exp3),
            pl.BlockSpec((1, 1, D_EXPERT), exp3),
            pl.BlockSpec((1, 1, D_EXPERT), exp3),
            pl.BlockSpec((1, D_EXPERT, D_MODEL), exp3),
            pl.BlockSpec((1, 1, D_MODEL), exp3),
        ],
        out_specs=pl.BlockSpec((bm * ROW_TILES, LANES), lambda i, be, bf, nu: (i, 0)),
        scratch_shapes=[pltpu.VMEM((D_MODEL, D_EXPERT), BF16), pltpu.VMEM((D_MODEL, D_EXPERT), BF16),
                        pltpu.VMEM((D_EXPERT, D_MODEL), BF16)],
    )
    return pl.pallas_call(
        _moe_kernel,
        grid_spec=grid_spec,
        out_shape=jax.ShapeDtypeStruct((p_rows * ROW_TILES, LANES), F32),
        compiler_params=pltpu.CompilerParams(
            dimension_semantics=("arbitrary",), vmem_limit_bytes=VMEM_LIMIT),
        name="moe",
    )(block_exp, block_first, n_used, xs, w_up, b_glu, b_lin, w_down, b_down)


def _dispatch_kernel(dest_ref, zstart_ref, xn_ref, xs_hbm, zero_ref, sem, zsem):
    i = pl.program_id(0)
    bm = MOE_BLOCK_ROWS
    tokens = xn_ref.shape[0] // ROW_TILES

    def zero_copy(e):
        start = pl.multiple_of(zstart_ref[e] * ROW_TILES, bm * ROW_TILES)
        return pltpu.make_async_copy(zero_ref, xs_hbm.at[pl.ds(start, bm * ROW_TILES)], zsem)

    @pl.when(i == 0)
    def _():
        zero_ref[...] = jnp.zeros_like(zero_ref)
        for e in range(N_EXPERTS):
            @pl.when(zstart_ref[e] >= 0)
            def _():
                zero_copy(e).start()
        for e in range(N_EXPERTS):
            @pl.when(zstart_ref[e] >= 0)
            def _():
                zero_copy(e).wait()

    t0 = i * tokens

    def body(j, carry):
        src = xn_ref.at[pl.ds(pl.multiple_of(j * ROW_TILES, ROW_TILES), ROW_TILES)]
        for k in range(TOP_K):
            d = pl.multiple_of(dest_ref[(t0 + j) * TOP_K + k] * ROW_TILES, ROW_TILES)
            pltpu.make_async_copy(src, xs_hbm.at[pl.ds(d, ROW_TILES)], sem).start(priority=k % 2)
        return carry

    lax.fori_loop(0, tokens, body, 0, unroll=8)
    for k in range(TOP_K):
        pltpu.make_async_copy(xn_ref, xs_hbm.at[pl.ds(0, tokens * ROW_TILES)], sem).wait()


def _dispatch(dest, zstart, xn, p_rows):
    n = xn.shape[0] // ROW_TILES
    t = DISPATCH_TOKENS
    grid_spec = pltpu.PrefetchScalarGridSpec(
        num_scalar_prefetch=2,
        grid=(n // t,),
        in_specs=[pl.BlockSpec((t * ROW_TILES, LANES), lambda i, d, z: (i, 0))],
        out_specs=pl.BlockSpec(memory_space=pl.ANY),
        scratch_shapes=[pltpu.VMEM((MOE_BLOCK_ROWS * ROW_TILES, LANES), F32),
                        pltpu.SemaphoreType.DMA, pltpu.SemaphoreType.DMA],
    )
    return pl.pallas_call(
        _dispatch_kernel,
        grid_spec=grid_spec,
        out_shape=jax.ShapeDtypeStruct((p_rows * ROW_TILES, LANES), F32),
        compiler_params=pltpu.CompilerParams(
            dimension_semantics=("arbitrary",), vmem_limit_bytes=VMEM_LIMIT),
        name="dispatch",
    )(dest, zstart, xn)


def _combine_kernel(dest_ref, y_hbm, x2_ref, gate_ref, out_ref, buf_ref, sems):
    i = pl.program_id(0)
    tc = COMBINE_TOKENS

    def issue(step, slot):
        t0 = step * tc

        def body(j, carry):
            for k in range(TOP_K):
                d = pl.multiple_of(dest_ref[(t0 + j) * TOP_K + k] * ROW_TILES, ROW_TILES)
                r = pl.multiple_of((k * tc + j) * ROW_TILES, ROW_TILES)
                pltpu.make_async_copy(y_hbm.at[pl.ds(d, ROW_TILES)], buf_ref.at[slot, pl.ds(r, ROW_TILES)],
                                      sems.at[slot]).start(priority=k % 2)
            return carry

        lax.fori_loop(0, tc, body, 0, unroll=8)

    @pl.when(i == 0)
    def _():
        issue(0, 0)

    @pl.when(i + 1 < pl.num_programs(0))
    def _():
        issue(i + 1, (i + 1) % 2)

    slot = i % 2
    pltpu.make_async_copy(y_hbm.at[pl.ds(0, TOP_K * tc * ROW_TILES)], buf_ref.at[slot], sems.at[slot]).wait()
    rows = buf_ref.at[slot]
    sub = 64
    for r0 in range(0, tc, sub):
        gate = gate_ref[r0:r0 + sub, :]
        gates = [jnp.broadcast_to(gate[:, k:k + 1], (sub, LANES)) for k in range(TOP_K)]
        for c in range(ROW_TILES):
            acc = x2_ref[r0:r0 + sub, c * LANES:(c + 1) * LANES]
            for k in range(TOP_K):
                acc = acc + gates[k] * rows[pl.ds((k * tc + r0) * ROW_TILES + c, sub, stride=ROW_TILES), :]
            out_ref[r0:r0 + sub, c * LANES:(c + 1) * LANES] = acc


def _combine(dest, y, x2, gates):
    n = x2.shape[0]
    tc = COMBINE_TOKENS
    grid_spec = pltpu.PrefetchScalarGridSpec(
        num_scalar_prefetch=1,
        grid=(n // tc,),
        in_specs=[pl.BlockSpec(memory_space=pl.ANY),
                  pl.BlockSpec((tc, D_MODEL), lambda i, d: (i, 0)),
                  pl.BlockSpec((tc, LANES), lambda i, d: (i, 0))],
        out_specs=pl.BlockSpec((tc, D_MODEL), lambda i, d: (i, 0)),
        scratch_shapes=[pltpu.VMEM((2, TOP_K * tc * ROW_TILES, LANES), F32), pltpu.SemaphoreType.DMA((2,))],
    )
    return pl.pallas_call(
        _combine_kernel,
        grid_spec=grid_spec,
        out_shape=jax.ShapeDtypeStruct((n, D_MODEL), F32),
        compiler_params=pltpu.CompilerParams(
            dimension_semantics=("arbitrary",), vmem_limit_bytes=VMEM_LIMIT),
        name="combine",
    )(dest, y, x2, gates)


def _route(top_idx):
    n = top_idx.shape[0]
    a = n * TOP_K
    bm = MOE_BLOCK_ROWS
    nb = a // bm + N_EXPERTS
    e_flat = top_idx.reshape(a)
    onehot = (e_flat[:, None] == jnp.arange(N_EXPERTS, dtype=jnp.int32)[None, :]).astype(jnp.int32)
    csum = jnp.cumsum(onehot, axis=0)
    counts = csum[-1]
    padded = ((counts + bm - 1) // bm) * bm
    pad_end = jnp.cumsum(padded)
    pad_start = pad_end - padded
    dest = jnp.sum(onehot * (csum - 1 + pad_start[None, :]), axis=1)
    n_used = (pad_end[-1] // bm).astype(jnp.int32)
    blk = jnp.arange(nb, dtype=jnp.int32)
    bexp = jnp.sum((pad_end[None, :] <= (blk * bm)[:, None]).astype(jnp.int32), axis=1)
    bexp = jnp.minimum(bexp, N_EXPERTS - 1)
    bexp = jnp.where(blk < n_used, bexp, bexp[jnp.maximum(n_used - 1, 0)])
    bfirst = jnp.concatenate([jnp.ones((1,), jnp.int32), (bexp[1:] != bexp[:-1]).astype(jnp.int32)])
    zstart = jnp.where(counts > 0, pad_end - bm, -1).astype(jnp.int32)
    return dest.astype(jnp.int32), zstart, bexp, bfirst, n_used.reshape(1)


def kernel(x, positions, norm1_g, w_in, q_norm_g, k_norm_g, hgrn_lower_bounds, hgrn_onorm_g,
           w_out, norm2_g, w_router, b_router, w_up, b_up, w_down, b_down):
    batch, seq, d = x.shape
    n = batch * seq
    depth = norm1_g.shape[0]
    lbs_all = jnp.cumsum(jax.nn.softmax(hgrn_lower_bounds.astype(F32), axis=0), axis=0)
    half = ATT_DIM // 2
    inv = 1.0 / (ROPE_THETA ** (jnp.arange(half, dtype=F32) / half))
    inv_tab = jnp.tile(inv, LANES // half).reshape(1, LANES)
    pos_col = positions.reshape(n, 1)

    x2d = x.reshape(n, d)
    for l in range(depth):
        lbs = lbs_all[l].reshape(2 * HG_HEADS, 1, HG_DIM)
        proj, qkv4, qkv16 = _inproj(
            x2d, pos_col, inv_tab, norm1_g[l].reshape(1, d), w_in[l].astype(BF16),
            jnp.tile(q_norm_g[l], LANES // ATT_DIM).reshape(1, LANES),
            jnp.tile(k_norm_g[l], LANES // ATT_DIM).reshape(1, LANES), batch, seq)
        o_f, o_b = _hgrn(proj, lbs, batch, seq)
        o1, l1 = _attention(proj.reshape(batch, 1, seq, IN_COLS), COL_AQ)
        o4, l4 = _attention(qkv4, 0)
        o16, l16 = _attention(qkv16, 0)
        atts = [o1.reshape(n, ATT_WIDTH), o4, o16]
        lses = [l1.reshape(n, ATT_WIDTH), l4, l16]

        wr = jnp.pad(w_router[l], ((0, 0), (0, LANES - N_EXPERTS)))
        wr_hi = wr.astype(BF16)
        wr_lo = (wr - wr_hi.astype(F32)).astype(BF16)
        br = jnp.pad(b_router[l], (0, LANES - N_EXPERTS)).reshape(1, LANES)
        x2, xn, gates, top_idx = _outproj(
            o_f, o_b, proj, atts, lses, x2d, hgrn_onorm_g[l].reshape(1, HG_DIM),
            w_out[l].astype(BF16), norm2_g[l].reshape(1, d), wr_hi, wr_lo, br, seq)
        dest, zstart, bexp, bfirst, n_used = _route(top_idx[:, :TOP_K])
        xs = _dispatch(dest, zstart, xn, bexp.shape[0] * MOE_BLOCK_ROWS)
        y = _moe(bexp, bfirst, n_used, xs, w_up[l],
                 b_up[l][:, 0::2].reshape(N_EXPERTS, 1, D_EXPERT),
                 b_up[l][:, 1::2].reshape(N_EXPERTS, 1, D_EXPERT),
                 w_down[l], b_down[l].reshape(N_EXPERTS, 1, D_MODEL))
        x2d = _combine(dest, y, x2, gates)
    return x2d.reshape(batch, seq, d)
```

```python
import functools

import jax
import jax.numpy as jnp
from jax import lax
from jax.experimental import pallas as pl
from jax.experimental.pallas import tpu as pltpu

F32 = jnp.float32
BF16 = jnp.bfloat16

D_MODEL = 1024
HG_HEADS = 4
HG_DIM = 128
HG_WIDTH = HG_HEADS * HG_DIM
HG_CHUNK = 64
ATT_HEADS = 8
ATT_DIM = 64
ATT_WIDTH = ATT_HEADS * ATT_DIM
DILATED_PATTERNS = ((128, 1), (512, 4), (2048, 16))
ATT_HALF = 64
ATT_QBLOCK = 128
ROPE_THETA = 10000.0
IN_COLS = 5 * HG_WIDTH + 3 * ATT_WIDTH
N_EXPERTS = 32
TOP_K = 4
D_EXPERT = D_MODEL
SWIGLU_LIMIT = 7.0
SWIGLU_ALPHA = 1.702
EPS = 1e-6
NEG = -1e30

COL_HQ, COL_HF_FWD, COL_HF_BWD, COL_HI, COL_HG, COL_AQ, COL_AK, COL_AV = range(8)

TOKEN_TILE = 512
MOE_BLOCK_ROWS = 512
DISPATCH_TOKENS = 512
COMBINE_TOKENS = 256
LANES = 128
ROW_TILES = D_MODEL // LANES
VMEM_LIMIT = 56 * 1024 * 1024


def _dot(a, b):
    return jnp.dot(a, b, preferred_element_type=F32)


def _dot_nt(a, b):
    return lax.dot_general(a, b, (((1,), (1,)), ((), ())), preferred_element_type=F32)


def _dot_tn(a, b):
    return lax.dot_general(a, b, (((0,), (0,)), ((), ())), preferred_element_type=F32)


def _sigmoid(x):
    return 1.0 / (1.0 + jnp.exp(-x))


def _rows_to_tiles(dst_ref, x):
    for c in range(ROW_TILES):
        dst_ref[pl.ds(c, x.shape[0], stride=ROW_TILES), :] = x[:, c * LANES:(c + 1) * LANES]


def _tiles_to_rows(src, rows, first_row=0):
    return [src[pl.ds(first_row * ROW_TILES + c, rows, stride=ROW_TILES), :] for c in range(ROW_TILES)]


def _head_norm_rope(p, gain, cos, sin_signed, scale):
    lane = lax.broadcasted_iota(jnp.int32, (p.shape[0], LANES), 1)
    low = lane < ATT_DIM
    first_half = (lane % ATT_DIM) < (ATT_DIM // 2)
    outs = []
    for t in range(ATT_WIDTH // LANES):
        blk = p[:, t * LANES:(t + 1) * LANES]
        sq = blk * blk
        s_low = jnp.sum(jnp.where(low, sq, 0.0), axis=-1, keepdims=True)
        s_high = jnp.sum(jnp.where(low, 0.0, sq), axis=-1, keepdims=True)
        r = jnp.where(low, lax.rsqrt(s_low * (1.0 / ATT_DIM) + EPS),
                      lax.rsqrt(s_high * (1.0 / ATT_DIM) + EPS))
        y = blk * r * gain
        partner = jnp.where(first_half, pltpu.roll(y, LANES - ATT_DIM // 2, axis=1),
                            pltpu.roll(y, ATT_DIM // 2, axis=1))
        outs.append((y * cos + partner * sin_signed) * scale)
    return jnp.concatenate(outs, axis=1)


def _inproj_kernel(x_ref, pos_ref, inv_ref, g1_ref, w_ref, qg_ref, kg_ref,
                   out_ref, d4_ref, d16_ref, stage_ref, stage2_ref):
    x = x_ref[...]
    ms = jnp.mean(x * x, axis=-1, keepdims=True)
    h = (x * lax.rsqrt(ms + EPS) * g1_ref[...]).astype(BF16)
    ang = pos_ref[...].astype(F32) * inv_ref[...]
    lane = lax.broadcasted_iota(jnp.int32, ang.shape, 1)
    cos = jnp.cos(ang)
    sin_signed = jnp.where((lane % ATT_DIM) < (ATT_DIM // 2), -jnp.sin(ang), jnp.sin(ang))
    for j in range(IN_COLS // 512):
        p = _dot(h, w_ref[:, j * 512:(j + 1) * 512])
        if j == COL_AQ:
            p = _head_norm_rope(p, qg_ref[...], cos, sin_signed, ATT_DIM ** -0.5)
        elif j == COL_AK:
            p = _head_norm_rope(p, kg_ref[...], cos, sin_signed, 1.0)
        out_ref[:, j * 512:(j + 1) * 512] = p.astype(BF16)
        if j >= COL_AQ:
            rows4, rows16 = x.shape[0] // 4, x.shape[0] // 16
            for c in range(ATT_WIDTH // LANES):
                cols = slice((j - COL_AQ) * ATT_WIDTH + c * LANES, (j - COL_AQ) * ATT_WIDTH + (c + 1) * LANES)
                stage_ref[c] = p[:, c * LANES:(c + 1) * LANES]
                for r4 in range(4):
                    group = stage_ref[c, pl.ds(r4, rows4, stride=4), :]
                    d4_ref[0, r4, :, cols] = group.astype(BF16)
                    stage2_ref[c, r4 * rows4:(r4 + 1) * rows4, :] = group
                for r4 in range(4):
                    for m in range(4):
                        d16_ref[0, r4 + 4 * m, :, cols] = (
                            stage2_ref[c, pl.ds(r4 * rows4 + m, rows16, stride=4), :].astype(BF16))


def _inproj(x2d, pos_col, inv_tab, g1, w_in_bf16, qg, kg, batch, seq):
    n = x2d.shape[0]
    t = TOKEN_TILE
    nt = seq // t
    const = lambda i: (0, 0)
    qkv = 3 * ATT_WIDTH

    def residue_major(dil):
        spec = pl.BlockSpec((1, dil, t // dil, qkv), lambda i: (i // nt, 0, i % nt, 0))
        return spec, jax.ShapeDtypeStruct((batch, dil, seq // dil, qkv), BF16)

    spec4, shape4 = residue_major(4)
    spec16, shape16 = residue_major(16)
    return pl.pallas_call(
        _inproj_kernel,
        grid=(n // t,),
        in_specs=[
            pl.BlockSpec((t, D_MODEL), lambda i: (i, 0)),
            pl.BlockSpec((t, 1), lambda i: (i, 0)),
            pl.BlockSpec((1, LANES), const),
            pl.BlockSpec((1, D_MODEL), const),
            pl.BlockSpec((D_MODEL, IN_COLS), const),
            pl.BlockSpec((1, LANES), const),
            pl.BlockSpec((1, LANES), const),
        ],
        out_specs=[pl.BlockSpec((t, IN_COLS), lambda i: (i, 0)), spec4, spec16],
        out_shape=[jax.ShapeDtypeStruct((n, IN_COLS), BF16), shape4, shape16],
        scratch_shapes=[pltpu.VMEM((ATT_WIDTH // LANES, t, LANES), F32)] * 2,
        compiler_params=pltpu.CompilerParams(
            dimension_semantics=("parallel",), vmem_limit_bytes=VMEM_LIMIT),
        name="inproj",
    )(x2d, pos_col, inv_tab, g1, w_in_bf16, qg, kg)


def _hgrn_direction(q_ref, z_ref, v_ref, lb, state_t, reverse):
    c = HG_CHUNK
    t = q_ref.shape[0]
    n = t // c
    row = lax.broadcasted_iota(jnp.int32, (c, c), 0)
    col = lax.broadcasted_iota(jnp.int32, (c, c), 1)
    mask = (row <= col) if reverse else (row >= col)
    tri = jnp.where(mask, 1.0, 0.0).astype(BF16)
    last_row = 0 if reverse else c - 1

    z = z_ref[...].astype(F32)
    q = q_ref[...].astype(F32)
    v = v_ref[...]
    sg = _sigmoid(z)
    f = lb + (1.0 - lb) * sg
    k = (1.0 - lb) * (1.0 - sg)
    lf = jnp.log(f)
    lf_hi = lf.astype(BF16)
    lf_lo = (lf - lf_hi.astype(F32)).astype(BF16)
    chunks = [slice(j * c, (j + 1) * c) for j in range(n)]
    b = jnp.concatenate([_dot(tri, lf_hi[rs]) + _dot(tri, lf_lo[rs]) for rs in chunks], axis=0)
    b_last = b.reshape(n, c, HG_DIM)[:, last_row:last_row + 1, :]
    decay = jnp.exp(b_last)
    qt = (q * _sigmoid(q) * jnp.exp(b)).astype(BF16)
    kt_f32 = k * jnp.exp(-b)
    kt = kt_f32.astype(BF16)
    kd = (kt_f32.reshape(n, c, HG_DIM) * decay).reshape(t, HG_DIM).astype(BF16)

    outs, updates = [], []
    for rs in chunks:
        a = jnp.where(mask, _dot_nt(qt[rs], kt[rs]), 0.0)
        outs.append(_dot(a.astype(BF16), v[rs]))
        updates.append(_dot_tn(v[rs], kd[rs]))
    for j in (reversed(range(n)) if reverse else range(n)):
        outs[j] = outs[j] + _dot_nt(qt[chunks[j]], state_t.astype(BF16))
        state_t = state_t * decay[j] + updates[j]
    return jnp.concatenate(outs, axis=0), state_t


def _hgrn_kernel(qf_ref, zf_ref, vf_ref, qb_ref, zb_ref, vb_ref, lbf_ref, lbb_ref,
                 of_ref, ob_ref, sf_ref, sb_ref):
    @pl.when(pl.program_id(2) == 0)
    def _():
        sf_ref[...] = jnp.zeros_like(sf_ref)
        sb_ref[...] = jnp.zeros_like(sb_ref)

    o, sf = _hgrn_direction(qf_ref, zf_ref, vf_ref, lbf_ref[0], sf_ref[...], False)
    of_ref[...] = o.astype(of_ref.dtype)
    sf_ref[...] = sf
    o, sb = _hgrn_direction(qb_ref, zb_ref, vb_ref, lbb_ref[0], sb_ref[...], True)
    ob_ref[...] = o.astype(ob_ref.dtype)
    sb_ref[...] = sb


def _hgrn(proj, lbs, batch, seq):
    n = proj.shape[0]
    t = TOKEN_TILE
    nblk = seq // t

    def fwd(colblk):
        return pl.BlockSpec((t, HG_DIM), lambda b, h, i: (b * nblk + i, colblk * HG_HEADS + h))

    def bwd(colblk):
        return pl.BlockSpec((t, HG_DIM), lambda b, h, i: (b * nblk + nblk - 1 - i, colblk * HG_HEADS + h))

    out_f = pl.BlockSpec((t, HG_DIM), lambda b, h, i: (b * nblk + i, h))
    out_b = pl.BlockSpec((t, HG_DIM), lambda b, h, i: (b * nblk + nblk - 1 - i, h))
    return pl.pallas_call(
        _hgrn_kernel,
        grid=(batch, HG_HEADS, nblk),
        in_specs=[
            fwd(COL_HQ), fwd(COL_HF_FWD), fwd(COL_HI),
            bwd(COL_HQ), bwd(COL_HF_BWD), bwd(COL_HI),
            pl.BlockSpec((1, 1, HG_DIM), lambda b, h, i: (h, 0, 0)),
            pl.BlockSpec((1, 1, HG_DIM), lambda b, h, i: (HG_HEADS + h, 0, 0)),
        ],
        out_specs=[out_f, out_b],
        out_shape=[jax.ShapeDtypeStruct((n, HG_WIDTH), BF16)] * 2,
        scratch_shapes=[pltpu.VMEM((HG_DIM, HG_DIM), F32)] * 2,
        compiler_params=pltpu.CompilerParams(
            dimension_semantics=("parallel", "parallel", "arbitrary"), vmem_limit_bytes=VMEM_LIMIT),
        name="hgrn",
    )(proj, proj, proj, proj, proj, proj, lbs, lbs)


def _attn_kernel(q_ref, kc_ref, kp_ref, kn_ref, vc_ref, vp_ref, vn_ref, o_ref, l_ref,
                 kw_ref, vw_ref, *, tq, length):
    n = pl.program_id(2)
    half = ATT_HALF
    kw_ref[0:half, :] = kp_ref[...]
    kw_ref[half:half + tq, :] = kc_ref[...]
    kw_ref[half + tq:, :] = kn_ref[...]
    vw_ref[0:half, :] = vp_ref[...]
    vw_ref[half:half + tq, :] = vc_ref[...]
    vw_ref[half + tq:, :] = vn_ref[...]

    qb_rows = ATT_QBLOCK
    win = qb_rows + 2 * half
    i_idx = lax.broadcasted_iota(jnp.int32, (qb_rows, win), 0)
    j_idx = lax.broadcasted_iota(jnp.int32, (qb_rows, win), 1)
    band = (j_idx >= i_idx) & (j_idx <= i_idx + 2 * half)

    def body(qb, carry):
        r0 = pl.multiple_of(qb * qb_rows, qb_rows)
        base = n * tq + r0 - half
        valid = band & (j_idx >= -base) & (j_idx < length - base)
        q = q_ref[pl.ds(r0, qb_rows), :]
        kw = kw_ref[pl.ds(r0, win), :]
        vw = vw_ref[pl.ds(r0, win), :]
        heads = [slice(h * ATT_DIM, (h + 1) * ATT_DIM) for h in range(ATT_HEADS)]
        scores = [jnp.where(valid, _dot_nt(q[:, cs], kw[:, cs]), NEG) for cs in heads]
        maxes = [jnp.max(s, axis=-1, keepdims=True) for s in scores]
        probs = [jnp.exp(s - m) for s, m in zip(scores, maxes)]
        dens = [jnp.sum(p, axis=-1, keepdims=True) for p in probs]
        outs = [_dot(p.astype(BF16), vw[:, cs]) / den for p, cs, den in zip(probs, heads, dens)]
        lses = [jnp.broadcast_to(m + jnp.log(den), (qb_rows, ATT_DIM)) for m, den in zip(maxes, dens)]
        o_ref[pl.ds(r0, qb_rows), :] = jnp.concatenate(outs, axis=1).astype(o_ref.dtype)
        l_ref[pl.ds(r0, qb_rows), :] = jnp.concatenate(lses, axis=1)
        return carry

    lax.fori_loop(0, tq // qb_rows, body, 0)


def _attention(qkv, col0):
    batch, dil, length, _ = qkv.shape
    tq = min(TOKEN_TILE, length)
    nq = length // tq
    hb = tq // ATT_HALF
    n_hblk = length // ATT_HALF

    def cur(col):
        return pl.BlockSpec((None, None, tq, ATT_WIDTH), lambda b, r, n: (b, r, n, col))

    def prev(col):
        return pl.BlockSpec((None, None, ATT_HALF, ATT_WIDTH),
                            lambda b, r, n: (b, r, jnp.maximum(n * hb - 1, 0), col))

    def nxt(col):
        return pl.BlockSpec((None, None, ATT_HALF, ATT_WIDTH),
                            lambda b, r, n: (b, r, jnp.minimum((n + 1) * hb, n_hblk - 1), col))

    out_spec = pl.BlockSpec((None, None, tq, ATT_WIDTH), lambda b, r, n: (b, r, n, 0))
    return pl.pallas_call(
        functools.partial(_attn_kernel, tq=tq, length=length),
        grid=(batch, dil, nq),
        in_specs=[cur(col0), cur(col0 + 1), prev(col0 + 1), nxt(col0 + 1),
                  cur(col0 + 2), prev(col0 + 2), nxt(col0 + 2)],
        out_specs=[out_spec, out_spec],
        out_shape=[jax.ShapeDtypeStruct((batch, dil, length, ATT_WIDTH), BF16),
                   jax.ShapeDtypeStruct((batch, dil, length, ATT_WIDTH), F32)],
        scratch_shapes=[pltpu.VMEM((tq + 2 * ATT_HALF, ATT_WIDTH), BF16)] * 2,
        compiler_params=pltpu.CompilerParams(
            dimension_semantics=("parallel", "parallel", "parallel"), vmem_limit_bytes=VMEM_LIMIT),
        name=f"attn_d{dil}",
    )(qkv, qkv, qkv, qkv, qkv, qkv, qkv)


def _token_major(src_ref, stage_ref, tmp_ref):
    dil, rows = src_ref.shape[1], src_ref.shape[2]
    nc = src_ref.shape[3] // LANES
    for c in range(nc):
        cols = slice(c * LANES, (c + 1) * LANES)
        if dil == 4:
            for r in range(dil):
                stage_ref[c, pl.ds(r, rows, stride=dil), :] = src_ref[0, r, :, cols].astype(F32)
        else:
            group = 4 * rows
            for r4 in range(4):
                for m in range(4):
                    tmp_ref[c, pl.ds(r4 * group + m, rows, stride=4), :] = (
                        src_ref[0, r4 + 4 * m, :, cols].astype(F32))
            for r4 in range(4):
                stage_ref[c, pl.ds(r4, group, stride=4), :] = tmp_ref[c, r4 * group:(r4 + 1) * group, :]
    return jnp.concatenate([stage_ref[c] for c in range(nc)], axis=1)


def _outproj_kernel(of_ref, ob_ref, hg_ref, o1_ref, o2_ref, o3_ref, l1_ref, l2_ref, l3_ref,
                    x_ref, og_ref, w_ref, g2_ref, wrh_ref, wrl_ref, br_ref,
                    x2_ref, xn_ref, gate_ref, idx_ref, st_o2, st_o3, st_l2, st_l3, st_tmp_o, st_tmp_l):
    o = of_ref[...].astype(F32) + ob_ref[...].astype(F32)
    hg = hg_ref[...].astype(F32)
    parts = []
    for h in range(HG_HEADS):
        blk = o[:, h * HG_DIM:(h + 1) * HG_DIM]
        ms = jnp.mean(blk * blk, axis=-1, keepdims=True)
        parts.append(blk * lax.rsqrt(ms + EPS) * og_ref[...])
    o_hg = jnp.concatenate(parts, axis=1) * (hg * _sigmoid(hg))

    l1 = l1_ref[...]
    l2 = _token_major(l2_ref, st_l2, st_tmp_l)
    l3 = _token_major(l3_ref, st_l3, st_tmp_l)
    mx = jnp.maximum(jnp.maximum(l1, l2), l3)
    e1, e2, e3 = jnp.exp(l1 - mx), jnp.exp(l2 - mx), jnp.exp(l3 - mx)
    o_att = (e1 * o1_ref[...].astype(F32) + e2 * _token_major(o2_ref, st_o2, st_tmp_o)
             + e3 * _token_major(o3_ref, st_o3, st_tmp_o)) / (e1 + e2 + e3)

    y = _dot(o_hg.astype(BF16), w_ref[0:HG_WIDTH, :]) + _dot(o_att.astype(BF16), w_ref[HG_WIDTH:, :])
    x2 = x_ref[...] + y
    x2_ref[...] = x2

    ms = jnp.mean(x2 * x2, axis=-1, keepdims=True)
    xn = x2 * lax.rsqrt(ms + EPS) * g2_ref[...]
    _rows_to_tiles(xn_ref, xn)
    xn_hi = xn.astype(BF16)
    xn_lo = (xn - xn_hi.astype(F32)).astype(BF16)
    logits = (_dot(xn_hi, wrh_ref[...]) + _dot(xn_lo, wrh_ref[...]) + _dot(xn_hi, wrl_ref[...])
              + br_ref[...])

    lane = lax.broadcasted_iota(jnp.int32, logits.shape, 1)
    lane_f = lane.astype(F32)
    work = jnp.where(lane < N_EXPERTS, logits, -jnp.inf)
    vals, idxs = [], []
    for _ in range(TOP_K):
        m = jnp.max(work, axis=-1, keepdims=True)
        idx = jnp.min(jnp.where(work == m, lane_f, float(LANES)), axis=-1, keepdims=True)
        vals.append(m)
        idxs.append(idx)
        work = jnp.where(lane_f == idx, -jnp.inf, work)
    es = [jnp.exp(v - vals[0]) for v in vals]
    den = es[0] + es[1] + es[2] + es[3]
    gate_out = jnp.zeros(logits.shape, F32)
    idx_out = jnp.zeros(logits.shape, F32)
    for k in range(TOP_K):
        gate_out = jnp.where(lane == k, es[k] / den, gate_out)
        idx_out = jnp.where(lane == k, idxs[k], idx_out)
    gate_ref[...] = gate_out
    idx_ref[...] = idx_out.astype(jnp.int32)


def _outproj(o_f, o_b, proj, atts, lses, x2d, og, w_out_bf16, g2, wr_hi, wr_lo, br, seq):
    n = x2d.shape[0]
    t = TOKEN_TILE
    nt = seq // t
    row = lambda i: (i, 0)
    const = lambda i: (0, 0)
    half = pl.BlockSpec((t, 512), row)

    def residue_major(dil):
        return pl.BlockSpec((1, dil, t // dil, ATT_WIDTH), lambda i: (i // nt, 0, i % nt, 0))

    rm4, rm16 = residue_major(4), residue_major(16)
    return pl.pallas_call(
        _outproj_kernel,
        grid=(n // t,),
        in_specs=[
            half, half, pl.BlockSpec((t, 512), lambda i: (i, COL_HG)),
            half, rm4, rm16, half, rm4, rm16,
            pl.BlockSpec((t, D_MODEL), row),
            pl.BlockSpec((1, HG_DIM), const),
            pl.BlockSpec((D_MODEL, D_MODEL), const),
            pl.BlockSpec((1, D_MODEL), const),
            pl.BlockSpec((D_MODEL, LANES), const),
            pl.BlockSpec((D_MODEL, LANES), const),
            pl.BlockSpec((1, LANES), const),
        ],
        out_specs=[pl.BlockSpec((t, D_MODEL), row), pl.BlockSpec((t * ROW_TILES, LANES), row),
                   pl.BlockSpec((t, LANES), row), pl.BlockSpec((t, LANES), row)],
        out_shape=[jax.ShapeDtypeStruct((n, D_MODEL), F32), jax.ShapeDtypeStruct((n * ROW_TILES, LANES), F32),
                   jax.ShapeDtypeStruct((n, LANES), F32), jax.ShapeDtypeStruct((n, LANES), jnp.int32)],
        scratch_shapes=[pltpu.VMEM((ATT_WIDTH // LANES, t, LANES), F32)] * 6,
        compiler_params=pltpu.CompilerParams(
            dimension_semantics=("parallel",), vmem_limit_bytes=VMEM_LIMIT),
        name="outproj",
    )(o_f, o_b, proj, *atts, *lses, x2d, og, w_out_bf16, g2, wr_hi, wr_lo, br)


def _moe_kernel(bexp_ref, bfirst_ref, bslot_ref, bnext_ref, nused_ref,
                xs_ref, wu_hbm, bg_ref, bl_ref, wd_hbm, bd_ref,
                y_ref, wu_buf, wd_buf, wg_s, wl_s, wd_s, sems):
    i = pl.program_id(0)

    def weight_copies(expert, slot):
        return (pltpu.make_async_copy(wu_hbm.at[expert], wu_buf.at[slot], sems.at[0, slot]),
                pltpu.make_async_copy(wd_hbm.at[expert], wd_buf.at[slot], sems.at[1, slot]))

    @pl.when(bfirst_ref[i] == 1)
    def _():
        slot = bslot_ref[i]

        @pl.when(i == 0)
        def _():
            for cp in weight_copies(bexp_ref[0], 0):
                cp.start()

        for cp in weight_copies(bexp_ref[i], slot):
            cp.wait()

        @pl.when(bnext_ref[i] >= 0)
        def _():
            for cp in weight_copies(bnext_ref[i], 1 - slot):
                cp.start()

        r = lax.broadcasted_iota(jnp.int32, (2 * LANES, 2 * LANES), 0)
        c = lax.broadcasted_iota(jnp.int32, (2 * LANES, 2 * LANES), 1)
        src = jnp.where(c < LANES, 2 * c, 2 * (c - LANES) + 1)
        perm = jnp.where(r == src, 1.0, 0.0).astype(BF16)
        rows = 256
        for rb in range(D_MODEL // rows):
            rs = slice(rb * rows, (rb + 1) * rows)
            for cb in range(D_EXPERT // LANES):
                w = wu_buf[slot, rs, cb * 2 * LANES:(cb + 1) * 2 * LANES].astype(BF16)
                split = _dot(w, perm).astype(BF16)
                wg_s[rs, cb * LANES:(cb + 1) * LANES] = split[:, :LANES]
                wl_s[rs, cb * LANES:(cb + 1) * LANES] = split[:, LANES:]
        wd_s[...] = wd_buf[slot].astype(BF16)

    @pl.when(i < nused_ref[0])
    def _():
        x = jnp.concatenate(_tiles_to_rows(xs_ref, MOE_BLOCK_ROWS), axis=1).astype(BF16)
        hglu = _dot(x, wg_s[...]) + bg_ref[0]
        hlin = _dot(x, wl_s[...]) + bl_ref[0]
        glu = jnp.minimum(hglu, SWIGLU_LIMIT)
        lin = jnp.clip(hlin, -SWIGLU_LIMIT, SWIGLU_LIMIT)
        act = glu * _sigmoid(SWIGLU_ALPHA * glu) * (lin + 1.0)
        _rows_to_tiles(y_ref, _dot(act.astype(BF16), wd_s[...]) + bd_ref[0])

    @pl.when(i >= nused_ref[0])
    def _():
        y_ref[...] = jnp.zeros_like(y_ref)


def _moe(block_exp, block_first, block_slot, block_next, n_used, xs, w_up, b_glu, b_lin, w_down, b_down):
    p_rows = xs.shape[0] // ROW_TILES
    bm = MOE_BLOCK_ROWS
    nb = p_rows // bm
    exp3 = lambda i, be, bf, bs, bn, nu: (be[i], 0, 0)
    grid_spec = pltpu.PrefetchScalarGridSpec(
        num_scalar_prefetch=5,
        grid=(nb,),
        in_specs=[
            pl.BlockSpec((bm * ROW_TILES, LANES), lambda i, be, bf, bs, bn, nu: (jnp.minimum(i, nu[0] - 1), 0)),
            pl.BlockSpec(memory_space=pl.ANY),
            pl.BlockSpec((1, 1, D_EXPERT), exp3),
            pl.BlockSpec((1, 1, D_EXPERT), exp3),
            pl.BlockSpec(memory_space=pl.ANY),
            pl.BlockSpec((1, 1, D_MODEL), exp3),
        ],
        out_specs=pl.BlockSpec((bm * ROW_TILES, LANES), lambda i, be, bf, bs, bn, nu: (i, 0)),
        scratch_shapes=[pltpu.VMEM((2, D_MODEL, 2 * D_EXPERT), F32), pltpu.VMEM((2, D_EXPERT, D_MODEL), F32),
                        pltpu.VMEM((D_MODEL, D_EXPERT), BF16), pltpu.VMEM((D_MODEL, D_EXPERT), BF16),
                        pltpu.VMEM((D_EXPERT, D_MODEL), BF16), pltpu.SemaphoreType.DMA((2, 2))],
    )
    return pl.pallas_call(
        _moe_kernel,
        grid_spec=grid_spec,
        out_shape=jax.ShapeDtypeStruct((p_rows * ROW_TILES, LANES), F32),
        compiler_params=pltpu.CompilerParams(
            dimension_semantics=("arbitrary",), vmem_limit_bytes=VMEM_LIMIT),
        name="moe",
    )(block_exp, block_first, block_slot, block_next, n_used, xs, w_up, b_glu, b_lin, w_down, b_down)


def _dispatch_kernel(dest_ref, zstart_ref, xn_ref, xs_hbm, zero_ref, sem, zsem):
    i = pl.program_id(0)
    bm = MOE_BLOCK_ROWS
    tokens = xn_ref.shape[0] // ROW_TILES

    def zero_copy(e):
        start = pl.multiple_of(zstart_ref[e] * ROW_TILES, bm * ROW_TILES)
        return pltpu.make_async_copy(zero_ref, xs_hbm.at[pl.ds(start, bm * ROW_TILES)], zsem)

    @pl.when(i == 0)
    def _():
        zero_ref[...] = jnp.zeros_like(zero_ref)
        for e in range(N_EXPERTS):
            @pl.when(zstart_ref[e] >= 0)
            def _():
                zero_copy(e).start()
        for e in range(N_EXPERTS):
            @pl.when(zstart_ref[e] >= 0)
            def _():
                zero_copy(e).wait()

    t0 = i * tokens

    def body(j, carry):
        src = xn_ref.at[pl.ds(pl.multiple_of(j * ROW_TILES, ROW_TILES), ROW_TILES)]
        for k in range(TOP_K):
            d = pl.multiple_of(dest_ref[(t0 + j) * TOP_K + k] * ROW_TILES, ROW_TILES)
            pltpu.make_async_copy(src, xs_hbm.at[pl.ds(d, ROW_TILES)], sem).start(priority=k % 2)
        return carry

    lax.fori_loop(0, tokens, body, 0, unroll=8)
    for k in range(TOP_K):
        pltpu.make_async_copy(xn_ref, xs_hbm.at[pl.ds(0, tokens * ROW_TILES)], sem).wait()


def _dispatch(dest, zstart, xn, p_rows):
    n = xn.shape[0] // ROW_TILES
    t = DISPATCH_TOKENS
    grid_spec = pltpu.PrefetchScalarGridSpec(
        num_scalar_prefetch=2,
        grid=(n // t,),
        in_specs=[pl.BlockSpec((t * ROW_TILES, LANES), lambda i, d, z: (i, 0))],
        out_specs=pl.BlockSpec(memory_space=pl.ANY),
        scratch_shapes=[pltpu.VMEM((MOE_BLOCK_ROWS * ROW_TILES, LANES), F32),
                        pltpu.SemaphoreType.DMA, pltpu.SemaphoreType.DMA],
    )
    return pl.pallas_call(
        _dispatch_kernel,
        grid_spec=grid_spec,
        out_shape=jax.ShapeDtypeStruct((p_rows * ROW_TILES, LANES), F32),
        compiler_params=pltpu.CompilerParams(
            dimension_semantics=("arbitrary",), vmem_limit_bytes=VMEM_LIMIT),
        name="dispatch",
    )(dest, zstart, xn)


def _combine_kernel(dest_ref, y_hbm, x2_ref, gate_ref, out_ref, buf_ref, sems):
    i = pl.program_id(0)
    tc = COMBINE_TOKENS

    def issue(step, slot):
        t0 = step * tc

        def body(j, carry):
            for k in range(TOP_K):
                d = pl.multiple_of(dest_ref[(t0 + j) * TOP_K + k] * ROW_TILES, ROW_TILES)
                r = pl.multiple_of((k * tc + j) * ROW_TILES, ROW_TILES)
                pltpu.make_async_copy(y_hbm.at[pl.ds(d, ROW_TILES)], buf_ref.at[slot, pl.ds(r, ROW_TILES)],
                                      sems.at[slot]).start(priority=k % 2)
            return carry

        lax.fori_loop(0, tc, body, 0, unroll=8)

    @pl.when(i == 0)
    def _():
        issue(0, 0)

    @pl.when(i + 1 < pl.num_programs(0))
    def _():
        issue(i + 1, (i + 1) % 2)

    slot = i % 2
    pltpu.make_async_copy(y_hbm.at[pl.ds(0, TOP_K * tc * ROW_TILES)], buf_ref.at[slot], sems.at[slot]).wait()
    rows = buf_ref.at[slot]
    sub = 64
    for r0 in range(0, tc, sub):
        gate = gate_ref[r0:r0 + sub, :]
        gates = [jnp.broadcast_to(gate[:, k:k + 1], (sub, LANES)) for k in range(TOP_K)]
        for c in range(ROW_TILES):
            acc = x2_ref[r0:r0 + sub, c * LANES:(c + 1) * LANES]
            for k in range(TOP_K):
                acc = acc + gates[k] * rows[pl.ds((k * tc + r0) * ROW_TILES + c, sub, stride=ROW_TILES), :]
            out_ref[r0:r0 + sub, c * LANES:(c + 1) * LANES] = acc


def _combine(dest, y, x2, gates):
    n = x2.shape[0]
    tc = COMBINE_TOKENS
    grid_spec = pltpu.PrefetchScalarGridSpec(
        num_scalar_prefetch=1,
        grid=(n // tc,),
        in_specs=[pl.BlockSpec(memory_space=pl.ANY),
                  pl.BlockSpec((tc, D_MODEL), lambda i, d: (i, 0)),
                  pl.BlockSpec((tc, LANES), lambda i, d: (i, 0))],
        out_specs=pl.BlockSpec((tc, D_MODEL), lambda i, d: (i, 0)),
        scratch_shapes=[pltpu.VMEM((2, TOP_K * tc * ROW_TILES, LANES), F32), pltpu.SemaphoreType.DMA((2,))],
    )
    return pl.pallas_call(
        _combine_kernel,
        grid_spec=grid_spec,
        out_shape=jax.ShapeDtypeStruct((n, D_MODEL), F32),
        compiler_params=pltpu.CompilerParams(
            dimension_semantics=("arbitrary",), vmem_limit_bytes=VMEM_LIMIT),
        name="combine",
    )(dest, y, x2, gates)


def _route(top_idx):
    n = top_idx.shape[0]
    a = n * TOP_K
    bm = MOE_BLOCK_ROWS
    nb = a // bm + N_EXPERTS
    e_flat = top_idx.reshape(a)
    onehot = (e_flat[:, None] == jnp.arange(N_EXPERTS, dtype=jnp.int32)[None, :]).astype(jnp.int32)
    csum = jnp.cumsum(onehot, axis=0)
    counts = csum[-1]
    padded = ((counts + bm - 1) // bm) * bm
    pad_end = jnp.cumsum(padded)
    pad_start = pad_end - padded
    dest = jnp.sum(onehot * (csum - 1 + pad_start[None, :]), axis=1)
    n_used = (pad_end[-1] // bm).astype(jnp.int32)
    blk = jnp.arange(nb, dtype=jnp.int32)
    bexp = jnp.sum((pad_end[None, :] <= (blk * bm)[:, None]).astype(jnp.int32), axis=1)
    bexp = jnp.minimum(bexp, N_EXPERTS - 1)
    bexp = jnp.where(blk < n_used, bexp, bexp[jnp.maximum(n_used - 1, 0)])
    bfirst = jnp.concatenate([jnp.ones((1,), jnp.int32), (bexp[1:] != bexp[:-1]).astype(jnp.int32)])
    bslot = (jnp.cumsum(bfirst) - 1) % 2
    later = jnp.where(bexp[None, :] > bexp[:, None], bexp[None, :], N_EXPERTS)
    bnext = jnp.min(later, axis=1)
    bnext = jnp.where(bnext < N_EXPERTS, bnext, -1).astype(jnp.int32)
    zstart = jnp.where(counts > 0, pad_end - bm, -1).astype(jnp.int32)
    return dest.astype(jnp.int32), zstart, (bexp, bfirst, bslot.astype(jnp.int32), bnext, n_used.reshape(1))


def kernel(x, positions, norm1_g, w_in, q_norm_g, k_norm_g, hgrn_lower_bounds, hgrn_onorm_g,
           w_out, norm2_g, w_router, b_router, w_up, b_up, w_down, b_down):
    batch, seq, d = x.shape
    n = batch * seq
    depth = norm1_g.shape[0]
    lbs_all = jnp.cumsum(jax.nn.softmax(hgrn_lower_bounds.astype(F32), axis=0), axis=0)
    half = ATT_DIM // 2
    inv = 1.0 / (ROPE_THETA ** (jnp.arange(half, dtype=F32) / half))
    inv_tab = jnp.tile(inv, LANES // half).reshape(1, LANES)
    pos_col = positions.reshape(n, 1)

    x2d = x.reshape(n, d)
    for l in range(depth):
        lbs = lbs_all[l].reshape(2 * HG_HEADS, 1, HG_DIM)
        proj, qkv4, qkv16 = _inproj(
            x2d, pos_col, inv_tab, norm1_g[l].reshape(1, d), w_in[l].astype(BF16),
            jnp.tile(q_norm_g[l], LANES // ATT_DIM).reshape(1, LANES),
            jnp.tile(k_norm_g[l], LANES // ATT_DIM).reshape(1, LANES), batch, seq)
        o_f, o_b = _hgrn(proj, lbs, batch, seq)
        o1, l1 = _attention(proj.reshape(batch, 1, seq, IN_COLS), COL_AQ)
        o4, l4 = _attention(qkv4, 0)
        o16, l16 = _attention(qkv16, 0)
        atts = [o1.reshape(n, ATT_WIDTH), o4, o16]
        lses = [l1.reshape(n, ATT_WIDTH), l4, l16]

        wr = jnp.pad(w_router[l], ((0, 0), (0, LANES - N_EXPERTS)))
        wr_hi = wr.astype(BF16)
        wr_lo = (wr - wr_hi.astype(F32)).astype(BF16)
        br = jnp.pad(b_router[l], (0, LANES - N_EXPERTS)).reshape(1, LANES)
        x2, xn, gates, top_idx = _outproj(
            o_f, o_b, proj, atts, lses, x2d, hgrn_onorm_g[l].reshape(1, HG_DIM),
            w_out[l].astype(BF16), norm2_g[l].reshape(1, d), wr_hi, wr_lo, br, seq)
        dest, zstart, blocks = _route(top_idx[:, :TOP_K])
        xs = _dispatch(dest, zstart, xn, blocks[0].shape[0] * MOE_BLOCK_ROWS)
        y = _moe(*blocks, xs, w_up[l],
                 b_up[l][:, 0::2].reshape(N_EXPERTS, 1, D_EXPERT),
                 b_up[l][:, 1::2].reshape(N_EXPERTS, 1, D_EXPERT),
                 w_down[l], b_down[l].reshape(N_EXPERTS, 1, D_MODEL))
        x2d = _combine(dest, y, x2, gates)
    return x2d.reshape(batch, seq, d)
```

```python
import functools

import jax
import jax.numpy as jnp
from jax import lax
from jax.experimental import pallas as pl
from jax.experimental.pallas import tpu as pltpu

F32 = jnp.float32
BF16 = jnp.bfloat16

D_MODEL = 1024
HG_HEADS = 4
HG_DIM = 128
HG_WIDTH = HG_HEADS * HG_DIM
HG_CHUNK = 64
ATT_HEADS = 8
ATT_DIM = 64
ATT_WIDTH = ATT_HEADS * ATT_DIM
DILATED_PATTERNS = ((128, 1), (512, 4), (2048, 16))
ATT_HALF = 64
ATT_QBLOCK = 128
ROPE_THETA = 10000.0
IN_COLS = 5 * HG_WIDTH + 3 * ATT_WIDTH
N_EXPERTS = 32
TOP_K = 4
D_EXPERT = D_MODEL
SWIGLU_LIMIT = 7.0
SWIGLU_ALPHA = 1.702
EPS = 1e-6
NEG = -1e30

COL_HQ, COL_HF_FWD, COL_HF_BWD, COL_HI, COL_HG, COL_AQ, COL_AK, COL_AV = range(8)

TOKEN_TILE = 512
INPROJ_TILE = 1024
HGRN_TILE = 1024
MOE_BLOCK_ROWS = 512
DISPATCH_TOKENS = 512
COMBINE_TOKENS = 256
LANES = 128
ROW_TILES = D_MODEL // LANES
VMEM_LIMIT = 60 * 1024 * 1024


def _dot(a, b):
    return jnp.dot(a, b, preferred_element_type=F32)


def _dot_nt(a, b):
    return lax.dot_general(a, b, (((1,), (1,)), ((), ())), preferred_element_type=F32)


def _dot_tn(a, b):
    return lax.dot_general(a, b, (((0,), (0,)), ((), ())), preferred_element_type=F32)


def _sigmoid(x):
    return 1.0 / (1.0 + jnp.exp(-x))


def _rows_to_tiles(dst_ref, x):
    for c in range(ROW_TILES):
        dst_ref[pl.ds(c, x.shape[0], stride=ROW_TILES), :] = x[:, c * LANES:(c + 1) * LANES]


def _tiles_to_rows(src, rows, first_row=0):
    return [src[pl.ds(first_row * ROW_TILES + c, rows, stride=ROW_TILES), :] for c in range(ROW_TILES)]


def _head_norm_rope(p, gain, cos, sin_signed, scale):
    lane = lax.broadcasted_iota(jnp.int32, (p.shape[0], LANES), 1)
    low = lane < ATT_DIM
    first_half = (lane % ATT_DIM) < (ATT_DIM // 2)
    outs = []
    for t in range(ATT_WIDTH // LANES):
        blk = p[:, t * LANES:(t + 1) * LANES]
        sq = blk * blk
        s_low = jnp.sum(jnp.where(low, sq, 0.0), axis=-1, keepdims=True)
        s_high = jnp.sum(jnp.where(low, 0.0, sq), axis=-1, keepdims=True)
        r = jnp.where(low, lax.rsqrt(s_low * (1.0 / ATT_DIM) + EPS),
                      lax.rsqrt(s_high * (1.0 / ATT_DIM) + EPS))
        y = blk * r * gain
        partner = jnp.where(first_half, pltpu.roll(y, LANES - ATT_DIM // 2, axis=1),
                            pltpu.roll(y, ATT_DIM // 2, axis=1))
        outs.append((y * cos + partner * sin_signed) * scale)
    return jnp.concatenate(outs, axis=1)


def _inproj_kernel(x_ref, pos_ref, inv_ref, g1_ref, w_ref, qg_ref, kg_ref,
                   out_ref, d4_ref, d16_ref, stage_ref, stage2_ref):
    x = x_ref[...]
    ms = jnp.mean(x * x, axis=-1, keepdims=True)
    h = (x * lax.rsqrt(ms + EPS) * g1_ref[...]).astype(BF16)
    ang = pos_ref[...].astype(F32) * inv_ref[...]
    lane = lax.broadcasted_iota(jnp.int32, ang.shape, 1)
    cos = jnp.cos(ang)
    sin_signed = jnp.where((lane % ATT_DIM) < (ATT_DIM // 2), -jnp.sin(ang), jnp.sin(ang))
    order = (COL_AQ, COL_AK, COL_AV, COL_HQ, COL_HF_FWD, COL_HF_BWD, COL_HI, COL_HG)
    nxt = _dot(h, w_ref[:, order[0] * 512:(order[0] + 1) * 512])
    for pos, j in enumerate(order):
        p = nxt
        if pos + 1 < len(order):
            jn = order[pos + 1]
            nxt = _dot(h, w_ref[:, jn * 512:(jn + 1) * 512])
        if j == COL_AQ:
            p = _head_norm_rope(p, qg_ref[...], cos, sin_signed, ATT_DIM ** -0.5)
        elif j == COL_AK:
            p = _head_norm_rope(p, kg_ref[...], cos, sin_signed, 1.0)
        out_ref[:, j * 512:(j + 1) * 512] = p.astype(BF16)
        if j >= COL_AQ:
            rows4, rows16 = x.shape[0] // 4, x.shape[0] // 16
            for c in range(ATT_WIDTH // LANES):
                cols = slice((j - COL_AQ) * ATT_WIDTH + c * LANES, (j - COL_AQ) * ATT_WIDTH + (c + 1) * LANES)
                stage_ref[c] = p[:, c * LANES:(c + 1) * LANES]
                for r4 in range(4):
                    group = stage_ref[c, pl.ds(r4, rows4, stride=4), :]
                    d4_ref[0, r4, :, cols] = group.astype(BF16)
                    stage2_ref[c, r4 * rows4:(r4 + 1) * rows4, :] = group
                for r4 in range(4):
                    for m in range(4):
                        d16_ref[0, r4 + 4 * m, :, cols] = (
                            stage2_ref[c, pl.ds(r4 * rows4 + m, rows16, stride=4), :].astype(BF16))


def _inproj(x2d, pos_col, inv_tab, g1, w_in_bf16, qg, kg, batch, seq):
    n = x2d.shape[0]
    t = INPROJ_TILE
    nt = seq // t
    const = lambda i: (0, 0)
    qkv = 3 * ATT_WIDTH

    def residue_major(dil):
        spec = pl.BlockSpec((1, dil, t // dil, qkv), lambda i: (i // nt, 0, i % nt, 0))
        return spec, jax.ShapeDtypeStruct((batch, dil, seq // dil, qkv), BF16)

    spec4, shape4 = residue_major(4)
    spec16, shape16 = residue_major(16)
    return pl.pallas_call(
        _inproj_kernel,
        grid=(n // t,),
        in_specs=[
            pl.BlockSpec((t, D_MODEL), lambda i: (i, 0)),
            pl.BlockSpec((t, 1), lambda i: (i, 0)),
            pl.BlockSpec((1, LANES), const),
            pl.BlockSpec((1, D_MODEL), const),
            pl.BlockSpec((D_MODEL, IN_COLS), const, pipeline_mode=pl.Buffered(1)),
            pl.BlockSpec((1, LANES), const),
            pl.BlockSpec((1, LANES), const),
        ],
        out_specs=[pl.BlockSpec((t, IN_COLS), lambda i: (i, 0)), spec4, spec16],
        out_shape=[jax.ShapeDtypeStruct((n, IN_COLS), BF16), shape4, shape16],
        scratch_shapes=[pltpu.VMEM((ATT_WIDTH // LANES, t, LANES), F32)] * 2,
        compiler_params=pltpu.CompilerParams(
            dimension_semantics=("parallel",), vmem_limit_bytes=VMEM_LIMIT),
        name="inproj",
    )(x2d, pos_col, inv_tab, g1, w_in_bf16, qg, kg)


def _hgrn_direction(q_ref, z_ref, v_ref, lb, state_t, reverse):
    c = HG_CHUNK
    t = q_ref.shape[0]
    n = t // c
    row = lax.broadcasted_iota(jnp.int32, (c, c), 0)
    col = lax.broadcasted_iota(jnp.int32, (c, c), 1)
    mask = (row <= col) if reverse else (row >= col)
    tri = jnp.where(mask, 1.0, 0.0).astype(BF16)
    last_row = 0 if reverse else c - 1

    z = z_ref[...].astype(F32)
    q = q_ref[...].astype(F32)
    v = v_ref[...]
    sg = _sigmoid(z)
    f = lb + (1.0 - lb) * sg
    k = (1.0 - lb) * (1.0 - sg)
    lf = jnp.log(f)
    lf_hi = lf.astype(BF16)
    lf_lo = (lf - lf_hi.astype(F32)).astype(BF16)
    chunks = [slice(j * c, (j + 1) * c) for j in range(n)]
    b = jnp.concatenate([_dot(tri, lf_hi[rs]) + _dot(tri, lf_lo[rs]) for rs in chunks], axis=0)
    b_last = b.reshape(n, c, HG_DIM)[:, last_row:last_row + 1, :]
    decay = jnp.exp(b_last)
    qt = (q * _sigmoid(q) * jnp.exp(b)).astype(BF16)
    kt_f32 = k * jnp.exp(-b)
    kt = kt_f32.astype(BF16)
    kd = (kt_f32.reshape(n, c, HG_DIM) * decay).reshape(t, HG_DIM).astype(BF16)

    outs, updates = [], []
    for rs in chunks:
        a = jnp.where(mask, _dot_nt(qt[rs], kt[rs]), 0.0)
        outs.append(_dot(a.astype(BF16), v[rs]))
        updates.append(_dot_tn(v[rs], kd[rs]))
    for j in (reversed(range(n)) if reverse else range(n)):
        outs[j] = outs[j] + _dot_nt(qt[chunks[j]], state_t.astype(BF16))
        state_t = state_t * decay[j] + updates[j]
    return jnp.concatenate(outs, axis=0), state_t


def _hgrn_kernel(qf_ref, zf_ref, vf_ref, qb_ref, zb_ref, vb_ref, lbf_ref, lbb_ref,
                 of_ref, ob_ref, sf_ref, sb_ref):
    @pl.when(pl.program_id(2) == 0)
    def _():
        sf_ref[...] = jnp.zeros_like(sf_ref)
        sb_ref[...] = jnp.zeros_like(sb_ref)

    o, sf = _hgrn_direction(qf_ref, zf_ref, vf_ref, lbf_ref[0], sf_ref[...], False)
    of_ref[...] = o.astype(of_ref.dtype)
    sf_ref[...] = sf
    o, sb = _hgrn_direction(qb_ref, zb_ref, vb_ref, lbb_ref[0], sb_ref[...], True)
    ob_ref[...] = o.astype(ob_ref.dtype)
    sb_ref[...] = sb


def _hgrn(proj, lbs, batch, seq):
    n = proj.shape[0]
    t = HGRN_TILE
    nblk = seq // t

    def fwd(colblk):
        return pl.BlockSpec((t, HG_DIM), lambda b, h, i: (b * nblk + i, colblk * HG_HEADS + h))

    def bwd(colblk):
        return pl.BlockSpec((t, HG_DIM), lambda b, h, i: (b * nblk + nblk - 1 - i, colblk * HG_HEADS + h))

    out_f = pl.BlockSpec((t, HG_DIM), lambda b, h, i: (b * nblk + i, h))
    out_b = pl.BlockSpec((t, HG_DIM), lambda b, h, i: (b * nblk + nblk - 1 - i, h))
    return pl.pallas_call(
        _hgrn_kernel,
        grid=(batch, HG_HEADS, nblk),
        in_specs=[
            fwd(COL_HQ), fwd(COL_HF_FWD), fwd(COL_HI),
            bwd(COL_HQ), bwd(COL_HF_BWD), bwd(COL_HI),
            pl.BlockSpec((1, 1, HG_DIM), lambda b, h, i: (h, 0, 0)),
            pl.BlockSpec((1, 1, HG_DIM), lambda b, h, i: (HG_HEADS + h, 0, 0)),
        ],
        out_specs=[out_f, out_b],
        out_shape=[jax.ShapeDtypeStruct((n, HG_WIDTH), BF16)] * 2,
        scratch_shapes=[pltpu.VMEM((HG_DIM, HG_DIM), F32)] * 2,
        compiler_params=pltpu.CompilerParams(
            dimension_semantics=("parallel", "parallel", "arbitrary"), vmem_limit_bytes=VMEM_LIMIT),
        name="hgrn",
    )(proj, proj, proj, proj, proj, proj, lbs, lbs)


def _attn_kernel(q_ref, kc_ref, kp_ref, kn_ref, vc_ref, vp_ref, vn_ref, o_ref, l_ref,
                 kw_ref, vw_ref, *, tq, length):
    n = pl.program_id(2)
    half = ATT_HALF
    kw_ref[0:half, :] = kp_ref[...]
    kw_ref[half:half + tq, :] = kc_ref[...]
    kw_ref[half + tq:, :] = kn_ref[...]
    vw_ref[0:half, :] = vp_ref[...]
    vw_ref[half:half + tq, :] = vc_ref[...]
    vw_ref[half + tq:, :] = vn_ref[...]

    qb_rows = ATT_QBLOCK
    win = qb_rows + 2 * half
    i_idx = lax.broadcasted_iota(jnp.int32, (qb_rows, win), 0)
    j_idx = lax.broadcasted_iota(jnp.int32, (qb_rows, win), 1)
    band = (j_idx >= i_idx) & (j_idx <= i_idx + 2 * half)

    heads = [slice(h * ATT_DIM, (h + 1) * ATT_DIM) for h in range(ATT_HEADS)]
    head_lane = lax.broadcasted_iota(jnp.int32, (qb_rows, LANES), 1)

    def masked_scores(r0):
        base = n * tq + r0 - half
        valid = band & (j_idx >= -base) & (j_idx < length - base)
        q = q_ref[r0:r0 + qb_rows, :]
        kw = kw_ref[r0:r0 + win, :]
        return [jnp.where(valid, _dot_nt(q[:, cs], kw[:, cs]), NEG) for cs in heads]

    blocks = list(range(0, tq, qb_rows))
    nxt = masked_scores(blocks[0])
    for pos, r0 in enumerate(blocks):
        scores = nxt
        if pos + 1 < len(blocks):
            nxt = masked_scores(blocks[pos + 1])
        vw = vw_ref[r0:r0 + win, :]
        maxes = [jnp.max(s, axis=-1, keepdims=True) for s in scores]
        probs = [jnp.exp(s - m) for s, m in zip(scores, maxes)]
        dens = [jnp.sum(p, axis=-1, keepdims=True) for p in probs]
        outs = [_dot(p.astype(BF16), vw[:, cs]) / den for p, cs, den in zip(probs, heads, dens)]
        o_ref[r0:r0 + qb_rows, :] = jnp.concatenate(outs, axis=1).astype(o_ref.dtype)
        lse = jnp.zeros((qb_rows, LANES), F32)
        for h, (m, den) in enumerate(zip(maxes, dens)):
            lse = jnp.where(head_lane == h, m + jnp.log(den), lse)
        l_ref[r0:r0 + qb_rows, :] = lse


def _attention(qkv, col0):
    batch, dil, length, _ = qkv.shape
    tq = min(TOKEN_TILE, length)
    nq = length // tq
    hb = tq // ATT_HALF
    n_hblk = length // ATT_HALF

    def cur(col):
        return pl.BlockSpec((None, None, tq, ATT_WIDTH), lambda b, r, n: (b, r, n, col))

    def prev(col):
        return pl.BlockSpec((None, None, ATT_HALF, ATT_WIDTH),
                            lambda b, r, n: (b, r, jnp.maximum(n * hb - 1, 0), col))

    def nxt(col):
        return pl.BlockSpec((None, None, ATT_HALF, ATT_WIDTH),
                            lambda b, r, n: (b, r, jnp.minimum((n + 1) * hb, n_hblk - 1), col))

    out_spec = pl.BlockSpec((None, None, tq, ATT_WIDTH), lambda b, r, n: (b, r, n, 0))
    return pl.pallas_call(
        functools.partial(_attn_kernel, tq=tq, length=length),
        grid=(batch, dil, nq),
        in_specs=[cur(col0), cur(col0 + 1), prev(col0 + 1), nxt(col0 + 1),
                  cur(col0 + 2), prev(col0 + 2), nxt(col0 + 2)],
        out_specs=[out_spec, pl.BlockSpec((None, None, tq, LANES), lambda b, r, n: (b, r, n, 0))],
        out_shape=[jax.ShapeDtypeStruct((batch, dil, length, ATT_WIDTH), BF16),
                   jax.ShapeDtypeStruct((batch, dil, length, LANES), F32)],
        scratch_shapes=[pltpu.VMEM((tq + 2 * ATT_HALF, ATT_WIDTH), BF16)] * 2,
        compiler_params=pltpu.CompilerParams(
            dimension_semantics=("parallel", "parallel", "parallel"), vmem_limit_bytes=VMEM_LIMIT),
        name=f"attn_d{dil}",
    )(qkv, qkv, qkv, qkv, qkv, qkv, qkv)


def _token_major(src_ref, stage_ref, tmp_ref):
    dil, rows = src_ref.shape[1], src_ref.shape[2]
    nc = src_ref.shape[3] // LANES
    for c in range(nc):
        cols = slice(c * LANES, (c + 1) * LANES)
        if dil == 4:
            for r in range(dil):
                stage_ref[c, pl.ds(r, rows, stride=dil), :] = src_ref[0, r, :, cols].astype(F32)
        else:
            group = 4 * rows
            for r4 in range(4):
                for m in range(4):
                    tmp_ref[c, pl.ds(r4 * group + m, rows, stride=4), :] = (
                        src_ref[0, r4 + 4 * m, :, cols].astype(F32))
            for r4 in range(4):
                stage_ref[c, pl.ds(r4, group, stride=4), :] = tmp_ref[c, r4 * group:(r4 + 1) * group, :]
    return jnp.concatenate([stage_ref[c] for c in range(nc)], axis=1)


def _outproj_kernel(of_ref, ob_ref, hg_ref, o1_ref, o2_ref, o3_ref, l1_ref, l2_ref, l3_ref,
                    x_ref, og_ref, w_ref, g2_ref, wrh_ref, wrl_ref, br_ref,
                    x2_ref, xn_ref, gate_ref, idx_ref, st_o2, st_o3, st_l2, st_l3, st_tmp_o, st_tmp_l):
    o = of_ref[...].astype(F32) + ob_ref[...].astype(F32)
    hg = hg_ref[...].astype(F32)
    parts = []
    for h in range(HG_HEADS):
        blk = o[:, h * HG_DIM:(h + 1) * HG_DIM]
        ms = jnp.mean(blk * blk, axis=-1, keepdims=True)
        parts.append(blk * lax.rsqrt(ms + EPS) * og_ref[...])
    o_hg = jnp.concatenate(parts, axis=1) * (hg * _sigmoid(hg))

    l1 = l1_ref[...]
    l2 = _token_major(l2_ref, st_l2, st_tmp_l)
    l3 = _token_major(l3_ref, st_l3, st_tmp_l)
    mx = jnp.maximum(jnp.maximum(l1, l2), l3)
    e1, e2, e3 = jnp.exp(l1 - mx), jnp.exp(l2 - mx), jnp.exp(l3 - mx)
    den = e1 + e2 + e3
    er = lax.broadcasted_iota(jnp.int32, (LANES, ATT_WIDTH), 0)
    ec = lax.broadcasted_iota(jnp.int32, (LANES, ATT_WIDTH), 1)
    expand = jnp.where(er == ec // ATT_DIM, 1.0, 0.0).astype(BF16)

    def per_lane(w):
        return _dot(w.astype(BF16), expand)

    o_att = (per_lane(e1 / den) * o1_ref[...].astype(F32)
             + per_lane(e2 / den) * _token_major(o2_ref, st_o2, st_tmp_o)
             + per_lane(e3 / den) * _token_major(o3_ref, st_o3, st_tmp_o))

    y = _dot(o_hg.astype(BF16), w_ref[0:HG_WIDTH, :]) + _dot(o_att.astype(BF16), w_ref[HG_WIDTH:, :])
    x2 = x_ref[...] + y
    x2_ref[...] = x2

    ms = jnp.mean(x2 * x2, axis=-1, keepdims=True)
    xn = x2 * lax.rsqrt(ms + EPS) * g2_ref[...]
    _rows_to_tiles(xn_ref, xn)
    xn_hi = xn.astype(BF16)
    xn_lo = (xn - xn_hi.astype(F32)).astype(BF16)
    logits = (_dot(xn_hi, wrh_ref[...]) + _dot(xn_lo, wrh_ref[...]) + _dot(xn_hi, wrl_ref[...])
              + br_ref[...])

    lane = lax.broadcasted_iota(jnp.int32, logits.shape, 1)
    lane_f = lane.astype(F32)
    work = jnp.where(lane < N_EXPERTS, logits, -jnp.inf)
    vals, idxs = [], []
    for _ in range(TOP_K):
        m = jnp.max(work, axis=-1, keepdims=True)
        idx = jnp.min(jnp.where(work == m, lane_f, float(LANES)), axis=-1, keepdims=True)
        vals.append(m)
        idxs.append(idx)
        work = jnp.where(lane_f == idx, -jnp.inf, work)
    es = [jnp.exp(v - vals[0]) for v in vals]
    den = es[0] + es[1] + es[2] + es[3]
    gate_out = jnp.zeros(logits.shape, F32)
    idx_out = jnp.zeros(logits.shape, F32)
    for k in range(TOP_K):
        gate_out = jnp.where(lane == k, es[k] / den, gate_out)
        idx_out = jnp.where(lane == k, idxs[k], idx_out)
    gate_ref[...] = gate_out
    idx_ref[...] = idx_out.astype(jnp.int32)


def _outproj(o_f, o_b, proj, atts, lses, x2d, og, w_out_bf16, g2, wr_hi, wr_lo, br, seq):
    n = x2d.shape[0]
    t = TOKEN_TILE
    nt = seq // t
    row = lambda i: (i, 0)
    const = lambda i: (0, 0)
    half = pl.BlockSpec((t, 512), row)

    def residue_major(dil, width):
        return pl.BlockSpec((1, dil, t // dil, width), lambda i: (i // nt, 0, i % nt, 0))

    rm4, rm16 = residue_major(4, ATT_WIDTH), residue_major(16, ATT_WIDTH)
    lse, lse4, lse16 = pl.BlockSpec((t, LANES), row), residue_major(4, LANES), residue_major(16, LANES)
    wide, narrow = pltpu.VMEM((ATT_WIDTH // LANES, t, LANES), F32), pltpu.VMEM((1, t, LANES), F32)
    return pl.pallas_call(
        _outproj_kernel,
        grid=(n // t,),
        in_specs=[
            half, half, pl.BlockSpec((t, 512), lambda i: (i, COL_HG)),
            half, rm4, rm16, lse, lse4, lse16,
            pl.BlockSpec((t, D_MODEL), row),
            pl.BlockSpec((1, HG_DIM), const),
            pl.BlockSpec((D_MODEL, D_MODEL), const),
            pl.BlockSpec((1, D_MODEL), const),
            pl.BlockSpec((D_MODEL, LANES), const),
            pl.BlockSpec((D_MODEL, LANES), const),
            pl.BlockSpec((1, LANES), const),
        ],
        out_specs=[pl.BlockSpec((t, D_MODEL), row), pl.BlockSpec((t * ROW_TILES, LANES), row),
                   pl.BlockSpec((t, LANES), row), pl.BlockSpec((t, LANES), row)],
        out_shape=[jax.ShapeDtypeStruct((n, D_MODEL), F32), jax.ShapeDtypeStruct((n * ROW_TILES, LANES), F32),
                   jax.ShapeDtypeStruct((n, LANES), F32), jax.ShapeDtypeStruct((n, LANES), jnp.int32)],
        scratch_shapes=[wide, wide, narrow, narrow, wide, narrow],
        compiler_params=pltpu.CompilerParams(
            dimension_semantics=("parallel",), vmem_limit_bytes=VMEM_LIMIT),
        name="outproj",
    )(o_f, o_b, proj, *atts, *lses, x2d, og, w_out_bf16, g2, wr_hi, wr_lo, br)


def _moe_kernel(bexp_ref, bfirst_ref, bslot_ref, bnext_ref, nused_ref,
                xs_ref, wu_hbm, bg_ref, bl_ref, wd_hbm, bd_ref,
                y_ref, wu_buf, wd_buf, wg_s, wl_s, wd_s, sems):
    i = pl.program_id(0)

    def weight_copies(expert, slot):
        return (pltpu.make_async_copy(wu_hbm.at[expert], wu_buf.at[slot], sems.at[0, slot]),
                pltpu.make_async_copy(wd_hbm.at[expert], wd_buf.at[slot], sems.at[1, slot]))

    @pl.when(bfirst_ref[i] == 1)
    def _():
        slot = bslot_ref[i]

        @pl.when(i == 0)
        def _():
            for cp in weight_copies(bexp_ref[0], 0):
                cp.start()

        for cp in weight_copies(bexp_ref[i], slot):
            cp.wait()

        @pl.when(bnext_ref[i] >= 0)
        def _():
            for cp in weight_copies(bnext_ref[i], 1 - slot):
                cp.start()

        r = lax.broadcasted_iota(jnp.int32, (2 * LANES, 2 * LANES), 0)
        c = lax.broadcasted_iota(jnp.int32, (2 * LANES, 2 * LANES), 1)
        src = jnp.where(c < LANES, 2 * c, 2 * (c - LANES) + 1)
        perm = jnp.where(r == src, 1.0, 0.0).astype(BF16)
        rows = 256
        for rb in range(D_MODEL // rows):
            rs = slice(rb * rows, (rb + 1) * rows)
            for cb in range(D_EXPERT // LANES):
                w = wu_buf[slot, rs, cb * 2 * LANES:(cb + 1) * 2 * LANES].astype(BF16)
                split = _dot(w, perm).astype(BF16)
                wg_s[rs, cb * LANES:(cb + 1) * LANES] = split[:, :LANES]
                wl_s[rs, cb * LANES:(cb + 1) * LANES] = split[:, LANES:]
        wd_s[...] = wd_buf[slot].astype(BF16)

    @pl.when(i < nused_ref[0])
    def _():
        x = jnp.concatenate(_tiles_to_rows(xs_ref, MOE_BLOCK_ROWS), axis=1).astype(BF16)
        hglu = _dot(x, wg_s[...]) + bg_ref[0]
        hlin = _dot(x, wl_s[...]) + bl_ref[0]
        glu = jnp.minimum(hglu, SWIGLU_LIMIT)
        lin = jnp.clip(hlin, -SWIGLU_LIMIT, SWIGLU_LIMIT)
        act = glu * _sigmoid(SWIGLU_ALPHA * glu) * (lin + 1.0)
        _rows_to_tiles(y_ref, _dot(act.astype(BF16), wd_s[...]) + bd_ref[0])

    @pl.when(i >= nused_ref[0])
    def _():
        y_ref[...] = jnp.zeros_like(y_ref)


def _moe(block_exp, block_first, block_slot, block_next, n_used, xs, w_up, b_glu, b_lin, w_down, b_down):
    p_rows = xs.shape[0] // ROW_TILES
    bm = MOE_BLOCK_ROWS
    nb = p_rows // bm
    exp3 = lambda i, be, bf, bs, bn, nu: (be[i], 0, 0)
    grid_spec = pltpu.PrefetchScalarGridSpec(
        num_scalar_prefetch=5,
        grid=(nb,),
        in_specs=[
            pl.BlockSpec((bm * ROW_TILES, LANES), lambda i, be, bf, bs, bn, nu: (jnp.minimum(i, nu[0] - 1), 0)),
            pl.BlockSpec(memory_space=pl.ANY),
            pl.BlockSpec((1, 1, D_EXPERT), exp3),
            pl.BlockSpec((1, 1, D_EXPERT), exp3),
            pl.BlockSpec(memory_space=pl.ANY),
            pl.BlockSpec((1, 1, D_MODEL), exp3),
        ],
        out_specs=pl.BlockSpec((bm * ROW_TILES, LANES), lambda i, be, bf, bs, bn, nu: (i, 0)),
        scratch_shapes=[pltpu.VMEM((2, D_MODEL, 2 * D_EXPERT), F32), pltpu.VMEM((2, D_EXPERT, D_MODEL), F32),
                        pltpu.VMEM((D_MODEL, D_EXPERT), BF16), pltpu.VMEM((D_MODEL, D_EXPERT), BF16),
                        pltpu.VMEM((D_EXPERT, D_MODEL), BF16), pltpu.SemaphoreType.DMA((2, 2))],
    )
    return pl.pallas_call(
        _moe_kernel,
        grid_spec=grid_spec,
        out_shape=jax.ShapeDtypeStruct((p_rows * ROW_TILES, LANES), F32),
        compiler_params=pltpu.CompilerParams(
            dimension_semantics=("arbitrary",), vmem_limit_bytes=VMEM_LIMIT),
        name="moe",
    )(block_exp, block_first, block_slot, block_next, n_used, xs, w_up, b_glu, b_lin, w_down, b_down)


def _dispatch_kernel(dest_ref, zstart_ref, xn_ref, xs_hbm, zero_ref, sem, zsem):
    i = pl.program_id(0)
    bm = MOE_BLOCK_ROWS
    tokens = xn_ref.shape[0] // ROW_TILES

    def zero_copy(e):
        start = pl.multiple_of(zstart_ref[e] * ROW_TILES, bm * ROW_TILES)
        return pltpu.make_async_copy(zero_ref, xs_hbm.at[pl.ds(start, bm * ROW_TILES)], zsem)

    @pl.when(i == 0)
    def _():
        zero_ref[...] = jnp.zeros_like(zero_ref)
        for e in range(N_EXPERTS):
            @pl.when(zstart_ref[e] >= 0)
            def _():
                zero_copy(e).start()
        for e in range(N_EXPERTS):
            @pl.when(zstart_ref[e] >= 0)
            def _():
                zero_copy(e).wait()

    t0 = i * tokens

    def body(j, carry):
        src = xn_ref.at[pl.ds(pl.multiple_of(j * ROW_TILES, ROW_TILES), ROW_TILES)]
        for k in range(TOP_K):
            d = pl.multiple_of(dest_ref[(t0 + j) * TOP_K + k] * ROW_TILES, ROW_TILES)
            pltpu.make_async_copy(src, xs_hbm.at[pl.ds(d, ROW_TILES)], sem).start(priority=k % 2)
        return carry

    lax.fori_loop(0, tokens, body, 0, unroll=8)
    for k in range(TOP_K):
        pltpu.make_async_copy(xn_ref, xs_hbm.at[pl.ds(0, tokens * ROW_TILES)], sem).wait()


def _dispatch(dest, zstart, xn, p_rows):
    n = xn.shape[0] // ROW_TILES
    t = DISPATCH_TOKENS
    grid_spec = pltpu.PrefetchScalarGridSpec(
        num_scalar_prefetch=2,
        grid=(n // t,),
        in_specs=[pl.BlockSpec((t * ROW_TILES, LANES), lambda i, d, z: (i, 0))],
        out_specs=pl.BlockSpec(memory_space=pl.ANY),
        scratch_shapes=[pltpu.VMEM((MOE_BLOCK_ROWS * ROW_TILES, LANES), F32),
                        pltpu.SemaphoreType.DMA, pltpu.SemaphoreType.DMA],
    )
    return pl.pallas_call(
        _dispatch_kernel,
        grid_spec=grid_spec,
        out_shape=jax.ShapeDtypeStruct((p_rows * ROW_TILES, LANES), F32),
        compiler_params=pltpu.CompilerParams(
            dimension_semantics=("arbitrary",), vmem_limit_bytes=VMEM_LIMIT),
        name="dispatch",
    )(dest, zstart, xn)


def _combine_kernel(dest_ref, y_hbm, x2_ref, gate_ref, out_ref, buf_ref, sems):
    i = pl.program_id(0)
    tc = COMBINE_TOKENS

    def issue(step, slot):
        t0 = step * tc

        def body(j, carry):
            for k in range(TOP_K):
                d = pl.multiple_of(dest_ref[(t0 + j) * TOP_K + k] * ROW_TILES, ROW_TILES)
                r = pl.multiple_of((k * tc + j) * ROW_TILES, ROW_TILES)
                pltpu.make_async_copy(y_hbm.at[pl.ds(d, ROW_TILES)], buf_ref.at[slot, pl.ds(r, ROW_TILES)],
                                      sems.at[slot]).start(priority=k % 2)
            return carry

        lax.fori_loop(0, tc, body, 0, unroll=8)

    @pl.when(i == 0)
    def _():
        issue(0, 0)

    @pl.when(i + 1 < pl.num_programs(0))
    def _():
        issue(i + 1, (i + 1) % 2)

    slot = i % 2
    pltpu.make_async_copy(y_hbm.at[pl.ds(0, TOP_K * tc * ROW_TILES)], buf_ref.at[slot], sems.at[slot]).wait()
    rows = buf_ref.at[slot]
    sub = 64
    for r0 in range(0, tc, sub):
        gate = gate_ref[r0:r0 + sub, :]
        gates = [jnp.broadcast_to(gate[:, k:k + 1], (sub, LANES)) for k in range(TOP_K)]
        for c in range(ROW_TILES):
            acc = x2_ref[r0:r0 + sub, c * LANES:(c + 1) * LANES]
            for k in range(TOP_K):
                acc = acc + gates[k] * rows[pl.ds((k * tc + r0) * ROW_TILES + c, sub, stride=ROW_TILES), :]
            out_ref[r0:r0 + sub, c * LANES:(c + 1) * LANES] = acc


def _combine(dest, y, x2, gates):
    n = x2.shape[0]
    tc = COMBINE_TOKENS
    grid_spec = pltpu.PrefetchScalarGridSpec(
        num_scalar_prefetch=1,
        grid=(n // tc,),
        in_specs=[pl.BlockSpec(memory_space=pl.ANY),
                  pl.BlockSpec((tc, D_MODEL), lambda i, d: (i, 0)),
                  pl.BlockSpec((tc, LANES), lambda i, d: (i, 0))],
        out_specs=pl.BlockSpec((tc, D_MODEL), lambda i, d: (i, 0)),
        scratch_shapes=[pltpu.VMEM((2, TOP_K * tc * ROW_TILES, LANES), F32), pltpu.SemaphoreType.DMA((2,))],
    )
    return pl.pallas_call(
        _combine_kernel,
        grid_spec=grid_spec,
        out_shape=jax.ShapeDtypeStruct((n, D_MODEL), F32),
        compiler_params=pltpu.CompilerParams(
            dimension_semantics=("arbitrary",), vmem_limit_bytes=VMEM_LIMIT),
        name="combine",
    )(dest, y, x2, gates)


def _route(top_idx):
    n = top_idx.shape[0]
    a = n * TOP_K
    bm = MOE_BLOCK_ROWS
    nb = a // bm + N_EXPERTS
    e_flat = top_idx.reshape(a)
    onehot = (e_flat[:, None] == jnp.arange(N_EXPERTS, dtype=jnp.int32)[None, :]).astype(jnp.int32)
    csum = jnp.cumsum(onehot, axis=0)
    counts = csum[-1]
    padded = ((counts + bm - 1) // bm) * bm
    pad_end = jnp.cumsum(padded)
    pad_start = pad_end - padded
    dest = jnp.sum(onehot * (csum - 1 + pad_start[None, :]), axis=1)
    n_used = (pad_end[-1] // bm).astype(jnp.int32)
    blk = jnp.arange(nb, dtype=jnp.int32)
    bexp = jnp.sum((pad_end[None, :] <= (blk * bm)[:, None]).astype(jnp.int32), axis=1)
    bexp = jnp.minimum(bexp, N_EXPERTS - 1)
    bexp = jnp.where(blk < n_used, bexp, bexp[jnp.maximum(n_used - 1, 0)])
    bfirst = jnp.concatenate([jnp.ones((1,), jnp.int32), (bexp[1:] != bexp[:-1]).astype(jnp.int32)])
    bslot = (jnp.cumsum(bfirst) - 1) % 2
    later = jnp.where(bexp[None, :] > bexp[:, None], bexp[None, :], N_EXPERTS)
    bnext = jnp.min(later, axis=1)
    bnext = jnp.where(bnext < N_EXPERTS, bnext, -1).astype(jnp.int32)
    zstart = jnp.where(counts > 0, pad_end - bm, -1).astype(jnp.int32)
    return dest.astype(jnp.int32), zstart, (bexp, bfirst, bslot.astype(jnp.int32), bnext, n_used.reshape(1))


def kernel(x, positions, norm1_g, w_in, q_norm_g, k_norm_g, hgrn_lower_bounds, hgrn_onorm_g,
           w_out, norm2_g, w_router, b_router, w_up, b_up, w_down, b_down):
    batch, seq, d = x.shape
    n = batch * seq
    depth = norm1_g.shape[0]
    lbs_all = jnp.cumsum(jax.nn.softmax(hgrn_lower_bounds.astype(F32), axis=0), axis=0)
    half = ATT_DIM // 2
    inv = 1.0 / (ROPE_THETA ** (jnp.arange(half, dtype=F32) / half))
    inv_tab = jnp.tile(inv, LANES // half).reshape(1, LANES)
    pos_col = positions.reshape(n, 1)

    x2d = x.reshape(n, d)
    for l in range(depth):
        lbs = lbs_all[l].reshape(2 * HG_HEADS, 1, HG_DIM)
        proj, qkv4, qkv16 = _inproj(
            x2d, pos_col, inv_tab, norm1_g[l].reshape(1, d), w_in[l].astype(BF16),
            jnp.tile(q_norm_g[l], LANES // ATT_DIM).reshape(1, LANES),
            jnp.tile(k_norm_g[l], LANES // ATT_DIM).reshape(1, LANES), batch, seq)
        o_f, o_b = _hgrn(proj, lbs, batch, seq)
        o1, l1 = _attention(proj.reshape(batch, 1, seq, IN_COLS), COL_AQ)
        o4, l4 = _attention(qkv4, 0)
        o16, l16 = _attention(qkv16, 0)
        atts = [o1.reshape(n, ATT_WIDTH), o4, o16]
        lses = [l1.reshape(n, LANES), l4, l16]

        wr = jnp.pad(w_router[l], ((0, 0), (0, LANES - N_EXPERTS)))
        wr_hi = wr.astype(BF16)
        wr_lo = (wr - wr_hi.astype(F32)).astype(BF16)
        br = jnp.pad(b_router[l], (0, LANES - N_EXPERTS)).reshape(1, LANES)
        x2, xn, gates, top_idx = _outproj(
            o_f, o_b, proj, atts, lses, x2d, hgrn_onorm_g[l].reshape(1, HG_DIM),
            w_out[l].astype(BF16), norm2_g[l].reshape(1, d), wr_hi, wr_lo, br, seq)
        dest, zstart, blocks = _route(top_idx[:, :TOP_K])
        xs = _dispatch(dest, zstart, xn, blocks[0].shape[0] * MOE_BLOCK_ROWS)
        y = _moe(*blocks, xs, w_up[l],
                 b_up[l][:, 0::2].reshape(N_EXPERTS, 1, D_EXPERT),
                 b_up[l][:, 1::2].reshape(N_EXPERTS, 1, D_EXPERT),
                 w_down[l], b_down[l].reshape(N_EXPERTS, 1, D_MODEL))
        x2d = _combine(dest, y, x2, gates)
    return x2d.reshape(batch, seq, d)
```

```python
import functools

import jax
import jax.numpy as jnp
from jax import lax
from jax.experimental import pallas as pl
from jax.experimental.pallas import tpu as pltpu

F32 = jnp.float32
BF16 = jnp.bfloat16

D_MODEL = 1024
HG_HEADS = 4
HG_DIM = 128
HG_WIDTH = HG_HEADS * HG_DIM
HG_CHUNK = 64
ATT_HEADS = 8
ATT_DIM = 64
ATT_WIDTH = ATT_HEADS * ATT_DIM
DILATED_PATTERNS = ((128, 1), (512, 4), (2048, 16))
ATT_HALF = 64
ATT_QBLOCK = 128
ROPE_THETA = 10000.0
IN_COLS = 5 * HG_WIDTH + 3 * ATT_WIDTH
N_EXPERTS = 32
TOP_K = 4
D_EXPERT = D_MODEL
SWIGLU_LIMIT = 7.0
SWIGLU_ALPHA = 1.702
EPS = 1e-6
NEG = -1e30

COL_HQ, COL_HF_FWD, COL_HF_BWD, COL_HI, COL_HG, COL_AQ, COL_AK, COL_AV = range(8)

TOKEN_TILE = 512
INPROJ_TILE = 1024
HGRN_TILE = 1024
MOE_BLOCK_ROWS = 512
DISPATCH_TOKENS = 512
COMBINE_TOKENS = 256
LANES = 128
ROW_TILES = D_MODEL // LANES
VMEM_LIMIT = 60 * 1024 * 1024


def _dot(a, b):
    return jnp.dot(a, b, preferred_element_type=F32)


def _dot_nt(a, b):
    return lax.dot_general(a, b, (((1,), (1,)), ((), ())), preferred_element_type=F32)


def _dot_tn(a, b):
    return lax.dot_general(a, b, (((0,), (0,)), ((), ())), preferred_element_type=F32)


def _sigmoid(x):
    return 1.0 / (1.0 + jnp.exp(-x))


def _rows_to_tiles(dst_ref, x):
    for c in range(ROW_TILES):
        dst_ref[pl.ds(c, x.shape[0], stride=ROW_TILES), :] = x[:, c * LANES:(c + 1) * LANES]


def _tiles_to_rows(src, rows, first_row=0):
    return [src[pl.ds(first_row * ROW_TILES + c, rows, stride=ROW_TILES), :] for c in range(ROW_TILES)]


def _head_norm_rope(p, gain, cos, sin_signed, scale):
    lane = lax.broadcasted_iota(jnp.int32, (p.shape[0], LANES), 1)
    low = lane < ATT_DIM
    first_half = (lane % ATT_DIM) < (ATT_DIM // 2)
    outs = []
    for t in range(ATT_WIDTH // LANES):
        blk = p[:, t * LANES:(t + 1) * LANES]
        sq = blk * blk
        s_low = jnp.sum(jnp.where(low, sq, 0.0), axis=-1, keepdims=True)
        s_high = jnp.sum(jnp.where(low, 0.0, sq), axis=-1, keepdims=True)
        r = jnp.where(low, lax.rsqrt(s_low * (1.0 / ATT_DIM) + EPS),
                      lax.rsqrt(s_high * (1.0 / ATT_DIM) + EPS))
        y = blk * r * gain
        partner = jnp.where(first_half, pltpu.roll(y, LANES - ATT_DIM // 2, axis=1),
                            pltpu.roll(y, ATT_DIM // 2, axis=1))
        outs.append((y * cos + partner * sin_signed) * scale)
    return jnp.concatenate(outs, axis=1)


def _inproj_kernel(x_ref, pos_ref, inv_ref, g1_ref, w_ref, qg_ref, kg_ref,
                   out_ref, d4_ref, d16_ref, stage_ref, stage2_ref):
    x = x_ref[...]
    ms = jnp.mean(x * x, axis=-1, keepdims=True)
    h = (x * lax.rsqrt(ms + EPS) * g1_ref[...]).astype(BF16)
    ang = pos_ref[...].astype(F32) * inv_ref[...]
    lane = lax.broadcasted_iota(jnp.int32, ang.shape, 1)
    cos = jnp.cos(ang)
    sin_signed = jnp.where((lane % ATT_DIM) < (ATT_DIM // 2), -jnp.sin(ang), jnp.sin(ang))
    order = (COL_AQ, COL_AK, COL_AV, COL_HQ, COL_HF_FWD, COL_HF_BWD, COL_HI, COL_HG)
    nxt = _dot(h, w_ref[:, order[0] * 512:(order[0] + 1) * 512])
    for pos, j in enumerate(order):
        p = nxt
        if pos + 1 < len(order):
            jn = order[pos + 1]
            nxt = _dot(h, w_ref[:, jn * 512:(jn + 1) * 512])
        if j == COL_AQ:
            p = _head_norm_rope(p, qg_ref[...], cos, sin_signed, ATT_DIM ** -0.5)
        elif j == COL_AK:
            p = _head_norm_rope(p, kg_ref[...], cos, sin_signed, 1.0)
        out_ref[:, j * 512:(j + 1) * 512] = p.astype(BF16)
        if j >= COL_AQ:
            rows4, rows16 = x.shape[0] // 4, x.shape[0] // 16
            for c in range(ATT_WIDTH // LANES):
                cols = slice((j - COL_AQ) * ATT_WIDTH + c * LANES, (j - COL_AQ) * ATT_WIDTH + (c + 1) * LANES)
                stage_ref[c] = p[:, c * LANES:(c + 1) * LANES]
                for r4 in range(4):
                    group = stage_ref[c, pl.ds(r4, rows4, stride=4), :]
                    d4_ref[0, r4, :, cols] = group.astype(BF16)
                    stage2_ref[c, r4 * rows4:(r4 + 1) * rows4, :] = group
                for r4 in range(4):
                    for m in range(4):
                        d16_ref[0, r4 + 4 * m, :, cols] = (
                            stage2_ref[c, pl.ds(r4 * rows4 + m, rows16, stride=4), :].astype(BF16))


def _inproj(x2d, pos_col, inv_tab, g1, w_in_bf16, qg, kg, batch, seq):
    n = x2d.shape[0]
    t = INPROJ_TILE
    nt = seq // t
    const = lambda i: (0, 0)
    qkv = 3 * ATT_WIDTH

    def residue_major(dil):
        spec = pl.BlockSpec((1, dil, t // dil, qkv), lambda i: (i // nt, 0, i % nt, 0))
        return spec, jax.ShapeDtypeStruct((batch, dil, seq // dil, qkv), BF16)

    spec4, shape4 = residue_major(4)
    spec16, shape16 = residue_major(16)
    return pl.pallas_call(
        _inproj_kernel,
        grid=(n // t,),
        in_specs=[
            pl.BlockSpec((t, D_MODEL), lambda i: (i, 0)),
            pl.BlockSpec((t, 1), lambda i: (i, 0)),
            pl.BlockSpec((1, LANES), const),
            pl.BlockSpec((1, D_MODEL), const),
            pl.BlockSpec((D_MODEL, IN_COLS), const, pipeline_mode=pl.Buffered(1)),
            pl.BlockSpec((1, LANES), const),
            pl.BlockSpec((1, LANES), const),
        ],
        out_specs=[pl.BlockSpec((t, IN_COLS), lambda i: (i, 0)), spec4, spec16],
        out_shape=[jax.ShapeDtypeStruct((n, IN_COLS), BF16), shape4, shape16],
        scratch_shapes=[pltpu.VMEM((ATT_WIDTH // LANES, t, LANES), F32)] * 2,
        compiler_params=pltpu.CompilerParams(
            dimension_semantics=("parallel",), vmem_limit_bytes=VMEM_LIMIT),
        name="inproj",
    )(x2d, pos_col, inv_tab, g1, w_in_bf16, qg, kg)


def _hgrn_direction(q_ref, z_ref, v_ref, lb, state_t, reverse):
    c = HG_CHUNK
    t = q_ref.shape[0]
    n = t // c
    row = lax.broadcasted_iota(jnp.int32, (c, c), 0)
    col = lax.broadcasted_iota(jnp.int32, (c, c), 1)
    mask = (row <= col) if reverse else (row >= col)
    tri = jnp.where(mask, 1.0, 0.0).astype(BF16)
    last_row = 0 if reverse else c - 1

    z = z_ref[...].astype(F32)
    q = q_ref[...].astype(F32)
    v = v_ref[...]
    sg = _sigmoid(z)
    f = lb + (1.0 - lb) * sg
    k = (1.0 - lb) * (1.0 - sg)
    lf = jnp.log(f)
    lf_hi = lf.astype(BF16)
    lf_lo = (lf - lf_hi.astype(F32)).astype(BF16)
    chunks = [slice(j * c, (j + 1) * c) for j in range(n)]
    b = jnp.concatenate([_dot(tri, lf_hi[rs]) + _dot(tri, lf_lo[rs]) for rs in chunks], axis=0)
    b_last = b.reshape(n, c, HG_DIM)[:, last_row:last_row + 1, :]
    decay = jnp.exp(b_last)
    qt = (q * _sigmoid(q) * jnp.exp(b)).astype(BF16)
    kt_f32 = k * jnp.exp(-b)
    kt = kt_f32.astype(BF16)
    kd = (kt_f32.reshape(n, c, HG_DIM) * decay).reshape(t, HG_DIM).astype(BF16)

    outs, updates = [], []
    for rs in chunks:
        a = jnp.where(mask, _dot_nt(qt[rs], kt[rs]), 0.0)
        outs.append(_dot(a.astype(BF16), v[rs]))
        updates.append(_dot_tn(v[rs], kd[rs]))
    for j in (reversed(range(n)) if reverse else range(n)):
        outs[j] = outs[j] + _dot_nt(qt[chunks[j]], state_t.astype(BF16))
        state_t = state_t * decay[j] + updates[j]
    return jnp.concatenate(outs, axis=0), state_t


def _hgrn_kernel(qf_ref, zf_ref, vf_ref, qb_ref, zb_ref, vb_ref, lbf_ref, lbb_ref,
                 of_ref, ob_ref, sf_ref, sb_ref):
    @pl.when(pl.program_id(2) == 0)
    def _():
        sf_ref[...] = jnp.zeros_like(sf_ref)
        sb_ref[...] = jnp.zeros_like(sb_ref)

    o, sf = _hgrn_direction(qf_ref, zf_ref, vf_ref, lbf_ref[0], sf_ref[...], False)
    of_ref[...] = o.astype(of_ref.dtype)
    sf_ref[...] = sf
    o, sb = _hgrn_direction(qb_ref, zb_ref, vb_ref, lbb_ref[0], sb_ref[...], True)
    ob_ref[...] = o.astype(ob_ref.dtype)
    sb_ref[...] = sb


def _hgrn(proj, lbs, batch, seq):
    n = proj.shape[0]
    t = HGRN_TILE
    nblk = seq // t

    def fwd(colblk):
        return pl.BlockSpec((t, HG_DIM), lambda b, h, i: (b * nblk + i, colblk * HG_HEADS + h))

    def bwd(colblk):
        return pl.BlockSpec((t, HG_DIM), lambda b, h, i: (b * nblk + nblk - 1 - i, colblk * HG_HEADS + h))

    out_f = pl.BlockSpec((t, HG_DIM), lambda b, h, i: (b * nblk + i, h))
    out_b = pl.BlockSpec((t, HG_DIM), lambda b, h, i: (b * nblk + nblk - 1 - i, h))
    return pl.pallas_call(
        _hgrn_kernel,
        grid=(batch, HG_HEADS, nblk),
        in_specs=[
            fwd(COL_HQ), fwd(COL_HF_FWD), fwd(COL_HI),
            bwd(COL_HQ), bwd(COL_HF_BWD), bwd(COL_HI),
            pl.BlockSpec((1, 1, HG_DIM), lambda b, h, i: (h, 0, 0)),
            pl.BlockSpec((1, 1, HG_DIM), lambda b, h, i: (HG_HEADS + h, 0, 0)),
        ],
        out_specs=[out_f, out_b],
        out_shape=[jax.ShapeDtypeStruct((n, HG_WIDTH), BF16)] * 2,
        scratch_shapes=[pltpu.VMEM((HG_DIM, HG_DIM), F32)] * 2,
        compiler_params=pltpu.CompilerParams(
            dimension_semantics=("parallel", "parallel", "arbitrary"), vmem_limit_bytes=VMEM_LIMIT),
        name="hgrn",
    )(proj, proj, proj, proj, proj, proj, lbs, lbs)


def _attn_kernel(q_ref, kc_ref, kp_ref, kn_ref, vc_ref, vp_ref, vn_ref, o_ref, l_ref,
                 kw_ref, vw_ref, *, tq, length):
    n = pl.program_id(2)
    half = ATT_HALF
    kw_ref[0:half, :] = kp_ref[...]
    kw_ref[half:half + tq, :] = kc_ref[...]
    kw_ref[half + tq:, :] = kn_ref[...]
    vw_ref[0:half, :] = vp_ref[...]
    vw_ref[half:half + tq, :] = vc_ref[...]
    vw_ref[half + tq:, :] = vn_ref[...]

    qb_rows = ATT_QBLOCK
    win = qb_rows + 2 * half
    i_idx = lax.broadcasted_iota(jnp.int32, (qb_rows, win), 0)
    j_idx = lax.broadcasted_iota(jnp.int32, (qb_rows, win), 1)
    band = (j_idx >= i_idx) & (j_idx <= i_idx + 2 * half)

    pairs = [slice(p * LANES, (p + 1) * LANES) for p in range(ATT_HEADS // 2)]
    head_lane = lax.broadcasted_iota(jnp.int32, (qb_rows, LANES), 1)
    even_half = head_lane < ATT_DIM
    keep_even = jnp.where(lax.broadcasted_iota(jnp.int32, (1, LANES), 1) < ATT_DIM, 1.0, 0.0).astype(BF16)
    keep_odd = (1.0 - keep_even.astype(F32)).astype(BF16)

    def masked_scores(r0):
        base = n * tq + r0 - half
        valid = band & (j_idx >= -base) & (j_idx < length - base)
        q = q_ref[r0:r0 + qb_rows, :]
        kw = kw_ref[r0:r0 + win, :]
        out = []
        for cs in pairs:
            for keep in (keep_even, keep_odd):
                out.append(jnp.where(valid, _dot_nt(q[:, cs] * keep, kw[:, cs]), NEG))
        return out

    blocks = list(range(0, tq, qb_rows))
    nxt = masked_scores(blocks[0])
    for pos, r0 in enumerate(blocks):
        scores = nxt
        if pos + 1 < len(blocks):
            nxt = masked_scores(blocks[pos + 1])
        vw = vw_ref[r0:r0 + win, :]
        maxes = [jnp.max(s, axis=-1, keepdims=True) for s in scores]
        probs = [jnp.exp(s - m) for s, m in zip(scores, maxes)]
        dens = [jnp.sum(p, axis=-1, keepdims=True) for p in probs]
        outs = []
        for p, cs in enumerate(pairs):
            pv_even = _dot(probs[2 * p].astype(BF16), vw[:, cs])
            pv_odd = _dot(probs[2 * p + 1].astype(BF16), vw[:, cs])
            outs.append(jnp.where(even_half, pv_even, pv_odd)
                        / jnp.where(even_half, dens[2 * p], dens[2 * p + 1]))
        o_ref[r0:r0 + qb_rows, :] = jnp.concatenate(outs, axis=1).astype(o_ref.dtype)
        lse = jnp.zeros((qb_rows, LANES), F32)
        for h, (m, den) in enumerate(zip(maxes, dens)):
            lse = jnp.where(head_lane == h, m + jnp.log(den), lse)
        l_ref[r0:r0 + qb_rows, :] = lse


def _attention(qkv, col0):
    batch, dil, length, _ = qkv.shape
    tq = min(TOKEN_TILE, length)
    nq = length // tq
    hb = tq // ATT_HALF
    n_hblk = length // ATT_HALF

    def cur(col):
        return pl.BlockSpec((None, None, tq, ATT_WIDTH), lambda b, r, n: (b, r, n, col))

    def prev(col):
        return pl.BlockSpec((None, None, ATT_HALF, ATT_WIDTH),
                            lambda b, r, n: (b, r, jnp.maximum(n * hb - 1, 0), col))

    def nxt(col):
        return pl.BlockSpec((None, None, ATT_HALF, ATT_WIDTH),
                            lambda b, r, n: (b, r, jnp.minimum((n + 1) * hb, n_hblk - 1), col))

    out_spec = pl.BlockSpec((None, None, tq, ATT_WIDTH), lambda b, r, n: (b, r, n, 0))
    return pl.pallas_call(
        functools.partial(_attn_kernel, tq=tq, length=length),
        grid=(batch, dil, nq),
        in_specs=[cur(col0), cur(col0 + 1), prev(col0 + 1), nxt(col0 + 1),
                  cur(col0 + 2), prev(col0 + 2), nxt(col0 + 2)],
        out_specs=[out_spec, pl.BlockSpec((None, None, tq, LANES), lambda b, r, n: (b, r, n, 0))],
        out_shape=[jax.ShapeDtypeStruct((batch, dil, length, ATT_WIDTH), BF16),
                   jax.ShapeDtypeStruct((batch, dil, length, LANES), F32)],
        scratch_shapes=[pltpu.VMEM((tq + 2 * ATT_HALF, ATT_WIDTH), BF16)] * 2,
        compiler_params=pltpu.CompilerParams(
            dimension_semantics=("parallel", "parallel", "parallel"), vmem_limit_bytes=VMEM_LIMIT),
        name=f"attn_d{dil}",
    )(qkv, qkv, qkv, qkv, qkv, qkv, qkv)


def _token_major(src_ref, stage_ref, tmp_ref):
    dil, rows = src_ref.shape[1], src_ref.shape[2]
    nc = src_ref.shape[3] // LANES
    for c in range(nc):
        cols = slice(c * LANES, (c + 1) * LANES)
        if dil == 4:
            for r in range(dil):
                stage_ref[c, pl.ds(r, rows, stride=dil), :] = src_ref[0, r, :, cols].astype(F32)
        else:
            group = 4 * rows
            for r4 in range(4):
                for m in range(4):
                    tmp_ref[c, pl.ds(r4 * group + m, rows, stride=4), :] = (
                        src_ref[0, r4 + 4 * m, :, cols].astype(F32))
            for r4 in range(4):
                stage_ref[c, pl.ds(r4, group, stride=4), :] = tmp_ref[c, r4 * group:(r4 + 1) * group, :]
    return jnp.concatenate([stage_ref[c] for c in range(nc)], axis=1)


def _outproj_kernel(of_ref, ob_ref, hg_ref, o1_ref, o2_ref, o3_ref, l1_ref, l2_ref, l3_ref,
                    x_ref, og_ref, w_ref, g2_ref, wrh_ref, wrl_ref, br_ref,
                    x2_ref, xn_ref, gate_ref, idx_ref, st_o2, st_o3, st_l2, st_l3, st_tmp_o, st_tmp_l):
    o = of_ref[...].astype(F32) + ob_ref[...].astype(F32)
    hg = hg_ref[...].astype(F32)
    parts = []
    for h in range(HG_HEADS):
        blk = o[:, h * HG_DIM:(h + 1) * HG_DIM]
        ms = jnp.mean(blk * blk, axis=-1, keepdims=True)
        parts.append(blk * lax.rsqrt(ms + EPS) * og_ref[...])
    o_hg = jnp.concatenate(parts, axis=1) * (hg * _sigmoid(hg))

    l1 = l1_ref[...]
    l2 = _token_major(l2_ref, st_l2, st_tmp_l)
    l3 = _token_major(l3_ref, st_l3, st_tmp_l)
    mx = jnp.maximum(jnp.maximum(l1, l2), l3)
    e1, e2, e3 = jnp.exp(l1 - mx), jnp.exp(l2 - mx), jnp.exp(l3 - mx)
    den = e1 + e2 + e3
    er = lax.broadcasted_iota(jnp.int32, (LANES, ATT_WIDTH), 0)
    ec = lax.broadcasted_iota(jnp.int32, (LANES, ATT_WIDTH), 1)
    expand = jnp.where(er == ec // ATT_DIM, 1.0, 0.0).astype(BF16)

    def per_lane(w):
        return _dot(w.astype(BF16), expand)

    o_att = (per_lane(e1 / den) * o1_ref[...].astype(F32)
             + per_lane(e2 / den) * _token_major(o2_ref, st_o2, st_tmp_o)
             + per_lane(e3 / den) * _token_major(o3_ref, st_o3, st_tmp_o))

    y = _dot(o_hg.astype(BF16), w_ref[0:HG_WIDTH, :]) + _dot(o_att.astype(BF16), w_ref[HG_WIDTH:, :])
    x2 = x_ref[...] + y
    x2_ref[...] = x2

    ms = jnp.mean(x2 * x2, axis=-1, keepdims=True)
    xn = x2 * lax.rsqrt(ms + EPS) * g2_ref[...]
    _rows_to_tiles(xn_ref, xn)
    xn_hi = xn.astype(BF16)
    xn_lo = (xn - xn_hi.astype(F32)).astype(BF16)
    logits = (_dot(xn_hi, wrh_ref[...]) + _dot(xn_lo, wrh_ref[...]) + _dot(xn_hi, wrl_ref[...])
              + br_ref[...])

    lane = lax.broadcasted_iota(jnp.int32, logits.shape, 1)
    lane_f = lane.astype(F32)
    work = jnp.where(lane < N_EXPERTS, logits, -jnp.inf)
    vals, idxs = [], []
    for _ in range(TOP_K):
        m = jnp.max(work, axis=-1, keepdims=True)
        idx = jnp.min(jnp.where(work == m, lane_f, float(LANES)), axis=-1, keepdims=True)
        vals.append(m)
        idxs.append(idx)
        work = jnp.where(lane_f == idx, -jnp.inf, work)
    es = [jnp.exp(v - vals[0]) for v in vals]
    den = es[0] + es[1] + es[2] + es[3]
    gate_out = jnp.zeros(logits.shape, F32)
    idx_out = jnp.zeros(logits.shape, F32)
    for k in range(TOP_K):
        gate_out = jnp.where(lane == k, es[k] / den, gate_out)
        idx_out = jnp.where(lane == k, idxs[k], idx_out)
    gate_ref[...] = gate_out
    idx_ref[...] = idx_out.astype(jnp.int32)


def _outproj(o_f, o_b, proj, atts, lses, x2d, og, w_out_bf16, g2, wr_hi, wr_lo, br, seq):
    n = x2d.shape[0]
    t = TOKEN_TILE
    nt = seq // t
    row = lambda i: (i, 0)
    const = lambda i: (0, 0)
    half = pl.BlockSpec((t, 512), row)

    def residue_major(dil, width):
        return pl.BlockSpec((1, dil, t // dil, width), lambda i: (i // nt, 0, i % nt, 0))

    rm4, rm16 = residue_major(4, ATT_WIDTH), residue_major(16, ATT_WIDTH)
    lse, lse4, lse16 = pl.BlockSpec((t, LANES), row), residue_major(4, LANES), residue_major(16, LANES)
    wide, narrow = pltpu.VMEM((ATT_WIDTH // LANES, t, LANES), F32), pltpu.VMEM((1, t, LANES), F32)
    return pl.pallas_call(
        _outproj_kernel,
        grid=(n // t,),
        in_specs=[
            half, half, pl.BlockSpec((t, 512), lambda i: (i, COL_HG)),
            half, rm4, rm16, lse, lse4, lse16,
            pl.BlockSpec((t, D_MODEL), row),
            pl.BlockSpec((1, HG_DIM), const),
            pl.BlockSpec((D_MODEL, D_MODEL), const),
            pl.BlockSpec((1, D_MODEL), const),
            pl.BlockSpec((D_MODEL, LANES), const),
            pl.BlockSpec((D_MODEL, LANES), const),
            pl.BlockSpec((1, LANES), const),
        ],
        out_specs=[pl.BlockSpec((t, D_MODEL), row), pl.BlockSpec((t * ROW_TILES, LANES), row),
                   pl.BlockSpec((t, LANES), row), pl.BlockSpec((t, LANES), row)],
        out_shape=[jax.ShapeDtypeStruct((n, D_MODEL), F32), jax.ShapeDtypeStruct((n * ROW_TILES, LANES), F32),
                   jax.ShapeDtypeStruct((n, LANES), F32), jax.ShapeDtypeStruct((n, LANES), jnp.int32)],
        scratch_shapes=[wide, wide, narrow, narrow, wide, narrow],
        compiler_params=pltpu.CompilerParams(
            dimension_semantics=("parallel",), vmem_limit_bytes=VMEM_LIMIT),
        name="outproj",
    )(o_f, o_b, proj, *atts, *lses, x2d, og, w_out_bf16, g2, wr_hi, wr_lo, br)


def _moe_kernel(bexp_ref, bfirst_ref, bslot_ref, bnext_ref, nused_ref,
                xs_ref, wu_hbm, bg_ref, bl_ref, wd_hbm, bd_ref,
                y_ref, wu_buf, wd_buf, wg_s, wl_s, wd_s, sems):
    i = pl.program_id(0)

    def weight_copies(expert, slot):
        return (pltpu.make_async_copy(wu_hbm.at[expert], wu_buf.at[slot], sems.at[0, slot]),
                pltpu.make_async_copy(wd_hbm.at[expert], wd_buf.at[slot], sems.at[1, slot]))

    @pl.when(bfirst_ref[i] == 1)
    def _():
        slot = bslot_ref[i]

        @pl.when(i == 0)
        def _():
            for cp in weight_copies(bexp_ref[0], 0):
                cp.start()

        for cp in weight_copies(bexp_ref[i], slot):
            cp.wait()

        @pl.when(bnext_ref[i] >= 0)
        def _():
            for cp in weight_copies(bnext_ref[i], 1 - slot):
                cp.start()

        r = lax.broadcasted_iota(jnp.int32, (2 * LANES, 2 * LANES), 0)
        c = lax.broadcasted_iota(jnp.int32, (2 * LANES, 2 * LANES), 1)
        src = jnp.where(c < LANES, 2 * c, 2 * (c - LANES) + 1)
        perm = jnp.where(r == src, 1.0, 0.0).astype(BF16)
        rows = 256
        for rb in range(D_MODEL // rows):
            rs = slice(rb * rows, (rb + 1) * rows)
            for cb in range(D_EXPERT // LANES):
                w = wu_buf[slot, rs, cb * 2 * LANES:(cb + 1) * 2 * LANES].astype(BF16)
                split = _dot(w, perm).astype(BF16)
                wg_s[rs, cb * LANES:(cb + 1) * LANES] = split[:, :LANES]
                wl_s[rs, cb * LANES:(cb + 1) * LANES] = split[:, LANES:]
        wd_s[...] = wd_buf[slot].astype(BF16)

    @pl.when(i < nused_ref[0])
    def _():
        x = jnp.concatenate(_tiles_to_rows(xs_ref, MOE_BLOCK_ROWS), axis=1).astype(BF16)
        hglu = _dot(x, wg_s[...]) + bg_ref[0]
        hlin = _dot(x, wl_s[...]) + bl_ref[0]
        glu = jnp.minimum(hglu, SWIGLU_LIMIT)
        lin = jnp.clip(hlin, -SWIGLU_LIMIT, SWIGLU_LIMIT)
        act = glu * _sigmoid(SWIGLU_ALPHA * glu) * (lin + 1.0)
        _rows_to_tiles(y_ref, _dot(act.astype(BF16), wd_s[...]) + bd_ref[0])

    @pl.when(i >= nused_ref[0])
    def _():
        y_ref[...] = jnp.zeros_like(y_ref)


def _moe(block_exp, block_first, block_slot, block_next, n_used, xs, w_up, b_glu, b_lin, w_down, b_down):
    p_rows = xs.shape[0] // ROW_TILES
    bm = MOE_BLOCK_ROWS
    nb = p_rows // bm
    exp3 = lambda i, be, bf, bs, bn, nu: (be[i], 0, 0)
    grid_spec = pltpu.PrefetchScalarGridSpec(
        num_scalar_prefetch=5,
        grid=(nb,),
        in_specs=[
            pl.BlockSpec((bm * ROW_TILES, LANES), lambda i, be, bf, bs, bn, nu: (jnp.minimum(i, nu[0] - 1), 0)),
            pl.BlockSpec(memory_space=pl.ANY),
            pl.BlockSpec((1, 1, D_EXPERT), exp3),
            pl.BlockSpec((1, 1, D_EXPERT), exp3),
            pl.BlockSpec(memory_space=pl.ANY),
            pl.BlockSpec((1, 1, D_MODEL), exp3),
        ],
        out_specs=pl.BlockSpec((bm * ROW_TILES, LANES), lambda i, be, bf, bs, bn, nu: (i, 0)),
        scratch_shapes=[pltpu.VMEM((2, D_MODEL, 2 * D_EXPERT), F32), pltpu.VMEM((2, D_EXPERT, D_MODEL), F32),
                        pltpu.VMEM((D_MODEL, D_EXPERT), BF16), pltpu.VMEM((D_MODEL, D_EXPERT), BF16),
                        pltpu.VMEM((D_EXPERT, D_MODEL), BF16), pltpu.SemaphoreType.DMA((2, 2))],
    )
    return pl.pallas_call(
        _moe_kernel,
        grid_spec=grid_spec,
        out_shape=jax.ShapeDtypeStruct((p_rows * ROW_TILES, LANES), F32),
        compiler_params=pltpu.CompilerParams(
            dimension_semantics=("arbitrary",), vmem_limit_bytes=VMEM_LIMIT),
        name="moe",
    )(block_exp, block_first, block_slot, block_next, n_used, xs, w_up, b_glu, b_lin, w_down, b_down)


def _dispatch_kernel(dest_ref, zstart_ref, xn_ref, xs_hbm, zero_ref, sem, zsem):
    i = pl.program_id(0)
    bm = MOE_BLOCK_ROWS
    tokens = xn_ref.shape[0] // ROW_TILES

    def zero_copy(e):
        start = pl.multiple_of(zstart_ref[e] * ROW_TILES, bm * ROW_TILES)
        return pltpu.make_async_copy(zero_ref, xs_hbm.at[pl.ds(start, bm * ROW_TILES)], zsem)

    @pl.when(i == 0)
    def _():
        zero_ref[...] = jnp.zeros_like(zero_ref)
        for e in range(N_EXPERTS):
            @pl.when(zstart_ref[e] >= 0)
            def _():
                zero_copy(e).start()
        for e in range(N_EXPERTS):
            @pl.when(zstart_ref[e] >= 0)
            def _():
                zero_copy(e).wait()

    t0 = i * tokens

    def body(j, carry):
        src = xn_ref.at[pl.ds(pl.multiple_of(j * ROW_TILES, ROW_TILES), ROW_TILES)]
        for k in range(TOP_K):
            d = pl.multiple_of(dest_ref[(t0 + j) * TOP_K + k] * ROW_TILES, ROW_TILES)
            pltpu.make_async_copy(src, xs_hbm.at[pl.ds(d, ROW_TILES)], sem).start(priority=k % 2)
        return carry

    lax.fori_loop(0, tokens, body, 0, unroll=8)
    for k in range(TOP_K):
        pltpu.make_async_copy(xn_ref, xs_hbm.at[pl.ds(0, tokens * ROW_TILES)], sem).wait()


def _dispatch(dest, zstart, xn, p_rows):
    n = xn.shape[0] // ROW_TILES
    t = DISPATCH_TOKENS
    grid_spec = pltpu.PrefetchScalarGridSpec(
        num_scalar_prefetch=2,
        grid=(n // t,),
        in_specs=[pl.BlockSpec((t * ROW_TILES, LANES), lambda i, d, z: (i, 0))],
        out_specs=pl.BlockSpec(memory_space=pl.ANY),
        scratch_shapes=[pltpu.VMEM((MOE_BLOCK_ROWS * ROW_TILES, LANES), F32),
                        pltpu.SemaphoreType.DMA, pltpu.SemaphoreType.DMA],
    )
    return pl.pallas_call(
        _dispatch_kernel,
        grid_spec=grid_spec,
        out_shape=jax.ShapeDtypeStruct((p_rows * ROW_TILES, LANES), F32),
        compiler_params=pltpu.CompilerParams(
            dimension_semantics=("arbitrary",), vmem_limit_bytes=VMEM_LIMIT),
        name="dispatch",
    )(dest, zstart, xn)


def _combine_kernel(dest_ref, y_hbm, x2_ref, gate_ref, out_ref, buf_ref, sems):
    i = pl.program_id(0)
    tc = COMBINE_TOKENS

    def issue(step, slot):
        t0 = step * tc

        def body(j, carry):
            for k in range(TOP_K):
                d = pl.multiple_of(dest_ref[(t0 + j) * TOP_K + k] * ROW_TILES, ROW_TILES)
                r = pl.multiple_of((k * tc + j) * ROW_TILES, ROW_TILES)
                pltpu.make_async_copy(y_hbm.at[pl.ds(d, ROW_TILES)], buf_ref.at[slot, pl.ds(r, ROW_TILES)],
                                      sems.at[slot]).start(priority=k % 2)
            return carry

        lax.fori_loop(0, tc, body, 0, unroll=8)

    @pl.when(i == 0)
    def _():
        issue(0, 0)

    @pl.when(i + 1 < pl.num_programs(0))
    def _():
        issue(i + 1, (i + 1) % 2)

    slot = i % 2
    pltpu.make_async_copy(y_hbm.at[pl.ds(0, TOP_K * tc * ROW_TILES)], buf_ref.at[slot], sems.at[slot]).wait()
    rows = buf_ref.at[slot]
    sub = 64
    for r0 in range(0, tc, sub):
        gate = gate_ref[r0:r0 + sub, :]
        gates = [jnp.broadcast_to(gate[:, k:k + 1], (sub, LANES)) for k in range(TOP_K)]
        for c in range(ROW_TILES):
            acc = x2_ref[r0:r0 + sub, c * LANES:(c + 1) * LANES]
            for k in range(TOP_K):
                acc = acc + gates[k] * rows[pl.ds((k * tc + r0) * ROW_TILES + c, sub, stride=ROW_TILES), :]
            out_ref[r0:r0 + sub, c * LANES:(c + 1) * LANES] = acc


def _combine(dest, y, x2, gates):
    n = x2.shape[0]
    tc = COMBINE_TOKENS
    grid_spec = pltpu.PrefetchScalarGridSpec(
        num_scalar_prefetch=1,
        grid=(n // tc,),
        in_specs=[pl.BlockSpec(memory_space=pl.ANY),
                  pl.BlockSpec((tc, D_MODEL), lambda i, d: (i, 0)),
                  pl.BlockSpec((tc, LANES), lambda i, d: (i, 0))],
        out_specs=pl.BlockSpec((tc, D_MODEL), lambda i, d: (i, 0)),
        scratch_shapes=[pltpu.VMEM((2, TOP_K * tc * ROW_TILES, LANES), F32), pltpu.SemaphoreType.DMA((2,))],
    )
    return pl.pallas_call(
        _combine_kernel,
        grid_spec=grid_spec,
        out_shape=jax.ShapeDtypeStruct((n, D_MODEL), F32),
        compiler_params=pltpu.CompilerParams(
            dimension_semantics=("arbitrary",), vmem_limit_bytes=VMEM_LIMIT),
        name="combine",
    )(dest, y, x2, gates)


def _route(top_idx):
    n = top_idx.shape[0]
    a = n * TOP_K
    bm = MOE_BLOCK_ROWS
    nb = a // bm + N_EXPERTS
    e_flat = top_idx.reshape(a)
    onehot = (e_flat[:, None] == jnp.arange(N_EXPERTS, dtype=jnp.int32)[None, :]).astype(jnp.int32)
    csum = jnp.cumsum(onehot, axis=0)
    counts = csum[-1]
    padded = ((counts + bm - 1) // bm) * bm
    pad_end = jnp.cumsum(padded)
    pad_start = pad_end - padded
    dest = jnp.sum(onehot * (csum - 1 + pad_start[None, :]), axis=1)
    n_used = (pad_end[-1] // bm).astype(jnp.int32)
    blk = jnp.arange(nb, dtype=jnp.int32)
    bexp = jnp.sum((pad_end[None, :] <= (blk * bm)[:, None]).astype(jnp.int32), axis=1)
    bexp = jnp.minimum(bexp, N_EXPERTS - 1)
    bexp = jnp.where(blk < n_used, bexp, bexp[jnp.maximum(n_used - 1, 0)])
    bfirst = jnp.concatenate([jnp.ones((1,), jnp.int32), (bexp[1:] != bexp[:-1]).astype(jnp.int32)])
    bslot = (jnp.cumsum(bfirst) - 1) % 2
    later = jnp.where(bexp[None, :] > bexp[:, None], bexp[None, :], N_EXPERTS)
    bnext = jnp.min(later, axis=1)
    bnext = jnp.where(bnext < N_EXPERTS, bnext, -1).astype(jnp.int32)
    zstart = jnp.where(counts > 0, pad_end - bm, -1).astype(jnp.int32)
    return dest.astype(jnp.int32), zstart, (bexp, bfirst, bslot.astype(jnp.int32), bnext, n_used.reshape(1))


def kernel(x, positions, norm1_g, w_in, q_norm_g, k_norm_g, hgrn_lower_bounds, hgrn_onorm_g,
           w_out, norm2_g, w_router, b_router, w_up, b_up, w_down, b_down):
    batch, seq, d = x.shape
    n = batch * seq
    depth = norm1_g.shape[0]
    lbs_all = jnp.cumsum(jax.nn.softmax(hgrn_lower_bounds.astype(F32), axis=0), axis=0)
    half = ATT_DIM // 2
    inv = 1.0 / (ROPE_THETA ** (jnp.arange(half, dtype=F32) / half))
    inv_tab = jnp.tile(inv, LANES // half).reshape(1, LANES)
    pos_col = positions.reshape(n, 1)

    x2d = x.reshape(n, d)
    for l in range(depth):
        lbs = lbs_all[l].reshape(2 * HG_HEADS, 1, HG_DIM)
        proj, qkv4, qkv16 = _inproj(
            x2d, pos_col, inv_tab, norm1_g[l].reshape(1, d), w_in[l].astype(BF16),
            jnp.tile(q_norm_g[l], LANES // ATT_DIM).reshape(1, LANES),
            jnp.tile(k_norm_g[l], LANES // ATT_DIM).reshape(1, LANES), batch, seq)
        o_f, o_b = _hgrn(proj, lbs, batch, seq)
        o1, l1 = _attention(proj.reshape(batch, 1, seq, IN_COLS), COL_AQ)
        o4, l4 = _attention(qkv4, 0)
        o16, l16 = _attention(qkv16, 0)
        atts = [o1.reshape(n, ATT_WIDTH), o4, o16]
        lses = [l1.reshape(n, LANES), l4, l16]

        wr = jnp.pad(w_router[l], ((0, 0), (0, LANES - N_EXPERTS)))
        wr_hi = wr.astype(BF16)
        wr_lo = (wr - wr_hi.astype(F32)).astype(BF16)
        br = jnp.pad(b_router[l], (0, LANES - N_EXPERTS)).reshape(1, LANES)
        x2, xn, gates, top_idx = _outproj(
            o_f, o_b, proj, atts, lses, x2d, hgrn_onorm_g[l].reshape(1, HG_DIM),
            w_out[l].astype(BF16), norm2_g[l].reshape(1, d), wr_hi, wr_lo, br, seq)
        dest, zstart, blocks = _route(top_idx[:, :TOP_K])
        xs = _dispatch(dest, zstart, xn, blocks[0].shape[0] * MOE_BLOCK_ROWS)
        y = _moe(*blocks, xs, w_up[l],
                 b_up[l][:, 0::2].reshape(N_EXPERTS, 1, D_EXPERT),
                 b_up[l][:, 1::2].reshape(N_EXPERTS, 1, D_EXPERT),
                 w_down[l], b_down[l].reshape(N_EXPERTS, 1, D_MODEL))
        x2d = _combine(dest, y, x2, gates)
    return x2d.reshape(batch, seq, d)
```

```python
import functools

import jax
import jax.numpy as jnp
from jax import lax
from jax.experimental import pallas as pl
from jax.experimental.pallas import tpu as pltpu

F32 = jnp.float32
BF16 = jnp.bfloat16

D_MODEL = 1024
HG_HEADS = 4
HG_DIM = 128
HG_WIDTH = HG_HEADS * HG_DIM
HG_CHUNK = 64
ATT_HEADS = 8
ATT_DIM = 64
ATT_WIDTH = ATT_HEADS * ATT_DIM
DILATED_PATTERNS = ((128, 1), (512, 4), (2048, 16))
ATT_HALF = 64
ATT_QBLOCK = 128
ROPE_THETA = 10000.0
IN_COLS = 5 * HG_WIDTH + 3 * ATT_WIDTH
N_EXPERTS = 32
TOP_K = 4
D_EXPERT = D_MODEL
SWIGLU_LIMIT = 7.0
SWIGLU_ALPHA = 1.702
EPS = 1e-6
NEG = -1e30

COL_HQ, COL_HF_FWD, COL_HF_BWD, COL_HI, COL_HG, COL_AQ, COL_AK, COL_AV = range(8)

TOKEN_TILE = 512
INPROJ_TILE = 1024
HGRN_TILE = 1024
MOE_BLOCK_ROWS = 512
DISPATCH_TOKENS = 512
COMBINE_TOKENS = 256
LANES = 128
ROW_TILES = D_MODEL // LANES
VMEM_LIMIT = 60 * 1024 * 1024


def _dot(a, b):
    return jnp.dot(a, b, preferred_element_type=F32)


def _dot_nt(a, b):
    return lax.dot_general(a, b, (((1,), (1,)), ((), ())), preferred_element_type=F32)


def _dot_tn(a, b):
    return lax.dot_general(a, b, (((0,), (0,)), ((), ())), preferred_element_type=F32)


def _sigmoid(x):
    return 0.5 * jnp.tanh(0.5 * x) + 0.5


def _rows_to_tiles(dst_ref, x):
    for c in range(ROW_TILES):
        dst_ref[pl.ds(c, x.shape[0], stride=ROW_TILES), :] = x[:, c * LANES:(c + 1) * LANES]


def _tiles_to_rows(src, rows, first_row=0):
    return [src[pl.ds(first_row * ROW_TILES + c, rows, stride=ROW_TILES), :] for c in range(ROW_TILES)]


def _head_norm_rope(p, gain, cos, sin_signed, scale):
    lane = lax.broadcasted_iota(jnp.int32, (p.shape[0], LANES), 1)
    low = lane < ATT_DIM
    first_half = (lane % ATT_DIM) < (ATT_DIM // 2)
    outs = []
    for t in range(ATT_WIDTH // LANES):
        blk = p[:, t * LANES:(t + 1) * LANES]
        sq = blk * blk
        s_low = jnp.sum(jnp.where(low, sq, 0.0), axis=-1, keepdims=True)
        s_high = jnp.sum(jnp.where(low, 0.0, sq), axis=-1, keepdims=True)
        r = jnp.where(low, lax.rsqrt(s_low * (1.0 / ATT_DIM) + EPS),
                      lax.rsqrt(s_high * (1.0 / ATT_DIM) + EPS))
        y = blk * r * gain
        partner = jnp.where(first_half, pltpu.roll(y, LANES - ATT_DIM // 2, axis=1),
                            pltpu.roll(y, ATT_DIM // 2, axis=1))
        outs.append((y * cos + partner * sin_signed) * scale)
    return jnp.concatenate(outs, axis=1)


def _inproj_kernel(x_ref, pos_ref, inv_ref, g1_ref, w_ref, qg_ref, kg_ref,
                   out_ref, d4_ref, d16_ref, stage_ref, stage2_ref):
    x = x_ref[...]
    ms = jnp.mean(x * x, axis=-1, keepdims=True)
    h = (x * lax.rsqrt(ms + EPS) * g1_ref[...]).astype(BF16)
    ang = pos_ref[...].astype(F32) * inv_ref[...]
    lane = lax.broadcasted_iota(jnp.int32, ang.shape, 1)
    cos = jnp.cos(ang)
    sin_signed = jnp.where((lane % ATT_DIM) < (ATT_DIM // 2), -jnp.sin(ang), jnp.sin(ang))
    order = (COL_AQ, COL_AK, COL_AV, COL_HQ, COL_HF_FWD, COL_HF_BWD, COL_HI, COL_HG)
    nxt = _dot(h, w_ref[:, order[0] * 512:(order[0] + 1) * 512])
    for pos, j in enumerate(order):
        p = nxt
        if pos + 1 < len(order):
            jn = order[pos + 1]
            nxt = _dot(h, w_ref[:, jn * 512:(jn + 1) * 512])
        if j == COL_AQ:
            p = _head_norm_rope(p, qg_ref[...], cos, sin_signed, ATT_DIM ** -0.5)
        elif j == COL_AK:
            p = _head_norm_rope(p, kg_ref[...], cos, sin_signed, 1.0)
        out_ref[:, j * 512:(j + 1) * 512] = p.astype(BF16)
        if j >= COL_AQ:
            rows4, rows16 = x.shape[0] // 4, x.shape[0] // 16
            for c in range(ATT_WIDTH // LANES):
                cols = slice((j - COL_AQ) * ATT_WIDTH + c * LANES, (j - COL_AQ) * ATT_WIDTH + (c + 1) * LANES)
                stage_ref[c] = p[:, c * LANES:(c + 1) * LANES]
                for r4 in range(4):
                    group = stage_ref[c, pl.ds(r4, rows4, stride=4), :]
                    d4_ref[0, r4, :, cols] = group.astype(BF16)
                    stage2_ref[c, r4 * rows4:(r4 + 1) * rows4, :] = group
                for r4 in range(4):
                    for m in range(4):
                        d16_ref[0, r4 + 4 * m, :, cols] = (
                            stage2_ref[c, pl.ds(r4 * rows4 + m, rows16, stride=4), :].astype(BF16))


def _inproj(x2d, pos_col, inv_tab, g1, w_in_bf16, qg, kg, batch, seq):
    n = x2d.shape[0]
    t = INPROJ_TILE
    nt = seq // t
    const = lambda i: (0, 0)
    qkv = 3 * ATT_WIDTH

    def residue_major(dil):
        spec = pl.BlockSpec((1, dil, t // dil, qkv), lambda i: (i // nt, 0, i % nt, 0))
        return spec, jax.ShapeDtypeStruct((batch, dil, seq // dil, qkv), BF16)

    spec4, shape4 = residue_major(4)
    spec16, shape16 = residue_major(16)
    return pl.pallas_call(
        _inproj_kernel,
        grid=(n // t,),
        in_specs=[
            pl.BlockSpec((t, D_MODEL), lambda i: (i, 0)),
            pl.BlockSpec((t, 1), lambda i: (i, 0)),
            pl.BlockSpec((1, LANES), const),
            pl.BlockSpec((1, D_MODEL), const),
            pl.BlockSpec((D_MODEL, IN_COLS), const, pipeline_mode=pl.Buffered(1)),
            pl.BlockSpec((1, LANES), const),
            pl.BlockSpec((1, LANES), const),
        ],
        out_specs=[pl.BlockSpec((t, IN_COLS), lambda i: (i, 0)), spec4, spec16],
        out_shape=[jax.ShapeDtypeStruct((n, IN_COLS), BF16), shape4, shape16],
        scratch_shapes=[pltpu.VMEM((ATT_WIDTH // LANES, t, LANES), F32)] * 2,
        compiler_params=pltpu.CompilerParams(
            dimension_semantics=("parallel",), vmem_limit_bytes=VMEM_LIMIT),
        name="inproj",
    )(x2d, pos_col, inv_tab, g1, w_in_bf16, qg, kg)


def _hgrn_direction(q_ref, z_ref, v_ref, lb, state_t, reverse):
    c = HG_CHUNK
    t = q_ref.shape[0]
    n = t // c
    row = lax.broadcasted_iota(jnp.int32, (c, c), 0)
    col = lax.broadcasted_iota(jnp.int32, (c, c), 1)
    mask = (row <= col) if reverse else (row >= col)
    tri = jnp.where(mask, 1.0, 0.0).astype(BF16)
    last_row = 0 if reverse else c - 1

    z = z_ref[...].astype(F32)
    q = q_ref[...].astype(F32)
    v = v_ref[...]
    sg = _sigmoid(z)
    f = lb + (1.0 - lb) * sg
    k = (1.0 - lb) * (1.0 - sg)
    lf = jnp.log(f)
    lf_hi = lf.astype(BF16)
    lf_lo = (lf - lf_hi.astype(F32)).astype(BF16)
    chunks = [slice(j * c, (j + 1) * c) for j in range(n)]
    b = jnp.concatenate([_dot(tri, lf_hi[rs]) + _dot(tri, lf_lo[rs]) for rs in chunks], axis=0)
    b_last = b.reshape(n, c, HG_DIM)[:, last_row:last_row + 1, :]
    decay = jnp.exp(b_last)
    qt = (q * _sigmoid(q) * jnp.exp(b)).astype(BF16)
    kt_f32 = k * jnp.exp(-b)
    kt = kt_f32.astype(BF16)
    kd = (kt_f32.reshape(n, c, HG_DIM) * decay).reshape(t, HG_DIM).astype(BF16)

    outs, updates = [], []
    for rs in chunks:
        a = jnp.where(mask, _dot_nt(qt[rs], kt[rs]), 0.0)
        outs.append(_dot(a.astype(BF16), v[rs]))
        updates.append(_dot_tn(v[rs], kd[rs]))
    for j in (reversed(range(n)) if reverse else range(n)):
        outs[j] = outs[j] + _dot_nt(qt[chunks[j]], state_t.astype(BF16))
        state_t = state_t * decay[j] + updates[j]
    return jnp.concatenate(outs, axis=0), state_t


def _hgrn_kernel(qf_ref, zf_ref, vf_ref, qb_ref, zb_ref, vb_ref, lbf_ref, lbb_ref,
                 of_ref, ob_ref, sf_ref, sb_ref):
    @pl.when(pl.program_id(2) == 0)
    def _():
        sf_ref[...] = jnp.zeros_like(sf_ref)
        sb_ref[...] = jnp.zeros_like(sb_ref)

    o, sf = _hgrn_direction(qf_ref, zf_ref, vf_ref, lbf_ref[0], sf_ref[...], False)
    of_ref[...] = o.astype(of_ref.dtype)
    sf_ref[...] = sf
    o, sb = _hgrn_direction(qb_ref, zb_ref, vb_ref, lbb_ref[0], sb_ref[...], True)
    ob_ref[...] = o.astype(ob_ref.dtype)
    sb_ref[...] = sb


def _hgrn(proj, lbs, batch, seq):
    n = proj.shape[0]
    t = HGRN_TILE
    nblk = seq // t

    def fwd(colblk):
        return pl.BlockSpec((t, HG_DIM), lambda b, h, i: (b * nblk + i, colblk * HG_HEADS + h))

    def bwd(colblk):
        return pl.BlockSpec((t, HG_DIM), lambda b, h, i: (b * nblk + nblk - 1 - i, colblk * HG_HEADS + h))

    out_f = pl.BlockSpec((t, HG_DIM), lambda b, h, i: (b * nblk + i, h))
    out_b = pl.BlockSpec((t, HG_DIM), lambda b, h, i: (b * nblk + nblk - 1 - i, h))
    return pl.pallas_call(
        _hgrn_kernel,
        grid=(batch, HG_HEADS, nblk),
        in_specs=[
            fwd(COL_HQ), fwd(COL_HF_FWD), fwd(COL_HI),
            bwd(COL_HQ), bwd(COL_HF_BWD), bwd(COL_HI),
            pl.BlockSpec((1, 1, HG_DIM), lambda b, h, i: (h, 0, 0)),
            pl.BlockSpec((1, 1, HG_DIM), lambda b, h, i: (HG_HEADS + h, 0, 0)),
        ],
        out_specs=[out_f, out_b],
        out_shape=[jax.ShapeDtypeStruct((n, HG_WIDTH), BF16)] * 2,
        scratch_shapes=[pltpu.VMEM((HG_DIM, HG_DIM), F32)] * 2,
        compiler_params=pltpu.CompilerParams(
            dimension_semantics=("parallel", "parallel", "arbitrary"), vmem_limit_bytes=VMEM_LIMIT),
        name="hgrn",
    )(proj, proj, proj, proj, proj, proj, lbs, lbs)


def _attn_kernel(q_ref, kc_ref, kp_ref, kn_ref, vc_ref, vp_ref, vn_ref, o_ref, l_ref,
                 kw_ref, vw_ref, *, tq, length):
    n = pl.program_id(2)
    half = ATT_HALF
    kw_ref[0:half, :] = kp_ref[...]
    kw_ref[half:half + tq, :] = kc_ref[...]
    kw_ref[half + tq:, :] = kn_ref[...]
    vw_ref[0:half, :] = vp_ref[...]
    vw_ref[half:half + tq, :] = vc_ref[...]
    vw_ref[half + tq:, :] = vn_ref[...]

    qb_rows = ATT_QBLOCK
    win = qb_rows + 2 * half
    i_idx = lax.broadcasted_iota(jnp.int32, (qb_rows, win), 0)
    j_idx = lax.broadcasted_iota(jnp.int32, (qb_rows, win), 1)
    band = (j_idx >= i_idx) & (j_idx <= i_idx + 2 * half)

    pairs = [slice(p * LANES, (p + 1) * LANES) for p in range(ATT_HEADS // 2)]
    head_lane = lax.broadcasted_iota(jnp.int32, (qb_rows, LANES), 1)
    even_half = head_lane < ATT_DIM
    keep_even = jnp.where(lax.broadcasted_iota(jnp.int32, (1, LANES), 1) < ATT_DIM, 1.0, 0.0).astype(BF16)
    keep_odd = (1.0 - keep_even.astype(F32)).astype(BF16)

    def masked_scores(r0):
        base = n * tq + r0 - half
        valid = band & (j_idx >= -base) & (j_idx < length - base)
        q = q_ref[r0:r0 + qb_rows, :]
        kw = kw_ref[r0:r0 + win, :]
        out = []
        for cs in pairs:
            for keep in (keep_even, keep_odd):
                out.append(jnp.where(valid, _dot_nt(q[:, cs] * keep, kw[:, cs]), NEG))
        return out

    blocks = list(range(0, tq, qb_rows))
    nxt = masked_scores(blocks[0])
    for pos, r0 in enumerate(blocks):
        scores = nxt
        if pos + 1 < len(blocks):
            nxt = masked_scores(blocks[pos + 1])
        vw = vw_ref[r0:r0 + win, :]
        maxes = [jnp.max(s, axis=-1, keepdims=True) for s in scores]
        probs = [jnp.exp(s - m) for s, m in zip(scores, maxes)]
        dens = [jnp.sum(p, axis=-1, keepdims=True) for p in probs]
        outs = []
        for p, cs in enumerate(pairs):
            pv_even = _dot(probs[2 * p].astype(BF16), vw[:, cs])
            pv_odd = _dot(probs[2 * p + 1].astype(BF16), vw[:, cs])
            outs.append(jnp.where(even_half, pv_even, pv_odd)
                        / jnp.where(even_half, dens[2 * p], dens[2 * p + 1]))
        o_ref[r0:r0 + qb_rows, :] = jnp.concatenate(outs, axis=1).astype(o_ref.dtype)
        lse = jnp.zeros((qb_rows, LANES), F32)
        for h, (m, den) in enumerate(zip(maxes, dens)):
            lse = jnp.where(head_lane == h, m + jnp.log(den), lse)
        l_ref[r0:r0 + qb_rows, :] = lse


def _attention(qkv, col0):
    batch, dil, length, _ = qkv.shape
    tq = min(TOKEN_TILE, length)
    nq = length // tq
    hb = tq // ATT_HALF
    n_hblk = length // ATT_HALF

    def cur(col):
        return pl.BlockSpec((None, None, tq, ATT_WIDTH), lambda b, r, n: (b, r, n, col))

    def prev(col):
        return pl.BlockSpec((None, None, ATT_HALF, ATT_WIDTH),
                            lambda b, r, n: (b, r, jnp.maximum(n * hb - 1, 0), col))

    def nxt(col):
        return pl.BlockSpec((None, None, ATT_HALF, ATT_WIDTH),
                            lambda b, r, n: (b, r, jnp.minimum((n + 1) * hb, n_hblk - 1), col))

    out_spec = pl.BlockSpec((None, None, tq, ATT_WIDTH), lambda b, r, n: (b, r, n, 0))
    return pl.pallas_call(
        functools.partial(_attn_kernel, tq=tq, length=length),
        grid=(batch, dil, nq),
        in_specs=[cur(col0), cur(col0 + 1), prev(col0 + 1), nxt(col0 + 1),
                  cur(col0 + 2), prev(col0 + 2), nxt(col0 + 2)],
        out_specs=[out_spec, pl.BlockSpec((None, None, tq, LANES), lambda b, r, n: (b, r, n, 0))],
        out_shape=[jax.ShapeDtypeStruct((batch, dil, length, ATT_WIDTH), BF16),
                   jax.ShapeDtypeStruct((batch, dil, length, LANES), F32)],
        scratch_shapes=[pltpu.VMEM((tq + 2 * ATT_HALF, ATT_WIDTH), BF16)] * 2,
        compiler_params=pltpu.CompilerParams(
            dimension_semantics=("parallel", "parallel", "parallel"), vmem_limit_bytes=VMEM_LIMIT),
        name=f"attn_d{dil}",
    )(qkv, qkv, qkv, qkv, qkv, qkv, qkv)


def _token_major(src_ref, stage_ref, tmp_ref):
    dil, rows = src_ref.shape[1], src_ref.shape[2]
    nc = src_ref.shape[3] // LANES
    for c in range(nc):
        cols = slice(c * LANES, (c + 1) * LANES)
        if dil == 4:
            for r in range(dil):
                stage_ref[c, pl.ds(r, rows, stride=dil), :] = src_ref[0, r, :, cols].astype(F32)
        else:
            group = 4 * rows
            for r4 in range(4):
                for m in range(4):
                    tmp_ref[c, pl.ds(r4 * group + m, rows, stride=4), :] = (
                        src_ref[0, r4 + 4 * m, :, cols].astype(F32))
            for r4 in range(4):
                stage_ref[c, pl.ds(r4, group, stride=4), :] = tmp_ref[c, r4 * group:(r4 + 1) * group, :]
    return jnp.concatenate([stage_ref[c] for c in range(nc)], axis=1)


def _outproj_kernel(of_ref, ob_ref, hg_ref, o1_ref, o2_ref, o3_ref, l1_ref, l2_ref, l3_ref,
                    x_ref, og_ref, w_ref, g2_ref, wrh_ref, wrl_ref, br_ref,
                    x2_ref, xn_ref, gate_ref, idx_ref, st_o2, st_o3, st_l2, st_l3, st_tmp_o, st_tmp_l):
    o = of_ref[...].astype(F32) + ob_ref[...].astype(F32)
    hg = hg_ref[...].astype(F32)
    parts = []
    for h in range(HG_HEADS):
        blk = o[:, h * HG_DIM:(h + 1) * HG_DIM]
        ms = jnp.mean(blk * blk, axis=-1, keepdims=True)
        parts.append(blk * lax.rsqrt(ms + EPS) * og_ref[...])
    o_hg = jnp.concatenate(parts, axis=1) * (hg * _sigmoid(hg))

    l1 = l1_ref[...]
    l2 = _token_major(l2_ref, st_l2, st_tmp_l)
    l3 = _token_major(l3_ref, st_l3, st_tmp_l)
    mx = jnp.maximum(jnp.maximum(l1, l2), l3)
    e1, e2, e3 = jnp.exp(l1 - mx), jnp.exp(l2 - mx), jnp.exp(l3 - mx)
    den = e1 + e2 + e3
    er = lax.broadcasted_iota(jnp.int32, (LANES, ATT_WIDTH), 0)
    ec = lax.broadcasted_iota(jnp.int32, (LANES, ATT_WIDTH), 1)
    expand = jnp.where(er == ec // ATT_DIM, 1.0, 0.0).astype(BF16)

    def per_lane(w):
        return _dot(w.astype(BF16), expand)

    o_att = (per_lane(e1 / den) * o1_ref[...].astype(F32)
             + per_lane(e2 / den) * _token_major(o2_ref, st_o2, st_tmp_o)
             + per_lane(e3 / den) * _token_major(o3_ref, st_o3, st_tmp_o))

    y = _dot(o_hg.astype(BF16), w_ref[0:HG_WIDTH, :]) + _dot(o_att.astype(BF16), w_ref[HG_WIDTH:, :])
    x2 = x_ref[...] + y
    x2_ref[...] = x2

    ms = jnp.mean(x2 * x2, axis=-1, keepdims=True)
    xn = x2 * lax.rsqrt(ms + EPS) * g2_ref[...]
    _rows_to_tiles(xn_ref, xn)
    xn_hi = xn.astype(BF16)
    xn_lo = (xn - xn_hi.astype(F32)).astype(BF16)
    logits = (_dot(xn_hi, wrh_ref[...]) + _dot(xn_lo, wrh_ref[...]) + _dot(xn_hi, wrl_ref[...])
              + br_ref[...])

    lane = lax.broadcasted_iota(jnp.int32, logits.shape, 1)
    lane_f = lane.astype(F32)
    work = jnp.where(lane < N_EXPERTS, logits, -jnp.inf)
    vals, idxs = [], []
    for _ in range(TOP_K):
        m = jnp.max(work, axis=-1, keepdims=True)
        idx = jnp.min(jnp.where(work == m, lane_f, float(LANES)), axis=-1, keepdims=True)
        vals.append(m)
        idxs.append(idx)
        work = jnp.where(lane_f == idx, -jnp.inf, work)
    es = [jnp.exp(v - vals[0]) for v in vals]
    den = es[0] + es[1] + es[2] + es[3]
    gate_out = jnp.zeros(logits.shape, F32)
    idx_out = jnp.zeros(logits.shape, F32)
    for k in range(TOP_K):
        gate_out = jnp.where(lane == k, es[k] / den, gate_out)
        idx_out = jnp.where(lane == k, idxs[k], idx_out)
    gate_ref[...] = gate_out
    idx_ref[...] = idx_out.astype(jnp.int32)


def _outproj(o_f, o_b, proj, atts, lses, x2d, og, w_out_bf16, g2, wr_hi, wr_lo, br, seq):
    n = x2d.shape[0]
    t = TOKEN_TILE
    nt = seq // t
    row = lambda i: (i, 0)
    const = lambda i: (0, 0)
    half = pl.BlockSpec((t, 512), row)

    def residue_major(dil, width):
        return pl.BlockSpec((1, dil, t // dil, width), lambda i: (i // nt, 0, i % nt, 0))

    rm4, rm16 = residue_major(4, ATT_WIDTH), residue_major(16, ATT_WIDTH)
    lse, lse4, lse16 = pl.BlockSpec((t, LANES), row), residue_major(4, LANES), residue_major(16, LANES)
    wide, narrow = pltpu.VMEM((ATT_WIDTH // LANES, t, LANES), F32), pltpu.VMEM((1, t, LANES), F32)
    return pl.pallas_call(
        _outproj_kernel,
        grid=(n // t,),
        in_specs=[
            half, half, pl.BlockSpec((t, 512), lambda i: (i, COL_HG)),
            half, rm4, rm16, lse, lse4, lse16,
            pl.BlockSpec((t, D_MODEL), row),
            pl.BlockSpec((1, HG_DIM), const),
            pl.BlockSpec((D_MODEL, D_MODEL), const),
            pl.BlockSpec((1, D_MODEL), const),
            pl.BlockSpec((D_MODEL, LANES), const),
            pl.BlockSpec((D_MODEL, LANES), const),
            pl.BlockSpec((1, LANES), const),
        ],
        out_specs=[pl.BlockSpec((t, D_MODEL), row), pl.BlockSpec((t * ROW_TILES, LANES), row),
                   pl.BlockSpec((t, LANES), row), pl.BlockSpec((t, LANES), row)],
        out_shape=[jax.ShapeDtypeStruct((n, D_MODEL), F32), jax.ShapeDtypeStruct((n * ROW_TILES, LANES), F32),
                   jax.ShapeDtypeStruct((n, LANES), F32), jax.ShapeDtypeStruct((n, LANES), jnp.int32)],
        scratch_shapes=[wide, wide, narrow, narrow, wide, narrow],
        compiler_params=pltpu.CompilerParams(
            dimension_semantics=("parallel",), vmem_limit_bytes=VMEM_LIMIT),
        name="outproj",
    )(o_f, o_b, proj, *atts, *lses, x2d, og, w_out_bf16, g2, wr_hi, wr_lo, br)


def _moe_kernel(bexp_ref, bfirst_ref, bslot_ref, bnext_ref, bvalid_ref, nused_ref,
                xs_ref, wu_hbm, bg_ref, bl_ref, wd_hbm, bd_ref,
                y_ref, wu_buf, wd_buf, wg_s, wl_s, wd_s, sems):
    i = pl.program_id(0)

    def weight_copies(expert, slot):
        return (pltpu.make_async_copy(wu_hbm.at[expert], wu_buf.at[slot], sems.at[0, slot]),
                pltpu.make_async_copy(wd_hbm.at[expert], wd_buf.at[slot], sems.at[1, slot]))

    @pl.when(bfirst_ref[i] == 1)
    def _():
        slot = bslot_ref[i]

        @pl.when(i == 0)
        def _():
            for cp in weight_copies(bexp_ref[0], 0):
                cp.start()

        for cp in weight_copies(bexp_ref[i], slot):
            cp.wait()

        @pl.when(bnext_ref[i] >= 0)
        def _():
            for cp in weight_copies(bnext_ref[i], 1 - slot):
                cp.start()

        r = lax.broadcasted_iota(jnp.int32, (2 * LANES, 2 * LANES), 0)
        c = lax.broadcasted_iota(jnp.int32, (2 * LANES, 2 * LANES), 1)
        src = jnp.where(c < LANES, 2 * c, 2 * (c - LANES) + 1)
        perm = jnp.where(r == src, 1.0, 0.0).astype(BF16)
        rows = 256
        for rb in range(D_MODEL // rows):
            rs = slice(rb * rows, (rb + 1) * rows)
            for cb in range(D_EXPERT // LANES):
                w = wu_buf[slot, rs, cb * 2 * LANES:(cb + 1) * 2 * LANES].astype(BF16)
                split = _dot(w, perm).astype(BF16)
                wg_s[rs, cb * LANES:(cb + 1) * LANES] = split[:, :LANES]
                wl_s[rs, cb * LANES:(cb + 1) * LANES] = split[:, LANES:]
        wd_s[...] = wd_buf[slot].astype(BF16)

    def expert_mlp(rows):
        x = jnp.concatenate(_tiles_to_rows(xs_ref, rows), axis=1).astype(BF16)
        hglu = _dot(x, wg_s[...]) + bg_ref[0]
        hlin = _dot(x, wl_s[...]) + bl_ref[0]
        glu = jnp.minimum(hglu, SWIGLU_LIMIT)
        lin = jnp.clip(hlin, -SWIGLU_LIMIT, SWIGLU_LIMIT)
        act = glu * _sigmoid(SWIGLU_ALPHA * glu) * (lin + 1.0)
        _rows_to_tiles(y_ref, _dot(act.astype(BF16), wd_s[...]) + bd_ref[0])

    valid = bvalid_ref[i]
    half = MOE_BLOCK_ROWS // 2

    @pl.when(valid > half)
    def _():
        expert_mlp(MOE_BLOCK_ROWS)

    @pl.when((valid > 0) & (valid <= half))
    def _():
        expert_mlp(half)
        y_ref[half * ROW_TILES:, :] = jnp.zeros((half * ROW_TILES, LANES), F32)

    @pl.when(valid == 0)
    def _():
        y_ref[...] = jnp.zeros_like(y_ref)


def _moe(block_exp, block_first, block_slot, block_next, block_valid, n_used,
         xs, w_up, b_glu, b_lin, w_down, b_down):
    p_rows = xs.shape[0] // ROW_TILES
    bm = MOE_BLOCK_ROWS
    nb = p_rows // bm
    exp3 = lambda i, be, *_: (be[i], 0, 0)
    grid_spec = pltpu.PrefetchScalarGridSpec(
        num_scalar_prefetch=6,
        grid=(nb,),
        in_specs=[
            pl.BlockSpec((bm * ROW_TILES, LANES), lambda i, be, bf, bs, bn, bv, nu: (jnp.minimum(i, nu[0] - 1), 0)),
            pl.BlockSpec(memory_space=pl.ANY),
            pl.BlockSpec((1, 1, D_EXPERT), exp3),
            pl.BlockSpec((1, 1, D_EXPERT), exp3),
            pl.BlockSpec(memory_space=pl.ANY),
            pl.BlockSpec((1, 1, D_MODEL), exp3),
        ],
        out_specs=pl.BlockSpec((bm * ROW_TILES, LANES), lambda i, *_: (i, 0)),
        scratch_shapes=[pltpu.VMEM((2, D_MODEL, 2 * D_EXPERT), F32), pltpu.VMEM((2, D_EXPERT, D_MODEL), F32),
                        pltpu.VMEM((D_MODEL, D_EXPERT), BF16), pltpu.VMEM((D_MODEL, D_EXPERT), BF16),
                        pltpu.VMEM((D_EXPERT, D_MODEL), BF16), pltpu.SemaphoreType.DMA((2, 2))],
    )
    return pl.pallas_call(
        _moe_kernel,
        grid_spec=grid_spec,
        out_shape=jax.ShapeDtypeStruct((p_rows * ROW_TILES, LANES), F32),
        compiler_params=pltpu.CompilerParams(
            dimension_semantics=("arbitrary",), vmem_limit_bytes=VMEM_LIMIT),
        name="moe",
    )(block_exp, block_first, block_slot, block_next, block_valid, n_used,
      xs, w_up, b_glu, b_lin, w_down, b_down)


def _dispatch_kernel(dest_ref, zstart_ref, xn_ref, xs_hbm, zero_ref, sem, zsem):
    i = pl.program_id(0)
    bm = MOE_BLOCK_ROWS
    tokens = xn_ref.shape[0] // ROW_TILES

    def zero_copy(e):
        start = pl.multiple_of(zstart_ref[e] * ROW_TILES, bm * ROW_TILES)
        return pltpu.make_async_copy(zero_ref, xs_hbm.at[pl.ds(start, bm * ROW_TILES)], zsem)

    @pl.when(i == 0)
    def _():
        zero_ref[...] = jnp.zeros_like(zero_ref)
        for e in range(N_EXPERTS):
            @pl.when(zstart_ref[e] >= 0)
            def _():
                zero_copy(e).start()
        for e in range(N_EXPERTS):
            @pl.when(zstart_ref[e] >= 0)
            def _():
                zero_copy(e).wait()

    t0 = i * tokens

    def body(j, carry):
        src = xn_ref.at[pl.ds(pl.multiple_of(j * ROW_TILES, ROW_TILES), ROW_TILES)]
        for k in range(TOP_K):
            d = pl.multiple_of(dest_ref[(t0 + j) * TOP_K + k] * ROW_TILES, ROW_TILES)
            pltpu.make_async_copy(src, xs_hbm.at[pl.ds(d, ROW_TILES)], sem).start(priority=k % 2)
        return carry

    lax.fori_loop(0, tokens, body, 0, unroll=8)
    for k in range(TOP_K):
        pltpu.make_async_copy(xn_ref, xs_hbm.at[pl.ds(0, tokens * ROW_TILES)], sem).wait()


def _dispatch(dest, zstart, xn, p_rows):
    n = xn.shape[0] // ROW_TILES
    t = DISPATCH_TOKENS
    grid_spec = pltpu.PrefetchScalarGridSpec(
        num_scalar_prefetch=2,
        grid=(n // t,),
        in_specs=[pl.BlockSpec((t * ROW_TILES, LANES), lambda i, d, z: (i, 0))],
        out_specs=pl.BlockSpec(memory_space=pl.ANY),
        scratch_shapes=[pltpu.VMEM((MOE_BLOCK_ROWS * ROW_TILES, LANES), F32),
                        pltpu.SemaphoreType.DMA, pltpu.SemaphoreType.DMA],
    )
    return pl.pallas_call(
        _dispatch_kernel,
        grid_spec=grid_spec,
        out_shape=jax.ShapeDtypeStruct((p_rows * ROW_TILES, LANES), F32),
        compiler_params=pltpu.CompilerParams(
            dimension_semantics=("arbitrary",), vmem_limit_bytes=VMEM_LIMIT),
        name="dispatch",
    )(dest, zstart, xn)


def _combine_kernel(dest_ref, y_hbm, x2_ref, gate_ref, out_ref, buf_ref, sems):
    i = pl.program_id(0)
    tc = COMBINE_TOKENS

    def issue(step, slot):
        t0 = step * tc

        def body(j, carry):
            for k in range(TOP_K):
                d = pl.multiple_of(dest_ref[(t0 + j) * TOP_K + k] * ROW_TILES, ROW_TILES)
                r = pl.multiple_of((k * tc + j) * ROW_TILES, ROW_TILES)
                pltpu.make_async_copy(y_hbm.at[pl.ds(d, ROW_TILES)], buf_ref.at[slot, pl.ds(r, ROW_TILES)],
                                      sems.at[slot]).start(priority=k % 2)
            return carry

        lax.fori_loop(0, tc, body, 0, unroll=8)

    @pl.when(i == 0)
    def _():
        issue(0, 0)

    @pl.when(i + 1 < pl.num_programs(0))
    def _():
        issue(i + 1, (i + 1) % 2)

    slot = i % 2
    pltpu.make_async_copy(y_hbm.at[pl.ds(0, TOP_K * tc * ROW_TILES)], buf_ref.at[slot], sems.at[slot]).wait()
    rows = buf_ref.at[slot]
    sub = 64
    for r0 in range(0, tc, sub):
        gate = gate_ref[r0:r0 + sub, :]
        gates = [jnp.broadcast_to(gate[:, k:k + 1], (sub, LANES)) for k in range(TOP_K)]
        for c in range(ROW_TILES):
            acc = x2_ref[r0:r0 + sub, c * LANES:(c + 1) * LANES]
            for k in range(TOP_K):
                acc = acc + gates[k] * rows[pl.ds((k * tc + r0) * ROW_TILES + c, sub, stride=ROW_TILES), :]
            out_ref[r0:r0 + sub, c * LANES:(c + 1) * LANES] = acc


def _combine(dest, y, x2, gates):
    n = x2.shape[0]
    tc = COMBINE_TOKENS
    grid_spec = pltpu.PrefetchScalarGridSpec(
        num_scalar_prefetch=1,
        grid=(n // tc,),
        in_specs=[pl.BlockSpec(memory_space=pl.ANY),
                  pl.BlockSpec((tc, D_MODEL), lambda i, d: (i, 0)),
                  pl.BlockSpec((tc, LANES), lambda i, d: (i, 0))],
        out_specs=pl.BlockSpec((tc, D_MODEL), lambda i, d: (i, 0)),
        scratch_shapes=[pltpu.VMEM((2, TOP_K * tc * ROW_TILES, LANES), F32), pltpu.SemaphoreType.DMA((2,))],
    )
    return pl.pallas_call(
        _combine_kernel,
        grid_spec=grid_spec,
        out_shape=jax.ShapeDtypeStruct((n, D_MODEL), F32),
        compiler_params=pltpu.CompilerParams(
            dimension_semantics=("arbitrary",), vmem_limit_bytes=VMEM_LIMIT),
        name="combine",
    )(dest, y, x2, gates)


def _route(top_idx):
    n = top_idx.shape[0]
    a = n * TOP_K
    bm = MOE_BLOCK_ROWS
    nb = a // bm + N_EXPERTS
    e_flat = top_idx.reshape(a)
    onehot = (e_flat[:, None] == jnp.arange(N_EXPERTS, dtype=jnp.int32)[None, :]).astype(jnp.int32)
    csum = jnp.cumsum(onehot, axis=0)
    counts = csum[-1]
    padded = ((counts + bm - 1) // bm) * bm
    pad_end = jnp.cumsum(padded)
    pad_start = pad_end - padded
    dest = jnp.sum(onehot * (csum - 1 + pad_start[None, :]), axis=1)
    n_used = (pad_end[-1] // bm).astype(jnp.int32)
    blk = jnp.arange(nb, dtype=jnp.int32)
    bexp = jnp.sum((pad_end[None, :] <= (blk * bm)[:, None]).astype(jnp.int32), axis=1)
    bexp = jnp.minimum(bexp, N_EXPERTS - 1)
    bexp = jnp.where(blk < n_used, bexp, bexp[jnp.maximum(n_used - 1, 0)])
    bfirst = jnp.concatenate([jnp.ones((1,), jnp.int32), (bexp[1:] != bexp[:-1]).astype(jnp.int32)])
    bslot = (jnp.cumsum(bfirst) - 1) % 2
    later = jnp.where(bexp[None, :] > bexp[:, None], bexp[None, :], N_EXPERTS)
    bnext = jnp.min(later, axis=1)
    bnext = jnp.where(bnext < N_EXPERTS, bnext, -1).astype(jnp.int32)
    valid_end = jnp.sum(jnp.where(bexp[:, None] == jnp.arange(N_EXPERTS)[None, :], (pad_start + counts)[None, :], 0), axis=1)
    bvalid = jnp.clip(valid_end - blk * bm, 0, bm).astype(jnp.int32)
    zstart = jnp.where(counts > 0, pad_end - bm, -1).astype(jnp.int32)
    return dest.astype(jnp.int32), zstart, (bexp, bfirst, bslot.astype(jnp.int32), bnext, bvalid, n_used.reshape(1))


def kernel(x, positions, norm1_g, w_in, q_norm_g, k_norm_g, hgrn_lower_bounds, hgrn_onorm_g,
           w_out, norm2_g, w_router, b_router, w_up, b_up, w_down, b_down):
    batch, seq, d = x.shape
    n = batch * seq
    depth = norm1_g.shape[0]
    lbs_all = jnp.cumsum(jax.nn.softmax(hgrn_lower_bounds.astype(F32), axis=0), axis=0)
    half = ATT_DIM // 2
    inv = 1.0 / (ROPE_THETA ** (jnp.arange(half, dtype=F32) / half))
    inv_tab = jnp.tile(inv, LANES // half).reshape(1, LANES)
    pos_col = positions.reshape(n, 1)

    x2d = x.reshape(n, d)
    for l in range(depth):
        lbs = lbs_all[l].reshape(2 * HG_HEADS, 1, HG_DIM)
        proj, qkv4, qkv16 = _inproj(
            x2d, pos_col, inv_tab, norm1_g[l].reshape(1, d), w_in[l].astype(BF16),
            jnp.tile(q_norm_g[l], LANES // ATT_DIM).reshape(1, LANES),
            jnp.tile(k_norm_g[l], LANES // ATT_DIM).reshape(1, LANES), batch, seq)
        o_f, o_b = _hgrn(proj, lbs, batch, seq)
        o1, l1 = _attention(proj.reshape(batch, 1, seq, IN_COLS), COL_AQ)
        o4, l4 = _attention(qkv4, 0)
        o16, l16 = _attention(qkv16, 0)
        atts = [o1.reshape(n, ATT_WIDTH), o4, o16]
        lses = [l1.reshape(n, LANES), l4, l16]

        wr = jnp.pad(w_router[l], ((0, 0), (0, LANES - N_EXPERTS)))
        wr_hi = wr.astype(BF16)
        wr_lo = (wr - wr_hi.astype(F32)).astype(BF16)
        br = jnp.pad(b_router[l], (0, LANES - N_EXPERTS)).reshape(1, LANES)
        x2, xn, gates, top_idx = _outproj(
            o_f, o_b, proj, atts, lses, x2d, hgrn_onorm_g[l].reshape(1, HG_DIM),
            w_out[l].astype(BF16), norm2_g[l].reshape(1, d), wr_hi, wr_lo, br, seq)
        dest, zstart, blocks = _route(top_idx[:, :TOP_K])
        xs = _dispatch(dest, zstart, xn, blocks[0].shape[0] * MOE_BLOCK_ROWS)
        y = _moe(*blocks, xs, w_up[l],
                 b_up[l][:, 0::2].reshape(N_EXPERTS, 1, D_EXPERT),
                 b_up[l][:, 1::2].reshape(N_EXPERTS, 1, D_EXPERT),
                 w_down[l], b_down[l].reshape(N_EXPERTS, 1, D_MODEL))
        x2d = _combine(dest, y, x2, gates)
    return x2d.reshape(batch, seq, d)
```

```python
import functools

import jax
import jax.numpy as jnp
from jax import lax
from jax.experimental import pallas as pl
from jax.experimental.pallas import tpu as pltpu

F32 = jnp.float32
BF16 = jnp.bfloat16

D_MODEL = 1024
HG_HEADS = 4
HG_DIM = 128
HG_WIDTH = HG_HEADS * HG_DIM
HG_CHUNK = 64
ATT_HEADS = 8
ATT_DIM = 64
ATT_WIDTH = ATT_HEADS * ATT_DIM
DILATED_PATTERNS = ((128, 1), (512, 4), (2048, 16))
ATT_HALF = 64
ATT_QBLOCK = 128
ROPE_THETA = 10000.0
IN_COLS = 5 * HG_WIDTH + 3 * ATT_WIDTH
N_EXPERTS = 32
TOP_K = 4
D_EXPERT = D_MODEL
SWIGLU_LIMIT = 7.0
SWIGLU_ALPHA = 1.702
EPS = 1e-6
NEG = -1e30

COL_HQ, COL_HF_FWD, COL_HF_BWD, COL_HI, COL_HG, COL_AQ, COL_AK, COL_AV = range(8)

TOKEN_TILE = 1024
INPROJ_TILE = 1024
HGRN_TILE = 2048
ATT_TILE = 1024
MOE_BLOCK_ROWS = 512
DISPATCH_TOKENS = 512
COMBINE_TOKENS = 256
LANES = 128
ROW_TILES = D_MODEL // LANES
VMEM_LIMIT = 60 * 1024 * 1024


def _dot(a, b):
    return jnp.dot(a, b, preferred_element_type=F32)


def _dot_nt(a, b):
    return lax.dot_general(a, b, (((1,), (1,)), ((), ())), preferred_element_type=F32)


def _dot_tn(a, b):
    return lax.dot_general(a, b, (((0,), (0,)), ((), ())), preferred_element_type=F32)


def _sigmoid(x):
    return 0.5 * jnp.tanh(0.5 * x) + 0.5


def _rows_to_tiles(dst_ref, x):
    for c in range(ROW_TILES):
        dst_ref[pl.ds(c, x.shape[0], stride=ROW_TILES), :] = x[:, c * LANES:(c + 1) * LANES]


def _tiles_to_rows(src, rows, first_row=0):
    return [src[pl.ds(first_row * ROW_TILES + c, rows, stride=ROW_TILES), :] for c in range(ROW_TILES)]


def _head_norm_rope(p, gain, cos, sin_signed, scale):
    lane = lax.broadcasted_iota(jnp.int32, (p.shape[0], LANES), 1)
    low = lane < ATT_DIM
    first_half = (lane % ATT_DIM) < (ATT_DIM // 2)
    outs = []
    for t in range(ATT_WIDTH // LANES):
        blk = p[:, t * LANES:(t + 1) * LANES]
        sq = blk * blk
        s_low = jnp.sum(jnp.where(low, sq, 0.0), axis=-1, keepdims=True)
        s_high = jnp.sum(jnp.where(low, 0.0, sq), axis=-1, keepdims=True)
        r = jnp.where(low, lax.rsqrt(s_low * (1.0 / ATT_DIM) + EPS),
                      lax.rsqrt(s_high * (1.0 / ATT_DIM) + EPS))
        y = blk * r * gain
        partner = jnp.where(first_half, pltpu.roll(y, LANES - ATT_DIM // 2, axis=1),
                            pltpu.roll(y, ATT_DIM // 2, axis=1))
        outs.append((y * cos + partner * sin_signed) * scale)
    return jnp.concatenate(outs, axis=1)


def _inproj_kernel(x_ref, pos_ref, inv_ref, g1_ref, w_ref, qg_ref, kg_ref,
                   out_ref, d4_ref, d16_ref, stage_ref, stage2_ref):
    x = x_ref[...]
    ms = jnp.mean(x * x, axis=-1, keepdims=True)
    h = (x * lax.rsqrt(ms + EPS) * g1_ref[...]).astype(BF16)
    ang = pos_ref[...].astype(F32) * inv_ref[...]
    lane = lax.broadcasted_iota(jnp.int32, ang.shape, 1)
    cos = jnp.cos(ang)
    sin_signed = jnp.where((lane % ATT_DIM) < (ATT_DIM // 2), -jnp.sin(ang), jnp.sin(ang))
    order = (COL_AQ, COL_AK, COL_AV, COL_HQ, COL_HF_FWD, COL_HF_BWD, COL_HI, COL_HG)
    nxt = _dot(h, w_ref[:, order[0] * 512:(order[0] + 1) * 512])
    for pos, j in enumerate(order):
        p = nxt
        if pos + 1 < len(order):
            jn = order[pos + 1]
            nxt = _dot(h, w_ref[:, jn * 512:(jn + 1) * 512])
        if j == COL_AQ:
            p = _head_norm_rope(p, qg_ref[...], cos, sin_signed, ATT_DIM ** -0.5)
        elif j == COL_AK:
            p = _head_norm_rope(p, kg_ref[...], cos, sin_signed, 1.0)
        out_ref[:, j * 512:(j + 1) * 512] = p.astype(BF16)
        if j >= COL_AQ:
            rows4, rows16 = x.shape[0] // 4, x.shape[0] // 16
            for c in range(ATT_WIDTH // LANES):
                cols = slice((j - COL_AQ) * ATT_WIDTH + c * LANES, (j - COL_AQ) * ATT_WIDTH + (c + 1) * LANES)
                stage_ref[c] = p[:, c * LANES:(c + 1) * LANES]
                for r4 in range(4):
                    group = stage_ref[c, pl.ds(r4, rows4, stride=4), :]
                    d4_ref[0, r4, :, cols] = group.astype(BF16)
                    stage2_ref[c, r4 * rows4:(r4 + 1) * rows4, :] = group
                for r4 in range(4):
                    for m in range(4):
                        d16_ref[0, r4 + 4 * m, :, cols] = (
                            stage2_ref[c, pl.ds(r4 * rows4 + m, rows16, stride=4), :].astype(BF16))


def _inproj(x2d, pos_col, inv_tab, g1, w_in_bf16, qg, kg, batch, seq):
    n = x2d.shape[0]
    t = INPROJ_TILE
    nt = seq // t
    const = lambda i: (0, 0)
    qkv = 3 * ATT_WIDTH

    def residue_major(dil):
        spec = pl.BlockSpec((1, dil, t // dil, qkv), lambda i: (i // nt, 0, i % nt, 0))
        return spec, jax.ShapeDtypeStruct((batch, dil, seq // dil, qkv), BF16)

    spec4, shape4 = residue_major(4)
    spec16, shape16 = residue_major(16)
    return pl.pallas_call(
        _inproj_kernel,
        grid=(n // t,),
        in_specs=[
            pl.BlockSpec((t, D_MODEL), lambda i: (i, 0)),
            pl.BlockSpec((t, 1), lambda i: (i, 0)),
            pl.BlockSpec((1, LANES), const),
            pl.BlockSpec((1, D_MODEL), const),
            pl.BlockSpec((D_MODEL, IN_COLS), const, pipeline_mode=pl.Buffered(1)),
            pl.BlockSpec((1, LANES), const),
            pl.BlockSpec((1, LANES), const),
        ],
        out_specs=[pl.BlockSpec((t, IN_COLS), lambda i: (i, 0)), spec4, spec16],
        out_shape=[jax.ShapeDtypeStruct((n, IN_COLS), BF16), shape4, shape16],
        scratch_shapes=[pltpu.VMEM((ATT_WIDTH // LANES, t, LANES), F32)] * 2,
        compiler_params=pltpu.CompilerParams(
            dimension_semantics=("parallel",), vmem_limit_bytes=VMEM_LIMIT),
        name="inproj",
    )(x2d, pos_col, inv_tab, g1, w_in_bf16, qg, kg)


def _hgrn_direction(q_ref, z_ref, v_ref, lb, state_t, reverse):
    c = HG_CHUNK
    t = q_ref.shape[0]
    n = t // c
    row = lax.broadcasted_iota(jnp.int32, (c, c), 0)
    col = lax.broadcasted_iota(jnp.int32, (c, c), 1)
    mask = (row <= col) if reverse else (row >= col)
    tri = jnp.where(mask, 1.0, 0.0).astype(BF16)
    last_row = 0 if reverse else c - 1

    z = z_ref[...].astype(F32)
    q = q_ref[...].astype(F32)
    v = v_ref[...]
    sg = _sigmoid(z)
    f = lb + (1.0 - lb) * sg
    k = (1.0 - lb) * (1.0 - sg)
    lf = jnp.log(f)
    lf_hi = lf.astype(BF16)
    lf_lo = (lf - lf_hi.astype(F32)).astype(BF16)
    chunks = [slice(j * c, (j + 1) * c) for j in range(n)]
    b = jnp.concatenate([_dot(tri, lf_hi[rs]) + _dot(tri, lf_lo[rs]) for rs in chunks], axis=0)
    b_last = b.reshape(n, c, HG_DIM)[:, last_row:last_row + 1, :]
    decay = jnp.exp(b_last)
    qt = (q * _sigmoid(q) * jnp.exp(b)).astype(BF16)
    kt_f32 = k * jnp.exp(-b)
    kt = kt_f32.astype(BF16)
    kd = (kt_f32.reshape(n, c, HG_DIM) * decay).reshape(t, HG_DIM).astype(BF16)

    outs, updates = [], []
    for rs in chunks:
        a = jnp.where(mask, _dot_nt(qt[rs], kt[rs]), 0.0)
        outs.append(_dot(a.astype(BF16), v[rs]))
        updates.append(_dot_tn(v[rs], kd[rs]))
    for j in (reversed(range(n)) if reverse else range(n)):
        outs[j] = outs[j] + _dot_nt(qt[chunks[j]], state_t.astype(BF16))
        state_t = state_t * decay[j] + updates[j]
    return jnp.concatenate(outs, axis=0), state_t


def _hgrn_kernel(qf_ref, zf_ref, vf_ref, qb_ref, zb_ref, vb_ref, lbf_ref, lbb_ref,
                 of_ref, ob_ref, sf_ref, sb_ref):
    @pl.when(pl.program_id(2) == 0)
    def _():
        sf_ref[...] = jnp.zeros_like(sf_ref)
        sb_ref[...] = jnp.zeros_like(sb_ref)

    o, sf = _hgrn_direction(qf_ref, zf_ref, vf_ref, lbf_ref[0], sf_ref[...], False)
    of_ref[...] = o.astype(of_ref.dtype)
    sf_ref[...] = sf
    o, sb = _hgrn_direction(qb_ref, zb_ref, vb_ref, lbb_ref[0], sb_ref[...], True)
    ob_ref[...] = o.astype(ob_ref.dtype)
    sb_ref[...] = sb


def _hgrn(proj, lbs, batch, seq):
    n = proj.shape[0]
    t = HGRN_TILE
    nblk = seq // t

    def fwd(colblk):
        return pl.BlockSpec((t, HG_DIM), lambda b, h, i: (b * nblk + i, colblk * HG_HEADS + h))

    def bwd(colblk):
        return pl.BlockSpec((t, HG_DIM), lambda b, h, i: (b * nblk + nblk - 1 - i, colblk * HG_HEADS + h))

    out_f = pl.BlockSpec((t, HG_DIM), lambda b, h, i: (b * nblk + i, h))
    out_b = pl.BlockSpec((t, HG_DIM), lambda b, h, i: (b * nblk + nblk - 1 - i, h))
    return pl.pallas_call(
        _hgrn_kernel,
        grid=(batch, HG_HEADS, nblk),
        in_specs=[
            fwd(COL_HQ), fwd(COL_HF_FWD), fwd(COL_HI),
            bwd(COL_HQ), bwd(COL_HF_BWD), bwd(COL_HI),
            pl.BlockSpec((1, 1, HG_DIM), lambda b, h, i: (h, 0, 0)),
            pl.BlockSpec((1, 1, HG_DIM), lambda b, h, i: (HG_HEADS + h, 0, 0)),
        ],
        out_specs=[out_f, out_b],
        out_shape=[jax.ShapeDtypeStruct((n, HG_WIDTH), BF16)] * 2,
        scratch_shapes=[pltpu.VMEM((HG_DIM, HG_DIM), F32)] * 2,
        compiler_params=pltpu.CompilerParams(
            dimension_semantics=("parallel", "parallel", "arbitrary"), vmem_limit_bytes=VMEM_LIMIT),
        name="hgrn",
    )(proj, proj, proj, proj, proj, proj, lbs, lbs)


def _attn_kernel(q_ref, kc_ref, kp_ref, kn_ref, vc_ref, vp_ref, vn_ref, o_ref, l_ref,
                 kw_ref, vw_ref, *, tq, length):
    n = pl.program_id(2)
    half = ATT_HALF
    kw_ref[0:half, :] = kp_ref[...]
    kw_ref[half:half + tq, :] = kc_ref[...]
    kw_ref[half + tq:, :] = kn_ref[...]
    vw_ref[0:half, :] = vp_ref[...]
    vw_ref[half:half + tq, :] = vc_ref[...]
    vw_ref[half + tq:, :] = vn_ref[...]

    qb_rows = ATT_QBLOCK
    win = qb_rows + 2 * half
    i_idx = lax.broadcasted_iota(jnp.int32, (qb_rows, win), 0)
    j_idx = lax.broadcasted_iota(jnp.int32, (qb_rows, win), 1)
    band = (j_idx >= i_idx) & (j_idx <= i_idx + 2 * half)

    pairs = [slice(p * LANES, (p + 1) * LANES) for p in range(ATT_HEADS // 2)]
    head_lane = lax.broadcasted_iota(jnp.int32, (qb_rows, LANES), 1)
    even_half = head_lane < ATT_DIM
    keep_even = jnp.where(lax.broadcasted_iota(jnp.int32, (1, LANES), 1) < ATT_DIM, 1.0, 0.0).astype(BF16)
    keep_odd = (1.0 - keep_even.astype(F32)).astype(BF16)

    def masked_scores(r0):
        base = n * tq + r0 - half
        valid = band & (j_idx >= -base) & (j_idx < length - base)
        q = q_ref[r0:r0 + qb_rows, :]
        kw = kw_ref[r0:r0 + win, :]
        out = []
        for cs in pairs:
            for keep in (keep_even, keep_odd):
                out.append(jnp.where(valid, _dot_nt(q[:, cs] * keep, kw[:, cs]), NEG))
        return out

    blocks = list(range(0, tq, qb_rows))
    nxt = masked_scores(blocks[0])
    for pos, r0 in enumerate(blocks):
        scores = nxt
        if pos + 1 < len(blocks):
            nxt = masked_scores(blocks[pos + 1])
        vw = vw_ref[r0:r0 + win, :]
        maxes = [jnp.max(s, axis=-1, keepdims=True) for s in scores]
        probs = [jnp.exp(s - m) for s, m in zip(scores, maxes)]
        dens = [jnp.sum(p, axis=-1, keepdims=True) for p in probs]
        outs = []
        for p, cs in enumerate(pairs):
            pv_even = _dot(probs[2 * p].astype(BF16), vw[:, cs])
            pv_odd = _dot(probs[2 * p + 1].astype(BF16), vw[:, cs])
            outs.append(jnp.where(even_half, pv_even, pv_odd)
                        / jnp.where(even_half, dens[2 * p], dens[2 * p + 1]))
        o_ref[r0:r0 + qb_rows, :] = jnp.concatenate(outs, axis=1).astype(o_ref.dtype)
        lse = jnp.zeros((qb_rows, LANES), F32)
        for h, (m, den) in enumerate(zip(maxes, dens)):
            lse = jnp.where(head_lane == h, m + jnp.log(den), lse)
        l_ref[r0:r0 + qb_rows, :] = lse


def _attention(qkv, col0):
    batch, dil, length, _ = qkv.shape
    tq = min(ATT_TILE, length)
    nq = length // tq
    hb = tq // ATT_HALF
    n_hblk = length // ATT_HALF

    def cur(col):
        return pl.BlockSpec((None, None, tq, ATT_WIDTH), lambda b, r, n: (b, r, n, col))

    def prev(col):
        return pl.BlockSpec((None, None, ATT_HALF, ATT_WIDTH),
                            lambda b, r, n: (b, r, jnp.maximum(n * hb - 1, 0), col))

    def nxt(col):
        return pl.BlockSpec((None, None, ATT_HALF, ATT_WIDTH),
                            lambda b, r, n: (b, r, jnp.minimum((n + 1) * hb, n_hblk - 1), col))

    out_spec = pl.BlockSpec((None, None, tq, ATT_WIDTH), lambda b, r, n: (b, r, n, 0))
    return pl.pallas_call(
        functools.partial(_attn_kernel, tq=tq, length=length),
        grid=(batch, dil, nq),
        in_specs=[cur(col0), cur(col0 + 1), prev(col0 + 1), nxt(col0 + 1),
                  cur(col0 + 2), prev(col0 + 2), nxt(col0 + 2)],
        out_specs=[out_spec, pl.BlockSpec((None, None, tq, LANES), lambda b, r, n: (b, r, n, 0))],
        out_shape=[jax.ShapeDtypeStruct((batch, dil, length, ATT_WIDTH), BF16),
                   jax.ShapeDtypeStruct((batch, dil, length, LANES), F32)],
        scratch_shapes=[pltpu.VMEM((tq + 2 * ATT_HALF, ATT_WIDTH), BF16)] * 2,
        compiler_params=pltpu.CompilerParams(
            dimension_semantics=("parallel", "parallel", "parallel"), vmem_limit_bytes=VMEM_LIMIT),
        name=f"attn_d{dil}",
    )(qkv, qkv, qkv, qkv, qkv, qkv, qkv)


def _token_major(src_ref, stage_ref, tmp_ref):
    dil, rows = src_ref.shape[1], src_ref.shape[2]
    nc = src_ref.shape[3] // LANES
    for c in range(nc):
        cols = slice(c * LANES, (c + 1) * LANES)
        if dil == 4:
            for r in range(dil):
                stage_ref[c, pl.ds(r, rows, stride=dil), :] = src_ref[0, r, :, cols].astype(F32)
        else:
            group = 4 * rows
            for r4 in range(4):
                for m in range(4):
                    tmp_ref[c, pl.ds(r4 * group + m, rows, stride=4), :] = (
                        src_ref[0, r4 + 4 * m, :, cols].astype(F32))
            for r4 in range(4):
                stage_ref[c, pl.ds(r4, group, stride=4), :] = tmp_ref[c, r4 * group:(r4 + 1) * group, :]
    return jnp.concatenate([stage_ref[c] for c in range(nc)], axis=1)


def _outproj_kernel(of_ref, ob_ref, hg_ref, o1_ref, o2_ref, o3_ref, l1_ref, l2_ref, l3_ref,
                    x_ref, og_ref, w_ref, g2_ref, wrh_ref, wrl_ref, br_ref,
                    x2_ref, xn_ref, gate_ref, idx_ref, st_o2, st_o3, st_l2, st_l3, st_tmp_o, st_tmp_l):
    o = of_ref[...].astype(F32) + ob_ref[...].astype(F32)
    hg = hg_ref[...].astype(F32)
    parts = []
    for h in range(HG_HEADS):
        blk = o[:, h * HG_DIM:(h + 1) * HG_DIM]
        ms = jnp.mean(blk * blk, axis=-1, keepdims=True)
        parts.append(blk * lax.rsqrt(ms + EPS) * og_ref[...])
    o_hg = jnp.concatenate(parts, axis=1) * (hg * _sigmoid(hg))

    l1 = l1_ref[...]
    l2 = _token_major(l2_ref, st_l2, st_tmp_l)
    l3 = _token_major(l3_ref, st_l3, st_tmp_l)
    mx = jnp.maximum(jnp.maximum(l1, l2), l3)
    e1, e2, e3 = jnp.exp(l1 - mx), jnp.exp(l2 - mx), jnp.exp(l3 - mx)
    den = e1 + e2 + e3
    er = lax.broadcasted_iota(jnp.int32, (LANES, ATT_WIDTH), 0)
    ec = lax.broadcasted_iota(jnp.int32, (LANES, ATT_WIDTH), 1)
    expand = jnp.where(er == ec // ATT_DIM, 1.0, 0.0).astype(BF16)

    def per_lane(w):
        return _dot(w.astype(BF16), expand)

    o_att = (per_lane(e1 / den) * o1_ref[...].astype(F32)
             + per_lane(e2 / den) * _token_major(o2_ref, st_o2, st_tmp_o)
             + per_lane(e3 / den) * _token_major(o3_ref, st_o3, st_tmp_o))

    y = _dot(o_hg.astype(BF16), w_ref[0:HG_WIDTH, :]) + _dot(o_att.astype(BF16), w_ref[HG_WIDTH:, :])
    x2 = x_ref[...] + y
    x2_ref[...] = x2

    ms = jnp.mean(x2 * x2, axis=-1, keepdims=True)
    xn = x2 * lax.rsqrt(ms + EPS) * g2_ref[...]
    _rows_to_tiles(xn_ref, xn)
    xn_hi = xn.astype(BF16)
    xn_lo = (xn - xn_hi.astype(F32)).astype(BF16)
    logits = (_dot(xn_hi, wrh_ref[...]) + _dot(xn_lo, wrh_ref[...]) + _dot(xn_hi, wrl_ref[...])
              + br_ref[...])

    lane = lax.broadcasted_iota(jnp.int32, logits.shape, 1)
    lane_f = lane.astype(F32)
    work = jnp.where(lane < N_EXPERTS, logits, -jnp.inf)
    vals, idxs = [], []
    for _ in range(TOP_K):
        m = jnp.max(work, axis=-1, keepdims=True)
        idx = jnp.min(jnp.where(work == m, lane_f, float(LANES)), axis=-1, keepdims=True)
        vals.append(m)
        idxs.append(idx)
        work = jnp.where(lane_f == idx, -jnp.inf, work)
    es = [jnp.exp(v - vals[0]) for v in vals]
    den = es[0] + es[1] + es[2] + es[3]
    gate_out = jnp.zeros(logits.shape, F32)
    idx_out = jnp.zeros(logits.shape, F32)
    for k in range(TOP_K):
        gate_out = jnp.where(lane == k, es[k] / den, gate_out)
        idx_out = jnp.where(lane == k, idxs[k], idx_out)
    gate_ref[...] = gate_out
    idx_ref[...] = idx_out.astype(jnp.int32)


def _outproj(o_f, o_b, proj, atts, lses, x2d, og, w_out_bf16, g2, wr_hi, wr_lo, br, seq):
    n = x2d.shape[0]
    t = TOKEN_TILE
    nt = seq // t
    row = lambda i: (i, 0)
    const = lambda i: (0, 0)
    half = pl.BlockSpec((t, 512), row)

    def residue_major(dil, width):
        return pl.BlockSpec((1, dil, t // dil, width), lambda i: (i // nt, 0, i % nt, 0))

    rm4, rm16 = residue_major(4, ATT_WIDTH), residue_major(16, ATT_WIDTH)
    lse, lse4, lse16 = pl.BlockSpec((t, LANES), row), residue_major(4, LANES), residue_major(16, LANES)
    wide, narrow = pltpu.VMEM((ATT_WIDTH // LANES, t, LANES), F32), pltpu.VMEM((1, t, LANES), F32)
    return pl.pallas_call(
        _outproj_kernel,
        grid=(n // t,),
        in_specs=[
            half, half, pl.BlockSpec((t, 512), lambda i: (i, COL_HG)),
            half, rm4, rm16, lse, lse4, lse16,
            pl.BlockSpec((t, D_MODEL), row),
            pl.BlockSpec((1, HG_DIM), const),
            pl.BlockSpec((D_MODEL, D_MODEL), const),
            pl.BlockSpec((1, D_MODEL), const),
            pl.BlockSpec((D_MODEL, LANES), const),
            pl.BlockSpec((D_MODEL, LANES), const),
            pl.BlockSpec((1, LANES), const),
        ],
        out_specs=[pl.BlockSpec((t, D_MODEL), row), pl.BlockSpec((t * ROW_TILES, LANES), row),
                   pl.BlockSpec((t, LANES), row), pl.BlockSpec((t, LANES), row)],
        out_shape=[jax.ShapeDtypeStruct((n, D_MODEL), F32), jax.ShapeDtypeStruct((n * ROW_TILES, LANES), F32),
                   jax.ShapeDtypeStruct((n, LANES), F32), jax.ShapeDtypeStruct((n, LANES), jnp.int32)],
        scratch_shapes=[wide, wide, narrow, narrow, wide, narrow],
        compiler_params=pltpu.CompilerParams(
            dimension_semantics=("parallel",), vmem_limit_bytes=VMEM_LIMIT),
        name="outproj",
    )(o_f, o_b, proj, *atts, *lses, x2d, og, w_out_bf16, g2, wr_hi, wr_lo, br)


def _moe_kernel(bexp_ref, bfirst_ref, bslot_ref, bnext_ref, bvalid_ref, nused_ref,
                xs_ref, wu_hbm, bg_ref, bl_ref, wd_hbm, bd_ref,
                y_ref, wu_buf, wd_buf, wg_s, wl_s, wd_s, sems):
    i = pl.program_id(0)

    def weight_copies(expert, slot):
        return (pltpu.make_async_copy(wu_hbm.at[expert], wu_buf.at[slot], sems.at[0, slot]),
                pltpu.make_async_copy(wd_hbm.at[expert], wd_buf.at[slot], sems.at[1, slot]))

    @pl.when(bfirst_ref[i] == 1)
    def _():
        slot = bslot_ref[i]

        @pl.when(i == 0)
        def _():
            for cp in weight_copies(bexp_ref[0], 0):
                cp.start()

        for cp in weight_copies(bexp_ref[i], slot):
            cp.wait()

        @pl.when(bnext_ref[i] >= 0)
        def _():
            for cp in weight_copies(bnext_ref[i], 1 - slot):
                cp.start()

        r = lax.broadcasted_iota(jnp.int32, (2 * LANES, 2 * LANES), 0)
        c = lax.broadcasted_iota(jnp.int32, (2 * LANES, 2 * LANES), 1)
        src = jnp.where(c < LANES, 2 * c, 2 * (c - LANES) + 1)
        perm = jnp.where(r == src, 1.0, 0.0).astype(BF16)
        rows = 256
        for rb in range(D_MODEL // rows):
            rs = slice(rb * rows, (rb + 1) * rows)
            for cb in range(D_EXPERT // LANES):
                w = wu_buf[slot, rs, cb * 2 * LANES:(cb + 1) * 2 * LANES].astype(BF16)
                split = _dot(w, perm).astype(BF16)
                wg_s[rs, cb * LANES:(cb + 1) * LANES] = split[:, :LANES]
                wl_s[rs, cb * LANES:(cb + 1) * LANES] = split[:, LANES:]
        wd_s[...] = wd_buf[slot].astype(BF16)

    def expert_mlp(rows):
        x = jnp.concatenate(_tiles_to_rows(xs_ref, rows), axis=1).astype(BF16)
        hglu = _dot(x, wg_s[...]) + bg_ref[0]
        hlin = _dot(x, wl_s[...]) + bl_ref[0]
        glu = jnp.minimum(hglu, SWIGLU_LIMIT)
        lin = jnp.clip(hlin, -SWIGLU_LIMIT, SWIGLU_LIMIT)
        act = glu * _sigmoid(SWIGLU_ALPHA * glu) * (lin + 1.0)
        _rows_to_tiles(y_ref, _dot(act.astype(BF16), wd_s[...]) + bd_ref[0])

    valid = bvalid_ref[i]
    half = MOE_BLOCK_ROWS // 2

    @pl.when(valid > half)
    def _():
        expert_mlp(MOE_BLOCK_ROWS)

    @pl.when((valid > 0) & (valid <= half))
    def _():
        expert_mlp(half)
        y_ref[half * ROW_TILES:, :] = jnp.zeros((half * ROW_TILES, LANES), F32)

    @pl.when(valid == 0)
    def _():
        y_ref[...] = jnp.zeros_like(y_ref)


def _moe(block_exp, block_first, block_slot, block_next, block_valid, n_used,
         xs, w_up, b_glu, b_lin, w_down, b_down):
    p_rows = xs.shape[0] // ROW_TILES
    bm = MOE_BLOCK_ROWS
    nb = p_rows // bm
    exp3 = lambda i, be, *_: (be[i], 0, 0)
    grid_spec = pltpu.PrefetchScalarGridSpec(
        num_scalar_prefetch=6,
        grid=(nb,),
        in_specs=[
            pl.BlockSpec((bm * ROW_TILES, LANES), lambda i, be, bf, bs, bn, bv, nu: (jnp.minimum(i, nu[0] - 1), 0)),
            pl.BlockSpec(memory_space=pl.ANY),
            pl.BlockSpec((1, 1, D_EXPERT), exp3),
            pl.BlockSpec((1, 1, D_EXPERT), exp3),
            pl.BlockSpec(memory_space=pl.ANY),
            pl.BlockSpec((1, 1, D_MODEL), exp3),
        ],
        out_specs=pl.BlockSpec((bm * ROW_TILES, LANES), lambda i, *_: (i, 0)),
        scratch_shapes=[pltpu.VMEM((2, D_MODEL, 2 * D_EXPERT), F32), pltpu.VMEM((2, D_EXPERT, D_MODEL), F32),
                        pltpu.VMEM((D_MODEL, D_EXPERT), BF16), pltpu.VMEM((D_MODEL, D_EXPERT), BF16),
                        pltpu.VMEM((D_EXPERT, D_MODEL), BF16), pltpu.SemaphoreType.DMA((2, 2))],
    )
    return pl.pallas_call(
        _moe_kernel,
        grid_spec=grid_spec,
        out_shape=jax.ShapeDtypeStruct((p_rows * ROW_TILES, LANES), F32),
        compiler_params=pltpu.CompilerParams(
            dimension_semantics=("arbitrary",), vmem_limit_bytes=VMEM_LIMIT),
        name="moe",
    )(block_exp, block_first, block_slot, block_next, block_valid, n_used,
      xs, w_up, b_glu, b_lin, w_down, b_down)


def _dispatch_kernel(dest_ref, zstart_ref, xn_ref, xs_hbm, zero_ref, sem, zsem):
    i = pl.program_id(0)
    bm = MOE_BLOCK_ROWS
    tokens = xn_ref.shape[0] // ROW_TILES

    def zero_copy(e):
        start = pl.multiple_of(zstart_ref[e] * ROW_TILES, bm * ROW_TILES)
        return pltpu.make_async_copy(zero_ref, xs_hbm.at[pl.ds(start, bm * ROW_TILES)], zsem)

    @pl.when(i == 0)
    def _():
        zero_ref[...] = jnp.zeros_like(zero_ref)
        for e in range(N_EXPERTS):
            @pl.when(zstart_ref[e] >= 0)
            def _():
                zero_copy(e).start()
        for e in range(N_EXPERTS):
            @pl.when(zstart_ref[e] >= 0)
            def _():
                zero_copy(e).wait()

    t0 = i * tokens

    def body(j, carry):
        src = xn_ref.at[pl.ds(pl.multiple_of(j * ROW_TILES, ROW_TILES), ROW_TILES)]
        for k in range(TOP_K):
            d = pl.multiple_of(dest_ref[(t0 + j) * TOP_K + k] * ROW_TILES, ROW_TILES)
            pltpu.make_async_copy(src, xs_hbm.at[pl.ds(d, ROW_TILES)], sem).start(priority=k % 2)
        return carry

    lax.fori_loop(0, tokens, body, 0, unroll=8)
    for k in range(TOP_K):
        pltpu.make_async_copy(xn_ref, xs_hbm.at[pl.ds(0, tokens * ROW_TILES)], sem).wait()


def _dispatch(dest, zstart, xn, p_rows):
    n = xn.shape[0] // ROW_TILES
    t = DISPATCH_TOKENS
    grid_spec = pltpu.PrefetchScalarGridSpec(
        num_scalar_prefetch=2,
        grid=(n // t,),
        in_specs=[pl.BlockSpec((t * ROW_TILES, LANES), lambda i, d, z: (i, 0))],
        out_specs=pl.BlockSpec(memory_space=pl.ANY),
        scratch_shapes=[pltpu.VMEM((MOE_BLOCK_ROWS * ROW_TILES, LANES), F32),
                        pltpu.SemaphoreType.DMA, pltpu.SemaphoreType.DMA],
    )
    return pl.pallas_call(
        _dispatch_kernel,
        grid_spec=grid_spec,
        out_shape=jax.ShapeDtypeStruct((p_rows * ROW_TILES, LANES), F32),
        compiler_params=pltpu.CompilerParams(
            dimension_semantics=("arbitrary",), vmem_limit_bytes=VMEM_LIMIT),
        name="dispatch",
    )(dest, zstart, xn)


def _combine_kernel(dest_ref, y_hbm, x2_ref, gate_ref, out_ref, buf_ref, sems):
    i = pl.program_id(0)
    tc = COMBINE_TOKENS

    def issue(step, slot):
        t0 = step * tc

        def body(j, carry):
            for k in range(TOP_K):
                d = pl.multiple_of(dest_ref[(t0 + j) * TOP_K + k] * ROW_TILES, ROW_TILES)
                r = pl.multiple_of((k * tc + j) * ROW_TILES, ROW_TILES)
                pltpu.make_async_copy(y_hbm.at[pl.ds(d, ROW_TILES)], buf_ref.at[slot, pl.ds(r, ROW_TILES)],
                                      sems.at[slot]).start(priority=k % 2)
            return carry

        lax.fori_loop(0, tc, body, 0, unroll=8)

    @pl.when(i == 0)
    def _():
        issue(0, 0)

    @pl.when(i + 1 < pl.num_programs(0))
    def _():
        issue(i + 1, (i + 1) % 2)

    slot = i % 2
    pltpu.make_async_copy(y_hbm.at[pl.ds(0, TOP_K * tc * ROW_TILES)], buf_ref.at[slot], sems.at[slot]).wait()
    rows = buf_ref.at[slot]
    sub = 64
    for r0 in range(0, tc, sub):
        gate = gate_ref[r0:r0 + sub, :]
        gates = [jnp.broadcast_to(gate[:, k:k + 1], (sub, LANES)) for k in range(TOP_K)]
        for c in range(ROW_TILES):
            acc = x2_ref[r0:r0 + sub, c * LANES:(c + 1) * LANES]
            for k in range(TOP_K):
                acc = acc + gates[k] * rows[pl.ds((k * tc + r0) * ROW_TILES + c, sub, stride=ROW_TILES), :]
            out_ref[r0:r0 + sub, c * LANES:(c + 1) * LANES] = acc


def _combine(dest, y, x2, gates):
    n = x2.shape[0]
    tc = COMBINE_TOKENS
    grid_spec = pltpu.PrefetchScalarGridSpec(
        num_scalar_prefetch=1,
        grid=(n // tc,),
        in_specs=[pl.BlockSpec(memory_space=pl.ANY),
                  pl.BlockSpec((tc, D_MODEL), lambda i, d: (i, 0)),
                  pl.BlockSpec((tc, LANES), lambda i, d: (i, 0))],
        out_specs=pl.BlockSpec((tc, D_MODEL), lambda i, d: (i, 0)),
        scratch_shapes=[pltpu.VMEM((2, TOP_K * tc * ROW_TILES, LANES), F32), pltpu.SemaphoreType.DMA((2,))],
    )
    return pl.pallas_call(
        _combine_kernel,
        grid_spec=grid_spec,
        out_shape=jax.ShapeDtypeStruct((n, D_MODEL), F32),
        compiler_params=pltpu.CompilerParams(
            dimension_semantics=("arbitrary",), vmem_limit_bytes=VMEM_LIMIT),
        name="combine",
    )(dest, y, x2, gates)


def _route(top_idx):
    n = top_idx.shape[0]
    a = n * TOP_K
    bm = MOE_BLOCK_ROWS
    nb = a // bm + N_EXPERTS
    e_flat = top_idx.reshape(a)
    onehot = (e_flat[:, None] == jnp.arange(N_EXPERTS, dtype=jnp.int32)[None, :]).astype(jnp.int32)
    csum = jnp.cumsum(onehot, axis=0)
    counts = csum[-1]
    padded = ((counts + bm - 1) // bm) * bm
    pad_end = jnp.cumsum(padded)
    pad_start = pad_end - padded
    dest = jnp.sum(onehot * (csum - 1 + pad_start[None, :]), axis=1)
    n_used = (pad_end[-1] // bm).astype(jnp.int32)
    blk = jnp.arange(nb, dtype=jnp.int32)
    bexp = jnp.sum((pad_end[None, :] <= (blk * bm)[:, None]).astype(jnp.int32), axis=1)
    bexp = jnp.minimum(bexp, N_EXPERTS - 1)
    bexp = jnp.where(blk < n_used, bexp, bexp[jnp.maximum(n_used - 1, 0)])
    bfirst = jnp.concatenate([jnp.ones((1,), jnp.int32), (bexp[1:] != bexp[:-1]).astype(jnp.int32)])
    bslot = (jnp.cumsum(bfirst) - 1) % 2
    later = jnp.where(bexp[None, :] > bexp[:, None], bexp[None, :], N_EXPERTS)
    bnext = jnp.min(later, axis=1)
    bnext = jnp.where(bnext < N_EXPERTS, bnext, -1).astype(jnp.int32)
    valid_end = jnp.sum(jnp.where(bexp[:, None] == jnp.arange(N_EXPERTS)[None, :], (pad_start + counts)[None, :], 0), axis=1)
    bvalid = jnp.clip(valid_end - blk * bm, 0, bm).astype(jnp.int32)
    zstart = jnp.where(counts > 0, pad_end - bm, -1).astype(jnp.int32)
    return dest.astype(jnp.int32), zstart, (bexp, bfirst, bslot.astype(jnp.int32), bnext, bvalid, n_used.reshape(1))


def kernel(x, positions, norm1_g, w_in, q_norm_g, k_norm_g, hgrn_lower_bounds, hgrn_onorm_g,
           w_out, norm2_g, w_router, b_router, w_up, b_up, w_down, b_down):
    batch, seq, d = x.shape
    n = batch * seq
    depth = norm1_g.shape[0]
    lbs_all = jnp.cumsum(jax.nn.softmax(hgrn_lower_bounds.astype(F32), axis=0), axis=0)
    half = ATT_DIM // 2
    inv = 1.0 / (ROPE_THETA ** (jnp.arange(half, dtype=F32) / half))
    inv_tab = jnp.tile(inv, LANES // half).reshape(1, LANES)
    pos_col = positions.reshape(n, 1)

    x2d = x.reshape(n, d)
    for l in range(depth):
        lbs = lbs_all[l].reshape(2 * HG_HEADS, 1, HG_DIM)
        proj, qkv4, qkv16 = _inproj(
            x2d, pos_col, inv_tab, norm1_g[l].reshape(1, d), w_in[l].astype(BF16),
            jnp.tile(q_norm_g[l], LANES // ATT_DIM).reshape(1, LANES),
            jnp.tile(k_norm_g[l], LANES // ATT_DIM).reshape(1, LANES), batch, seq)
        o_f, o_b = _hgrn(proj, lbs, batch, seq)
        o1, l1 = _attention(proj.reshape(batch, 1, seq, IN_COLS), COL_AQ)
        o4, l4 = _attention(qkv4, 0)
        o16, l16 = _attention(qkv16, 0)
        atts = [o1.reshape(n, ATT_WIDTH), o4, o16]
        lses = [l1.reshape(n, LANES), l4, l16]

        wr = jnp.pad(w_router[l], ((0, 0), (0, LANES - N_EXPERTS)))
        wr_hi = wr.astype(BF16)
        wr_lo = (wr - wr_hi.astype(F32)).astype(BF16)
        br = jnp.pad(b_router[l], (0, LANES - N_EXPERTS)).reshape(1, LANES)
        x2, xn, gates, top_idx = _outproj(
            o_f, o_b, proj, atts, lses, x2d, hgrn_onorm_g[l].reshape(1, HG_DIM),
            w_out[l].astype(BF16), norm2_g[l].reshape(1, d), wr_hi, wr_lo, br, seq)
        dest, zstart, blocks = _route(top_idx[:, :TOP_K])
        xs = _dispatch(dest, zstart, xn, blocks[0].shape[0] * MOE_BLOCK_ROWS)
        y = _moe(*blocks, xs, w_up[l],
                 b_up[l][:, 0::2].reshape(N_EXPERTS, 1, D_EXPERT),
                 b_up[l][:, 1::2].reshape(N_EXPERTS, 1, D_EXPERT),
                 w_down[l], b_down[l].reshape(N_EXPERTS, 1, D_MODEL))
        x2d = _combine(dest, y, x2, gates)
    return x2d.reshape(batch, seq, d)
```

```python
import functools

import jax
import jax.numpy as jnp
from jax import lax
from jax.experimental import pallas as pl
from jax.experimental.pallas import tpu as pltpu

F32 = jnp.float32
BF16 = jnp.bfloat16
F8 = jnp.float8_e4m3fn
FP8_TARGET = 240.0
ROW_FP8_TARGET = 8.0
ACT_FP8_SCALE = 4.0

D_MODEL = 1024
HG_HEADS = 4
HG_DIM = 128
HG_WIDTH = HG_HEADS * HG_DIM
HG_CHUNK = 64
ATT_HEADS = 8
ATT_DIM = 64
ATT_WIDTH = ATT_HEADS * ATT_DIM
DILATED_PATTERNS = ((128, 1), (512, 4), (2048, 16))
ATT_HALF = 64
ATT_QBLOCK = 128
ROPE_THETA = 10000.0
IN_COLS = 5 * HG_WIDTH + 3 * ATT_WIDTH
N_EXPERTS = 32
TOP_K = 4
D_EXPERT = D_MODEL
SWIGLU_LIMIT = 7.0
SWIGLU_ALPHA = 1.702
EPS = 1e-6
NEG = -1e30

COL_HQ, COL_HF_FWD, COL_HF_BWD, COL_HI, COL_HG, COL_AQ, COL_AK, COL_AV = range(8)

TOKEN_TILE = 1024
INPROJ_TILE = 1024
HGRN_TILE = 2048
ATT_TILE = 1024
MOE_BLOCK_ROWS = 512
DISPATCH_TOKENS = 512
COMBINE_TOKENS = 256
LANES = 128
ROW_TILES = D_MODEL // LANES
VMEM_LIMIT = 60 * 1024 * 1024


def _dot(a, b):
    return jnp.dot(a, b, preferred_element_type=F32)


def _dot_nt(a, b):
    return lax.dot_general(a, b, (((1,), (1,)), ((), ())), preferred_element_type=F32)


def _dot_tn(a, b):
    return lax.dot_general(a, b, (((0,), (0,)), ((), ())), preferred_element_type=F32)


def _sigmoid(x):
    return 0.5 * jnp.tanh(0.5 * x) + 0.5


def _abs_max(x):
    return jnp.max(jnp.max(jnp.abs(x), axis=0, keepdims=True), axis=1, keepdims=True)


def _fp8_scale(amax):
    return jnp.where(amax > 0.0, FP8_TARGET / amax, 1.0)


def _rows_to_tiles(dst_ref, x):
    for c in range(ROW_TILES):
        dst_ref[pl.ds(c, x.shape[0], stride=ROW_TILES), :] = x[:, c * LANES:(c + 1) * LANES]


def _tiles_to_rows(src, rows, first_row=0):
    return [src[pl.ds(first_row * ROW_TILES + c, rows, stride=ROW_TILES), :] for c in range(ROW_TILES)]


def _head_norm_rope(p, gain, cos, sin_signed, scale):
    lane = lax.broadcasted_iota(jnp.int32, (p.shape[0], LANES), 1)
    low = lane < ATT_DIM
    first_half = (lane % ATT_DIM) < (ATT_DIM // 2)
    outs = []
    for t in range(ATT_WIDTH // LANES):
        blk = p[:, t * LANES:(t + 1) * LANES]
        sq = blk * blk
        s_low = jnp.sum(jnp.where(low, sq, 0.0), axis=-1, keepdims=True)
        s_high = jnp.sum(jnp.where(low, 0.0, sq), axis=-1, keepdims=True)
        r = jnp.where(low, lax.rsqrt(s_low * (1.0 / ATT_DIM) + EPS),
                      lax.rsqrt(s_high * (1.0 / ATT_DIM) + EPS))
        y = blk * r * gain
        partner = jnp.where(first_half, pltpu.roll(y, LANES - ATT_DIM // 2, axis=1),
                            pltpu.roll(y, ATT_DIM // 2, axis=1))
        outs.append((y * cos + partner * sin_signed) * scale)
    return jnp.concatenate(outs, axis=1)


def _inproj_kernel(x_ref, pos_ref, inv_ref, g1_ref, w_ref, qg_ref, kg_ref,
                   out_ref, d4_ref, d16_ref, stage_ref, stage2_ref):
    x = x_ref[...]
    ms = jnp.mean(x * x, axis=-1, keepdims=True)
    h = (x * lax.rsqrt(ms + EPS) * g1_ref[...]).astype(BF16)
    ang = pos_ref[...].astype(F32) * inv_ref[...]
    lane = lax.broadcasted_iota(jnp.int32, ang.shape, 1)
    cos = jnp.cos(ang)
    sin_signed = jnp.where((lane % ATT_DIM) < (ATT_DIM // 2), -jnp.sin(ang), jnp.sin(ang))
    order = (COL_AQ, COL_AK, COL_AV, COL_HQ, COL_HF_FWD, COL_HF_BWD, COL_HI, COL_HG)
    nxt = _dot(h, w_ref[:, order[0] * 512:(order[0] + 1) * 512])
    for pos, j in enumerate(order):
        p = nxt
        if pos + 1 < len(order):
            jn = order[pos + 1]
            nxt = _dot(h, w_ref[:, jn * 512:(jn + 1) * 512])
        if j == COL_AQ:
            p = _head_norm_rope(p, qg_ref[...], cos, sin_signed, ATT_DIM ** -0.5)
        elif j == COL_AK:
            p = _head_norm_rope(p, kg_ref[...], cos, sin_signed, 1.0)
        out_ref[:, j * 512:(j + 1) * 512] = p.astype(BF16)
        if j >= COL_AQ:
            rows4, rows16 = x.shape[0] // 4, x.shape[0] // 16
            for c in range(ATT_WIDTH // LANES):
                cols = slice((j - COL_AQ) * ATT_WIDTH + c * LANES, (j - COL_AQ) * ATT_WIDTH + (c + 1) * LANES)
                stage_ref[c] = p[:, c * LANES:(c + 1) * LANES]
                for r4 in range(4):
                    group = stage_ref[c, pl.ds(r4, rows4, stride=4), :]
                    d4_ref[0, r4, :, cols] = group.astype(BF16)
                    stage2_ref[c, r4 * rows4:(r4 + 1) * rows4, :] = group
                for r4 in range(4):
                    for m in range(4):
                        d16_ref[0, r4 + 4 * m, :, cols] = (
                            stage2_ref[c, pl.ds(r4 * rows4 + m, rows16, stride=4), :].astype(BF16))


def _inproj(x2d, pos_col, inv_tab, g1, w_in_bf16, qg, kg, batch, seq):
    n = x2d.shape[0]
    t = INPROJ_TILE
    nt = seq // t
    const = lambda i: (0, 0)
    qkv = 3 * ATT_WIDTH

    def residue_major(dil):
        spec = pl.BlockSpec((1, dil, t // dil, qkv), lambda i: (i // nt, 0, i % nt, 0))
        return spec, jax.ShapeDtypeStruct((batch, dil, seq // dil, qkv), BF16)

    spec4, shape4 = residue_major(4)
    spec16, shape16 = residue_major(16)
    return pl.pallas_call(
        _inproj_kernel,
        grid=(n // t,),
        in_specs=[
            pl.BlockSpec((t, D_MODEL), lambda i: (i, 0)),
            pl.BlockSpec((t, 1), lambda i: (i, 0)),
            pl.BlockSpec((1, LANES), const),
            pl.BlockSpec((1, D_MODEL), const),
            pl.BlockSpec((D_MODEL, IN_COLS), const, pipeline_mode=pl.Buffered(1)),
            pl.BlockSpec((1, LANES), const),
            pl.BlockSpec((1, LANES), const),
        ],
        out_specs=[pl.BlockSpec((t, IN_COLS), lambda i: (i, 0)), spec4, spec16],
        out_shape=[jax.ShapeDtypeStruct((n, IN_COLS), BF16), shape4, shape16],
        scratch_shapes=[pltpu.VMEM((ATT_WIDTH // LANES, t, LANES), F32)] * 2,
        compiler_params=pltpu.CompilerParams(
            dimension_semantics=("parallel",), vmem_limit_bytes=VMEM_LIMIT),
        name="inproj",
    )(x2d, pos_col, inv_tab, g1, w_in_bf16, qg, kg)


def _hgrn_direction(q_ref, z_ref, v_ref, lb, state_t, reverse):
    c = HG_CHUNK
    t = q_ref.shape[0]
    n = t // c
    row = lax.broadcasted_iota(jnp.int32, (c, c), 0)
    col = lax.broadcasted_iota(jnp.int32, (c, c), 1)
    mask = (row <= col) if reverse else (row >= col)
    tri = jnp.where(mask, 1.0, 0.0).astype(BF16)
    last_row = 0 if reverse else c - 1

    z = z_ref[...].astype(F32)
    q = q_ref[...].astype(F32)
    v = v_ref[...]
    sg = _sigmoid(z)
    f = lb + (1.0 - lb) * sg
    k = (1.0 - lb) * (1.0 - sg)
    lf = jnp.log(f)
    lf_hi = lf.astype(BF16)
    lf_lo = (lf - lf_hi.astype(F32)).astype(BF16)
    chunks = [slice(j * c, (j + 1) * c) for j in range(n)]
    b = jnp.concatenate([_dot(tri, lf_hi[rs]) + _dot(tri, lf_lo[rs]) for rs in chunks], axis=0)
    b_last = b.reshape(n, c, HG_DIM)[:, last_row:last_row + 1, :]
    decay = jnp.exp(b_last)
    qt = (q * _sigmoid(q) * jnp.exp(b)).astype(BF16)
    kt_f32 = k * jnp.exp(-b)
    kt = kt_f32.astype(BF16)
    kd = (kt_f32.reshape(n, c, HG_DIM) * decay).reshape(t, HG_DIM).astype(BF16)

    outs, updates = [], []
    for rs in chunks:
        a = jnp.where(mask, _dot_nt(qt[rs], kt[rs]), 0.0)
        outs.append(_dot(a.astype(BF16), v[rs]))
        updates.append(_dot_tn(v[rs], kd[rs]))
    for j in (reversed(range(n)) if reverse else range(n)):
        outs[j] = outs[j] + _dot_nt(qt[chunks[j]], state_t.astype(BF16))
        state_t = state_t * decay[j] + updates[j]
    return jnp.concatenate(outs, axis=0), state_t


def _hgrn_kernel(qf_ref, zf_ref, vf_ref, qb_ref, zb_ref, vb_ref, lbf_ref, lbb_ref,
                 of_ref, ob_ref, sf_ref, sb_ref):
    @pl.when(pl.program_id(2) == 0)
    def _():
        sf_ref[...] = jnp.zeros_like(sf_ref)
        sb_ref[...] = jnp.zeros_like(sb_ref)

    o, sf = _hgrn_direction(qf_ref, zf_ref, vf_ref, lbf_ref[0], sf_ref[...], False)
    of_ref[...] = o.astype(of_ref.dtype)
    sf_ref[...] = sf
    o, sb = _hgrn_direction(qb_ref, zb_ref, vb_ref, lbb_ref[0], sb_ref[...], True)
    ob_ref[...] = o.astype(ob_ref.dtype)
    sb_ref[...] = sb


def _hgrn(proj, lbs, batch, seq):
    n = proj.shape[0]
    t = HGRN_TILE
    nblk = seq // t

    def fwd(colblk):
        return pl.BlockSpec((t, HG_DIM), lambda b, h, i: (b * nblk + i, colblk * HG_HEADS + h))

    def bwd(colblk):
        return pl.BlockSpec((t, HG_DIM), lambda b, h, i: (b * nblk + nblk - 1 - i, colblk * HG_HEADS + h))

    out_f = pl.BlockSpec((t, HG_DIM), lambda b, h, i: (b * nblk + i, h))
    out_b = pl.BlockSpec((t, HG_DIM), lambda b, h, i: (b * nblk + nblk - 1 - i, h))
    return pl.pallas_call(
        _hgrn_kernel,
        grid=(batch, HG_HEADS, nblk),
        in_specs=[
            fwd(COL_HQ), fwd(COL_HF_FWD), fwd(COL_HI),
            bwd(COL_HQ), bwd(COL_HF_BWD), bwd(COL_HI),
            pl.BlockSpec((1, 1, HG_DIM), lambda b, h, i: (h, 0, 0)),
            pl.BlockSpec((1, 1, HG_DIM), lambda b, h, i: (HG_HEADS + h, 0, 0)),
        ],
        out_specs=[out_f, out_b],
        out_shape=[jax.ShapeDtypeStruct((n, HG_WIDTH), BF16)] * 2,
        scratch_shapes=[pltpu.VMEM((HG_DIM, HG_DIM), F32)] * 2,
        compiler_params=pltpu.CompilerParams(
            dimension_semantics=("parallel", "parallel", "arbitrary"), vmem_limit_bytes=VMEM_LIMIT),
        name="hgrn",
    )(proj, proj, proj, proj, proj, proj, lbs, lbs)


def _attn_kernel(q_ref, kc_ref, kp_ref, kn_ref, vc_ref, vp_ref, vn_ref, o_ref, l_ref,
                 kw_ref, vw_ref, *, tq, length):
    n = pl.program_id(2)
    half = ATT_HALF
    kw_ref[0:half, :] = kp_ref[...]
    kw_ref[half:half + tq, :] = kc_ref[...]
    kw_ref[half + tq:, :] = kn_ref[...]
    vw_ref[0:half, :] = vp_ref[...]
    vw_ref[half:half + tq, :] = vc_ref[...]
    vw_ref[half + tq:, :] = vn_ref[...]

    qb_rows = ATT_QBLOCK
    win = qb_rows + 2 * half
    i_idx = lax.broadcasted_iota(jnp.int32, (qb_rows, win), 0)
    j_idx = lax.broadcasted_iota(jnp.int32, (qb_rows, win), 1)
    band = (j_idx >= i_idx) & (j_idx <= i_idx + 2 * half)

    pairs = [slice(p * LANES, (p + 1) * LANES) for p in range(ATT_HEADS // 2)]
    head_lane = lax.broadcasted_iota(jnp.int32, (qb_rows, LANES), 1)
    even_half = head_lane < ATT_DIM
    keep_even = jnp.where(lax.broadcasted_iota(jnp.int32, (1, LANES), 1) < ATT_DIM, 1.0, 0.0).astype(BF16)
    keep_odd = (1.0 - keep_even.astype(F32)).astype(BF16)

    def masked_scores(r0):
        base = n * tq + r0 - half
        valid = band & (j_idx >= -base) & (j_idx < length - base)
        q = q_ref[r0:r0 + qb_rows, :]
        kw = kw_ref[r0:r0 + win, :]
        out = []
        for cs in pairs:
            for keep in (keep_even, keep_odd):
                out.append(jnp.where(valid, _dot_nt(q[:, cs] * keep, kw[:, cs]), NEG))
        return out

    blocks = list(range(0, tq, qb_rows))
    nxt = masked_scores(blocks[0])
    for pos, r0 in enumerate(blocks):
        scores = nxt
        if pos + 1 < len(blocks):
            nxt = masked_scores(blocks[pos + 1])
        vw = vw_ref[r0:r0 + win, :]
        maxes = [jnp.max(s, axis=-1, keepdims=True) for s in scores]
        probs = [jnp.exp(s - m) for s, m in zip(scores, maxes)]
        dens = [jnp.sum(p, axis=-1, keepdims=True) for p in probs]
        outs = []
        for p, cs in enumerate(pairs):
            pv_even = _dot(probs[2 * p].astype(BF16), vw[:, cs])
            pv_odd = _dot(probs[2 * p + 1].astype(BF16), vw[:, cs])
            outs.append(jnp.where(even_half, pv_even, pv_odd)
                        / jnp.where(even_half, dens[2 * p], dens[2 * p + 1]))
        o_ref[r0:r0 + qb_rows, :] = jnp.concatenate(outs, axis=1).astype(o_ref.dtype)
        lse = jnp.zeros((qb_rows, LANES), F32)
        for h, (m, den) in enumerate(zip(maxes, dens)):
            lse = jnp.where(head_lane == h, m + jnp.log(den), lse)
        l_ref[r0:r0 + qb_rows, :] = lse


def _attention(qkv, col0):
    batch, dil, length, _ = qkv.shape
    tq = min(ATT_TILE, length)
    nq = length // tq
    hb = tq // ATT_HALF
    n_hblk = length // ATT_HALF

    def cur(col):
        return pl.BlockSpec((None, None, tq, ATT_WIDTH), lambda b, r, n: (b, r, n, col))

    def prev(col):
        return pl.BlockSpec((None, None, ATT_HALF, ATT_WIDTH),
                            lambda b, r, n: (b, r, jnp.maximum(n * hb - 1, 0), col))

    def nxt(col):
        return pl.BlockSpec((None, None, ATT_HALF, ATT_WIDTH),
                            lambda b, r, n: (b, r, jnp.minimum((n + 1) * hb, n_hblk - 1), col))

    out_spec = pl.BlockSpec((None, None, tq, ATT_WIDTH), lambda b, r, n: (b, r, n, 0))
    return pl.pallas_call(
        functools.partial(_attn_kernel, tq=tq, length=length),
        grid=(batch, dil, nq),
        in_specs=[cur(col0), cur(col0 + 1), prev(col0 + 1), nxt(col0 + 1),
                  cur(col0 + 2), prev(col0 + 2), nxt(col0 + 2)],
        out_specs=[out_spec, pl.BlockSpec((None, None, tq, LANES), lambda b, r, n: (b, r, n, 0))],
        out_shape=[jax.ShapeDtypeStruct((batch, dil, length, ATT_WIDTH), BF16),
                   jax.ShapeDtypeStruct((batch, dil, length, LANES), F32)],
        scratch_shapes=[pltpu.VMEM((tq + 2 * ATT_HALF, ATT_WIDTH), BF16)] * 2,
        compiler_params=pltpu.CompilerParams(
            dimension_semantics=("parallel", "parallel", "parallel"), vmem_limit_bytes=VMEM_LIMIT),
        name=f"attn_d{dil}",
    )(qkv, qkv, qkv, qkv, qkv, qkv, qkv)


def _token_major(src_ref, stage_ref, tmp_ref):
    dil, rows = src_ref.shape[1], src_ref.shape[2]
    nc = src_ref.shape[3] // LANES
    for c in range(nc):
        cols = slice(c * LANES, (c + 1) * LANES)
        if dil == 4:
            for r in range(dil):
                stage_ref[c, pl.ds(r, rows, stride=dil), :] = src_ref[0, r, :, cols].astype(F32)
        else:
            group = 4 * rows
            for r4 in range(4):
                for m in range(4):
                    tmp_ref[c, pl.ds(r4 * group + m, rows, stride=4), :] = (
                        src_ref[0, r4 + 4 * m, :, cols].astype(F32))
            for r4 in range(4):
                stage_ref[c, pl.ds(r4, group, stride=4), :] = tmp_ref[c, r4 * group:(r4 + 1) * group, :]
    return jnp.concatenate([stage_ref[c] for c in range(nc)], axis=1)


def _outproj_kernel(of_ref, ob_ref, hg_ref, o1_ref, o2_ref, o3_ref, l1_ref, l2_ref, l3_ref,
                    x_ref, og_ref, w_ref, g2_ref, wrh_ref, wrl_ref, br_ref,
                    x2_ref, xn_ref, gate_ref, idx_ref, st_o2, st_o3, st_l2, st_l3, st_tmp_o, st_tmp_l):
    o = of_ref[...].astype(F32) + ob_ref[...].astype(F32)
    hg = hg_ref[...].astype(F32)
    parts = []
    for h in range(HG_HEADS):
        blk = o[:, h * HG_DIM:(h + 1) * HG_DIM]
        ms = jnp.mean(blk * blk, axis=-1, keepdims=True)
        parts.append(blk * lax.rsqrt(ms + EPS) * og_ref[...])
    o_hg = jnp.concatenate(parts, axis=1) * (hg * _sigmoid(hg))

    l1 = l1_ref[...]
    l2 = _token_major(l2_ref, st_l2, st_tmp_l)
    l3 = _token_major(l3_ref, st_l3, st_tmp_l)
    mx = jnp.maximum(jnp.maximum(l1, l2), l3)
    e1, e2, e3 = jnp.exp(l1 - mx), jnp.exp(l2 - mx), jnp.exp(l3 - mx)
    den = e1 + e2 + e3
    er = lax.broadcasted_iota(jnp.int32, (LANES, ATT_WIDTH), 0)
    ec = lax.broadcasted_iota(jnp.int32, (LANES, ATT_WIDTH), 1)
    expand = jnp.where(er == ec // ATT_DIM, 1.0, 0.0).astype(BF16)

    def per_lane(w):
        return _dot(w.astype(BF16), expand)

    o_att = (per_lane(e1 / den) * o1_ref[...].astype(F32)
             + per_lane(e2 / den) * _token_major(o2_ref, st_o2, st_tmp_o)
             + per_lane(e3 / den) * _token_major(o3_ref, st_o3, st_tmp_o))

    y = _dot(o_hg.astype(BF16), w_ref[0:HG_WIDTH, :]) + _dot(o_att.astype(BF16), w_ref[HG_WIDTH:, :])
    x2 = x_ref[...] + y
    x2_ref[...] = x2

    ms = jnp.mean(x2 * x2, axis=-1, keepdims=True)
    unit = x2 * lax.rsqrt(ms + EPS)
    xn = unit * g2_ref[0:1, :]
    _rows_to_tiles(xn_ref, unit * g2_ref[1:2, :])
    xn_hi = xn.astype(BF16)
    xn_lo = (xn - xn_hi.astype(F32)).astype(BF16)
    logits = (_dot(xn_hi, wrh_ref[...]) + _dot(xn_lo, wrh_ref[...]) + _dot(xn_hi, wrl_ref[...])
              + br_ref[...])

    lane = lax.broadcasted_iota(jnp.int32, logits.shape, 1)
    lane_f = lane.astype(F32)
    work = jnp.where(lane < N_EXPERTS, logits, -jnp.inf)
    vals, idxs = [], []
    for _ in range(TOP_K):
        m = jnp.max(work, axis=-1, keepdims=True)
        idx = jnp.min(jnp.where(work == m, lane_f, float(LANES)), axis=-1, keepdims=True)
        vals.append(m)
        idxs.append(idx)
        work = jnp.where(lane_f == idx, -jnp.inf, work)
    es = [jnp.exp(v - vals[0]) for v in vals]
    den = es[0] + es[1] + es[2] + es[3]
    gate_out = jnp.zeros(logits.shape, F32)
    idx_out = jnp.zeros(logits.shape, F32)
    for k in range(TOP_K):
        gate_out = jnp.where(lane == k, es[k] / den, gate_out)
        idx_out = jnp.where(lane == k, idxs[k], idx_out)
    gate_ref[...] = gate_out
    idx_ref[...] = idx_out.astype(jnp.int32)


def _outproj(o_f, o_b, proj, atts, lses, x2d, og, w_out_bf16, g2, wr_hi, wr_lo, br, seq):
    n = x2d.shape[0]
    t = TOKEN_TILE
    nt = seq // t
    row = lambda i: (i, 0)
    const = lambda i: (0, 0)
    half = pl.BlockSpec((t, 512), row)

    def residue_major(dil, width):
        return pl.BlockSpec((1, dil, t // dil, width), lambda i: (i // nt, 0, i % nt, 0))

    rm4, rm16 = residue_major(4, ATT_WIDTH), residue_major(16, ATT_WIDTH)
    lse, lse4, lse16 = pl.BlockSpec((t, LANES), row), residue_major(4, LANES), residue_major(16, LANES)
    wide, narrow = pltpu.VMEM((ATT_WIDTH // LANES, t, LANES), F32), pltpu.VMEM((1, t, LANES), F32)
    return pl.pallas_call(
        _outproj_kernel,
        grid=(n // t,),
        in_specs=[
            half, half, pl.BlockSpec((t, 512), lambda i: (i, COL_HG)),
            half, rm4, rm16, lse, lse4, lse16,
            pl.BlockSpec((t, D_MODEL), row),
            pl.BlockSpec((1, HG_DIM), const),
            pl.BlockSpec((D_MODEL, D_MODEL), const),
            pl.BlockSpec((2, D_MODEL), const),
            pl.BlockSpec((D_MODEL, LANES), const),
            pl.BlockSpec((D_MODEL, LANES), const),
            pl.BlockSpec((1, LANES), const),
        ],
        out_specs=[pl.BlockSpec((t, D_MODEL), row), pl.BlockSpec((t * ROW_TILES, LANES), row),
                   pl.BlockSpec((t, LANES), row), pl.BlockSpec((t, LANES), row)],
        out_shape=[jax.ShapeDtypeStruct((n, D_MODEL), F32), jax.ShapeDtypeStruct((n * ROW_TILES, LANES), F32),
                   jax.ShapeDtypeStruct((n, LANES), F32), jax.ShapeDtypeStruct((n, LANES), jnp.int32)],
        scratch_shapes=[wide, wide, narrow, narrow, wide, narrow],
        compiler_params=pltpu.CompilerParams(
            dimension_semantics=("parallel",), vmem_limit_bytes=VMEM_LIMIT),
        name="outproj",
    )(o_f, o_b, proj, *atts, *lses, x2d, og, w_out_bf16, g2, wr_hi, wr_lo, br)


def _moe_kernel(bexp_ref, bfirst_ref, bslot_ref, bnext_ref, bvalid_ref, nused_ref,
                xs_ref, xinv_ref, wu_hbm, bg_ref, bl_ref, wd_hbm, bd_ref,
                y_ref, wu_buf, wd_buf, wg_s, wl_s, wd_s, up_unscale_s, down_unscale_s, sems):
    i = pl.program_id(0)

    def weight_copies(expert, slot):
        return (pltpu.make_async_copy(wu_hbm.at[expert], wu_buf.at[slot], sems.at[0, slot]),
                pltpu.make_async_copy(wd_hbm.at[expert], wd_buf.at[slot], sems.at[1, slot]))

    @pl.when(bfirst_ref[i] == 1)
    def _():
        slot = bslot_ref[i]

        @pl.when(i == 0)
        def _():
            for cp in weight_copies(bexp_ref[0], 0):
                cp.start()

        for cp in weight_copies(bexp_ref[i], slot):
            cp.wait()

        @pl.when(bnext_ref[i] >= 0)
        def _():
            for cp in weight_copies(bnext_ref[i], 1 - slot):
                cp.start()

        su = _fp8_scale(_abs_max(wu_buf[slot]))
        sd = _fp8_scale(_abs_max(wd_buf[slot]))
        up_unscale_s[...] = xinv_ref[...] / su
        down_unscale_s[...] = (1.0 / ACT_FP8_SCALE) / sd
        r = lax.broadcasted_iota(jnp.int32, (2 * LANES, 2 * LANES), 0)
        c = lax.broadcasted_iota(jnp.int32, (2 * LANES, 2 * LANES), 1)
        src = jnp.where(c < LANES, 2 * c, 2 * (c - LANES) + 1)
        perm = jnp.where(r == src, 1.0, 0.0).astype(BF16)
        rows = 256
        for rb in range(D_MODEL // rows):
            rs = slice(rb * rows, (rb + 1) * rows)
            for cb in range(D_EXPERT // LANES):
                w = wu_buf[slot, rs, cb * 2 * LANES:(cb + 1) * 2 * LANES].astype(BF16)
                split = (_dot(w, perm) * su).astype(F8)
                wg_s[rs, cb * LANES:(cb + 1) * LANES] = split[:, :LANES]
                wl_s[rs, cb * LANES:(cb + 1) * LANES] = split[:, LANES:]
        wd_s[...] = (wd_buf[slot] * sd).astype(F8)

    def expert_mlp(rows):
        x8 = jnp.concatenate(_tiles_to_rows(xs_ref, rows), axis=1).astype(F8)
        hglu = _dot(x8, wg_s[...]) * up_unscale_s[...] + bg_ref[0]
        hlin = _dot(x8, wl_s[...]) * up_unscale_s[...] + bl_ref[0]
        glu = jnp.minimum(hglu, SWIGLU_LIMIT)
        lin = jnp.clip(hlin, -SWIGLU_LIMIT, SWIGLU_LIMIT)
        act = glu * _sigmoid(SWIGLU_ALPHA * glu) * (lin + 1.0)
        y = _dot((act * ACT_FP8_SCALE).astype(F8), wd_s[...]) * down_unscale_s[...] + bd_ref[0]
        _rows_to_tiles(y_ref, y)

    valid = bvalid_ref[i]
    half = MOE_BLOCK_ROWS // 2

    @pl.when(valid > half)
    def _():
        expert_mlp(MOE_BLOCK_ROWS)

    @pl.when((valid > 0) & (valid <= half))
    def _():
        expert_mlp(half)
        y_ref[half * ROW_TILES:, :] = jnp.zeros((half * ROW_TILES, LANES), F32)

    @pl.when(valid == 0)
    def _():
        y_ref[...] = jnp.zeros_like(y_ref)


def _moe(block_exp, block_first, block_slot, block_next, block_valid, n_used,
         xs, x_unscale, w_up, b_glu, b_lin, w_down, b_down):
    p_rows = xs.shape[0] // ROW_TILES
    bm = MOE_BLOCK_ROWS
    nb = p_rows // bm
    exp3 = lambda i, be, *_: (be[i], 0, 0)
    grid_spec = pltpu.PrefetchScalarGridSpec(
        num_scalar_prefetch=6,
        grid=(nb,),
        in_specs=[
            pl.BlockSpec((bm * ROW_TILES, LANES), lambda i, be, bf, bs, bn, bv, nu: (jnp.minimum(i, nu[0] - 1), 0)),
            pl.BlockSpec((1, 1), lambda i, *_: (0, 0)),
            pl.BlockSpec(memory_space=pl.ANY),
            pl.BlockSpec((1, 1, D_EXPERT), exp3),
            pl.BlockSpec((1, 1, D_EXPERT), exp3),
            pl.BlockSpec(memory_space=pl.ANY),
            pl.BlockSpec((1, 1, D_MODEL), exp3),
        ],
        out_specs=pl.BlockSpec((bm * ROW_TILES, LANES), lambda i, *_: (i, 0)),
        scratch_shapes=[pltpu.VMEM((2, D_MODEL, 2 * D_EXPERT), F32), pltpu.VMEM((2, D_EXPERT, D_MODEL), F32),
                        pltpu.VMEM((D_MODEL, D_EXPERT), F8), pltpu.VMEM((D_MODEL, D_EXPERT), F8),
                        pltpu.VMEM((D_EXPERT, D_MODEL), F8), pltpu.VMEM((1, 1), F32), pltpu.VMEM((1, 1), F32),
                        pltpu.SemaphoreType.DMA((2, 2))],
    )
    return pl.pallas_call(
        _moe_kernel,
        grid_spec=grid_spec,
        out_shape=jax.ShapeDtypeStruct((p_rows * ROW_TILES, LANES), F32),
        compiler_params=pltpu.CompilerParams(
            dimension_semantics=("arbitrary",), vmem_limit_bytes=VMEM_LIMIT),
        name="moe",
    )(block_exp, block_first, block_slot, block_next, block_valid, n_used,
      xs, x_unscale, w_up, b_glu, b_lin, w_down, b_down)


def _dispatch_kernel(dest_ref, zstart_ref, xn_ref, xs_hbm, zero_ref, sem, zsem):
    i = pl.program_id(0)
    bm = MOE_BLOCK_ROWS
    tokens = xn_ref.shape[0] // ROW_TILES

    def zero_copy(e):
        start = pl.multiple_of(zstart_ref[e] * ROW_TILES, bm * ROW_TILES)
        return pltpu.make_async_copy(zero_ref, xs_hbm.at[pl.ds(start, bm * ROW_TILES)], zsem)

    @pl.when(i == 0)
    def _():
        zero_ref[...] = jnp.zeros_like(zero_ref)
        for e in range(N_EXPERTS):
            @pl.when(zstart_ref[e] >= 0)
            def _():
                zero_copy(e).start()
        for e in range(N_EXPERTS):
            @pl.when(zstart_ref[e] >= 0)
            def _():
                zero_copy(e).wait()

    t0 = i * tokens

    def body(j, carry):
        src = xn_ref.at[pl.ds(pl.multiple_of(j * ROW_TILES, ROW_TILES), ROW_TILES)]
        for k in range(TOP_K):
            d = pl.multiple_of(dest_ref[(t0 + j) * TOP_K + k] * ROW_TILES, ROW_TILES)
            pltpu.make_async_copy(src, xs_hbm.at[pl.ds(d, ROW_TILES)], sem).start(priority=k % 2)
        return carry

    lax.fori_loop(0, tokens, body, 0, unroll=8)
    for k in range(TOP_K):
        pltpu.make_async_copy(xn_ref, xs_hbm.at[pl.ds(0, tokens * ROW_TILES)], sem).wait()


def _dispatch(dest, zstart, xn, p_rows):
    n = xn.shape[0] // ROW_TILES
    t = DISPATCH_TOKENS
    grid_spec = pltpu.PrefetchScalarGridSpec(
        num_scalar_prefetch=2,
        grid=(n // t,),
        in_specs=[pl.BlockSpec((t * ROW_TILES, LANES), lambda i, d, z: (i, 0))],
        out_specs=pl.BlockSpec(memory_space=pl.ANY),
        scratch_shapes=[pltpu.VMEM((MOE_BLOCK_ROWS * ROW_TILES, LANES), F32),
                        pltpu.SemaphoreType.DMA, pltpu.SemaphoreType.DMA],
    )
    return pl.pallas_call(
        _dispatch_kernel,
        grid_spec=grid_spec,
        out_shape=jax.ShapeDtypeStruct((p_rows * ROW_TILES, LANES), F32),
        compiler_params=pltpu.CompilerParams(
            dimension_semantics=("arbitrary",), vmem_limit_bytes=VMEM_LIMIT),
        name="dispatch",
    )(dest, zstart, xn)


def _combine_kernel(dest_ref, y_hbm, x2_ref, gate_ref, out_ref, buf_ref, sems):
    i = pl.program_id(0)
    tc = COMBINE_TOKENS

    def issue(step, slot):
        t0 = step * tc

        def body(j, carry):
            for k in range(TOP_K):
                d = pl.multiple_of(dest_ref[(t0 + j) * TOP_K + k] * ROW_TILES, ROW_TILES)
                r = pl.multiple_of((k * tc + j) * ROW_TILES, ROW_TILES)
                pltpu.make_async_copy(y_hbm.at[pl.ds(d, ROW_TILES)], buf_ref.at[slot, pl.ds(r, ROW_TILES)],
                                      sems.at[slot]).start(priority=k % 2)
            return carry

        lax.fori_loop(0, tc, body, 0, unroll=8)

    @pl.when(i == 0)
    def _():
        issue(0, 0)

    @pl.when(i + 1 < pl.num_programs(0))
    def _():
        issue(i + 1, (i + 1) % 2)

    slot = i % 2
    pltpu.make_async_copy(y_hbm.at[pl.ds(0, TOP_K * tc * ROW_TILES)], buf_ref.at[slot], sems.at[slot]).wait()
    rows = buf_ref.at[slot]
    sub = 64
    for r0 in range(0, tc, sub):
        gate = gate_ref[r0:r0 + sub, :]
        gates = [jnp.broadcast_to(gate[:, k:k + 1], (sub, LANES)) for k in range(TOP_K)]
        for c in range(ROW_TILES):
            acc = x2_ref[r0:r0 + sub, c * LANES:(c + 1) * LANES]
            for k in range(TOP_K):
                acc = acc + gates[k] * rows[pl.ds((k * tc + r0) * ROW_TILES + c, sub, stride=ROW_TILES), :]
            out_ref[r0:r0 + sub, c * LANES:(c + 1) * LANES] = acc


def _combine(dest, y, x2, gates):
    n = x2.shape[0]
    tc = COMBINE_TOKENS
    grid_spec = pltpu.PrefetchScalarGridSpec(
        num_scalar_prefetch=1,
        grid=(n // tc,),
        in_specs=[pl.BlockSpec(memory_space=pl.ANY),
                  pl.BlockSpec((tc, D_MODEL), lambda i, d: (i, 0)),
                  pl.BlockSpec((tc, LANES), lambda i, d: (i, 0))],
        out_specs=pl.BlockSpec((tc, D_MODEL), lambda i, d: (i, 0)),
        scratch_shapes=[pltpu.VMEM((2, TOP_K * tc * ROW_TILES, LANES), F32), pltpu.SemaphoreType.DMA((2,))],
    )
    return pl.pallas_call(
        _combine_kernel,
        grid_spec=grid_spec,
        out_shape=jax.ShapeDtypeStruct((n, D_MODEL), F32),
        compiler_params=pltpu.CompilerParams(
            dimension_semantics=("arbitrary",), vmem_limit_bytes=VMEM_LIMIT),
        name="combine",
    )(dest, y, x2, gates)


def _route(top_idx):
    n = top_idx.shape[0]
    a = n * TOP_K
    bm = MOE_BLOCK_ROWS
    nb = a // bm + N_EXPERTS
    e_flat = top_idx.reshape(a)
    onehot = (e_flat[:, None] == jnp.arange(N_EXPERTS, dtype=jnp.int32)[None, :]).astype(jnp.int32)
    csum = jnp.cumsum(onehot, axis=0)
    counts = csum[-1]
    padded = ((counts + bm - 1) // bm) * bm
    pad_end = jnp.cumsum(padded)
    pad_start = pad_end - padded
    dest = jnp.sum(onehot * (csum - 1 + pad_start[None, :]), axis=1)
    n_used = (pad_end[-1] // bm).astype(jnp.int32)
    blk = jnp.arange(nb, dtype=jnp.int32)
    bexp = jnp.sum((pad_end[None, :] <= (blk * bm)[:, None]).astype(jnp.int32), axis=1)
    bexp = jnp.minimum(bexp, N_EXPERTS - 1)
    bexp = jnp.where(blk < n_used, bexp, bexp[jnp.maximum(n_used - 1, 0)])
    bfirst = jnp.concatenate([jnp.ones((1,), jnp.int32), (bexp[1:] != bexp[:-1]).astype(jnp.int32)])
    bslot = (jnp.cumsum(bfirst) - 1) % 2
    later = jnp.where(bexp[None, :] > bexp[:, None], bexp[None, :], N_EXPERTS)
    bnext = jnp.min(later, axis=1)
    bnext = jnp.where(bnext < N_EXPERTS, bnext, -1).astype(jnp.int32)
    valid_end = jnp.sum(jnp.where(bexp[:, None] == jnp.arange(N_EXPERTS)[None, :], (pad_start + counts)[None, :], 0), axis=1)
    bvalid = jnp.clip(valid_end - blk * bm, 0, bm).astype(jnp.int32)
    zstart = jnp.where(counts > 0, pad_end - bm, -1).astype(jnp.int32)
    return dest.astype(jnp.int32), zstart, (bexp, bfirst, bslot.astype(jnp.int32), bnext, bvalid, n_used.reshape(1))


def kernel(x, positions, norm1_g, w_in, q_norm_g, k_norm_g, hgrn_lower_bounds, hgrn_onorm_g,
           w_out, norm2_g, w_router, b_router, w_up, b_up, w_down, b_down):
    batch, seq, d = x.shape
    n = batch * seq
    depth = norm1_g.shape[0]
    lbs_all = jnp.cumsum(jax.nn.softmax(hgrn_lower_bounds.astype(F32), axis=0), axis=0)
    half = ATT_DIM // 2
    inv = 1.0 / (ROPE_THETA ** (jnp.arange(half, dtype=F32) / half))
    inv_tab = jnp.tile(inv, LANES // half).reshape(1, LANES)
    pos_col = positions.reshape(n, 1)

    x2d = x.reshape(n, d)
    for l in range(depth):
        lbs = lbs_all[l].reshape(2 * HG_HEADS, 1, HG_DIM)
        proj, qkv4, qkv16 = _inproj(
            x2d, pos_col, inv_tab, norm1_g[l].reshape(1, d), w_in[l].astype(BF16),
            jnp.tile(q_norm_g[l], LANES // ATT_DIM).reshape(1, LANES),
            jnp.tile(k_norm_g[l], LANES // ATT_DIM).reshape(1, LANES), batch, seq)
        o_f, o_b = _hgrn(proj, lbs, batch, seq)
        o1, l1 = _attention(proj.reshape(batch, 1, seq, IN_COLS), COL_AQ)
        o4, l4 = _attention(qkv4, 0)
        o16, l16 = _attention(qkv16, 0)
        atts = [o1.reshape(n, ATT_WIDTH), o4, o16]
        lses = [l1.reshape(n, LANES), l4, l16]

        wr = jnp.pad(w_router[l], ((0, 0), (0, LANES - N_EXPERTS)))
        wr_hi = wr.astype(BF16)
        wr_lo = (wr - wr_hi.astype(F32)).astype(BF16)
        br = jnp.pad(b_router[l], (0, LANES - N_EXPERTS)).reshape(1, LANES)
        g2 = norm2_g[l].astype(F32)
        g2_max = jnp.max(jnp.abs(g2))
        x_scale = jnp.where(g2_max > 0.0, ROW_FP8_TARGET / g2_max, 1.0)
        x2, xn, gates, top_idx = _outproj(
            o_f, o_b, proj, atts, lses, x2d, hgrn_onorm_g[l].reshape(1, HG_DIM),
            w_out[l].astype(BF16), jnp.stack([g2, g2 * x_scale]), wr_hi, wr_lo, br, seq)
        dest, zstart, blocks = _route(top_idx[:, :TOP_K])
        xs = _dispatch(dest, zstart, xn, blocks[0].shape[0] * MOE_BLOCK_ROWS)
        y = _moe(*blocks, xs, (1.0 / x_scale).reshape(1, 1), w_up[l],
                 b_up[l][:, 0::2].reshape(N_EXPERTS, 1, D_EXPERT),
                 b_up[l][:, 1::2].reshape(N_EXPERTS, 1, D_EXPERT),
                 w_down[l], b_down[l].reshape(N_EXPERTS, 1, D_MODEL))
        x2d = _combine(dest, y, x2, gates)
    return x2d.reshape(batch, seq, d)
```

```python
import functools

import jax
import jax.numpy as jnp
from jax import lax
from jax.experimental import pallas as pl
from jax.experimental.pallas import tpu as pltpu

F32 = jnp.float32
BF16 = jnp.bfloat16
F8 = jnp.float8_e4m3fn
FP8_TARGET = 240.0
ROW_FP8_TARGET = 8.0
ACT_FP8_SCALE = 4.0

D_MODEL = 1024
HG_HEADS = 4
HG_DIM = 128
HG_WIDTH = HG_HEADS * HG_DIM
HG_CHUNK = 64
ATT_HEADS = 8
ATT_DIM = 64
ATT_WIDTH = ATT_HEADS * ATT_DIM
DILATED_PATTERNS = ((128, 1), (512, 4), (2048, 16))
ATT_HALF = 64
assert all(window // (2 * dil) == ATT_HALF for window, dil in DILATED_PATTERNS)
assert tuple(dil for _, dil in DILATED_PATTERNS) == (1, 4, 16)
ATT_QBLOCK = 128
ROPE_THETA = 10000.0
IN_COLS = 5 * HG_WIDTH + 3 * ATT_WIDTH
N_EXPERTS = 32
TOP_K = 4
D_EXPERT = D_MODEL
SWIGLU_LIMIT = 7.0
SWIGLU_ALPHA = 1.702
EPS = 1e-6
NEG = -1e30

COL_HQ, COL_HF_FWD, COL_HF_BWD, COL_HI, COL_HG, COL_AQ, COL_AK, COL_AV = range(8)

TOKEN_TILE = 1024
INPROJ_TILE = 1024
HGRN_TILE = 512
ATT_TILE = 1024
MOE_BLOCK_ROWS = 512
DISPATCH_TOKENS = 512
COMBINE_TOKENS = 256
LANES = 128
ROW_TILES = D_MODEL // LANES
VMEM_LIMIT = 60 * 1024 * 1024


def _dot(a, b):
    return jnp.dot(a, b, preferred_element_type=F32)


def _dot_nt(a, b):
    return lax.dot_general(a, b, (((1,), (1,)), ((), ())), preferred_element_type=F32)


def _dot_tn(a, b):
    return lax.dot_general(a, b, (((0,), (0,)), ((), ())), preferred_element_type=F32)


def _sigmoid(x):
    return 0.5 * jnp.tanh(0.5 * x) + 0.5


def _abs_max(x):
    return jnp.max(jnp.max(jnp.abs(x), axis=0, keepdims=True), axis=1, keepdims=True)


def _fp8_scale(amax):
    return jnp.where(amax > 0.0, FP8_TARGET / amax, 1.0)


def _rows_to_tiles(dst_ref, x):
    for c in range(ROW_TILES):
        dst_ref[pl.ds(c, x.shape[0], stride=ROW_TILES), :] = x[:, c * LANES:(c + 1) * LANES]


def _tiles_to_rows(src, rows, first_row=0):
    return [src[pl.ds(first_row * ROW_TILES + c, rows, stride=ROW_TILES), :] for c in range(ROW_TILES)]


def _head_norm_rope(p, gain, cos, sin_signed, scale):
    lane = lax.broadcasted_iota(jnp.int32, (p.shape[0], LANES), 1)
    low = lane < ATT_DIM
    first_half = (lane % ATT_DIM) < (ATT_DIM // 2)
    outs = []
    for t in range(ATT_WIDTH // LANES):
        blk = p[:, t * LANES:(t + 1) * LANES]
        sq = blk * blk
        s_low = jnp.sum(jnp.where(low, sq, 0.0), axis=-1, keepdims=True)
        s_high = jnp.sum(jnp.where(low, 0.0, sq), axis=-1, keepdims=True)
        r = jnp.where(low, lax.rsqrt(s_low * (1.0 / ATT_DIM) + EPS),
                      lax.rsqrt(s_high * (1.0 / ATT_DIM) + EPS))
        y = blk * r * gain
        partner = jnp.where(first_half, pltpu.roll(y, LANES - ATT_DIM // 2, axis=1),
                            pltpu.roll(y, ATT_DIM // 2, axis=1))
        outs.append((y * cos + partner * sin_signed) * scale)
    return jnp.concatenate(outs, axis=1)


def _inproj_kernel(x_ref, pos_ref, inv_ref, g1_ref, w_ref, qg_ref, kg_ref,
                   out_ref, d4_ref, d16_ref, stage_ref, stage2_ref):
    x = x_ref[...]
    ms = jnp.mean(x * x, axis=-1, keepdims=True)
    h = (x * lax.rsqrt(ms + EPS) * g1_ref[...]).astype(BF16)
    ang = pos_ref[...].astype(F32) * inv_ref[...]
    lane = lax.broadcasted_iota(jnp.int32, ang.shape, 1)
    cos = jnp.cos(ang)
    sin_signed = jnp.where((lane % ATT_DIM) < (ATT_DIM // 2), -jnp.sin(ang), jnp.sin(ang))
    order = (COL_AQ, COL_AK, COL_AV, COL_HQ, COL_HF_FWD, COL_HF_BWD, COL_HI, COL_HG)
    nxt = _dot(h, w_ref[:, order[0] * 512:(order[0] + 1) * 512])
    for pos, j in enumerate(order):
        p = nxt
        if pos + 1 < len(order):
            jn = order[pos + 1]
            nxt = _dot(h, w_ref[:, jn * 512:(jn + 1) * 512])
        if j == COL_AQ:
            p = _head_norm_rope(p, qg_ref[...], cos, sin_signed, ATT_DIM ** -0.5)
        elif j == COL_AK:
            p = _head_norm_rope(p, kg_ref[...], cos, sin_signed, 1.0)
        out_ref[:, j * 512:(j + 1) * 512] = p.astype(BF16)
        if j >= COL_AQ:
            rows4, rows16 = x.shape[0] // 4, x.shape[0] // 16
            for c in range(ATT_WIDTH // LANES):
                cols = slice((j - COL_AQ) * ATT_WIDTH + c * LANES, (j - COL_AQ) * ATT_WIDTH + (c + 1) * LANES)
                stage_ref[c] = p[:, c * LANES:(c + 1) * LANES]
                for r4 in range(4):
                    group = stage_ref[c, pl.ds(r4, rows4, stride=4), :]
                    d4_ref[0, r4, :, cols] = group.astype(BF16)
                    stage2_ref[c, r4 * rows4:(r4 + 1) * rows4, :] = group
                for r4 in range(4):
                    for m in range(4):
                        d16_ref[0, r4 + 4 * m, :, cols] = (
                            stage2_ref[c, pl.ds(r4 * rows4 + m, rows16, stride=4), :].astype(BF16))


def _inproj(x2d, pos_col, inv_tab, g1, w_in_bf16, qg, kg, batch, seq):
    n = x2d.shape[0]
    t = INPROJ_TILE
    nt = seq // t
    const = lambda i: (0, 0)
    qkv = 3 * ATT_WIDTH

    def residue_major(dil):
        spec = pl.BlockSpec((1, dil, t // dil, qkv), lambda i: (i // nt, 0, i % nt, 0))
        return spec, jax.ShapeDtypeStruct((batch, dil, seq // dil, qkv), BF16)

    spec4, shape4 = residue_major(4)
    spec16, shape16 = residue_major(16)
    return pl.pallas_call(
        _inproj_kernel,
        grid=(n // t,),
        in_specs=[
            pl.BlockSpec((t, D_MODEL), lambda i: (i, 0)),
            pl.BlockSpec((t, 1), lambda i: (i, 0)),
            pl.BlockSpec((1, LANES), const),
            pl.BlockSpec((1, D_MODEL), const),
            pl.BlockSpec((D_MODEL, IN_COLS), const, pipeline_mode=pl.Buffered(1)),
            pl.BlockSpec((1, LANES), const),
            pl.BlockSpec((1, LANES), const),
        ],
        out_specs=[pl.BlockSpec((t, IN_COLS), lambda i: (i, 0)), spec4, spec16],
        out_shape=[jax.ShapeDtypeStruct((n, IN_COLS), BF16), shape4, shape16],
        scratch_shapes=[pltpu.VMEM((ATT_WIDTH // LANES, t, LANES), F32)] * 2,
        compiler_params=pltpu.CompilerParams(
            dimension_semantics=("parallel",), vmem_limit_bytes=VMEM_LIMIT),
        name="inproj",
    )(x2d, pos_col, inv_tab, g1, w_in_bf16, qg, kg)


def _hgrn_direction(q, z, v, lb, state_t, reverse):
    c = HG_CHUNK
    t = q.shape[0]
    n = t // c
    row = lax.broadcasted_iota(jnp.int32, (c, c), 0)
    col = lax.broadcasted_iota(jnp.int32, (c, c), 1)
    mask = (row <= col) if reverse else (row >= col)
    tri = jnp.where(mask, 1.0, 0.0).astype(BF16)
    last_row = 0 if reverse else c - 1

    z = z.astype(F32)
    q = q.astype(F32)
    sg = _sigmoid(z)
    f = lb + (1.0 - lb) * sg
    k = (1.0 - lb) * (1.0 - sg)
    lf = jnp.log(f)
    lf_hi = lf.astype(BF16)
    lf_lo = (lf - lf_hi.astype(F32)).astype(BF16)
    chunks = [slice(j * c, (j + 1) * c) for j in range(n)]
    b = jnp.concatenate([_dot(tri, lf_hi[rs]) + _dot(tri, lf_lo[rs]) for rs in chunks], axis=0)
    b_last = b.reshape(n, c, HG_DIM)[:, last_row:last_row + 1, :]
    decay = jnp.exp(b_last)
    qt = (q * _sigmoid(q) * jnp.exp(b)).astype(BF16)
    kt_f32 = k * jnp.exp(-b)
    kt = kt_f32.astype(BF16)
    kd = (kt_f32.reshape(n, c, HG_DIM) * decay).reshape(t, HG_DIM).astype(BF16)

    outs, updates = [], []
    for rs in chunks:
        a = jnp.where(mask, _dot_nt(qt[rs], kt[rs]), 0.0)
        outs.append(_dot(a.astype(BF16), v[rs]))
        updates.append(_dot_tn(v[rs], kd[rs]))
    for j in (reversed(range(n)) if reverse else range(n)):
        outs[j] = outs[j] + _dot_nt(qt[chunks[j]], state_t.astype(BF16))
        state_t = state_t * decay[j] + updates[j]
    return jnp.concatenate(outs, axis=0), state_t


def _hgrn_kernel(qf_ref, zf_ref, vf_ref, qb_ref, zb_ref, vb_ref, lb_ref,
                 of_ref, ob_ref, sf_ref, sb_ref):
    @pl.when(pl.program_id(1) == 0)
    def _():
        sf_ref[...] = jnp.zeros_like(sf_ref)
        sb_ref[...] = jnp.zeros_like(sb_ref)

    for h in range(HG_HEADS):
        cols = slice(h * HG_DIM, (h + 1) * HG_DIM)
        o, sf = _hgrn_direction(qf_ref[:, cols], zf_ref[:, cols], vf_ref[:, cols],
                                lb_ref[h], sf_ref[h], False)
        of_ref[:, cols] = o.astype(of_ref.dtype)
        sf_ref[h] = sf
        o, sb = _hgrn_direction(qb_ref[:, cols], zb_ref[:, cols], vb_ref[:, cols],
                                lb_ref[HG_HEADS + h], sb_ref[h], True)
        ob_ref[:, cols] = o.astype(ob_ref.dtype)
        sb_ref[h] = sb


def _hgrn(proj, lbs, batch, seq):
    n = proj.shape[0]
    t = HGRN_TILE
    nblk = seq // t

    def fwd(colblk):
        return pl.BlockSpec((t, HG_WIDTH), lambda b, i: (b * nblk + i, colblk))

    def bwd(colblk):
        return pl.BlockSpec((t, HG_WIDTH), lambda b, i: (b * nblk + nblk - 1 - i, colblk))

    return pl.pallas_call(
        _hgrn_kernel,
        grid=(batch, nblk),
        in_specs=[
            fwd(COL_HQ), fwd(COL_HF_FWD), fwd(COL_HI),
            bwd(COL_HQ), bwd(COL_HF_BWD), bwd(COL_HI),
            pl.BlockSpec((2 * HG_HEADS, 1, HG_DIM), lambda b, i: (0, 0, 0)),
        ],
        out_specs=[fwd(0), bwd(0)],
        out_shape=[jax.ShapeDtypeStruct((n, HG_WIDTH), BF16)] * 2,
        scratch_shapes=[pltpu.VMEM((HG_HEADS, HG_DIM, HG_DIM), F32)] * 2,
        compiler_params=pltpu.CompilerParams(
            dimension_semantics=("parallel", "arbitrary"), vmem_limit_bytes=VMEM_LIMIT),
        name="hgrn",
    )(proj, proj, proj, proj, proj, proj, lbs)


def _attn_kernel(q_ref, kc_ref, kp_ref, kn_ref, vc_ref, vp_ref, vn_ref, o_ref, l_ref,
                 kw_ref, vw_ref, *, tq, length):
    n = pl.program_id(2)
    half = ATT_HALF
    kw_ref[0:half, :] = kp_ref[...]
    kw_ref[half:half + tq, :] = kc_ref[...]
    kw_ref[half + tq:, :] = kn_ref[...]
    vw_ref[0:half, :] = vp_ref[...]
    vw_ref[half:half + tq, :] = vc_ref[...]
    vw_ref[half + tq:, :] = vn_ref[...]

    qb_rows = ATT_QBLOCK
    win = qb_rows + 2 * half
    i_idx = lax.broadcasted_iota(jnp.int32, (qb_rows, win), 0)
    j_idx = lax.broadcasted_iota(jnp.int32, (qb_rows, win), 1)
    band = (j_idx >= i_idx) & (j_idx <= i_idx + 2 * half)

    pairs = [slice(p * LANES, (p + 1) * LANES) for p in range(ATT_HEADS // 2)]
    head_lane = lax.broadcasted_iota(jnp.int32, (qb_rows, LANES), 1)
    even_half = head_lane < ATT_DIM
    keep_even = jnp.where(lax.broadcasted_iota(jnp.int32, (1, LANES), 1) < ATT_DIM, 1.0, 0.0).astype(BF16)
    keep_odd = (1.0 - keep_even.astype(F32)).astype(BF16)

    def masked_scores(r0):
        base = n * tq + r0 - half
        valid = band & (j_idx >= -base) & (j_idx < length - base)
        q = q_ref[r0:r0 + qb_rows, :]
        kw = kw_ref[r0:r0 + win, :]
        out = []
        for cs in pairs:
            for keep in (keep_even, keep_odd):
                out.append(jnp.where(valid, _dot_nt(q[:, cs] * keep, kw[:, cs]), NEG))
        return out

    blocks = list(range(0, tq, qb_rows))
    nxt = masked_scores(blocks[0])
    for pos, r0 in enumerate(blocks):
        scores = nxt
        if pos + 1 < len(blocks):
            nxt = masked_scores(blocks[pos + 1])
        vw = vw_ref[r0:r0 + win, :]
        maxes = [jnp.max(s, axis=-1, keepdims=True) for s in scores]
        probs = [jnp.exp(s - m) for s, m in zip(scores, maxes)]
        dens = [jnp.sum(p, axis=-1, keepdims=True) for p in probs]
        outs = []
        for p, cs in enumerate(pairs):
            pv_even = _dot(probs[2 * p].astype(BF16), vw[:, cs])
            pv_odd = _dot(probs[2 * p + 1].astype(BF16), vw[:, cs])
            outs.append(jnp.where(even_half, pv_even, pv_odd)
                        / jnp.where(even_half, dens[2 * p], dens[2 * p + 1]))
        o_ref[r0:r0 + qb_rows, :] = jnp.concatenate(outs, axis=1).astype(o_ref.dtype)
        lse = jnp.zeros((qb_rows, LANES), F32)
        for h, (m, den) in enumerate(zip(maxes, dens)):
            lse = jnp.where(head_lane == h, m + jnp.log(den), lse)
        l_ref[r0:r0 + qb_rows, :] = lse


def _attention(qkv, col0):
    batch, dil, length, _ = qkv.shape
    tq = min(ATT_TILE, length)
    nq = length // tq
    hb = tq // ATT_HALF
    n_hblk = length // ATT_HALF

    def cur(col):
        return pl.BlockSpec((None, None, tq, ATT_WIDTH), lambda b, r, n: (b, r, n, col))

    def prev(col):
        return pl.BlockSpec((None, None, ATT_HALF, ATT_WIDTH),
                            lambda b, r, n: (b, r, jnp.maximum(n * hb - 1, 0), col))

    def nxt(col):
        return pl.BlockSpec((None, None, ATT_HALF, ATT_WIDTH),
                            lambda b, r, n: (b, r, jnp.minimum((n + 1) * hb, n_hblk - 1), col))

    out_spec = pl.BlockSpec((None, None, tq, ATT_WIDTH), lambda b, r, n: (b, r, n, 0))
    return pl.pallas_call(
        functools.partial(_attn_kernel, tq=tq, length=length),
        grid=(batch, dil, nq),
        in_specs=[cur(col0), cur(col0 + 1), prev(col0 + 1), nxt(col0 + 1),
                  cur(col0 + 2), prev(col0 + 2), nxt(col0 + 2)],
        out_specs=[out_spec, pl.BlockSpec((None, None, tq, LANES), lambda b, r, n: (b, r, n, 0))],
        out_shape=[jax.ShapeDtypeStruct((batch, dil, length, ATT_WIDTH), BF16),
                   jax.ShapeDtypeStruct((batch, dil, length, LANES), F32)],
        scratch_shapes=[pltpu.VMEM((tq + 2 * ATT_HALF, ATT_WIDTH), BF16)] * 2,
        compiler_params=pltpu.CompilerParams(
            dimension_semantics=("parallel", "parallel", "parallel"), vmem_limit_bytes=VMEM_LIMIT),
        name=f"attn_d{dil}",
    )(qkv, qkv, qkv, qkv, qkv, qkv, qkv)


def _token_major(src_ref, stage_ref, tmp_ref):
    dil, rows = src_ref.shape[1], src_ref.shape[2]
    nc = src_ref.shape[3] // LANES
    for c in range(nc):
        cols = slice(c * LANES, (c + 1) * LANES)
        if dil == 4:
            for r in range(dil):
                stage_ref[c, pl.ds(r, rows, stride=dil), :] = src_ref[0, r, :, cols].astype(F32)
        else:
            group = 4 * rows
            for r4 in range(4):
                for m in range(4):
                    tmp_ref[c, pl.ds(r4 * group + m, rows, stride=4), :] = (
                        src_ref[0, r4 + 4 * m, :, cols].astype(F32))
            for r4 in range(4):
                stage_ref[c, pl.ds(r4, group, stride=4), :] = tmp_ref[c, r4 * group:(r4 + 1) * group, :]
    return jnp.concatenate([stage_ref[c] for c in range(nc)], axis=1)


def _outproj_kernel(of_ref, ob_ref, hg_ref, o1_ref, o2_ref, o3_ref, l1_ref, l2_ref, l3_ref,
                    x_ref, og_ref, w_ref, g2_ref, wrh_ref, wrl_ref, br_ref,
                    x2_ref, xn_ref, gate_ref, idx_ref, st_o2, st_o3, st_l2, st_l3, st_tmp_o, st_tmp_l):
    o = of_ref[...].astype(F32) + ob_ref[...].astype(F32)
    hg = hg_ref[...].astype(F32)
    parts = []
    for h in range(HG_HEADS):
        blk = o[:, h * HG_DIM:(h + 1) * HG_DIM]
        ms = jnp.mean(blk * blk, axis=-1, keepdims=True)
        parts.append(blk * lax.rsqrt(ms + EPS) * og_ref[...])
    o_hg = jnp.concatenate(parts, axis=1) * (hg * _sigmoid(hg))

    l1 = l1_ref[...]
    l2 = _token_major(l2_ref, st_l2, st_tmp_l)
    l3 = _token_major(l3_ref, st_l3, st_tmp_l)
    mx = jnp.maximum(jnp.maximum(l1, l2), l3)
    e1, e2, e3 = jnp.exp(l1 - mx), jnp.exp(l2 - mx), jnp.exp(l3 - mx)
    den = e1 + e2 + e3
    er = lax.broadcasted_iota(jnp.int32, (LANES, ATT_WIDTH), 0)
    ec = lax.broadcasted_iota(jnp.int32, (LANES, ATT_WIDTH), 1)
    expand = jnp.where(er == ec // ATT_DIM, 1.0, 0.0).astype(BF16)

    def per_lane(w):
        return _dot(w.astype(BF16), expand)

    o_att = (per_lane(e1 / den) * o1_ref[...].astype(F32)
             + per_lane(e2 / den) * _token_major(o2_ref, st_o2, st_tmp_o)
             + per_lane(e3 / den) * _token_major(o3_ref, st_o3, st_tmp_o))

    y = _dot(o_hg.astype(BF16), w_ref[0:HG_WIDTH, :]) + _dot(o_att.astype(BF16), w_ref[HG_WIDTH:, :])
    x2 = x_ref[...] + y
    x2_ref[...] = x2

    ms = jnp.mean(x2 * x2, axis=-1, keepdims=True)
    unit = x2 * lax.rsqrt(ms + EPS)
    xn = unit * g2_ref[0:1, :]
    _rows_to_tiles(xn_ref, unit * g2_ref[1:2, :])
    xn_hi = xn.astype(BF16)
    xn_lo = (xn - xn_hi.astype(F32)).astype(BF16)
    logits = (_dot(xn_hi, wrh_ref[...]) + _dot(xn_lo, wrh_ref[...]) + _dot(xn_hi, wrl_ref[...])
              + br_ref[...])

    lane = lax.broadcasted_iota(jnp.int32, logits.shape, 1)
    lane_f = lane.astype(F32)
    work = jnp.where(lane < N_EXPERTS, logits, -jnp.inf)
    vals, idxs = [], []
    for _ in range(TOP_K):
        m = jnp.max(work, axis=-1, keepdims=True)
        idx = jnp.min(jnp.where(work == m, lane_f, float(LANES)), axis=-1, keepdims=True)
        vals.append(m)
        idxs.append(idx)
        work = jnp.where(lane_f == idx, -jnp.inf, work)
    es = [jnp.exp(v - vals[0]) for v in vals]
    den = es[0] + es[1] + es[2] + es[3]
    gate_out = jnp.zeros(logits.shape, F32)
    idx_out = jnp.zeros(logits.shape, F32)
    for k in range(TOP_K):
        gate_out = jnp.where(lane == k, es[k] / den, gate_out)
        idx_out = jnp.where(lane == k, idxs[k], idx_out)
    gate_ref[...] = gate_out
    idx_ref[...] = idx_out.astype(jnp.int32)


def _outproj(o_f, o_b, proj, atts, lses, x2d, og, w_out_bf16, g2, wr_hi, wr_lo, br, seq):
    n = x2d.shape[0]
    t = TOKEN_TILE
    nt = seq // t
    row = lambda i: (i, 0)
    const = lambda i: (0, 0)
    half = pl.BlockSpec((t, 512), row)

    def residue_major(dil, width):
        return pl.BlockSpec((1, dil, t // dil, width), lambda i: (i // nt, 0, i % nt, 0))

    rm4, rm16 = residue_major(4, ATT_WIDTH), residue_major(16, ATT_WIDTH)
    lse, lse4, lse16 = pl.BlockSpec((t, LANES), row), residue_major(4, LANES), residue_major(16, LANES)
    wide, narrow = pltpu.VMEM((ATT_WIDTH // LANES, t, LANES), F32), pltpu.VMEM((1, t, LANES), F32)
    return pl.pallas_call(
        _outproj_kernel,
        grid=(n // t,),
        in_specs=[
            half, half, pl.BlockSpec((t, 512), lambda i: (i, COL_HG)),
            half, rm4, rm16, lse, lse4, lse16,
            pl.BlockSpec((t, D_MODEL), row),
            pl.BlockSpec((1, HG_DIM), const),
            pl.BlockSpec((D_MODEL, D_MODEL), const),
            pl.BlockSpec((2, D_MODEL), const),
            pl.BlockSpec((D_MODEL, LANES), const),
            pl.BlockSpec((D_MODEL, LANES), const),
            pl.BlockSpec((1, LANES), const),
        ],
        out_specs=[pl.BlockSpec((t, D_MODEL), row), pl.BlockSpec((t * ROW_TILES, LANES), row),
                   pl.BlockSpec((t, LANES), row), pl.BlockSpec((t, LANES), row)],
        out_shape=[jax.ShapeDtypeStruct((n, D_MODEL), F32), jax.ShapeDtypeStruct((n * ROW_TILES, LANES), F32),
                   jax.ShapeDtypeStruct((n, LANES), F32), jax.ShapeDtypeStruct((n, LANES), jnp.int32)],
        scratch_shapes=[wide, wide, narrow, narrow, wide, narrow],
        compiler_params=pltpu.CompilerParams(
            dimension_semantics=("parallel",), vmem_limit_bytes=VMEM_LIMIT),
        name="outproj",
    )(o_f, o_b, proj, *atts, *lses, x2d, og, w_out_bf16, g2, wr_hi, wr_lo, br)


def _moe_kernel(bexp_ref, bfirst_ref, bslot_ref, bnext_ref, bvalid_ref, nused_ref,
                xs_ref, xinv_ref, wu_hbm, bg_ref, bl_ref, wd_hbm, bd_ref,
                y_ref, wu_buf, wd_buf, wg_s, wl_s, wd_s, up_unscale_s, down_unscale_s, sems):
    i = pl.program_id(0)

    def weight_copies(expert, slot):
        return (pltpu.make_async_copy(wu_hbm.at[expert], wu_buf.at[slot], sems.at[0, slot]),
                pltpu.make_async_copy(wd_hbm.at[expert], wd_buf.at[slot], sems.at[1, slot]))

    @pl.when(bfirst_ref[i] == 1)
    def _():
        slot = bslot_ref[i]

        @pl.when(i == 0)
        def _():
            for cp in weight_copies(bexp_ref[0], 0):
                cp.start()

        for cp in weight_copies(bexp_ref[i], slot):
            cp.wait()

        @pl.when(bnext_ref[i] >= 0)
        def _():
            for cp in weight_copies(bnext_ref[i], 1 - slot):
                cp.start()

        su = _fp8_scale(_abs_max(wu_buf[slot]))
        sd = _fp8_scale(_abs_max(wd_buf[slot]))
        up_unscale_s[...] = xinv_ref[...] / su
        down_unscale_s[...] = (1.0 / ACT_FP8_SCALE) / sd
        r = lax.broadcasted_iota(jnp.int32, (2 * LANES, 2 * LANES), 0)
        c = lax.broadcasted_iota(jnp.int32, (2 * LANES, 2 * LANES), 1)
        src = jnp.where(c < LANES, 2 * c, 2 * (c - LANES) + 1)
        perm = jnp.where(r == src, 1.0, 0.0).astype(BF16)
        rows = 256
        for rb in range(D_MODEL // rows):
            rs = slice(rb * rows, (rb + 1) * rows)
            for cb in range(D_EXPERT // LANES):
                w = wu_buf[slot, rs, cb * 2 * LANES:(cb + 1) * 2 * LANES].astype(BF16)
                split = (_dot(w, perm) * su).astype(F8)
                wg_s[rs, cb * LANES:(cb + 1) * LANES] = split[:, :LANES]
                wl_s[rs, cb * LANES:(cb + 1) * LANES] = split[:, LANES:]
        wd_s[...] = (wd_buf[slot] * sd).astype(F8)

    def expert_mlp(rows):
        x8 = jnp.concatenate(_tiles_to_rows(xs_ref, rows), axis=1).astype(F8)
        hglu = _dot(x8, wg_s[...]) * up_unscale_s[...] + bg_ref[0]
        hlin = _dot(x8, wl_s[...]) * up_unscale_s[...] + bl_ref[0]
        glu = jnp.minimum(hglu, SWIGLU_LIMIT)
        lin = jnp.clip(hlin, -SWIGLU_LIMIT, SWIGLU_LIMIT)
        act = glu * _sigmoid(SWIGLU_ALPHA * glu) * (lin + 1.0)
        y = _dot((act * ACT_FP8_SCALE).astype(F8), wd_s[...]) * down_unscale_s[...] + bd_ref[0]
        _rows_to_tiles(y_ref, y)

    valid = bvalid_ref[i]
    half = MOE_BLOCK_ROWS // 2

    @pl.when(valid > half)
    def _():
        expert_mlp(MOE_BLOCK_ROWS)

    @pl.when((valid > 0) & (valid <= half))
    def _():
        expert_mlp(half)
        y_ref[half * ROW_TILES:, :] = jnp.zeros((half * ROW_TILES, LANES), F32)

    @pl.when(valid == 0)
    def _():
        y_ref[...] = jnp.zeros_like(y_ref)


def _moe(block_exp, block_first, block_slot, block_next, block_valid, n_used,
         xs, x_unscale, w_up, b_glu, b_lin, w_down, b_down):
    p_rows = xs.shape[0] // ROW_TILES
    bm = MOE_BLOCK_ROWS
    nb = p_rows // bm
    exp3 = lambda i, be, *_: (be[i], 0, 0)
    grid_spec = pltpu.PrefetchScalarGridSpec(
        num_scalar_prefetch=6,
        grid=(nb,),
        in_specs=[
            pl.BlockSpec((bm * ROW_TILES, LANES), lambda i, be, bf, bs, bn, bv, nu: (jnp.minimum(i, nu[0] - 1), 0)),
            pl.BlockSpec((1, 1), lambda i, *_: (0, 0)),
            pl.BlockSpec(memory_space=pl.ANY),
            pl.BlockSpec((1, 1, D_EXPERT), exp3),
            pl.BlockSpec((1, 1, D_EXPERT), exp3),
            pl.BlockSpec(memory_space=pl.ANY),
            pl.BlockSpec((1, 1, D_MODEL), exp3),
        ],
        out_specs=pl.BlockSpec((bm * ROW_TILES, LANES), lambda i, *_: (i, 0)),
        scratch_shapes=[pltpu.VMEM((2, D_MODEL, 2 * D_EXPERT), F32), pltpu.VMEM((2, D_EXPERT, D_MODEL), F32),
                        pltpu.VMEM((D_MODEL, D_EXPERT), F8), pltpu.VMEM((D_MODEL, D_EXPERT), F8),
                        pltpu.VMEM((D_EXPERT, D_MODEL), F8), pltpu.VMEM((1, 1), F32), pltpu.VMEM((1, 1), F32),
                        pltpu.SemaphoreType.DMA((2, 2))],
    )
    return pl.pallas_call(
        _moe_kernel,
        grid_spec=grid_spec,
        out_shape=jax.ShapeDtypeStruct((p_rows * ROW_TILES, LANES), F32),
        compiler_params=pltpu.CompilerParams(
            dimension_semantics=("arbitrary",), vmem_limit_bytes=VMEM_LIMIT),
        name="moe",
    )(block_exp, block_first, block_slot, block_next, block_valid, n_used,
      xs, x_unscale, w_up, b_glu, b_lin, w_down, b_down)


def _dispatch_kernel(dest_ref, zstart_ref, xn_ref, xs_hbm, zero_ref, sem, zsem):
    i = pl.program_id(0)
    bm = MOE_BLOCK_ROWS
    tokens = xn_ref.shape[0] // ROW_TILES

    def zero_copy(e):
        start = pl.multiple_of(zstart_ref[e] * ROW_TILES, bm * ROW_TILES)
        return pltpu.make_async_copy(zero_ref, xs_hbm.at[pl.ds(start, bm * ROW_TILES)], zsem)

    @pl.when(i == 0)
    def _():
        zero_ref[...] = jnp.zeros_like(zero_ref)
        for e in range(N_EXPERTS):
            @pl.when(zstart_ref[e] >= 0)
            def _():
                zero_copy(e).start()
        for e in range(N_EXPERTS):
            @pl.when(zstart_ref[e] >= 0)
            def _():
                zero_copy(e).wait()

    t0 = i * tokens

    def body(j, carry):
        src = xn_ref.at[pl.ds(pl.multiple_of(j * ROW_TILES, ROW_TILES), ROW_TILES)]
        for k in range(TOP_K):
            d = pl.multiple_of(dest_ref[(t0 + j) * TOP_K + k] * ROW_TILES, ROW_TILES)
            pltpu.make_async_copy(src, xs_hbm.at[pl.ds(d, ROW_TILES)], sem).start(priority=k % 2)
        return carry

    lax.fori_loop(0, tokens, body, 0, unroll=8)
    for k in range(TOP_K):
        pltpu.make_async_copy(xn_ref, xs_hbm.at[pl.ds(0, tokens * ROW_TILES)], sem).wait()


def _dispatch(dest, zstart, xn, p_rows):
    n = xn.shape[0] // ROW_TILES
    t = DISPATCH_TOKENS
    grid_spec = pltpu.PrefetchScalarGridSpec(
        num_scalar_prefetch=2,
        grid=(n // t,),
        in_specs=[pl.BlockSpec((t * ROW_TILES, LANES), lambda i, d, z: (i, 0))],
        out_specs=pl.BlockSpec(memory_space=pl.ANY),
        scratch_shapes=[pltpu.VMEM((MOE_BLOCK_ROWS * ROW_TILES, LANES), F32),
                        pltpu.SemaphoreType.DMA, pltpu.SemaphoreType.DMA],
    )
    return pl.pallas_call(
        _dispatch_kernel,
        grid_spec=grid_spec,
        out_shape=jax.ShapeDtypeStruct((p_rows * ROW_TILES, LANES), F32),
        compiler_params=pltpu.CompilerParams(
            dimension_semantics=("arbitrary",), vmem_limit_bytes=VMEM_LIMIT),
        name="dispatch",
    )(dest, zstart, xn)


def _combine_kernel(dest_ref, y_hbm, x2_ref, gate_ref, out_ref, buf_ref, sems):
    i = pl.program_id(0)
    tc = COMBINE_TOKENS

    def issue(step, slot):
        t0 = step * tc

        def body(j, carry):
            for k in range(TOP_K):
                d = pl.multiple_of(dest_ref[(t0 + j) * TOP_K + k] * ROW_TILES, ROW_TILES)
                r = pl.multiple_of((k * tc + j) * ROW_TILES, ROW_TILES)
                pltpu.make_async_copy(y_hbm.at[pl.ds(d, ROW_TILES)], buf_ref.at[slot, pl.ds(r, ROW_TILES)],
                                      sems.at[slot]).start(priority=k % 2)
            return carry

        lax.fori_loop(0, tc, body, 0, unroll=8)

    @pl.when(i == 0)
    def _():
        issue(0, 0)

    @pl.when(i + 1 < pl.num_programs(0))
    def _():
        issue(i + 1, (i + 1) % 2)

    slot = i % 2
    pltpu.make_async_copy(y_hbm.at[pl.ds(0, TOP_K * tc * ROW_TILES)], buf_ref.at[slot], sems.at[slot]).wait()
    rows = buf_ref.at[slot]
    sub = 64
    for r0 in range(0, tc, sub):
        gate = gate_ref[r0:r0 + sub, :]
        gates = [jnp.broadcast_to(gate[:, k:k + 1], (sub, LANES)) for k in range(TOP_K)]
        for c in range(ROW_TILES):
            acc = x2_ref[r0:r0 + sub, c * LANES:(c + 1) * LANES]
            for k in range(TOP_K):
                acc = acc + gates[k] * rows[pl.ds((k * tc + r0) * ROW_TILES + c, sub, stride=ROW_TILES), :]
            out_ref[r0:r0 + sub, c * LANES:(c + 1) * LANES] = acc


def _combine(dest, y, x2, gates):
    n = x2.shape[0]
    tc = COMBINE_TOKENS
    grid_spec = pltpu.PrefetchScalarGridSpec(
        num_scalar_prefetch=1,
        grid=(n // tc,),
        in_specs=[pl.BlockSpec(memory_space=pl.ANY),
                  pl.BlockSpec((tc, D_MODEL), lambda i, d: (i, 0)),
                  pl.BlockSpec((tc, LANES), lambda i, d: (i, 0))],
        out_specs=pl.BlockSpec((tc, D_MODEL), lambda i, d: (i, 0)),
        scratch_shapes=[pltpu.VMEM((2, TOP_K * tc * ROW_TILES, LANES), F32), pltpu.SemaphoreType.DMA((2,))],
    )
    return pl.pallas_call(
        _combine_kernel,
        grid_spec=grid_spec,
        out_shape=jax.ShapeDtypeStruct((n, D_MODEL), F32),
        compiler_params=pltpu.CompilerParams(
            dimension_semantics=("arbitrary",), vmem_limit_bytes=VMEM_LIMIT),
        name="combine",
    )(dest, y, x2, gates)


def _route(top_idx):
    n = top_idx.shape[0]
    a = n * TOP_K
    bm = MOE_BLOCK_ROWS
    nb = a // bm + N_EXPERTS
    e_flat = top_idx.reshape(a)
    onehot = (e_flat[:, None] == jnp.arange(N_EXPERTS, dtype=jnp.int32)[None, :]).astype(jnp.int32)
    csum = jnp.cumsum(onehot, axis=0)
    counts = csum[-1]
    padded = ((counts + bm - 1) // bm) * bm
    pad_end = jnp.cumsum(padded)
    pad_start = pad_end - padded
    dest = jnp.sum(onehot * (csum - 1 + pad_start[None, :]), axis=1)
    n_used = (pad_end[-1] // bm).astype(jnp.int32)
    blk = jnp.arange(nb, dtype=jnp.int32)
    bexp = jnp.sum((pad_end[None, :] <= (blk * bm)[:, None]).astype(jnp.int32), axis=1)
    bexp = jnp.minimum(bexp, N_EXPERTS - 1)
    bexp = jnp.where(blk < n_used, bexp, bexp[jnp.maximum(n_used - 1, 0)])
    bfirst = jnp.concatenate([jnp.ones((1,), jnp.int32), (bexp[1:] != bexp[:-1]).astype(jnp.int32)])
    bslot = (jnp.cumsum(bfirst) - 1) % 2
    later = jnp.where(bexp[None, :] > bexp[:, None], bexp[None, :], N_EXPERTS)
    bnext = jnp.min(later, axis=1)
    bnext = jnp.where(bnext < N_EXPERTS, bnext, -1).astype(jnp.int32)
    valid_end = jnp.sum(jnp.where(bexp[:, None] == jnp.arange(N_EXPERTS)[None, :], (pad_start + counts)[None, :], 0), axis=1)
    bvalid = jnp.clip(valid_end - blk * bm, 0, bm).astype(jnp.int32)
    zstart = jnp.where(counts > 0, pad_end - bm, -1).astype(jnp.int32)
    return dest.astype(jnp.int32), zstart, (bexp, bfirst, bslot.astype(jnp.int32), bnext, bvalid, n_used.reshape(1))


def kernel(x, positions, norm1_g, w_in, q_norm_g, k_norm_g, hgrn_lower_bounds, hgrn_onorm_g,
           w_out, norm2_g, w_router, b_router, w_up, b_up, w_down, b_down):
    batch, seq, d = x.shape
    n = batch * seq
    depth = norm1_g.shape[0]
    lbs_all = jnp.cumsum(jax.nn.softmax(hgrn_lower_bounds.astype(F32), axis=0), axis=0)
    half = ATT_DIM // 2
    inv = 1.0 / (ROPE_THETA ** (jnp.arange(half, dtype=F32) / half))
    inv_tab = jnp.tile(inv, LANES // half).reshape(1, LANES)
    pos_col = positions.reshape(n, 1)

    x2d = x.reshape(n, d)
    for l in range(depth):
        lbs = lbs_all[l].reshape(2 * HG_HEADS, 1, HG_DIM)
        proj, qkv4, qkv16 = _inproj(
            x2d, pos_col, inv_tab, norm1_g[l].reshape(1, d), w_in[l].astype(BF16),
            jnp.tile(q_norm_g[l], LANES // ATT_DIM).reshape(1, LANES),
            jnp.tile(k_norm_g[l], LANES // ATT_DIM).reshape(1, LANES), batch, seq)
        o_f, o_b = _hgrn(proj, lbs, batch, seq)
        o1, l1 = _attention(proj.reshape(batch, 1, seq, IN_COLS), COL_AQ)
        o4, l4 = _attention(qkv4, 0)
        o16, l16 = _attention(qkv16, 0)
        atts = [o1.reshape(n, ATT_WIDTH), o4, o16]
        lses = [l1.reshape(n, LANES), l4, l16]

        wr = jnp.pad(w_router[l], ((0, 0), (0, LANES - N_EXPERTS)))
        wr_hi = wr.astype(BF16)
        wr_lo = (wr - wr_hi.astype(F32)).astype(BF16)
        br = jnp.pad(b_router[l], (0, LANES - N_EXPERTS)).reshape(1, LANES)
        g2 = norm2_g[l].astype(F32)
        g2_max = jnp.max(jnp.abs(g2))
        x_scale = jnp.where(g2_max > 0.0, ROW_FP8_TARGET / g2_max, 1.0)
        x2, xn, gates, top_idx = _outproj(
            o_f, o_b, proj, atts, lses, x2d, hgrn_onorm_g[l].reshape(1, HG_DIM),
            w_out[l].astype(BF16), jnp.stack([g2, g2 * x_scale]), wr_hi, wr_lo, br, seq)
        dest, zstart, blocks = _route(top_idx[:, :TOP_K])
        xs = _dispatch(dest, zstart, xn, blocks[0].shape[0] * MOE_BLOCK_ROWS)
        y = _moe(*blocks, xs, (1.0 / x_scale).reshape(1, 1), w_up[l],
                 b_up[l][:, 0::2].reshape(N_EXPERTS, 1, D_EXPERT),
                 b_up[l][:, 1::2].reshape(N_EXPERTS, 1, D_EXPERT),
                 w_down[l], b_down[l].reshape(N_EXPERTS, 1, D_MODEL))
        x2d = _combine(dest, y, x2, gates)
    return x2d.reshape(batch, seq, d)
```

```python
import functools

import jax
import jax.numpy as jnp
from jax import lax
from jax.experimental import pallas as pl
from jax.experimental.pallas import tpu as pltpu

F32 = jnp.float32
BF16 = jnp.bfloat16
F8 = jnp.float8_e4m3fn
FP8_TARGET = 240.0
ROW_FP8_TARGET = 8.0
ACT_FP8_SCALE = 4.0

D_MODEL = 1024
HG_HEADS = 4
HG_DIM = 128
HG_WIDTH = HG_HEADS * HG_DIM
HG_CHUNK = 64
ATT_HEADS = 8
ATT_DIM = 64
ATT_WIDTH = ATT_HEADS * ATT_DIM
DILATED_PATTERNS = ((128, 1), (512, 4), (2048, 16))
ATT_HALF = 64
assert all(window // (2 * dil) == ATT_HALF for window, dil in DILATED_PATTERNS)
assert tuple(dil for _, dil in DILATED_PATTERNS) == (1, 4, 16)
ATT_QBLOCK = 128
ROPE_THETA = 10000.0
IN_COLS = 5 * HG_WIDTH + 3 * ATT_WIDTH
N_EXPERTS = 32
TOP_K = 4
D_EXPERT = D_MODEL
SWIGLU_LIMIT = 7.0
SWIGLU_ALPHA = 1.702
EPS = 1e-6
NEG = -1e30

COL_HQ, COL_HF_FWD, COL_HF_BWD, COL_HI, COL_HG, COL_AQ, COL_AK, COL_AV = range(8)

TOKEN_TILE = 1024
INPROJ_TILE = 1024
HGRN_TILE = 512
ATT_TILE = 1024
MOE_BLOCK_ROWS = 512
DISPATCH_TOKENS = 2048
COMBINE_TOKENS = 256
LANES = 128
ROW_TILES = D_MODEL // LANES
VMEM_LIMIT = 60 * 1024 * 1024


def _dot(a, b):
    return jnp.dot(a, b, preferred_element_type=F32)


def _dot_nt(a, b):
    return lax.dot_general(a, b, (((1,), (1,)), ((), ())), preferred_element_type=F32)


def _dot_tn(a, b):
    return lax.dot_general(a, b, (((0,), (0,)), ((), ())), preferred_element_type=F32)


def _sigmoid(x):
    return 0.5 * jnp.tanh(0.5 * x) + 0.5


def _abs_max(x):
    return jnp.max(jnp.max(jnp.abs(x), axis=0, keepdims=True), axis=1, keepdims=True)


def _fp8_scale(amax):
    return jnp.where(amax > 0.0, FP8_TARGET / amax, 1.0)


def _rows_to_tiles(dst_ref, x):
    for c in range(ROW_TILES):
        dst_ref[pl.ds(c, x.shape[0], stride=ROW_TILES), :] = x[:, c * LANES:(c + 1) * LANES]


def _tiles_to_rows(src, rows, first_row=0):
    return [src[pl.ds(first_row * ROW_TILES + c, rows, stride=ROW_TILES), :] for c in range(ROW_TILES)]


def _head_norm_rope(p, gain, cos, sin_signed, scale):
    lane = lax.broadcasted_iota(jnp.int32, (p.shape[0], LANES), 1)
    low = lane < ATT_DIM
    first_half = (lane % ATT_DIM) < (ATT_DIM // 2)
    outs = []
    for t in range(ATT_WIDTH // LANES):
        blk = p[:, t * LANES:(t + 1) * LANES]
        sq = blk * blk
        s_low = jnp.sum(jnp.where(low, sq, 0.0), axis=-1, keepdims=True)
        s_high = jnp.sum(jnp.where(low, 0.0, sq), axis=-1, keepdims=True)
        r = jnp.where(low, lax.rsqrt(s_low * (1.0 / ATT_DIM) + EPS),
                      lax.rsqrt(s_high * (1.0 / ATT_DIM) + EPS))
        y = blk * r * gain
        partner = jnp.where(first_half, pltpu.roll(y, LANES - ATT_DIM // 2, axis=1),
                            pltpu.roll(y, ATT_DIM // 2, axis=1))
        outs.append((y * cos + partner * sin_signed) * scale)
    return jnp.concatenate(outs, axis=1)


def _inproj_kernel(x_ref, pos_ref, inv_ref, g1_ref, w_ref, qg_ref, kg_ref,
                   out_ref, d4_ref, d16_ref, stage_ref, stage2_ref):
    x = x_ref[...]
    ms = jnp.mean(x * x, axis=-1, keepdims=True)
    h = (x * lax.rsqrt(ms + EPS) * g1_ref[...]).astype(BF16)
    ang = pos_ref[...].astype(F32) * inv_ref[...]
    lane = lax.broadcasted_iota(jnp.int32, ang.shape, 1)
    cos = jnp.cos(ang)
    sin_signed = jnp.where((lane % ATT_DIM) < (ATT_DIM // 2), -jnp.sin(ang), jnp.sin(ang))
    order = (COL_AQ, COL_AK, COL_AV, COL_HQ, COL_HF_FWD, COL_HF_BWD, COL_HI, COL_HG)
    nxt = _dot(h, w_ref[:, order[0] * 512:(order[0] + 1) * 512])
    for pos, j in enumerate(order):
        p = nxt
        if pos + 1 < len(order):
            jn = order[pos + 1]
            nxt = _dot(h, w_ref[:, jn * 512:(jn + 1) * 512])
        if j == COL_AQ:
            p = _head_norm_rope(p, qg_ref[...], cos, sin_signed, ATT_DIM ** -0.5)
        elif j == COL_AK:
            p = _head_norm_rope(p, kg_ref[...], cos, sin_signed, 1.0)
        out_ref[:, j * 512:(j + 1) * 512] = p.astype(BF16)
        if j >= COL_AQ:
            rows4, rows16 = x.shape[0] // 4, x.shape[0] // 16
            for c in range(ATT_WIDTH // LANES):
                cols = slice((j - COL_AQ) * ATT_WIDTH + c * LANES, (j - COL_AQ) * ATT_WIDTH + (c + 1) * LANES)
                stage_ref[c] = p[:, c * LANES:(c + 1) * LANES]
                for r4 in range(4):
                    group = stage_ref[c, pl.ds(r4, rows4, stride=4), :]
                    d4_ref[0, r4, :, cols] = group.astype(BF16)
                    stage2_ref[c, r4 * rows4:(r4 + 1) * rows4, :] = group
                for r4 in range(4):
                    for m in range(4):
                        d16_ref[0, r4 + 4 * m, :, cols] = (
                            stage2_ref[c, pl.ds(r4 * rows4 + m, rows16, stride=4), :].astype(BF16))


def _inproj(x2d, pos_col, inv_tab, g1, w_in_bf16, qg, kg, batch, seq):
    n = x2d.shape[0]
    t = INPROJ_TILE
    nt = seq // t
    const = lambda i: (0, 0)
    qkv = 3 * ATT_WIDTH

    def residue_major(dil):
        spec = pl.BlockSpec((1, dil, t // dil, qkv), lambda i: (i // nt, 0, i % nt, 0))
        return spec, jax.ShapeDtypeStruct((batch, dil, seq // dil, qkv), BF16)

    spec4, shape4 = residue_major(4)
    spec16, shape16 = residue_major(16)
    return pl.pallas_call(
        _inproj_kernel,
        grid=(n // t,),
        in_specs=[
            pl.BlockSpec((t, D_MODEL), lambda i: (i, 0)),
            pl.BlockSpec((t, 1), lambda i: (i, 0)),
            pl.BlockSpec((1, LANES), const),
            pl.BlockSpec((1, D_MODEL), const),
            pl.BlockSpec((D_MODEL, IN_COLS), const, pipeline_mode=pl.Buffered(1)),
            pl.BlockSpec((1, LANES), const),
            pl.BlockSpec((1, LANES), const),
        ],
        out_specs=[pl.BlockSpec((t, IN_COLS), lambda i: (i, 0)), spec4, spec16],
        out_shape=[jax.ShapeDtypeStruct((n, IN_COLS), BF16), shape4, shape16],
        scratch_shapes=[pltpu.VMEM((ATT_WIDTH // LANES, t, LANES), F32)] * 2,
        compiler_params=pltpu.CompilerParams(
            dimension_semantics=("parallel",), vmem_limit_bytes=VMEM_LIMIT),
        name="inproj",
    )(x2d, pos_col, inv_tab, g1, w_in_bf16, qg, kg)


def _hgrn_direction(q, z, v, lb, state_t, reverse):
    c = HG_CHUNK
    t = q.shape[0]
    n = t // c
    row = lax.broadcasted_iota(jnp.int32, (c, c), 0)
    col = lax.broadcasted_iota(jnp.int32, (c, c), 1)
    mask = (row <= col) if reverse else (row >= col)
    tri = jnp.where(mask, 1.0, 0.0).astype(BF16)
    last_row = 0 if reverse else c - 1

    z = z.astype(F32)
    q = q.astype(F32)
    sg = _sigmoid(z)
    f = lb + (1.0 - lb) * sg
    k = (1.0 - lb) * (1.0 - sg)
    lf = jnp.log(f)
    lf_hi = lf.astype(BF16)
    lf_lo = (lf - lf_hi.astype(F32)).astype(BF16)
    chunks = [slice(j * c, (j + 1) * c) for j in range(n)]
    b = jnp.concatenate([_dot(tri, lf_hi[rs]) + _dot(tri, lf_lo[rs]) for rs in chunks], axis=0)
    b_last = b.reshape(n, c, HG_DIM)[:, last_row:last_row + 1, :]
    decay = jnp.exp(b_last)
    qt = (q * _sigmoid(q) * jnp.exp(b)).astype(BF16)
    kt_f32 = k * jnp.exp(-b)
    kt = kt_f32.astype(BF16)
    kd = (kt_f32.reshape(n, c, HG_DIM) * decay).reshape(t, HG_DIM).astype(BF16)

    outs, updates = [], []
    for rs in chunks:
        a = jnp.where(mask, _dot_nt(qt[rs], kt[rs]), 0.0)
        outs.append(_dot(a.astype(BF16), v[rs]))
        updates.append(_dot_tn(v[rs], kd[rs]))
    for j in (reversed(range(n)) if reverse else range(n)):
        outs[j] = outs[j] + _dot_nt(qt[chunks[j]], state_t.astype(BF16))
        state_t = state_t * decay[j] + updates[j]
    return jnp.concatenate(outs, axis=0), state_t


def _hgrn_kernel(qf_ref, zf_ref, vf_ref, qb_ref, zb_ref, vb_ref, lb_ref,
                 of_ref, ob_ref, sf_ref, sb_ref):
    @pl.when(pl.program_id(1) == 0)
    def _():
        sf_ref[...] = jnp.zeros_like(sf_ref)
        sb_ref[...] = jnp.zeros_like(sb_ref)

    for h in range(HG_HEADS):
        cols = slice(h * HG_DIM, (h + 1) * HG_DIM)
        o, sf = _hgrn_direction(qf_ref[:, cols], zf_ref[:, cols], vf_ref[:, cols],
                                lb_ref[h], sf_ref[h], False)
        of_ref[:, cols] = o.astype(of_ref.dtype)
        sf_ref[h] = sf
        o, sb = _hgrn_direction(qb_ref[:, cols], zb_ref[:, cols], vb_ref[:, cols],
                                lb_ref[HG_HEADS + h], sb_ref[h], True)
        ob_ref[:, cols] = o.astype(ob_ref.dtype)
        sb_ref[h] = sb


def _hgrn(proj, lbs, batch, seq):
    n = proj.shape[0]
    t = HGRN_TILE
    nblk = seq // t

    def fwd(colblk):
        return pl.BlockSpec((t, HG_WIDTH), lambda b, i: (b * nblk + i, colblk))

    def bwd(colblk):
        return pl.BlockSpec((t, HG_WIDTH), lambda b, i: (b * nblk + nblk - 1 - i, colblk))

    return pl.pallas_call(
        _hgrn_kernel,
        grid=(batch, nblk),
        in_specs=[
            fwd(COL_HQ), fwd(COL_HF_FWD), fwd(COL_HI),
            bwd(COL_HQ), bwd(COL_HF_BWD), bwd(COL_HI),
            pl.BlockSpec((2 * HG_HEADS, 1, HG_DIM), lambda b, i: (0, 0, 0)),
        ],
        out_specs=[fwd(0), bwd(0)],
        out_shape=[jax.ShapeDtypeStruct((n, HG_WIDTH), BF16)] * 2,
        scratch_shapes=[pltpu.VMEM((HG_HEADS, HG_DIM, HG_DIM), F32)] * 2,
        compiler_params=pltpu.CompilerParams(
            dimension_semantics=("parallel", "arbitrary"), vmem_limit_bytes=VMEM_LIMIT),
        name="hgrn",
    )(proj, proj, proj, proj, proj, proj, lbs)


def _attn_kernel(q_ref, kc_ref, kp_ref, kn_ref, vc_ref, vp_ref, vn_ref, o_ref, l_ref,
                 kw_ref, vw_ref, *, tq, length):
    n = pl.program_id(2)
    half = ATT_HALF
    kw_ref[0:half, :] = kp_ref[...]
    kw_ref[half:half + tq, :] = kc_ref[...]
    kw_ref[half + tq:, :] = kn_ref[...]
    vw_ref[0:half, :] = vp_ref[...]
    vw_ref[half:half + tq, :] = vc_ref[...]
    vw_ref[half + tq:, :] = vn_ref[...]

    qb_rows = ATT_QBLOCK
    win = qb_rows + 2 * half
    i_idx = lax.broadcasted_iota(jnp.int32, (qb_rows, win), 0)
    j_idx = lax.broadcasted_iota(jnp.int32, (qb_rows, win), 1)
    band = (j_idx >= i_idx) & (j_idx <= i_idx + 2 * half)

    pairs = [slice(p * LANES, (p + 1) * LANES) for p in range(ATT_HEADS // 2)]
    head_lane = lax.broadcasted_iota(jnp.int32, (qb_rows, LANES), 1)
    even_half = head_lane < ATT_DIM
    keep_even = jnp.where(lax.broadcasted_iota(jnp.int32, (1, LANES), 1) < ATT_DIM, 1.0, 0.0).astype(BF16)
    keep_odd = (1.0 - keep_even.astype(F32)).astype(BF16)

    def masked_scores(r0):
        base = n * tq + r0 - half
        valid = band & (j_idx >= -base) & (j_idx < length - base)
        q = q_ref[r0:r0 + qb_rows, :]
        kw = kw_ref[r0:r0 + win, :]
        out = []
        for cs in pairs:
            for keep in (keep_even, keep_odd):
                out.append(jnp.where(valid, _dot_nt(q[:, cs] * keep, kw[:, cs]), NEG))
        return out

    blocks = list(range(0, tq, qb_rows))
    nxt = masked_scores(blocks[0])
    for pos, r0 in enumerate(blocks):
        scores = nxt
        if pos + 1 < len(blocks):
            nxt = masked_scores(blocks[pos + 1])
        vw = vw_ref[r0:r0 + win, :]
        maxes = [jnp.max(s, axis=-1, keepdims=True) for s in scores]
        probs = [jnp.exp(s - m) for s, m in zip(scores, maxes)]
        dens = [jnp.sum(p, axis=-1, keepdims=True) for p in probs]
        outs = []
        for p, cs in enumerate(pairs):
            pv_even = _dot(probs[2 * p].astype(BF16), vw[:, cs])
            pv_odd = _dot(probs[2 * p + 1].astype(BF16), vw[:, cs])
            outs.append(jnp.where(even_half, pv_even, pv_odd)
                        / jnp.where(even_half, dens[2 * p], dens[2 * p + 1]))
        o_ref[r0:r0 + qb_rows, :] = jnp.concatenate(outs, axis=1).astype(o_ref.dtype)
        lse = jnp.zeros((qb_rows, LANES), F32)
        for h, (m, den) in enumerate(zip(maxes, dens)):
            lse = jnp.where(head_lane == h, m + jnp.log(den), lse)
        l_ref[r0:r0 + qb_rows, :] = lse


def _attention(qkv, col0):
    batch, dil, length, _ = qkv.shape
    tq = min(ATT_TILE, length)
    nq = length // tq
    hb = tq // ATT_HALF
    n_hblk = length // ATT_HALF

    def cur(col):
        return pl.BlockSpec((None, None, tq, ATT_WIDTH), lambda b, r, n: (b, r, n, col))

    def prev(col):
        return pl.BlockSpec((None, None, ATT_HALF, ATT_WIDTH),
                            lambda b, r, n: (b, r, jnp.maximum(n * hb - 1, 0), col))

    def nxt(col):
        return pl.BlockSpec((None, None, ATT_HALF, ATT_WIDTH),
                            lambda b, r, n: (b, r, jnp.minimum((n + 1) * hb, n_hblk - 1), col))

    out_spec = pl.BlockSpec((None, None, tq, ATT_WIDTH), lambda b, r, n: (b, r, n, 0))
    return pl.pallas_call(
        functools.partial(_attn_kernel, tq=tq, length=length),
        grid=(batch, dil, nq),
        in_specs=[cur(col0), cur(col0 + 1), prev(col0 + 1), nxt(col0 + 1),
                  cur(col0 + 2), prev(col0 + 2), nxt(col0 + 2)],
        out_specs=[out_spec, pl.BlockSpec((None, None, tq, LANES), lambda b, r, n: (b, r, n, 0))],
        out_shape=[jax.ShapeDtypeStruct((batch, dil, length, ATT_WIDTH), BF16),
                   jax.ShapeDtypeStruct((batch, dil, length, LANES), F32)],
        scratch_shapes=[pltpu.VMEM((tq + 2 * ATT_HALF, ATT_WIDTH), BF16)] * 2,
        compiler_params=pltpu.CompilerParams(
            dimension_semantics=("parallel", "parallel", "parallel"), vmem_limit_bytes=VMEM_LIMIT),
        name=f"attn_d{dil}",
    )(qkv, qkv, qkv, qkv, qkv, qkv, qkv)


def _token_major(src_ref, stage_ref, tmp_ref):
    dil, rows = src_ref.shape[1], src_ref.shape[2]
    nc = src_ref.shape[3] // LANES
    for c in range(nc):
        cols = slice(c * LANES, (c + 1) * LANES)
        if dil == 4:
            for r in range(dil):
                stage_ref[c, pl.ds(r, rows, stride=dil), :] = src_ref[0, r, :, cols].astype(F32)
        else:
            group = 4 * rows
            for r4 in range(4):
                for m in range(4):
                    tmp_ref[c, pl.ds(r4 * group + m, rows, stride=4), :] = (
                        src_ref[0, r4 + 4 * m, :, cols].astype(F32))
            for r4 in range(4):
                stage_ref[c, pl.ds(r4, group, stride=4), :] = tmp_ref[c, r4 * group:(r4 + 1) * group, :]
    return jnp.concatenate([stage_ref[c] for c in range(nc)], axis=1)


def _outproj_kernel(of_ref, ob_ref, hg_ref, o1_ref, o2_ref, o3_ref, l1_ref, l2_ref, l3_ref,
                    x_ref, og_ref, w_ref, g2_ref, wrh_ref, wrl_ref, br_ref,
                    x2_ref, xn_ref, gate_ref, idx_ref, st_o2, st_o3, st_l2, st_l3, st_tmp_o, st_tmp_l):
    o = of_ref[...].astype(F32) + ob_ref[...].astype(F32)
    hg = hg_ref[...].astype(F32)
    parts = []
    for h in range(HG_HEADS):
        blk = o[:, h * HG_DIM:(h + 1) * HG_DIM]
        ms = jnp.mean(blk * blk, axis=-1, keepdims=True)
        parts.append(blk * lax.rsqrt(ms + EPS) * og_ref[...])
    o_hg = jnp.concatenate(parts, axis=1) * (hg * _sigmoid(hg))

    l1 = l1_ref[...]
    l2 = _token_major(l2_ref, st_l2, st_tmp_l)
    l3 = _token_major(l3_ref, st_l3, st_tmp_l)
    mx = jnp.maximum(jnp.maximum(l1, l2), l3)
    e1, e2, e3 = jnp.exp(l1 - mx), jnp.exp(l2 - mx), jnp.exp(l3 - mx)
    den = e1 + e2 + e3
    er = lax.broadcasted_iota(jnp.int32, (LANES, ATT_WIDTH), 0)
    ec = lax.broadcasted_iota(jnp.int32, (LANES, ATT_WIDTH), 1)
    expand = jnp.where(er == ec // ATT_DIM, 1.0, 0.0).astype(BF16)

    def per_lane(w):
        return _dot(w.astype(BF16), expand)

    o_att = (per_lane(e1 / den) * o1_ref[...].astype(F32)
             + per_lane(e2 / den) * _token_major(o2_ref, st_o2, st_tmp_o)
             + per_lane(e3 / den) * _token_major(o3_ref, st_o3, st_tmp_o))

    y = _dot(o_hg.astype(BF16), w_ref[0:HG_WIDTH, :]) + _dot(o_att.astype(BF16), w_ref[HG_WIDTH:, :])
    x2 = x_ref[...] + y
    x2_ref[...] = x2

    ms = jnp.mean(x2 * x2, axis=-1, keepdims=True)
    unit = x2 * lax.rsqrt(ms + EPS)
    xn = unit * g2_ref[0:1, :]
    _rows_to_tiles(xn_ref, unit * g2_ref[1:2, :])
    xn_hi = xn.astype(BF16)
    xn_lo = (xn - xn_hi.astype(F32)).astype(BF16)
    logits = (_dot(xn_hi, wrh_ref[...]) + _dot(xn_lo, wrh_ref[...]) + _dot(xn_hi, wrl_ref[...])
              + br_ref[...])

    lane = lax.broadcasted_iota(jnp.int32, logits.shape, 1)
    lane_f = lane.astype(F32)
    work = jnp.where(lane < N_EXPERTS, logits, -jnp.inf)
    vals, idxs = [], []
    for _ in range(TOP_K):
        m = jnp.max(work, axis=-1, keepdims=True)
        idx = jnp.min(jnp.where(work == m, lane_f, float(LANES)), axis=-1, keepdims=True)
        vals.append(m)
        idxs.append(idx)
        work = jnp.where(lane_f == idx, -jnp.inf, work)
    es = [jnp.exp(v - vals[0]) for v in vals]
    den = es[0] + es[1] + es[2] + es[3]
    gate_out = jnp.zeros(logits.shape, F32)
    idx_out = jnp.zeros(logits.shape, F32)
    for k in range(TOP_K):
        gate_out = jnp.where(lane == k, es[k] / den, gate_out)
        idx_out = jnp.where(lane == k, idxs[k], idx_out)
    gate_ref[...] = gate_out
    idx_ref[...] = idx_out.astype(jnp.int32)


def _outproj(o_f, o_b, proj, atts, lses, x2d, og, w_out_bf16, g2, wr_hi, wr_lo, br, seq):
    n = x2d.shape[0]
    t = TOKEN_TILE
    nt = seq // t
    row = lambda i: (i, 0)
    const = lambda i: (0, 0)
    half = pl.BlockSpec((t, 512), row)

    def residue_major(dil, width):
        return pl.BlockSpec((1, dil, t // dil, width), lambda i: (i // nt, 0, i % nt, 0))

    rm4, rm16 = residue_major(4, ATT_WIDTH), residue_major(16, ATT_WIDTH)
    lse, lse4, lse16 = pl.BlockSpec((t, LANES), row), residue_major(4, LANES), residue_major(16, LANES)
    wide, narrow = pltpu.VMEM((ATT_WIDTH // LANES, t, LANES), F32), pltpu.VMEM((1, t, LANES), F32)
    return pl.pallas_call(
        _outproj_kernel,
        grid=(n // t,),
        in_specs=[
            half, half, pl.BlockSpec((t, 512), lambda i: (i, COL_HG)),
            half, rm4, rm16, lse, lse4, lse16,
            pl.BlockSpec((t, D_MODEL), row),
            pl.BlockSpec((1, HG_DIM), const),
            pl.BlockSpec((D_MODEL, D_MODEL), const),
            pl.BlockSpec((2, D_MODEL), const),
            pl.BlockSpec((D_MODEL, LANES), const),
            pl.BlockSpec((D_MODEL, LANES), const),
            pl.BlockSpec((1, LANES), const),
        ],
        out_specs=[pl.BlockSpec((t, D_MODEL), row), pl.BlockSpec((t * ROW_TILES, LANES), row),
                   pl.BlockSpec((t, LANES), row), pl.BlockSpec((t, LANES), row)],
        out_shape=[jax.ShapeDtypeStruct((n, D_MODEL), F32), jax.ShapeDtypeStruct((n * ROW_TILES, LANES), F32),
                   jax.ShapeDtypeStruct((n, LANES), F32), jax.ShapeDtypeStruct((n, LANES), jnp.int32)],
        scratch_shapes=[wide, wide, narrow, narrow, wide, narrow],
        compiler_params=pltpu.CompilerParams(
            dimension_semantics=("parallel",), vmem_limit_bytes=VMEM_LIMIT),
        name="outproj",
    )(o_f, o_b, proj, *atts, *lses, x2d, og, w_out_bf16, g2, wr_hi, wr_lo, br)


def _moe_kernel(bexp_ref, bfirst_ref, bslot_ref, bnext_ref, bvalid_ref, nused_ref,
                xs_ref, xinv_ref, wu_hbm, bg_ref, bl_ref, wd_hbm, bd_ref,
                y_ref, wu_buf, wd_buf, wg_s, wl_s, wd_s, up_unscale_s, down_unscale_s, sems):
    i = pl.program_id(0)

    def weight_copies(expert, slot):
        return (pltpu.make_async_copy(wu_hbm.at[expert], wu_buf.at[slot], sems.at[0, slot]),
                pltpu.make_async_copy(wd_hbm.at[expert], wd_buf.at[slot], sems.at[1, slot]))

    @pl.when(bfirst_ref[i] == 1)
    def _():
        slot = bslot_ref[i]

        @pl.when(i == 0)
        def _():
            for cp in weight_copies(bexp_ref[0], 0):
                cp.start()

        for cp in weight_copies(bexp_ref[i], slot):
            cp.wait()

        @pl.when(bnext_ref[i] >= 0)
        def _():
            for cp in weight_copies(bnext_ref[i], 1 - slot):
                cp.start()

        su = _fp8_scale(_abs_max(wu_buf[slot]))
        sd = _fp8_scale(_abs_max(wd_buf[slot]))
        up_unscale_s[...] = xinv_ref[...] / su
        down_unscale_s[...] = (1.0 / ACT_FP8_SCALE) / sd
        r = lax.broadcasted_iota(jnp.int32, (2 * LANES, 2 * LANES), 0)
        c = lax.broadcasted_iota(jnp.int32, (2 * LANES, 2 * LANES), 1)
        src = jnp.where(c < LANES, 2 * c, 2 * (c - LANES) + 1)
        perm = jnp.where(r == src, 1.0, 0.0).astype(BF16)
        rows = 256
        for rb in range(D_MODEL // rows):
            rs = slice(rb * rows, (rb + 1) * rows)
            for cb in range(D_EXPERT // LANES):
                w = wu_buf[slot, rs, cb * 2 * LANES:(cb + 1) * 2 * LANES].astype(BF16)
                split = (_dot(w, perm) * su).astype(F8)
                wg_s[rs, cb * LANES:(cb + 1) * LANES] = split[:, :LANES]
                wl_s[rs, cb * LANES:(cb + 1) * LANES] = split[:, LANES:]
        wd_s[...] = (wd_buf[slot] * sd).astype(F8)

    def expert_mlp(rows):
        x8 = jnp.concatenate(_tiles_to_rows(xs_ref, rows), axis=1).astype(F8)
        hglu = _dot(x8, wg_s[...]) * up_unscale_s[...] + bg_ref[0]
        hlin = _dot(x8, wl_s[...]) * up_unscale_s[...] + bl_ref[0]
        glu = jnp.minimum(hglu, SWIGLU_LIMIT)
        lin = jnp.clip(hlin, -SWIGLU_LIMIT, SWIGLU_LIMIT)
        act = glu * _sigmoid(SWIGLU_ALPHA * glu) * (lin + 1.0)
        y = _dot((act * ACT_FP8_SCALE).astype(F8), wd_s[...]) * down_unscale_s[...] + bd_ref[0]
        _rows_to_tiles(y_ref, y)

    valid = bvalid_ref[i]
    half = MOE_BLOCK_ROWS // 2

    @pl.when(valid > half)
    def _():
        expert_mlp(MOE_BLOCK_ROWS)

    @pl.when((valid > 0) & (valid <= half))
    def _():
        expert_mlp(half)
        y_ref[half * ROW_TILES:, :] = jnp.zeros((half * ROW_TILES, LANES), F32)

    @pl.when(valid == 0)
    def _():
        y_ref[...] = jnp.zeros_like(y_ref)


def _moe(block_exp, block_first, block_slot, block_next, block_valid, n_used,
         xs, x_unscale, w_up, b_glu, b_lin, w_down, b_down):
    p_rows = xs.shape[0] // ROW_TILES
    bm = MOE_BLOCK_ROWS
    nb = p_rows // bm
    exp3 = lambda i, be, *_: (be[i], 0, 0)
    grid_spec = pltpu.PrefetchScalarGridSpec(
        num_scalar_prefetch=6,
        grid=(nb,),
        in_specs=[
            pl.BlockSpec((bm * ROW_TILES, LANES), lambda i, be, bf, bs, bn, bv, nu: (jnp.minimum(i, nu[0] - 1), 0)),
            pl.BlockSpec((1, 1), lambda i, *_: (0, 0)),
            pl.BlockSpec(memory_space=pl.ANY),
            pl.BlockSpec((1, 1, D_EXPERT), exp3),
            pl.BlockSpec((1, 1, D_EXPERT), exp3),
            pl.BlockSpec(memory_space=pl.ANY),
            pl.BlockSpec((1, 1, D_MODEL), exp3),
        ],
        out_specs=pl.BlockSpec((bm * ROW_TILES, LANES), lambda i, *_: (i, 0)),
        scratch_shapes=[pltpu.VMEM((2, D_MODEL, 2 * D_EXPERT), F32), pltpu.VMEM((2, D_EXPERT, D_MODEL), F32),
                        pltpu.VMEM((D_MODEL, D_EXPERT), F8), pltpu.VMEM((D_MODEL, D_EXPERT), F8),
                        pltpu.VMEM((D_EXPERT, D_MODEL), F8), pltpu.VMEM((1, 1), F32), pltpu.VMEM((1, 1), F32),
                        pltpu.SemaphoreType.DMA((2, 2))],
    )
    return pl.pallas_call(
        _moe_kernel,
        grid_spec=grid_spec,
        out_shape=jax.ShapeDtypeStruct((p_rows * ROW_TILES, LANES), F32),
        compiler_params=pltpu.CompilerParams(
            dimension_semantics=("arbitrary",), vmem_limit_bytes=VMEM_LIMIT),
        name="moe",
    )(block_exp, block_first, block_slot, block_next, block_valid, n_used,
      xs, x_unscale, w_up, b_glu, b_lin, w_down, b_down)


def _dispatch_kernel(dest_ref, zstart_ref, xn_ref, xs_hbm, zero_ref, sem, zsem):
    i = pl.program_id(0)
    bm = MOE_BLOCK_ROWS
    tokens = xn_ref.shape[0] // ROW_TILES

    def zero_copy(e):
        start = pl.multiple_of(zstart_ref[e] * ROW_TILES, bm * ROW_TILES)
        return pltpu.make_async_copy(zero_ref, xs_hbm.at[pl.ds(start, bm * ROW_TILES)], zsem)

    @pl.when(i == 0)
    def _():
        zero_ref[...] = jnp.zeros_like(zero_ref)
        for e in range(N_EXPERTS):
            @pl.when(zstart_ref[e] >= 0)
            def _():
                zero_copy(e).start()
        for e in range(N_EXPERTS):
            @pl.when(zstart_ref[e] >= 0)
            def _():
                zero_copy(e).wait()

    t0 = i * tokens

    def body(j, carry):
        src = xn_ref.at[pl.ds(pl.multiple_of(j * ROW_TILES, ROW_TILES), ROW_TILES)]
        for k in range(TOP_K):
            d = pl.multiple_of(dest_ref[(t0 + j) * TOP_K + k] * ROW_TILES, ROW_TILES)
            pltpu.make_async_copy(src, xs_hbm.at[pl.ds(d, ROW_TILES)], sem).start(priority=k % 2)
        return carry

    lax.fori_loop(0, tokens, body, 0, unroll=8)
    for k in range(TOP_K):
        pltpu.make_async_copy(xn_ref, xs_hbm.at[pl.ds(0, tokens * ROW_TILES)], sem).wait()


def _dispatch(dest, zstart, xn, p_rows):
    n = xn.shape[0] // ROW_TILES
    t = DISPATCH_TOKENS
    grid_spec = pltpu.PrefetchScalarGridSpec(
        num_scalar_prefetch=2,
        grid=(n // t,),
        in_specs=[pl.BlockSpec((t * ROW_TILES, LANES), lambda i, d, z: (i, 0))],
        out_specs=pl.BlockSpec(memory_space=pl.ANY),
        scratch_shapes=[pltpu.VMEM((MOE_BLOCK_ROWS * ROW_TILES, LANES), F32),
                        pltpu.SemaphoreType.DMA, pltpu.SemaphoreType.DMA],
    )
    return pl.pallas_call(
        _dispatch_kernel,
        grid_spec=grid_spec,
        out_shape=jax.ShapeDtypeStruct((p_rows * ROW_TILES, LANES), F32),
        compiler_params=pltpu.CompilerParams(
            dimension_semantics=("arbitrary",), vmem_limit_bytes=VMEM_LIMIT),
        name="dispatch",
    )(dest, zstart, xn)


def _combine_kernel(dest_ref, y_hbm, x2_ref, gate_ref, out_ref, buf_ref, sems):
    i = pl.program_id(0)
    tc = COMBINE_TOKENS

    def issue(step, slot):
        t0 = step * tc

        def body(j, carry):
            for k in range(TOP_K):
                d = pl.multiple_of(dest_ref[(t0 + j) * TOP_K + k] * ROW_TILES, ROW_TILES)
                r = pl.multiple_of((k * tc + j) * ROW_TILES, ROW_TILES)
                pltpu.make_async_copy(y_hbm.at[pl.ds(d, ROW_TILES)], buf_ref.at[slot, pl.ds(r, ROW_TILES)],
                                      sems.at[slot]).start(priority=k % 2)
            return carry

        lax.fori_loop(0, tc, body, 0, unroll=8)

    @pl.when(i == 0)
    def _():
        issue(0, 0)

    @pl.when(i + 1 < pl.num_programs(0))
    def _():
        issue(i + 1, (i + 1) % 2)

    slot = i % 2
    pltpu.make_async_copy(y_hbm.at[pl.ds(0, TOP_K * tc * ROW_TILES)], buf_ref.at[slot], sems.at[slot]).wait()
    rows = buf_ref.at[slot]
    sub = 64
    for r0 in range(0, tc, sub):
        gate = gate_ref[r0:r0 + sub, :]
        gates = [jnp.broadcast_to(gate[:, k:k + 1], (sub, LANES)) for k in range(TOP_K)]
        for c in range(ROW_TILES):
            acc = x2_ref[r0:r0 + sub, c * LANES:(c + 1) * LANES]
            for k in range(TOP_K):
                acc = acc + gates[k] * rows[pl.ds((k * tc + r0) * ROW_TILES + c, sub, stride=ROW_TILES), :]
            out_ref[r0:r0 + sub, c * LANES:(c + 1) * LANES] = acc


def _combine(dest, y, x2, gates):
    n = x2.shape[0]
    tc = COMBINE_TOKENS
    grid_spec = pltpu.PrefetchScalarGridSpec(
        num_scalar_prefetch=1,
        grid=(n // tc,),
        in_specs=[pl.BlockSpec(memory_space=pl.ANY),
                  pl.BlockSpec((tc, D_MODEL), lambda i, d: (i, 0)),
                  pl.BlockSpec((tc, LANES), lambda i, d: (i, 0))],
        out_specs=pl.BlockSpec((tc, D_MODEL), lambda i, d: (i, 0)),
        scratch_shapes=[pltpu.VMEM((2, TOP_K * tc * ROW_TILES, LANES), F32), pltpu.SemaphoreType.DMA((2,))],
    )
    return pl.pallas_call(
        _combine_kernel,
        grid_spec=grid_spec,
        out_shape=jax.ShapeDtypeStruct((n, D_MODEL), F32),
        compiler_params=pltpu.CompilerParams(
            dimension_semantics=("arbitrary",), vmem_limit_bytes=VMEM_LIMIT),
        name="combine",
    )(dest, y, x2, gates)


def _route(top_idx):
    n = top_idx.shape[0]
    a = n * TOP_K
    bm = MOE_BLOCK_ROWS
    nb = a // bm + N_EXPERTS
    e_flat = top_idx.reshape(a)
    onehot = (e_flat[:, None] == jnp.arange(N_EXPERTS, dtype=jnp.int32)[None, :]).astype(jnp.int32)
    csum = jnp.cumsum(onehot, axis=0)
    counts = csum[-1]
    padded = ((counts + bm - 1) // bm) * bm
    pad_end = jnp.cumsum(padded)
    pad_start = pad_end - padded
    dest = jnp.sum(onehot * (csum - 1 + pad_start[None, :]), axis=1)
    n_used = (pad_end[-1] // bm).astype(jnp.int32)
    blk = jnp.arange(nb, dtype=jnp.int32)
    bexp = jnp.sum((pad_end[None, :] <= (blk * bm)[:, None]).astype(jnp.int32), axis=1)
    bexp = jnp.minimum(bexp, N_EXPERTS - 1)
    bexp = jnp.where(blk < n_used, bexp, bexp[jnp.maximum(n_used - 1, 0)])
    bfirst = jnp.concatenate([jnp.ones((1,), jnp.int32), (bexp[1:] != bexp[:-1]).astype(jnp.int32)])
    bslot = (jnp.cumsum(bfirst) - 1) % 2
    later = jnp.where(bexp[None, :] > bexp[:, None], bexp[None, :], N_EXPERTS)
    bnext = jnp.min(later, axis=1)
    bnext = jnp.where(bnext < N_EXPERTS, bnext, -1).astype(jnp.int32)
    valid_end = jnp.sum(jnp.where(bexp[:, None] == jnp.arange(N_EXPERTS)[None, :], (pad_start + counts)[None, :], 0), axis=1)
    bvalid = jnp.clip(valid_end - blk * bm, 0, bm).astype(jnp.int32)
    zstart = jnp.where(counts > 0, pad_end - bm, -1).astype(jnp.int32)
    return dest.astype(jnp.int32), zstart, (bexp, bfirst, bslot.astype(jnp.int32), bnext, bvalid, n_used.reshape(1))


def kernel(x, positions, norm1_g, w_in, q_norm_g, k_norm_g, hgrn_lower_bounds, hgrn_onorm_g,
           w_out, norm2_g, w_router, b_router, w_up, b_up, w_down, b_down):
    batch, seq, d = x.shape
    n = batch * seq
    depth = norm1_g.shape[0]
    lbs_all = jnp.cumsum(jax.nn.softmax(hgrn_lower_bounds.astype(F32), axis=0), axis=0)
    half = ATT_DIM // 2
    inv = 1.0 / (ROPE_THETA ** (jnp.arange(half, dtype=F32) / half))
    inv_tab = jnp.tile(inv, LANES // half).reshape(1, LANES)
    pos_col = positions.reshape(n, 1)

    x2d = x.reshape(n, d)
    for l in range(depth):
        lbs = lbs_all[l].reshape(2 * HG_HEADS, 1, HG_DIM)
        proj, qkv4, qkv16 = _inproj(
            x2d, pos_col, inv_tab, norm1_g[l].reshape(1, d), w_in[l].astype(BF16),
            jnp.tile(q_norm_g[l], LANES // ATT_DIM).reshape(1, LANES),
            jnp.tile(k_norm_g[l], LANES // ATT_DIM).reshape(1, LANES), batch, seq)
        o_f, o_b = _hgrn(proj, lbs, batch, seq)
        o1, l1 = _attention(proj.reshape(batch, 1, seq, IN_COLS), COL_AQ)
        o4, l4 = _attention(qkv4, 0)
        o16, l16 = _attention(qkv16, 0)
        atts = [o1.reshape(n, ATT_WIDTH), o4, o16]
        lses = [l1.reshape(n, LANES), l4, l16]

        wr = jnp.pad(w_router[l], ((0, 0), (0, LANES - N_EXPERTS)))
        wr_hi = wr.astype(BF16)
        wr_lo = (wr - wr_hi.astype(F32)).astype(BF16)
        br = jnp.pad(b_router[l], (0, LANES - N_EXPERTS)).reshape(1, LANES)
        g2 = norm2_g[l].astype(F32)
        g2_max = jnp.max(jnp.abs(g2))
        x_scale = jnp.where(g2_max > 0.0, ROW_FP8_TARGET / g2_max, 1.0)
        x2, xn, gates, top_idx = _outproj(
            o_f, o_b, proj, atts, lses, x2d, hgrn_onorm_g[l].reshape(1, HG_DIM),
            w_out[l].astype(BF16), jnp.stack([g2, g2 * x_scale]), wr_hi, wr_lo, br, seq)
        dest, zstart, blocks = _route(top_idx[:, :TOP_K])
        xs = _dispatch(dest, zstart, xn, blocks[0].shape[0] * MOE_BLOCK_ROWS)
        y = _moe(*blocks, xs, (1.0 / x_scale).reshape(1, 1), w_up[l],
                 b_up[l][:, 0::2].reshape(N_EXPERTS, 1, D_EXPERT),
                 b_up[l][:, 1::2].reshape(N_EXPERTS, 1, D_EXPERT),
                 w_down[l], b_down[l].reshape(N_EXPERTS, 1, D_MODEL))
        x2d = _combine(dest, y, x2, gates)
    return x2d.reshape(batch, seq, d)
```

```python
import functools

import jax
import jax.numpy as jnp
from jax import lax
from jax.experimental import pallas as pl
from jax.experimental.pallas import tpu as pltpu

F32 = jnp.float32
BF16 = jnp.bfloat16
F8 = jnp.float8_e4m3fn
FP8_TARGET = 240.0
ROW_FP8_TARGET = 8.0
ACT_FP8_SCALE = 4.0

D_MODEL = 1024
HG_HEADS = 4
HG_DIM = 128
HG_WIDTH = HG_HEADS * HG_DIM
HG_CHUNK = 64
ATT_HEADS = 8
ATT_DIM = 64
ATT_WIDTH = ATT_HEADS * ATT_DIM
DILATED_PATTERNS = ((128, 1), (512, 4), (2048, 16))
ATT_HALF = 64
assert all(window // (2 * dil) == ATT_HALF for window, dil in DILATED_PATTERNS)
assert tuple(dil for _, dil in DILATED_PATTERNS) == (1, 4, 16)
ATT_QBLOCK = 128
ROPE_THETA = 10000.0
IN_COLS = 5 * HG_WIDTH + 3 * ATT_WIDTH
N_EXPERTS = 32
TOP_K = 4
D_EXPERT = D_MODEL
SWIGLU_LIMIT = 7.0
SWIGLU_ALPHA = 1.702
EPS = 1e-6
NEG = -1e30

COL_HQ, COL_HF_FWD, COL_HF_BWD, COL_HI, COL_HG, COL_AQ, COL_AK, COL_AV = range(8)

TOKEN_TILE = 1024
INPROJ_TILE = 1024
HGRN_TILE = 512
ATT_TILE = 1024
MOE_BLOCK_ROWS = 512
DISPATCH_TOKENS = 2048
COMBINE_TOKENS = 256
LANES = 128
ROW_TILES = D_MODEL // LANES
VMEM_LIMIT = 60 * 1024 * 1024


def _dot(a, b):
    return jnp.dot(a, b, preferred_element_type=F32)


def _dot_nt(a, b):
    return lax.dot_general(a, b, (((1,), (1,)), ((), ())), preferred_element_type=F32)


def _dot_tn(a, b):
    return lax.dot_general(a, b, (((0,), (0,)), ((), ())), preferred_element_type=F32)


def _sigmoid(x):
    return 0.5 * jnp.tanh(0.5 * x) + 0.5


def _abs_max(x):
    return jnp.max(jnp.max(jnp.abs(x), axis=0, keepdims=True), axis=1, keepdims=True)


def _fp8_scale(amax):
    return jnp.where(amax > 0.0, FP8_TARGET / amax, 1.0)


def _rows_to_tiles(dst_ref, x):
    for c in range(ROW_TILES):
        dst_ref[pl.ds(c, x.shape[0], stride=ROW_TILES), :] = x[:, c * LANES:(c + 1) * LANES]


def _tiles_to_rows(src, rows, first_row=0):
    return [src[pl.ds(first_row * ROW_TILES + c, rows, stride=ROW_TILES), :] for c in range(ROW_TILES)]


def _head_norm_rope(p, gain, cos, sin_signed, scale):
    lane = lax.broadcasted_iota(jnp.int32, (p.shape[0], LANES), 1)
    low = lane < ATT_DIM
    first_half = (lane % ATT_DIM) < (ATT_DIM // 2)
    outs = []
    for t in range(ATT_WIDTH // LANES):
        blk = p[:, t * LANES:(t + 1) * LANES]
        sq = blk * blk
        s_low = jnp.sum(jnp.where(low, sq, 0.0), axis=-1, keepdims=True)
        s_high = jnp.sum(jnp.where(low, 0.0, sq), axis=-1, keepdims=True)
        r = jnp.where(low, lax.rsqrt(s_low * (1.0 / ATT_DIM) + EPS),
                      lax.rsqrt(s_high * (1.0 / ATT_DIM) + EPS))
        y = blk * r * gain
        partner = jnp.where(first_half, pltpu.roll(y, LANES - ATT_DIM // 2, axis=1),
                            pltpu.roll(y, ATT_DIM // 2, axis=1))
        outs.append((y * cos + partner * sin_signed) * scale)
    return jnp.concatenate(outs, axis=1)


def _inproj_kernel(x_ref, pos_ref, inv_ref, g1_ref, w_ref, qg_ref, kg_ref,
                   out_ref, d4_ref, d16_ref, stage_ref, stage2_ref):
    x = x_ref[...]
    ms = jnp.mean(x * x, axis=-1, keepdims=True)
    h = (x * lax.rsqrt(ms + EPS) * g1_ref[...]).astype(BF16)
    ang = pos_ref[...].astype(F32) * inv_ref[...]
    lane = lax.broadcasted_iota(jnp.int32, ang.shape, 1)
    cos = jnp.cos(ang)
    sin_signed = jnp.where((lane % ATT_DIM) < (ATT_DIM // 2), -jnp.sin(ang), jnp.sin(ang))
    order = (COL_AQ, COL_AK, COL_AV, COL_HQ, COL_HF_FWD, COL_HF_BWD, COL_HI, COL_HG)
    nxt = _dot(h, w_ref[:, order[0] * 512:(order[0] + 1) * 512])
    for pos, j in enumerate(order):
        p = nxt
        if pos + 1 < len(order):
            jn = order[pos + 1]
            nxt = _dot(h, w_ref[:, jn * 512:(jn + 1) * 512])
        if j == COL_AQ:
            p = _head_norm_rope(p, qg_ref[...], cos, sin_signed, ATT_DIM ** -0.5)
        elif j == COL_AK:
            p = _head_norm_rope(p, kg_ref[...], cos, sin_signed, 1.0)
        out_ref[:, j * 512:(j + 1) * 512] = p.astype(BF16)
        if j >= COL_AQ:
            rows4, rows16 = x.shape[0] // 4, x.shape[0] // 16
            for c in range(ATT_WIDTH // LANES):
                cols = slice((j - COL_AQ) * ATT_WIDTH + c * LANES, (j - COL_AQ) * ATT_WIDTH + (c + 1) * LANES)
                stage_ref[c] = p[:, c * LANES:(c + 1) * LANES]
                for r4 in range(4):
                    group = stage_ref[c, pl.ds(r4, rows4, stride=4), :]
                    d4_ref[0, r4, :, cols] = group.astype(BF16)
                    stage2_ref[c, r4 * rows4:(r4 + 1) * rows4, :] = group
                for r4 in range(4):
                    for m in range(4):
                        d16_ref[0, r4 + 4 * m, :, cols] = (
                            stage2_ref[c, pl.ds(r4 * rows4 + m, rows16, stride=4), :].astype(BF16))


def _inproj(x2d, pos_col, inv_tab, g1, w_in_bf16, qg, kg, batch, seq):
    n = x2d.shape[0]
    t = INPROJ_TILE
    nt = seq // t
    const = lambda i: (0, 0)
    qkv = 3 * ATT_WIDTH

    def residue_major(dil):
        spec = pl.BlockSpec((1, dil, t // dil, qkv), lambda i: (i // nt, 0, i % nt, 0))
        return spec, jax.ShapeDtypeStruct((batch, dil, seq // dil, qkv), BF16)

    spec4, shape4 = residue_major(4)
    spec16, shape16 = residue_major(16)
    return pl.pallas_call(
        _inproj_kernel,
        grid=(n // t,),
        in_specs=[
            pl.BlockSpec((t, D_MODEL), lambda i: (i, 0)),
            pl.BlockSpec((t, 1), lambda i: (i, 0)),
            pl.BlockSpec((1, LANES), const),
            pl.BlockSpec((1, D_MODEL), const),
            pl.BlockSpec((D_MODEL, IN_COLS), const, pipeline_mode=pl.Buffered(1)),
            pl.BlockSpec((1, LANES), const),
            pl.BlockSpec((1, LANES), const),
        ],
        out_specs=[pl.BlockSpec((t, IN_COLS), lambda i: (i, 0)), spec4, spec16],
        out_shape=[jax.ShapeDtypeStruct((n, IN_COLS), BF16), shape4, shape16],
        scratch_shapes=[pltpu.VMEM((ATT_WIDTH // LANES, t, LANES), F32)] * 2,
        compiler_params=pltpu.CompilerParams(
            dimension_semantics=("parallel",), vmem_limit_bytes=VMEM_LIMIT),
        name="inproj",
    )(x2d, pos_col, inv_tab, g1, w_in_bf16, qg, kg)


def _hgrn_direction(q, z, v, lb, state_t, reverse):
    c = HG_CHUNK
    t = q.shape[0]
    n = t // c
    row = lax.broadcasted_iota(jnp.int32, (c, c), 0)
    col = lax.broadcasted_iota(jnp.int32, (c, c), 1)
    mask = (row <= col) if reverse else (row >= col)
    tri = jnp.where(mask, 1.0, 0.0).astype(BF16)
    last_row = 0 if reverse else c - 1

    z = z.astype(F32)
    q = q.astype(F32)
    sg = _sigmoid(z)
    f = lb + (1.0 - lb) * sg
    k = (1.0 - lb) * (1.0 - sg)
    lf = jnp.log(f)
    lf_hi = lf.astype(BF16)
    lf_lo = (lf - lf_hi.astype(F32)).astype(BF16)
    chunks = [slice(j * c, (j + 1) * c) for j in range(n)]
    b = jnp.concatenate([_dot(tri, lf_hi[rs]) + _dot(tri, lf_lo[rs]) for rs in chunks], axis=0)
    b_last = b.reshape(n, c, HG_DIM)[:, last_row:last_row + 1, :]
    decay = jnp.exp(b_last)
    qt = (q * _sigmoid(q) * jnp.exp(b)).astype(BF16)
    kt_f32 = k * jnp.exp(-b)
    kt = kt_f32.astype(BF16)
    kd = (kt_f32.reshape(n, c, HG_DIM) * decay).reshape(t, HG_DIM).astype(BF16)

    outs, updates = [], []
    for rs in chunks:
        a = jnp.where(mask, _dot_nt(qt[rs], kt[rs]), 0.0)
        outs.append(_dot(a.astype(BF16), v[rs]))
        updates.append(_dot_tn(v[rs], kd[rs]))
    for j in (reversed(range(n)) if reverse else range(n)):
        outs[j] = outs[j] + _dot_nt(qt[chunks[j]], state_t.astype(BF16))
        state_t = state_t * decay[j] + updates[j]
    return jnp.concatenate(outs, axis=0), state_t


def _hgrn_kernel(qf_ref, zf_ref, vf_ref, qb_ref, zb_ref, vb_ref, lb_ref,
                 of_ref, ob_ref, sf_ref, sb_ref):
    @pl.when(pl.program_id(1) == 0)
    def _():
        sf_ref[...] = jnp.zeros_like(sf_ref)
        sb_ref[...] = jnp.zeros_like(sb_ref)

    for h in range(HG_HEADS):
        cols = slice(h * HG_DIM, (h + 1) * HG_DIM)
        o, sf = _hgrn_direction(qf_ref[:, cols], zf_ref[:, cols], vf_ref[:, cols],
                                lb_ref[h], sf_ref[h], False)
        of_ref[:, cols] = o.astype(of_ref.dtype)
        sf_ref[h] = sf
        o, sb = _hgrn_direction(qb_ref[:, cols], zb_ref[:, cols], vb_ref[:, cols],
                                lb_ref[HG_HEADS + h], sb_ref[h], True)
        ob_ref[:, cols] = o.astype(ob_ref.dtype)
        sb_ref[h] = sb


def _hgrn(proj, lbs, batch, seq):
    n = proj.shape[0]
    t = HGRN_TILE
    nblk = seq // t

    def fwd(colblk):
        return pl.BlockSpec((t, HG_WIDTH), lambda b, i: (b * nblk + i, colblk))

    def bwd(colblk):
        return pl.BlockSpec((t, HG_WIDTH), lambda b, i: (b * nblk + nblk - 1 - i, colblk))

    return pl.pallas_call(
        _hgrn_kernel,
        grid=(batch, nblk),
        in_specs=[
            fwd(COL_HQ), fwd(COL_HF_FWD), fwd(COL_HI),
            bwd(COL_HQ), bwd(COL_HF_BWD), bwd(COL_HI),
            pl.BlockSpec((2 * HG_HEADS, 1, HG_DIM), lambda b, i: (0, 0, 0)),
        ],
        out_specs=[fwd(0), bwd(0)],
        out_shape=[jax.ShapeDtypeStruct((n, HG_WIDTH), BF16)] * 2,
        scratch_shapes=[pltpu.VMEM((HG_HEADS, HG_DIM, HG_DIM), F32)] * 2,
        compiler_params=pltpu.CompilerParams(
            dimension_semantics=("parallel", "arbitrary"), vmem_limit_bytes=VMEM_LIMIT),
        name="hgrn",
    )(proj, proj, proj, proj, proj, proj, lbs)


def _attn_kernel(q_ref, kc_ref, kp_ref, kn_ref, vc_ref, vp_ref, vn_ref, o_ref, l_ref,
                 kw_ref, vw_ref, *, tq, length):
    n = pl.program_id(2)
    half = ATT_HALF
    kw_ref[0:half, :] = kp_ref[...]
    kw_ref[half:half + tq, :] = kc_ref[...]
    kw_ref[half + tq:, :] = kn_ref[...]
    vw_ref[0:half, :] = vp_ref[...]
    vw_ref[half:half + tq, :] = vc_ref[...]
    vw_ref[half + tq:, :] = vn_ref[...]

    qb_rows = ATT_QBLOCK
    win = qb_rows + 2 * half
    i_idx = lax.broadcasted_iota(jnp.int32, (qb_rows, win), 0)
    j_idx = lax.broadcasted_iota(jnp.int32, (qb_rows, win), 1)
    band = (j_idx >= i_idx) & (j_idx <= i_idx + 2 * half)

    pairs = [slice(p * LANES, (p + 1) * LANES) for p in range(ATT_HEADS // 2)]
    head_lane = lax.broadcasted_iota(jnp.int32, (qb_rows, LANES), 1)
    even_half = head_lane < ATT_DIM
    keep_even = jnp.where(lax.broadcasted_iota(jnp.int32, (1, LANES), 1) < ATT_DIM, 1.0, 0.0).astype(BF16)
    keep_odd = (1.0 - keep_even.astype(F32)).astype(BF16)

    def masked_scores(r0):
        base = n * tq + r0 - half
        valid = band & (j_idx >= -base) & (j_idx < length - base)
        q = q_ref[r0:r0 + qb_rows, :]
        kw = kw_ref[r0:r0 + win, :]
        out = []
        for cs in pairs:
            for keep in (keep_even, keep_odd):
                out.append(jnp.where(valid, _dot_nt(q[:, cs] * keep, kw[:, cs]), NEG))
        return out

    blocks = list(range(0, tq, qb_rows))
    nxt = masked_scores(blocks[0])
    for pos, r0 in enumerate(blocks):
        scores = nxt
        if pos + 1 < len(blocks):
            nxt = masked_scores(blocks[pos + 1])
        vw = vw_ref[r0:r0 + win, :]
        maxes = [jnp.max(s, axis=-1, keepdims=True) for s in scores]
        probs = [jnp.exp(s - m) for s, m in zip(scores, maxes)]
        dens = [jnp.sum(p, axis=-1, keepdims=True) for p in probs]
        outs = []
        for p, cs in enumerate(pairs):
            pv_even = _dot(probs[2 * p].astype(BF16), vw[:, cs])
            pv_odd = _dot(probs[2 * p + 1].astype(BF16), vw[:, cs])
            outs.append(jnp.where(even_half, pv_even, pv_odd)
                        / jnp.where(even_half, dens[2 * p], dens[2 * p + 1]))
        o_ref[r0:r0 + qb_rows, :] = jnp.concatenate(outs, axis=1).astype(o_ref.dtype)
        lse = jnp.zeros((qb_rows, LANES), F32)
        for h, (m, den) in enumerate(zip(maxes, dens)):
            lse = jnp.where(head_lane == h, m + jnp.log(den), lse)
        l_ref[r0:r0 + qb_rows, :] = lse


def _attention(qkv, col0):
    batch, dil, length, _ = qkv.shape
    tq = min(ATT_TILE, length)
    nq = length // tq
    hb = tq // ATT_HALF
    n_hblk = length // ATT_HALF

    def cur(col):
        return pl.BlockSpec((None, None, tq, ATT_WIDTH), lambda b, r, n: (b, r, n, col))

    def prev(col):
        return pl.BlockSpec((None, None, ATT_HALF, ATT_WIDTH),
                            lambda b, r, n: (b, r, jnp.maximum(n * hb - 1, 0), col))

    def nxt(col):
        return pl.BlockSpec((None, None, ATT_HALF, ATT_WIDTH),
                            lambda b, r, n: (b, r, jnp.minimum((n + 1) * hb, n_hblk - 1), col))

    out_spec = pl.BlockSpec((None, None, tq, ATT_WIDTH), lambda b, r, n: (b, r, n, 0))
    return pl.pallas_call(
        functools.partial(_attn_kernel, tq=tq, length=length),
        grid=(batch, dil, nq),
        in_specs=[cur(col0), cur(col0 + 1), prev(col0 + 1), nxt(col0 + 1),
                  cur(col0 + 2), prev(col0 + 2), nxt(col0 + 2)],
        out_specs=[out_spec, pl.BlockSpec((None, None, tq, LANES), lambda b, r, n: (b, r, n, 0))],
        out_shape=[jax.ShapeDtypeStruct((batch, dil, length, ATT_WIDTH), BF16),
                   jax.ShapeDtypeStruct((batch, dil, length, LANES), F32)],
        scratch_shapes=[pltpu.VMEM((tq + 2 * ATT_HALF, ATT_WIDTH), BF16)] * 2,
        compiler_params=pltpu.CompilerParams(
            dimension_semantics=("parallel", "parallel", "parallel"), vmem_limit_bytes=VMEM_LIMIT),
        name=f"attn_d{dil}",
    )(qkv, qkv, qkv, qkv, qkv, qkv, qkv)


def _token_major(src_ref, stage_ref, tmp_ref):
    dil, rows = src_ref.shape[1], src_ref.shape[2]
    nc = src_ref.shape[3] // LANES
    for c in range(nc):
        cols = slice(c * LANES, (c + 1) * LANES)
        if dil == 4:
            for r in range(dil):
                stage_ref[c, pl.ds(r, rows, stride=dil), :] = src_ref[0, r, :, cols].astype(F32)
        else:
            group = 4 * rows
            for r4 in range(4):
                for m in range(4):
                    tmp_ref[c, pl.ds(r4 * group + m, rows, stride=4), :] = (
                        src_ref[0, r4 + 4 * m, :, cols].astype(F32))
            for r4 in range(4):
                stage_ref[c, pl.ds(r4, group, stride=4), :] = tmp_ref[c, r4 * group:(r4 + 1) * group, :]
    return jnp.concatenate([stage_ref[c] for c in range(nc)], axis=1)


def _outproj_kernel(of_ref, ob_ref, hg_ref, o1_ref, o2_ref, o3_ref, l1_ref, l2_ref, l3_ref,
                    x_ref, og_ref, w_ref, g2_ref, wrh_ref, wrl_ref, br_ref,
                    x2_ref, xn_ref, gate_ref, idx_ref, st_o2, st_o3, st_l2, st_l3, st_tmp_o, st_tmp_l):
    o = of_ref[...].astype(F32) + ob_ref[...].astype(F32)
    hg = hg_ref[...].astype(F32)
    parts = []
    for h in range(HG_HEADS):
        blk = o[:, h * HG_DIM:(h + 1) * HG_DIM]
        ms = jnp.mean(blk * blk, axis=-1, keepdims=True)
        parts.append(blk * lax.rsqrt(ms + EPS) * og_ref[...])
    o_hg = jnp.concatenate(parts, axis=1) * (hg * _sigmoid(hg))

    l1 = l1_ref[...]
    l2 = _token_major(l2_ref, st_l2, st_tmp_l)
    l3 = _token_major(l3_ref, st_l3, st_tmp_l)
    mx = jnp.maximum(jnp.maximum(l1, l2), l3)
    e1, e2, e3 = jnp.exp(l1 - mx), jnp.exp(l2 - mx), jnp.exp(l3 - mx)
    den = e1 + e2 + e3
    er = lax.broadcasted_iota(jnp.int32, (LANES, ATT_WIDTH), 0)
    ec = lax.broadcasted_iota(jnp.int32, (LANES, ATT_WIDTH), 1)
    expand = jnp.where(er == ec // ATT_DIM, 1.0, 0.0).astype(BF16)

    def per_lane(w):
        return _dot(w.astype(BF16), expand)

    o_att = (per_lane(e1 / den) * o1_ref[...].astype(F32)
             + per_lane(e2 / den) * _token_major(o2_ref, st_o2, st_tmp_o)
             + per_lane(e3 / den) * _token_major(o3_ref, st_o3, st_tmp_o))

    y = _dot(o_hg.astype(BF16), w_ref[0:HG_WIDTH, :]) + _dot(o_att.astype(BF16), w_ref[HG_WIDTH:, :])
    x2 = x_ref[...] + y
    x2_ref[...] = x2

    ms = jnp.mean(x2 * x2, axis=-1, keepdims=True)
    unit = x2 * lax.rsqrt(ms + EPS)
    xn = unit * g2_ref[0:1, :]
    _rows_to_tiles(xn_ref, unit * g2_ref[1:2, :])
    xn_hi = xn.astype(BF16)
    xn_lo = (xn - xn_hi.astype(F32)).astype(BF16)
    logits = (_dot(xn_hi, wrh_ref[...]) + _dot(xn_lo, wrh_ref[...]) + _dot(xn_hi, wrl_ref[...])
              + br_ref[...])

    lane = lax.broadcasted_iota(jnp.int32, logits.shape, 1)
    lane_f = lane.astype(F32)
    work = jnp.where(lane < N_EXPERTS, logits, -jnp.inf)
    vals, idxs = [], []
    for _ in range(TOP_K):
        m = jnp.max(work, axis=-1, keepdims=True)
        idx = jnp.min(jnp.where(work == m, lane_f, float(LANES)), axis=-1, keepdims=True)
        vals.append(m)
        idxs.append(idx)
        work = jnp.where(lane_f == idx, -jnp.inf, work)
    es = [jnp.exp(v - vals[0]) for v in vals]
    den = es[0] + es[1] + es[2] + es[3]
    gate_out = jnp.zeros(logits.shape, F32)
    idx_out = jnp.zeros(logits.shape, F32)
    for k in range(TOP_K):
        gate_out = jnp.where(lane == k, es[k] / den, gate_out)
        idx_out = jnp.where(lane == k, idxs[k], idx_out)
    gate_ref[...] = gate_out
    idx_ref[...] = idx_out.astype(jnp.int32)


def _outproj(o_f, o_b, proj, atts, lses, x2d, og, w_out_bf16, g2, wr_hi, wr_lo, br, seq):
    n = x2d.shape[0]
    t = TOKEN_TILE
    nt = seq // t
    row = lambda i: (i, 0)
    const = lambda i: (0, 0)
    half = pl.BlockSpec((t, 512), row)

    def residue_major(dil, width):
        return pl.BlockSpec((1, dil, t // dil, width), lambda i: (i // nt, 0, i % nt, 0))

    rm4, rm16 = residue_major(4, ATT_WIDTH), residue_major(16, ATT_WIDTH)
    lse, lse4, lse16 = pl.BlockSpec((t, LANES), row), residue_major(4, LANES), residue_major(16, LANES)
    wide, narrow = pltpu.VMEM((ATT_WIDTH // LANES, t, LANES), F32), pltpu.VMEM((1, t, LANES), F32)
    return pl.pallas_call(
        _outproj_kernel,
        grid=(n // t,),
        in_specs=[
            half, half, pl.BlockSpec((t, 512), lambda i: (i, COL_HG)),
            half, rm4, rm16, lse, lse4, lse16,
            pl.BlockSpec((t, D_MODEL), row),
            pl.BlockSpec((1, HG_DIM), const),
            pl.BlockSpec((D_MODEL, D_MODEL), const),
            pl.BlockSpec((2, D_MODEL), const),
            pl.BlockSpec((D_MODEL, LANES), const),
            pl.BlockSpec((D_MODEL, LANES), const),
            pl.BlockSpec((1, LANES), const),
        ],
        out_specs=[pl.BlockSpec((t, D_MODEL), row), pl.BlockSpec((t * ROW_TILES, LANES), row),
                   pl.BlockSpec((t, LANES), row), pl.BlockSpec((t, LANES), row)],
        out_shape=[jax.ShapeDtypeStruct((n, D_MODEL), F32), jax.ShapeDtypeStruct((n * ROW_TILES, LANES), F32),
                   jax.ShapeDtypeStruct((n, LANES), F32), jax.ShapeDtypeStruct((n, LANES), jnp.int32)],
        scratch_shapes=[wide, wide, narrow, narrow, wide, narrow],
        compiler_params=pltpu.CompilerParams(
            dimension_semantics=("parallel",), vmem_limit_bytes=VMEM_LIMIT),
        name="outproj",
    )(o_f, o_b, proj, *atts, *lses, x2d, og, w_out_bf16, g2, wr_hi, wr_lo, br)


def _moe_kernel(bexp_ref, bfirst_ref, bslot_ref, bnext_ref, bvalid_ref, nused_ref,
                xs_ref, xinv_ref, wu_hbm, bg_ref, bl_ref, wd_hbm, bd_ref,
                y_ref, wu_buf, wd_buf, wg_s, wl_s, wd_s, up_unscale_s, down_unscale_s, sems):
    i = pl.program_id(0)

    def weight_copies(expert, slot):
        return (pltpu.make_async_copy(wu_hbm.at[expert], wu_buf.at[slot], sems.at[0, slot]),
                pltpu.make_async_copy(wd_hbm.at[expert], wd_buf.at[slot], sems.at[1, slot]))

    @pl.when(bfirst_ref[i] == 1)
    def _():
        slot = bslot_ref[i]

        @pl.when(i == 0)
        def _():
            for cp in weight_copies(bexp_ref[0], 0):
                cp.start()

        for cp in weight_copies(bexp_ref[i], slot):
            cp.wait()

        @pl.when(bnext_ref[i] >= 0)
        def _():
            for cp in weight_copies(bnext_ref[i], 1 - slot):
                cp.start()

        su = _fp8_scale(_abs_max(wu_buf[slot]))
        sd = _fp8_scale(_abs_max(wd_buf[slot]))
        up_unscale_s[...] = xinv_ref[...] / su
        down_unscale_s[...] = (1.0 / ACT_FP8_SCALE) / sd
        r = lax.broadcasted_iota(jnp.int32, (2 * LANES, 2 * LANES), 0)
        c = lax.broadcasted_iota(jnp.int32, (2 * LANES, 2 * LANES), 1)
        src = jnp.where(c < LANES, 2 * c, 2 * (c - LANES) + 1)
        perm = jnp.where(r == src, 1.0, 0.0).astype(BF16)
        rows = 256
        for rb in range(D_MODEL // rows):
            rs = slice(rb * rows, (rb + 1) * rows)
            for cb in range(D_EXPERT // LANES):
                w = wu_buf[slot, rs, cb * 2 * LANES:(cb + 1) * 2 * LANES].astype(BF16)
                split = (_dot(w, perm) * su).astype(F8)
                wg_s[rs, cb * LANES:(cb + 1) * LANES] = split[:, :LANES]
                wl_s[rs, cb * LANES:(cb + 1) * LANES] = split[:, LANES:]
        wd_s[...] = (wd_buf[slot] * sd).astype(F8)

    def expert_mlp(rows):
        x8 = jnp.concatenate(_tiles_to_rows(xs_ref, rows), axis=1).astype(F8)
        hglu = _dot(x8, wg_s[...]) * up_unscale_s[...] + bg_ref[0]
        hlin = _dot(x8, wl_s[...]) * up_unscale_s[...] + bl_ref[0]
        glu = jnp.minimum(hglu, SWIGLU_LIMIT)
        lin = jnp.clip(hlin, -SWIGLU_LIMIT, SWIGLU_LIMIT)
        act = glu * _sigmoid(SWIGLU_ALPHA * glu) * (lin + 1.0)
        y = _dot((act * ACT_FP8_SCALE).astype(F8), wd_s[...]) * down_unscale_s[...] + bd_ref[0]
        _rows_to_tiles(y_ref, y)

    valid = bvalid_ref[i]
    half = MOE_BLOCK_ROWS // 2

    @pl.when(valid > half)
    def _():
        expert_mlp(MOE_BLOCK_ROWS)

    @pl.when((valid > 0) & (valid <= half))
    def _():
        expert_mlp(half)
        y_ref[half * ROW_TILES:, :] = jnp.zeros((half * ROW_TILES, LANES), F32)

    @pl.when(valid == 0)
    def _():
        y_ref[...] = jnp.zeros_like(y_ref)


def _moe(block_exp, block_first, block_slot, block_next, block_valid, n_used,
         xs, x_unscale, w_up, b_glu, b_lin, w_down, b_down):
    p_rows = xs.shape[0] // ROW_TILES
    bm = MOE_BLOCK_ROWS
    nb = p_rows // bm
    exp3 = lambda i, be, *_: (be[i], 0, 0)
    grid_spec = pltpu.PrefetchScalarGridSpec(
        num_scalar_prefetch=6,
        grid=(nb,),
        in_specs=[
            pl.BlockSpec((bm * ROW_TILES, LANES), lambda i, be, bf, bs, bn, bv, nu: (jnp.minimum(i, nu[0] - 1), 0)),
            pl.BlockSpec((1, 1), lambda i, *_: (0, 0)),
            pl.BlockSpec(memory_space=pl.ANY),
            pl.BlockSpec((1, 1, D_EXPERT), exp3),
            pl.BlockSpec((1, 1, D_EXPERT), exp3),
            pl.BlockSpec(memory_space=pl.ANY),
            pl.BlockSpec((1, 1, D_MODEL), exp3),
        ],
        out_specs=pl.BlockSpec((bm * ROW_TILES, LANES), lambda i, *_: (i, 0)),
        scratch_shapes=[pltpu.VMEM((2, D_MODEL, 2 * D_EXPERT), F32), pltpu.VMEM((2, D_EXPERT, D_MODEL), F32),
                        pltpu.VMEM((D_MODEL, D_EXPERT), F8), pltpu.VMEM((D_MODEL, D_EXPERT), F8),
                        pltpu.VMEM((D_EXPERT, D_MODEL), F8), pltpu.VMEM((1, 1), F32), pltpu.VMEM((1, 1), F32),
                        pltpu.SemaphoreType.DMA((2, 2))],
    )
    return pl.pallas_call(
        _moe_kernel,
        grid_spec=grid_spec,
        out_shape=jax.ShapeDtypeStruct((p_rows * ROW_TILES, LANES), F32),
        compiler_params=pltpu.CompilerParams(
            dimension_semantics=("arbitrary",), vmem_limit_bytes=VMEM_LIMIT),
        name="moe",
    )(block_exp, block_first, block_slot, block_next, block_valid, n_used,
      xs, x_unscale, w_up, b_glu, b_lin, w_down, b_down)


def _dispatch_kernel(dest_ref, zstart_ref, xn_ref, xs_hbm, zero_ref, sem, zsem):
    i = pl.program_id(0)
    bm = MOE_BLOCK_ROWS
    tokens = xn_ref.shape[0] // ROW_TILES

    def zero_copy(e):
        start = pl.multiple_of(zstart_ref[e] * ROW_TILES, bm * ROW_TILES)
        return pltpu.make_async_copy(zero_ref, xs_hbm.at[pl.ds(start, bm * ROW_TILES)], zsem)

    @pl.when(i == 0)
    def _():
        zero_ref[...] = jnp.zeros_like(zero_ref)
        for e in range(N_EXPERTS):
            @pl.when(zstart_ref[e] >= 0)
            def _():
                zero_copy(e).start()
        for e in range(N_EXPERTS):
            @pl.when(zstart_ref[e] >= 0)
            def _():
                zero_copy(e).wait()

    t0 = i * tokens

    def body(j, carry):
        src = xn_ref.at[pl.ds(pl.multiple_of(j * ROW_TILES, ROW_TILES), ROW_TILES)]
        for k in range(TOP_K):
            d = pl.multiple_of(dest_ref[(t0 + j) * TOP_K + k] * ROW_TILES, ROW_TILES)
            pltpu.make_async_copy(src, xs_hbm.at[pl.ds(d, ROW_TILES)], sem).start(priority=k % 2)
        return carry

    lax.fori_loop(0, tokens, body, 0, unroll=8)
    for k in range(TOP_K):
        pltpu.make_async_copy(xn_ref, xs_hbm.at[pl.ds(0, tokens * ROW_TILES)], sem).wait()


def _dispatch(dest, zstart, xn, p_rows):
    n = xn.shape[0] // ROW_TILES
    t = DISPATCH_TOKENS
    grid_spec = pltpu.PrefetchScalarGridSpec(
        num_scalar_prefetch=2,
        grid=(n // t,),
        in_specs=[pl.BlockSpec((t * ROW_TILES, LANES), lambda i, d, z: (i, 0))],
        out_specs=pl.BlockSpec(memory_space=pl.ANY),
        scratch_shapes=[pltpu.VMEM((MOE_BLOCK_ROWS * ROW_TILES, LANES), F32),
                        pltpu.SemaphoreType.DMA, pltpu.SemaphoreType.DMA],
    )
    return pl.pallas_call(
        _dispatch_kernel,
        grid_spec=grid_spec,
        out_shape=jax.ShapeDtypeStruct((p_rows * ROW_TILES, LANES), F32),
        compiler_params=pltpu.CompilerParams(
            dimension_semantics=("arbitrary",), vmem_limit_bytes=VMEM_LIMIT),
        name="dispatch",
    )(dest, zstart, xn)


def _combine_kernel(dest_ref, y_hbm, x2_ref, gate_ref, out_ref, buf_ref, sems):
    i = pl.program_id(0)
    tc = COMBINE_TOKENS

    def issue(step, slot):
        t0 = step * tc

        def body(j, carry):
            for k in range(TOP_K):
                d = pl.multiple_of(dest_ref[(t0 + j) * TOP_K + k] * ROW_TILES, ROW_TILES)
                r = pl.multiple_of((k * tc + j) * ROW_TILES, ROW_TILES)
                pltpu.make_async_copy(y_hbm.at[pl.ds(d, ROW_TILES)], buf_ref.at[slot, pl.ds(r, ROW_TILES)],
                                      sems.at[slot]).start(priority=k % 2)
            return carry

        lax.fori_loop(0, tc, body, 0, unroll=8)

    @pl.when(i == 0)
    def _():
        issue(0, 0)

    @pl.when(i + 1 < pl.num_programs(0))
    def _():
        issue(i + 1, (i + 1) % 2)

    slot = i % 2
    pltpu.make_async_copy(y_hbm.at[pl.ds(0, TOP_K * tc * ROW_TILES)], buf_ref.at[slot], sems.at[slot]).wait()
    rows = buf_ref.at[slot]
    sub = 64
    for r0 in range(0, tc, sub):
        gate = gate_ref[r0:r0 + sub, :]
        gates = [jnp.broadcast_to(gate[:, k:k + 1], (sub, LANES)) for k in range(TOP_K)]
        for c in range(ROW_TILES):
            acc = x2_ref[r0:r0 + sub, c * LANES:(c + 1) * LANES]
            for k in range(TOP_K):
                acc = acc + gates[k] * rows[pl.ds((k * tc + r0) * ROW_TILES + c, sub, stride=ROW_TILES), :]
            out_ref[r0:r0 + sub, c * LANES:(c + 1) * LANES] = acc


def _combine(dest, y, x2, gates):
    n = x2.shape[0]
    tc = COMBINE_TOKENS
    grid_spec = pltpu.PrefetchScalarGridSpec(
        num_scalar_prefetch=1,
        grid=(n // tc,),
        in_specs=[pl.BlockSpec(memory_space=pl.ANY),
                  pl.BlockSpec((tc, D_MODEL), lambda i, d: (i, 0)),
                  pl.BlockSpec((tc, LANES), lambda i, d: (i, 0))],
        out_specs=pl.BlockSpec((tc, D_MODEL), lambda i, d: (i, 0)),
        scratch_shapes=[pltpu.VMEM((2, TOP_K * tc * ROW_TILES, LANES), F32), pltpu.SemaphoreType.DMA((2,))],
    )
    return pl.pallas_call(
        _combine_kernel,
        grid_spec=grid_spec,
        out_shape=jax.ShapeDtypeStruct((n, D_MODEL), F32),
        compiler_params=pltpu.CompilerParams(
            dimension_semantics=("arbitrary",), vmem_limit_bytes=VMEM_LIMIT),
        name="combine",
    )(dest, y, x2, gates)


def _route(top_idx):
    n = top_idx.shape[0]
    a = n * TOP_K
    bm = MOE_BLOCK_ROWS
    nb = a // bm + N_EXPERTS
    experts = jnp.arange(N_EXPERTS, dtype=jnp.int32)[None, :]
    picked = [top_idx[:, k:k + 1] == experts for k in range(TOP_K)]
    hits = sum(p.astype(jnp.int32) for p in picked)
    csum = jnp.cumsum(hits, axis=0)
    counts = csum[-1]
    padded = ((counts + bm - 1) // bm) * bm
    pad_end = jnp.cumsum(padded)
    pad_start = pad_end - padded
    row_of = csum - hits + pad_start[None, :]
    dest = jnp.stack([jnp.sum(jnp.where(p, row_of, 0), axis=1) for p in picked], axis=1).reshape(a)
    n_used = (pad_end[-1] // bm).astype(jnp.int32)
    blk = jnp.arange(nb, dtype=jnp.int32)
    bexp = jnp.sum((pad_end[None, :] <= (blk * bm)[:, None]).astype(jnp.int32), axis=1)
    bexp = jnp.minimum(bexp, N_EXPERTS - 1)
    bexp = jnp.where(blk < n_used, bexp, bexp[jnp.maximum(n_used - 1, 0)])
    bfirst = jnp.concatenate([jnp.ones((1,), jnp.int32), (bexp[1:] != bexp[:-1]).astype(jnp.int32)])
    bslot = (jnp.cumsum(bfirst) - 1) % 2
    later = jnp.where(bexp[None, :] > bexp[:, None], bexp[None, :], N_EXPERTS)
    bnext = jnp.min(later, axis=1)
    bnext = jnp.where(bnext < N_EXPERTS, bnext, -1).astype(jnp.int32)
    valid_end = jnp.sum(jnp.where(bexp[:, None] == jnp.arange(N_EXPERTS)[None, :], (pad_start + counts)[None, :], 0), axis=1)
    bvalid = jnp.clip(valid_end - blk * bm, 0, bm).astype(jnp.int32)
    zstart = jnp.where(counts > 0, pad_end - bm, -1).astype(jnp.int32)
    return dest.astype(jnp.int32), zstart, (bexp, bfirst, bslot.astype(jnp.int32), bnext, bvalid, n_used.reshape(1))


def kernel(x, positions, norm1_g, w_in, q_norm_g, k_norm_g, hgrn_lower_bounds, hgrn_onorm_g,
           w_out, norm2_g, w_router, b_router, w_up, b_up, w_down, b_down):
    batch, seq, d = x.shape
    n = batch * seq
    depth = norm1_g.shape[0]
    lbs_all = jnp.cumsum(jax.nn.softmax(hgrn_lower_bounds.astype(F32), axis=0), axis=0)
    half = ATT_DIM // 2
    inv = 1.0 / (ROPE_THETA ** (jnp.arange(half, dtype=F32) / half))
    inv_tab = jnp.tile(inv, LANES // half).reshape(1, LANES)
    pos_col = positions.reshape(n, 1)

    x2d = x.reshape(n, d)
    for l in range(depth):
        lbs = lbs_all[l].reshape(2 * HG_HEADS, 1, HG_DIM)
        proj, qkv4, qkv16 = _inproj(
            x2d, pos_col, inv_tab, norm1_g[l].reshape(1, d), w_in[l].astype(BF16),
            jnp.tile(q_norm_g[l], LANES // ATT_DIM).reshape(1, LANES),
            jnp.tile(k_norm_g[l], LANES // ATT_DIM).reshape(1, LANES), batch, seq)
        o_f, o_b = _hgrn(proj, lbs, batch, seq)
        o1, l1 = _attention(proj.reshape(batch, 1, seq, IN_COLS), COL_AQ)
        o4, l4 = _attention(qkv4, 0)
        o16, l16 = _attention(qkv16, 0)
        atts = [o1.reshape(n, ATT_WIDTH), o4, o16]
        lses = [l1.reshape(n, LANES), l4, l16]

        wr = jnp.pad(w_router[l], ((0, 0), (0, LANES - N_EXPERTS)))
        wr_hi = wr.astype(BF16)
        wr_lo = (wr - wr_hi.astype(F32)).astype(BF16)
        br = jnp.pad(b_router[l], (0, LANES - N_EXPERTS)).reshape(1, LANES)
        g2 = norm2_g[l].astype(F32)
        g2_max = jnp.max(jnp.abs(g2))
        x_scale = jnp.where(g2_max > 0.0, ROW_FP8_TARGET / g2_max, 1.0)
        x2, xn, gates, top_idx = _outproj(
            o_f, o_b, proj, atts, lses, x2d, hgrn_onorm_g[l].reshape(1, HG_DIM),
            w_out[l].astype(BF16), jnp.stack([g2, g2 * x_scale]), wr_hi, wr_lo, br, seq)
        dest, zstart, blocks = _route(top_idx[:, :TOP_K])
        xs = _dispatch(dest, zstart, xn, blocks[0].shape[0] * MOE_BLOCK_ROWS)
        y = _moe(*blocks, xs, (1.0 / x_scale).reshape(1, 1), w_up[l],
                 b_up[l][:, 0::2].reshape(N_EXPERTS, 1, D_EXPERT),
                 b_up[l][:, 1::2].reshape(N_EXPERTS, 1, D_EXPERT),
                 w_down[l], b_down[l].reshape(N_EXPERTS, 1, D_MODEL))
        x2d = _combine(dest, y, x2, gates)
    return x2d.reshape(batch, seq, d)
```

```python
import functools

import jax
import jax.numpy as jnp
from jax import lax
from jax.experimental import pallas as pl
from jax.experimental.pallas import tpu as pltpu

F32 = jnp.float32
BF16 = jnp.bfloat16
F8 = jnp.float8_e4m3fn
FP8_TARGET = 240.0
ROW_FP8_TARGET = 8.0
ACT_FP8_SCALE = 4.0

D_MODEL = 1024
HG_HEADS = 4
HG_DIM = 128
HG_WIDTH = HG_HEADS * HG_DIM
HG_CHUNK = 64
ATT_HEADS = 8
ATT_DIM = 64
ATT_WIDTH = ATT_HEADS * ATT_DIM
DILATED_PATTERNS = ((128, 1), (512, 4), (2048, 16))
ATT_HALF = 64
assert all(window // (2 * dil) == ATT_HALF for window, dil in DILATED_PATTERNS)
assert tuple(dil for _, dil in DILATED_PATTERNS) == (1, 4, 16)
ATT_QBLOCK = 128
ROPE_THETA = 10000.0
IN_COLS = 5 * HG_WIDTH + 3 * ATT_WIDTH
N_EXPERTS = 32
TOP_K = 4
D_EXPERT = D_MODEL
SWIGLU_LIMIT = 7.0
SWIGLU_ALPHA = 1.702
EPS = 1e-6
NEG = -1e30

COL_HQ, COL_HF_FWD, COL_HF_BWD, COL_HI, COL_HG, COL_AQ, COL_AK, COL_AV = range(8)

TOKEN_TILE = 1024
INPROJ_TILE = 1024
HGRN_TILE = 512
ATT_TILE = 1024
MOE_BLOCK_ROWS = 512
DISPATCH_TOKENS = 2048
COMBINE_TOKENS = 256
LANES = 128
ROW_TILES = D_MODEL // LANES
VMEM_LIMIT = 60 * 1024 * 1024


def _dot(a, b):
    return jnp.dot(a, b, preferred_element_type=F32)


def _dot_nt(a, b):
    return lax.dot_general(a, b, (((1,), (1,)), ((), ())), preferred_element_type=F32)


def _dot_tn(a, b):
    return lax.dot_general(a, b, (((0,), (0,)), ((), ())), preferred_element_type=F32)


def _sigmoid(x):
    return 0.5 * jnp.tanh(0.5 * x) + 0.5


def _abs_max(x):
    return jnp.max(jnp.max(jnp.abs(x), axis=0, keepdims=True), axis=1, keepdims=True)


def _fp8_scale(amax):
    return jnp.where(amax > 0.0, FP8_TARGET / amax, 1.0)


def _rows_to_tiles(dst_ref, x):
    for c in range(ROW_TILES):
        dst_ref[pl.ds(c, x.shape[0], stride=ROW_TILES), :] = x[:, c * LANES:(c + 1) * LANES]


def _tiles_to_rows(src, rows, first_row=0):
    return [src[pl.ds(first_row * ROW_TILES + c, rows, stride=ROW_TILES), :] for c in range(ROW_TILES)]


def _head_norm_rope(p, gain, cos, sin_signed, scale):
    lane = lax.broadcasted_iota(jnp.int32, (p.shape[0], LANES), 1)
    low = lane < ATT_DIM
    first_half = (lane % ATT_DIM) < (ATT_DIM // 2)
    outs = []
    for t in range(ATT_WIDTH // LANES):
        blk = p[:, t * LANES:(t + 1) * LANES]
        sq = blk * blk
        s_low = jnp.sum(jnp.where(low, sq, 0.0), axis=-1, keepdims=True)
        s_high = jnp.sum(jnp.where(low, 0.0, sq), axis=-1, keepdims=True)
        r = jnp.where(low, lax.rsqrt(s_low * (1.0 / ATT_DIM) + EPS),
                      lax.rsqrt(s_high * (1.0 / ATT_DIM) + EPS))
        y = blk * r * gain
        partner = jnp.where(first_half, pltpu.roll(y, LANES - ATT_DIM // 2, axis=1),
                            pltpu.roll(y, ATT_DIM // 2, axis=1))
        outs.append((y * cos + partner * sin_signed) * scale)
    return jnp.concatenate(outs, axis=1)


def _inproj_kernel(x_ref, pos_ref, inv_ref, g1_ref, w_ref, qg_ref, kg_ref,
                   out_ref, d4_ref, d16_ref, stage_ref, stage2_ref):
    x = x_ref[...]
    ms = jnp.mean(x * x, axis=-1, keepdims=True)
    h = (x * lax.rsqrt(ms + EPS) * g1_ref[...]).astype(BF16)
    ang = pos_ref[...].astype(F32) * inv_ref[...]
    lane = lax.broadcasted_iota(jnp.int32, ang.shape, 1)
    cos = jnp.cos(ang)
    sin_signed = jnp.where((lane % ATT_DIM) < (ATT_DIM // 2), -jnp.sin(ang), jnp.sin(ang))
    order = (COL_AQ, COL_AK, COL_AV, COL_HQ, COL_HF_FWD, COL_HF_BWD, COL_HI, COL_HG)
    nxt = _dot(h, w_ref[:, order[0] * 512:(order[0] + 1) * 512])
    for pos, j in enumerate(order):
        p = nxt
        if pos + 1 < len(order):
            jn = order[pos + 1]
            nxt = _dot(h, w_ref[:, jn * 512:(jn + 1) * 512])
        if j == COL_AQ:
            p = _head_norm_rope(p, qg_ref[...], cos, sin_signed, ATT_DIM ** -0.5)
        elif j == COL_AK:
            p = _head_norm_rope(p, kg_ref[...], cos, sin_signed, 1.0)
        out_ref[:, j * 512:(j + 1) * 512] = p.astype(BF16)
        if j >= COL_AQ:
            rows4, rows16 = x.shape[0] // 4, x.shape[0] // 16
            for c in range(ATT_WIDTH // LANES):
                cols = slice((j - COL_AQ) * ATT_WIDTH + c * LANES, (j - COL_AQ) * ATT_WIDTH + (c + 1) * LANES)
                stage_ref[c] = p[:, c * LANES:(c + 1) * LANES]
                for r4 in range(4):
                    group = stage_ref[c, pl.ds(r4, rows4, stride=4), :]
                    d4_ref[0, r4, :, cols] = group.astype(BF16)
                    stage2_ref[c, r4 * rows4:(r4 + 1) * rows4, :] = group
                for r4 in range(4):
                    for m in range(4):
                        d16_ref[0, r4 + 4 * m, :, cols] = (
                            stage2_ref[c, pl.ds(r4 * rows4 + m, rows16, stride=4), :].astype(BF16))


def _inproj(x2d, pos_col, inv_tab, g1, w_in_bf16, qg, kg, batch, seq):
    n = x2d.shape[0]
    t = INPROJ_TILE
    nt = seq // t
    const = lambda i: (0, 0)
    qkv = 3 * ATT_WIDTH

    def residue_major(dil):
        spec = pl.BlockSpec((1, dil, t // dil, qkv), lambda i: (i // nt, 0, i % nt, 0))
        return spec, jax.ShapeDtypeStruct((batch, dil, seq // dil, qkv), BF16)

    spec4, shape4 = residue_major(4)
    spec16, shape16 = residue_major(16)
    return pl.pallas_call(
        _inproj_kernel,
        grid=(n // t,),
        in_specs=[
            pl.BlockSpec((t, D_MODEL), lambda i: (i, 0)),
            pl.BlockSpec((t, 1), lambda i: (i, 0)),
            pl.BlockSpec((1, LANES), const),
            pl.BlockSpec((1, D_MODEL), const),
            pl.BlockSpec((D_MODEL, IN_COLS), const, pipeline_mode=pl.Buffered(1)),
            pl.BlockSpec((1, LANES), const),
            pl.BlockSpec((1, LANES), const),
        ],
        out_specs=[pl.BlockSpec((t, IN_COLS), lambda i: (i, 0)), spec4, spec16],
        out_shape=[jax.ShapeDtypeStruct((n, IN_COLS), BF16), shape4, shape16],
        scratch_shapes=[pltpu.VMEM((ATT_WIDTH // LANES, t, LANES), F32)] * 2,
        compiler_params=pltpu.CompilerParams(
            dimension_semantics=("parallel",), vmem_limit_bytes=VMEM_LIMIT),
        name="inproj",
    )(x2d, pos_col, inv_tab, g1, w_in_bf16, qg, kg)


def _hgrn_direction(q, z, v, lb, state_t, reverse):
    c = HG_CHUNK
    t = q.shape[0]
    n = t // c
    row = lax.broadcasted_iota(jnp.int32, (c, c), 0)
    col = lax.broadcasted_iota(jnp.int32, (c, c), 1)
    mask = (row <= col) if reverse else (row >= col)
    tri = jnp.where(mask, 1.0, 0.0).astype(BF16)
    last_row = 0 if reverse else c - 1

    z = z.astype(F32)
    q = q.astype(F32)
    sg = _sigmoid(z)
    f = lb + (1.0 - lb) * sg
    k = (1.0 - lb) * (1.0 - sg)
    lf = jnp.log(f)
    lf_hi = lf.astype(BF16)
    lf_lo = (lf - lf_hi.astype(F32)).astype(BF16)
    chunks = [slice(j * c, (j + 1) * c) for j in range(n)]
    b = jnp.concatenate([_dot(tri, lf_hi[rs]) + _dot(tri, lf_lo[rs]) for rs in chunks], axis=0)
    b_last = b.reshape(n, c, HG_DIM)[:, last_row:last_row + 1, :]
    decay = jnp.exp(b_last)
    qt = (q * _sigmoid(q) * jnp.exp(b)).astype(BF16)
    kt_f32 = k * jnp.exp(-b)
    kt = kt_f32.astype(BF16)
    kd = (kt_f32.reshape(n, c, HG_DIM) * decay).reshape(t, HG_DIM).astype(BF16)

    outs, updates = [], []
    for rs in chunks:
        a = jnp.where(mask, _dot_nt(qt[rs], kt[rs]), 0.0)
        outs.append(_dot(a.astype(BF16), v[rs]))
        updates.append(_dot_tn(v[rs], kd[rs]))
    for j in (reversed(range(n)) if reverse else range(n)):
        outs[j] = outs[j] + _dot_nt(qt[chunks[j]], state_t.astype(BF16))
        state_t = state_t * decay[j] + updates[j]
    return jnp.concatenate(outs, axis=0), state_t


def _hgrn_kernel(qf_ref, zf_ref, vf_ref, qb_ref, zb_ref, vb_ref, lb_ref,
                 of_ref, ob_ref, sf_ref, sb_ref):
    @pl.when(pl.program_id(1) == 0)
    def _():
        sf_ref[...] = jnp.zeros_like(sf_ref)
        sb_ref[...] = jnp.zeros_like(sb_ref)

    for h in range(HG_HEADS):
        cols = slice(h * HG_DIM, (h + 1) * HG_DIM)
        o, sf = _hgrn_direction(qf_ref[:, cols], zf_ref[:, cols], vf_ref[:, cols],
                                lb_ref[h], sf_ref[h], False)
        of_ref[:, cols] = o.astype(of_ref.dtype)
        sf_ref[h] = sf
        o, sb = _hgrn_direction(qb_ref[:, cols], zb_ref[:, cols], vb_ref[:, cols],
                                lb_ref[HG_HEADS + h], sb_ref[h], True)
        ob_ref[:, cols] = o.astype(ob_ref.dtype)
        sb_ref[h] = sb


def _hgrn(proj, lbs, batch, seq):
    n = proj.shape[0]
    t = HGRN_TILE
    nblk = seq // t

    def fwd(colblk):
        return pl.BlockSpec((t, HG_WIDTH), lambda b, i: (b * nblk + i, colblk))

    def bwd(colblk):
        return pl.BlockSpec((t, HG_WIDTH), lambda b, i: (b * nblk + nblk - 1 - i, colblk))

    return pl.pallas_call(
        _hgrn_kernel,
        grid=(batch, nblk),
        in_specs=[
            fwd(COL_HQ), fwd(COL_HF_FWD), fwd(COL_HI),
            bwd(COL_HQ), bwd(COL_HF_BWD), bwd(COL_HI),
            pl.BlockSpec((2 * HG_HEADS, 1, HG_DIM), lambda b, i: (0, 0, 0)),
        ],
        out_specs=[fwd(0), bwd(0)],
        out_shape=[jax.ShapeDtypeStruct((n, HG_WIDTH), BF16)] * 2,
        scratch_shapes=[pltpu.VMEM((HG_HEADS, HG_DIM, HG_DIM), F32)] * 2,
        compiler_params=pltpu.CompilerParams(
            dimension_semantics=("parallel", "arbitrary"), vmem_limit_bytes=VMEM_LIMIT),
        name="hgrn",
    )(proj, proj, proj, proj, proj, proj, lbs)


def _attn_kernel(q_ref, kc_ref, kp_ref, kn_ref, vc_ref, vp_ref, vn_ref, o_ref, l_ref,
                 kw_ref, vw_ref, *, tq, length):
    n = pl.program_id(2)
    half = ATT_HALF
    kw_ref[0:half, :] = kp_ref[...]
    kw_ref[half:half + tq, :] = kc_ref[...]
    kw_ref[half + tq:, :] = kn_ref[...]
    vw_ref[0:half, :] = vp_ref[...]
    vw_ref[half:half + tq, :] = vc_ref[...]
    vw_ref[half + tq:, :] = vn_ref[...]

    qb_rows = ATT_QBLOCK
    win = qb_rows + 2 * half
    i_idx = lax.broadcasted_iota(jnp.int32, (qb_rows, win), 0)
    j_idx = lax.broadcasted_iota(jnp.int32, (qb_rows, win), 1)
    band = (j_idx >= i_idx) & (j_idx <= i_idx + 2 * half)

    pairs = [slice(p * LANES, (p + 1) * LANES) for p in range(ATT_HEADS // 2)]
    head_lane = lax.broadcasted_iota(jnp.int32, (qb_rows, LANES), 1)
    even_half = head_lane < ATT_DIM
    keep_even = jnp.where(lax.broadcasted_iota(jnp.int32, (1, LANES), 1) < ATT_DIM, 1.0, 0.0).astype(BF16)
    keep_odd = (1.0 - keep_even.astype(F32)).astype(BF16)

    def masked_scores(r0):
        base = n * tq + r0 - half
        valid = band & (j_idx >= -base) & (j_idx < length - base)
        q = q_ref[r0:r0 + qb_rows, :]
        kw = kw_ref[r0:r0 + win, :]
        out = []
        for cs in pairs:
            for keep in (keep_even, keep_odd):
                out.append(jnp.where(valid, _dot_nt(q[:, cs] * keep, kw[:, cs]), NEG))
        return out

    blocks = list(range(0, tq, qb_rows))
    nxt = masked_scores(blocks[0])
    for pos, r0 in enumerate(blocks):
        scores = nxt
        if pos + 1 < len(blocks):
            nxt = masked_scores(blocks[pos + 1])
        vw = vw_ref[r0:r0 + win, :]
        maxes = [jnp.max(s, axis=-1, keepdims=True) for s in scores]
        probs = [jnp.exp(s - m) for s, m in zip(scores, maxes)]
        dens = [jnp.sum(p, axis=-1, keepdims=True) for p in probs]
        outs = []
        for p, cs in enumerate(pairs):
            pv_even = _dot(probs[2 * p].astype(BF16), vw[:, cs])
            pv_odd = _dot(probs[2 * p + 1].astype(BF16), vw[:, cs])
            outs.append(jnp.where(even_half, pv_even, pv_odd)
                        / jnp.where(even_half, dens[2 * p], dens[2 * p + 1]))
        o_ref[r0:r0 + qb_rows, :] = jnp.concatenate(outs, axis=1).astype(o_ref.dtype)
        lse = jnp.zeros((qb_rows, LANES), F32)
        for h, (m, den) in enumerate(zip(maxes, dens)):
            lse = jnp.where(head_lane == h, m + jnp.log(den), lse)
        l_ref[r0:r0 + qb_rows, :] = lse


def _attention(qkv, col0):
    batch, dil, length, _ = qkv.shape
    tq = min(ATT_TILE, length)
    nq = length // tq
    hb = tq // ATT_HALF
    n_hblk = length // ATT_HALF

    def cur(col):
        return pl.BlockSpec((None, None, tq, ATT_WIDTH), lambda b, r, n: (b, r, n, col))

    def prev(col):
        return pl.BlockSpec((None, None, ATT_HALF, ATT_WIDTH),
                            lambda b, r, n: (b, r, jnp.maximum(n * hb - 1, 0), col))

    def nxt(col):
        return pl.BlockSpec((None, None, ATT_HALF, ATT_WIDTH),
                            lambda b, r, n: (b, r, jnp.minimum((n + 1) * hb, n_hblk - 1), col))

    out_spec = pl.BlockSpec((None, None, tq, ATT_WIDTH), lambda b, r, n: (b, r, n, 0))
    return pl.pallas_call(
        functools.partial(_attn_kernel, tq=tq, length=length),
        grid=(batch, dil, nq),
        in_specs=[cur(col0), cur(col0 + 1), prev(col0 + 1), nxt(col0 + 1),
                  cur(col0 + 2), prev(col0 + 2), nxt(col0 + 2)],
        out_specs=[out_spec, pl.BlockSpec((None, None, tq, LANES), lambda b, r, n: (b, r, n, 0))],
        out_shape=[jax.ShapeDtypeStruct((batch, dil, length, ATT_WIDTH), BF16),
                   jax.ShapeDtypeStruct((batch, dil, length, LANES), F32)],
        scratch_shapes=[pltpu.VMEM((tq + 2 * ATT_HALF, ATT_WIDTH), BF16)] * 2,
        compiler_params=pltpu.CompilerParams(
            dimension_semantics=("parallel", "parallel", "parallel"), vmem_limit_bytes=VMEM_LIMIT),
        name=f"attn_d{dil}",
    )(qkv, qkv, qkv, qkv, qkv, qkv, qkv)


def _token_major(src_ref, stage_ref, tmp_ref):
    dil, rows = src_ref.shape[1], src_ref.shape[2]
    nc = src_ref.shape[3] // LANES
    for c in range(nc):
        cols = slice(c * LANES, (c + 1) * LANES)
        if dil == 4:
            for r in range(dil):
                stage_ref[c, pl.ds(r, rows, stride=dil), :] = src_ref[0, r, :, cols].astype(F32)
        else:
            group = 4 * rows
            for r4 in range(4):
                for m in range(4):
                    tmp_ref[c, pl.ds(r4 * group + m, rows, stride=4), :] = (
                        src_ref[0, r4 + 4 * m, :, cols].astype(F32))
            for r4 in range(4):
                stage_ref[c, pl.ds(r4, group, stride=4), :] = tmp_ref[c, r4 * group:(r4 + 1) * group, :]
    return jnp.concatenate([stage_ref[c] for c in range(nc)], axis=1)


def _outproj_kernel(of_ref, ob_ref, hg_ref, o1_ref, o2_ref, o3_ref, l1_ref, l2_ref, l3_ref,
                    x_ref, og_ref, w_ref, g2_ref, wrh_ref, wrl_ref, br_ref,
                    x2_ref, xn_ref, gate_ref, idx_ref, st_o2, st_o3, st_l2, st_l3, st_tmp_o, st_tmp_l):
    o = of_ref[...].astype(F32) + ob_ref[...].astype(F32)
    hg = hg_ref[...].astype(F32)
    parts = []
    for h in range(HG_HEADS):
        blk = o[:, h * HG_DIM:(h + 1) * HG_DIM]
        ms = jnp.mean(blk * blk, axis=-1, keepdims=True)
        parts.append(blk * lax.rsqrt(ms + EPS) * og_ref[...])
    o_hg = jnp.concatenate(parts, axis=1) * (hg * _sigmoid(hg))

    l1 = l1_ref[...]
    l2 = _token_major(l2_ref, st_l2, st_tmp_l)
    l3 = _token_major(l3_ref, st_l3, st_tmp_l)
    mx = jnp.maximum(jnp.maximum(l1, l2), l3)
    e1, e2, e3 = jnp.exp(l1 - mx), jnp.exp(l2 - mx), jnp.exp(l3 - mx)
    den = e1 + e2 + e3
    er = lax.broadcasted_iota(jnp.int32, (LANES, ATT_WIDTH), 0)
    ec = lax.broadcasted_iota(jnp.int32, (LANES, ATT_WIDTH), 1)
    expand = jnp.where(er == ec // ATT_DIM, 1.0, 0.0).astype(BF16)

    def per_lane(w):
        return _dot(w.astype(BF16), expand)

    o_att = (per_lane(e1 / den) * o1_ref[...].astype(F32)
             + per_lane(e2 / den) * _token_major(o2_ref, st_o2, st_tmp_o)
             + per_lane(e3 / den) * _token_major(o3_ref, st_o3, st_tmp_o))

    y = _dot(o_hg.astype(BF16), w_ref[0:HG_WIDTH, :]) + _dot(o_att.astype(BF16), w_ref[HG_WIDTH:, :])
    x2 = x_ref[...] + y
    x2_ref[...] = x2

    ms = jnp.mean(x2 * x2, axis=-1, keepdims=True)
    unit = x2 * lax.rsqrt(ms + EPS)
    xn = unit * g2_ref[0:1, :]
    _rows_to_tiles(xn_ref, unit * g2_ref[1:2, :])
    xn_hi = xn.astype(BF16)
    xn_lo = (xn - xn_hi.astype(F32)).astype(BF16)
    logits = (_dot(xn_hi, wrh_ref[...]) + _dot(xn_lo, wrh_ref[...]) + _dot(xn_hi, wrl_ref[...])
              + br_ref[...])

    lane = lax.broadcasted_iota(jnp.int32, logits.shape, 1)
    lane_f = lane.astype(F32)
    work = jnp.where(lane < N_EXPERTS, logits, -jnp.inf)
    vals, idxs = [], []
    for _ in range(TOP_K):
        m = jnp.max(work, axis=-1, keepdims=True)
        idx = jnp.min(jnp.where(work == m, lane_f, float(LANES)), axis=-1, keepdims=True)
        vals.append(m)
        idxs.append(idx)
        work = jnp.where(lane_f == idx, -jnp.inf, work)
    es = [jnp.exp(v - vals[0]) for v in vals]
    den = es[0] + es[1] + es[2] + es[3]
    gate_out = jnp.zeros(logits.shape, F32)
    idx_out = jnp.zeros(logits.shape, F32)
    for k in range(TOP_K):
        gate_out = jnp.where(lane == k, es[k] / den, gate_out)
        idx_out = jnp.where(lane == k, idxs[k], idx_out)
    gate_ref[...] = gate_out
    idx_ref[...] = idx_out.astype(jnp.int32)


def _outproj(o_f, o_b, proj, atts, lses, x2d, og, w_out_bf16, g2, wr_hi, wr_lo, br, seq):
    n = x2d.shape[0]
    t = TOKEN_TILE
    nt = seq // t
    row = lambda i: (i, 0)
    const = lambda i: (0, 0)
    half = pl.BlockSpec((t, 512), row)

    def residue_major(dil, width):
        return pl.BlockSpec((1, dil, t // dil, width), lambda i: (i // nt, 0, i % nt, 0))

    rm4, rm16 = residue_major(4, ATT_WIDTH), residue_major(16, ATT_WIDTH)
    lse, lse4, lse16 = pl.BlockSpec((t, LANES), row), residue_major(4, LANES), residue_major(16, LANES)
    wide, narrow = pltpu.VMEM((ATT_WIDTH // LANES, t, LANES), F32), pltpu.VMEM((1, t, LANES), F32)
    return pl.pallas_call(
        _outproj_kernel,
        grid=(n // t,),
        in_specs=[
            half, half, pl.BlockSpec((t, 512), lambda i: (i, COL_HG)),
            half, rm4, rm16, lse, lse4, lse16,
            pl.BlockSpec((t, D_MODEL), row),
            pl.BlockSpec((1, HG_DIM), const),
            pl.BlockSpec((D_MODEL, D_MODEL), const),
            pl.BlockSpec((2, D_MODEL), const),
            pl.BlockSpec((D_MODEL, LANES), const),
            pl.BlockSpec((D_MODEL, LANES), const),
            pl.BlockSpec((1, LANES), const),
        ],
        out_specs=[pl.BlockSpec((t, D_MODEL), row), pl.BlockSpec((t * ROW_TILES, LANES), row),
                   pl.BlockSpec((t, LANES), row), pl.BlockSpec((t, LANES), row)],
        out_shape=[jax.ShapeDtypeStruct((n, D_MODEL), F32), jax.ShapeDtypeStruct((n * ROW_TILES, LANES), F32),
                   jax.ShapeDtypeStruct((n, LANES), F32), jax.ShapeDtypeStruct((n, LANES), jnp.int32)],
        scratch_shapes=[wide, wide, narrow, narrow, wide, narrow],
        compiler_params=pltpu.CompilerParams(
            dimension_semantics=("parallel",), vmem_limit_bytes=VMEM_LIMIT),
        name="outproj",
    )(o_f, o_b, proj, *atts, *lses, x2d, og, w_out_bf16, g2, wr_hi, wr_lo, br)


def _moe_kernel(bexp_ref, bfirst_ref, bslot_ref, bnext_ref, bvalid_ref, nused_ref,
                xs_ref, xinv_ref, wu_hbm, bg_ref, bl_ref, wd_hbm, bd_ref,
                y_ref, wu_buf, wd_buf, wg_s, wl_s, wd_s, up_unscale_s, down_unscale_s, sems):
    i = pl.program_id(0)

    def weight_copies(expert, slot):
        return (pltpu.make_async_copy(wu_hbm.at[expert], wu_buf.at[slot], sems.at[0, slot]),
                pltpu.make_async_copy(wd_hbm.at[expert], wd_buf.at[slot], sems.at[1, slot]))

    @pl.when(bfirst_ref[i] == 1)
    def _():
        slot = bslot_ref[i]

        @pl.when(i == 0)
        def _():
            for cp in weight_copies(bexp_ref[0], 0):
                cp.start()

        for cp in weight_copies(bexp_ref[i], slot):
            cp.wait()

        @pl.when(bnext_ref[i] >= 0)
        def _():
            for cp in weight_copies(bnext_ref[i], 1 - slot):
                cp.start()

        su = _fp8_scale(_abs_max(wu_buf[slot]))
        sd = _fp8_scale(_abs_max(wd_buf[slot]))
        up_unscale_s[...] = xinv_ref[...] / su
        down_unscale_s[...] = (1.0 / ACT_FP8_SCALE) / sd
        r = lax.broadcasted_iota(jnp.int32, (2 * LANES, 2 * LANES), 0)
        c = lax.broadcasted_iota(jnp.int32, (2 * LANES, 2 * LANES), 1)
        src = jnp.where(c < LANES, 2 * c, 2 * (c - LANES) + 1)
        perm = jnp.where(r == src, 1.0, 0.0).astype(BF16)
        rows = 256
        for rb in range(D_MODEL // rows):
            rs = slice(rb * rows, (rb + 1) * rows)
            for cb in range(D_EXPERT // LANES):
                w = wu_buf[slot, rs, cb * 2 * LANES:(cb + 1) * 2 * LANES].astype(BF16)
                split = (_dot(w, perm) * su).astype(F8)
                wg_s[rs, cb * LANES:(cb + 1) * LANES] = split[:, :LANES]
                wl_s[rs, cb * LANES:(cb + 1) * LANES] = split[:, LANES:]
        wd_s[...] = (wd_buf[slot] * sd).astype(F8)

    def expert_mlp(rows):
        x8 = jnp.concatenate(_tiles_to_rows(xs_ref, rows), axis=1).astype(F8)
        hglu = _dot(x8, wg_s[...]) * up_unscale_s[...] + bg_ref[0]
        hlin = _dot(x8, wl_s[...]) * up_unscale_s[...] + bl_ref[0]
        glu = jnp.minimum(hglu, SWIGLU_LIMIT)
        lin = jnp.clip(hlin, -SWIGLU_LIMIT, SWIGLU_LIMIT)
        act = glu * _sigmoid(SWIGLU_ALPHA * glu) * (lin + 1.0)
        y = _dot((act * ACT_FP8_SCALE).astype(F8), wd_s[...]) * down_unscale_s[...] + bd_ref[0]
        _rows_to_tiles(y_ref, y)

    valid = bvalid_ref[i]
    half = MOE_BLOCK_ROWS // 2

    @pl.when(valid > half)
    def _():
        expert_mlp(MOE_BLOCK_ROWS)

    @pl.when((valid > 0) & (valid <= half))
    def _():
        expert_mlp(half)
        y_ref[half * ROW_TILES:, :] = jnp.zeros((half * ROW_TILES, LANES), F32)

    @pl.when(valid == 0)
    def _():
        y_ref[...] = jnp.zeros_like(y_ref)


def _moe(block_exp, block_first, block_slot, block_next, block_valid, n_used,
         xs, x_unscale, w_up, b_glu, b_lin, w_down, b_down):
    p_rows = xs.shape[0] // ROW_TILES
    bm = MOE_BLOCK_ROWS
    nb = p_rows // bm
    exp3 = lambda i, be, *_: (be[i], 0, 0)
    grid_spec = pltpu.PrefetchScalarGridSpec(
        num_scalar_prefetch=6,
        grid=(nb,),
        in_specs=[
            pl.BlockSpec((bm * ROW_TILES, LANES), lambda i, be, bf, bs, bn, bv, nu: (jnp.minimum(i, nu[0] - 1), 0)),
            pl.BlockSpec((1, 1), lambda i, *_: (0, 0)),
            pl.BlockSpec(memory_space=pl.ANY),
            pl.BlockSpec((1, 1, D_EXPERT), exp3),
            pl.BlockSpec((1, 1, D_EXPERT), exp3),
            pl.BlockSpec(memory_space=pl.ANY),
            pl.BlockSpec((1, 1, D_MODEL), exp3),
        ],
        out_specs=pl.BlockSpec((bm * ROW_TILES, LANES), lambda i, *_: (i, 0)),
        scratch_shapes=[pltpu.VMEM((2, D_MODEL, 2 * D_EXPERT), F32), pltpu.VMEM((2, D_EXPERT, D_MODEL), F32),
                        pltpu.VMEM((D_MODEL, D_EXPERT), F8), pltpu.VMEM((D_MODEL, D_EXPERT), F8),
                        pltpu.VMEM((D_EXPERT, D_MODEL), F8), pltpu.VMEM((1, 1), F32), pltpu.VMEM((1, 1), F32),
                        pltpu.SemaphoreType.DMA((2, 2))],
    )
    return pl.pallas_call(
        _moe_kernel,
        grid_spec=grid_spec,
        out_shape=jax.ShapeDtypeStruct((p_rows * ROW_TILES, LANES), F32),
        compiler_params=pltpu.CompilerParams(
            dimension_semantics=("arbitrary",), vmem_limit_bytes=VMEM_LIMIT),
        name="moe",
    )(block_exp, block_first, block_slot, block_next, block_valid, n_used,
      xs, x_unscale, w_up, b_glu, b_lin, w_down, b_down)


def _dispatch_kernel(dest_ref, zstart_ref, xn_ref, xs_hbm, zero_ref, sem, zsem):
    i = pl.program_id(0)
    bm = MOE_BLOCK_ROWS
    tokens = xn_ref.shape[0] // ROW_TILES

    def zero_copy(e):
        start = pl.multiple_of(zstart_ref[e] * ROW_TILES, bm * ROW_TILES)
        return pltpu.make_async_copy(zero_ref, xs_hbm.at[pl.ds(start, bm * ROW_TILES)], zsem)

    @pl.when(i == 0)
    def _():
        zero_ref[...] = jnp.zeros_like(zero_ref)
        for e in range(N_EXPERTS):
            @pl.when(zstart_ref[e] >= 0)
            def _():
                zero_copy(e).start()
        for e in range(N_EXPERTS):
            @pl.when(zstart_ref[e] >= 0)
            def _():
                zero_copy(e).wait()

    t0 = i * tokens

    def body(j, carry):
        src = xn_ref.at[pl.ds(pl.multiple_of(j * ROW_TILES, ROW_TILES), ROW_TILES)]
        for k in range(TOP_K):
            d = pl.multiple_of(dest_ref[k * (dest_ref.shape[0] // TOP_K) + t0 + j] * ROW_TILES, ROW_TILES)
            pltpu.make_async_copy(src, xs_hbm.at[pl.ds(d, ROW_TILES)], sem).start(priority=k % 2)
        return carry

    lax.fori_loop(0, tokens, body, 0, unroll=8)
    for k in range(TOP_K):
        pltpu.make_async_copy(xn_ref, xs_hbm.at[pl.ds(0, tokens * ROW_TILES)], sem).wait()


def _dispatch(dest, zstart, xn, p_rows):
    n = xn.shape[0] // ROW_TILES
    t = DISPATCH_TOKENS
    grid_spec = pltpu.PrefetchScalarGridSpec(
        num_scalar_prefetch=2,
        grid=(n // t,),
        in_specs=[pl.BlockSpec((t * ROW_TILES, LANES), lambda i, d, z: (i, 0))],
        out_specs=pl.BlockSpec(memory_space=pl.ANY),
        scratch_shapes=[pltpu.VMEM((MOE_BLOCK_ROWS * ROW_TILES, LANES), F32),
                        pltpu.SemaphoreType.DMA, pltpu.SemaphoreType.DMA],
    )
    return pl.pallas_call(
        _dispatch_kernel,
        grid_spec=grid_spec,
        out_shape=jax.ShapeDtypeStruct((p_rows * ROW_TILES, LANES), F32),
        compiler_params=pltpu.CompilerParams(
            dimension_semantics=("arbitrary",), vmem_limit_bytes=VMEM_LIMIT),
        name="dispatch",
    )(dest, zstart, xn)


def _combine_kernel(dest_ref, y_hbm, x2_ref, gate_ref, out_ref, buf_ref, sems):
    i = pl.program_id(0)
    tc = COMBINE_TOKENS

    def issue(step, slot):
        t0 = step * tc

        def body(j, carry):
            for k in range(TOP_K):
                d = pl.multiple_of(dest_ref[k * (dest_ref.shape[0] // TOP_K) + t0 + j] * ROW_TILES, ROW_TILES)
                r = pl.multiple_of((k * tc + j) * ROW_TILES, ROW_TILES)
                pltpu.make_async_copy(y_hbm.at[pl.ds(d, ROW_TILES)], buf_ref.at[slot, pl.ds(r, ROW_TILES)],
                                      sems.at[slot]).start(priority=k % 2)
            return carry

        lax.fori_loop(0, tc, body, 0, unroll=8)

    @pl.when(i == 0)
    def _():
        issue(0, 0)

    @pl.when(i + 1 < pl.num_programs(0))
    def _():
        issue(i + 1, (i + 1) % 2)

    slot = i % 2
    pltpu.make_async_copy(y_hbm.at[pl.ds(0, TOP_K * tc * ROW_TILES)], buf_ref.at[slot], sems.at[slot]).wait()
    rows = buf_ref.at[slot]
    sub = 64
    for r0 in range(0, tc, sub):
        gate = gate_ref[r0:r0 + sub, :]
        gates = [jnp.broadcast_to(gate[:, k:k + 1], (sub, LANES)) for k in range(TOP_K)]
        for c in range(ROW_TILES):
            acc = x2_ref[r0:r0 + sub, c * LANES:(c + 1) * LANES]
            for k in range(TOP_K):
                acc = acc + gates[k] * rows[pl.ds((k * tc + r0) * ROW_TILES + c, sub, stride=ROW_TILES), :]
            out_ref[r0:r0 + sub, c * LANES:(c + 1) * LANES] = acc


def _combine(dest, y, x2, gates):
    n = x2.shape[0]
    tc = COMBINE_TOKENS
    grid_spec = pltpu.PrefetchScalarGridSpec(
        num_scalar_prefetch=1,
        grid=(n // tc,),
        in_specs=[pl.BlockSpec(memory_space=pl.ANY),
                  pl.BlockSpec((tc, D_MODEL), lambda i, d: (i, 0)),
                  pl.BlockSpec((tc, LANES), lambda i, d: (i, 0))],
        out_specs=pl.BlockSpec((tc, D_MODEL), lambda i, d: (i, 0)),
        scratch_shapes=[pltpu.VMEM((2, TOP_K * tc * ROW_TILES, LANES), F32), pltpu.SemaphoreType.DMA((2,))],
    )
    return pl.pallas_call(
        _combine_kernel,
        grid_spec=grid_spec,
        out_shape=jax.ShapeDtypeStruct((n, D_MODEL), F32),
        compiler_params=pltpu.CompilerParams(
            dimension_semantics=("arbitrary",), vmem_limit_bytes=VMEM_LIMIT),
        name="combine",
    )(dest, y, x2, gates)


def _route(top_idx):
    n = top_idx.shape[0]
    a = n * TOP_K
    bm = MOE_BLOCK_ROWS
    nb = a // bm + N_EXPERTS
    experts = jnp.arange(N_EXPERTS, dtype=jnp.int32)[None, :]
    picked = [top_idx[:, k:k + 1] == experts for k in range(TOP_K)]
    hits = sum(p.astype(jnp.int32) for p in picked)
    csum = jnp.cumsum(hits, axis=0)
    counts = csum[-1]
    padded = ((counts + bm - 1) // bm) * bm
    pad_end = jnp.cumsum(padded)
    pad_start = pad_end - padded
    row_of = csum - hits + pad_start[None, :]
    dest = jnp.concatenate([jnp.sum(jnp.where(p, row_of, 0), axis=1) for p in picked])
    n_used = (pad_end[-1] // bm).astype(jnp.int32)
    blk = jnp.arange(nb, dtype=jnp.int32)
    bexp = jnp.sum((pad_end[None, :] <= (blk * bm)[:, None]).astype(jnp.int32), axis=1)
    bexp = jnp.minimum(bexp, N_EXPERTS - 1)
    bexp = jnp.where(blk < n_used, bexp, bexp[jnp.maximum(n_used - 1, 0)])
    bfirst = jnp.concatenate([jnp.ones((1,), jnp.int32), (bexp[1:] != bexp[:-1]).astype(jnp.int32)])
    bslot = (jnp.cumsum(bfirst) - 1) % 2
    later = jnp.where(bexp[None, :] > bexp[:, None], bexp[None, :], N_EXPERTS)
    bnext = jnp.min(later, axis=1)
    bnext = jnp.where(bnext < N_EXPERTS, bnext, -1).astype(jnp.int32)
    valid_end = jnp.sum(jnp.where(bexp[:, None] == jnp.arange(N_EXPERTS)[None, :], (pad_start + counts)[None, :], 0), axis=1)
    bvalid = jnp.clip(valid_end - blk * bm, 0, bm).astype(jnp.int32)
    zstart = jnp.where(counts > 0, pad_end - bm, -1).astype(jnp.int32)
    return dest.astype(jnp.int32), zstart, (bexp, bfirst, bslot.astype(jnp.int32), bnext, bvalid, n_used.reshape(1))


def kernel(x, positions, norm1_g, w_in, q_norm_g, k_norm_g, hgrn_lower_bounds, hgrn_onorm_g,
           w_out, norm2_g, w_router, b_router, w_up, b_up, w_down, b_down):
    batch, seq, d = x.shape
    n = batch * seq
    depth = norm1_g.shape[0]
    lbs_all = jnp.cumsum(jax.nn.softmax(hgrn_lower_bounds.astype(F32), axis=0), axis=0)
    half = ATT_DIM // 2
    inv = 1.0 / (ROPE_THETA ** (jnp.arange(half, dtype=F32) / half))
    inv_tab = jnp.tile(inv, LANES // half).reshape(1, LANES)
    pos_col = positions.reshape(n, 1)

    x2d = x.reshape(n, d)
    for l in range(depth):
        lbs = lbs_all[l].reshape(2 * HG_HEADS, 1, HG_DIM)
        proj, qkv4, qkv16 = _inproj(
            x2d, pos_col, inv_tab, norm1_g[l].reshape(1, d), w_in[l].astype(BF16),
            jnp.tile(q_norm_g[l], LANES // ATT_DIM).reshape(1, LANES),
            jnp.tile(k_norm_g[l], LANES // ATT_DIM).reshape(1, LANES), batch, seq)
        o_f, o_b = _hgrn(proj, lbs, batch, seq)
        o1, l1 = _attention(proj.reshape(batch, 1, seq, IN_COLS), COL_AQ)
        o4, l4 = _attention(qkv4, 0)
        o16, l16 = _attention(qkv16, 0)
        atts = [o1.reshape(n, ATT_WIDTH), o4, o16]
        lses = [l1.reshape(n, LANES), l4, l16]

        wr = jnp.pad(w_router[l], ((0, 0), (0, LANES - N_EXPERTS)))
        wr_hi = wr.astype(BF16)
        wr_lo = (wr - wr_hi.astype(F32)).astype(BF16)
        br = jnp.pad(b_router[l], (0, LANES - N_EXPERTS)).reshape(1, LANES)
        g2 = norm2_g[l].astype(F32)
        g2_max = jnp.max(jnp.abs(g2))
        x_scale = jnp.where(g2_max > 0.0, ROW_FP8_TARGET / g2_max, 1.0)
        x2, xn, gates, top_idx = _outproj(
            o_f, o_b, proj, atts, lses, x2d, hgrn_onorm_g[l].reshape(1, HG_DIM),
            w_out[l].astype(BF16), jnp.stack([g2, g2 * x_scale]), wr_hi, wr_lo, br, seq)
        dest, zstart, blocks = _route(top_idx[:, :TOP_K])
        xs = _dispatch(dest, zstart, xn, blocks[0].shape[0] * MOE_BLOCK_ROWS)
        y = _moe(*blocks, xs, (1.0 / x_scale).reshape(1, 1), w_up[l],
                 b_up[l][:, 0::2].reshape(N_EXPERTS, 1, D_EXPERT),
                 b_up[l][:, 1::2].reshape(N_EXPERTS, 1, D_EXPERT),
                 w_down[l], b_down[l].reshape(N_EXPERTS, 1, D_MODEL))
        x2d = _combine(dest, y, x2, gates)
    return x2d.reshape(batch, seq, d)
```

```python
import functools

import jax
import jax.numpy as jnp
from jax import lax
from jax.experimental import pallas as pl
from jax.experimental.pallas import tpu as pltpu

F32 = jnp.float32
BF16 = jnp.bfloat16
F8 = jnp.float8_e4m3fn
FP8_TARGET = 240.0
ROW_FP8_TARGET = 8.0
ACT_FP8_SCALE = 4.0

D_MODEL = 1024
HG_HEADS = 4
HG_DIM = 128
HG_WIDTH = HG_HEADS * HG_DIM
HG_CHUNK = 64
ATT_HEADS = 8
ATT_DIM = 64
ATT_WIDTH = ATT_HEADS * ATT_DIM
DILATED_PATTERNS = ((128, 1), (512, 4), (2048, 16))
ATT_HALF = 64
assert all(window // (2 * dil) == ATT_HALF for window, dil in DILATED_PATTERNS)
assert tuple(dil for _, dil in DILATED_PATTERNS) == (1, 4, 16)
ATT_QBLOCK = 128
ROPE_THETA = 10000.0
IN_COLS = 5 * HG_WIDTH + 3 * ATT_WIDTH
N_EXPERTS = 32
TOP_K = 4
D_EXPERT = D_MODEL
SWIGLU_LIMIT = 7.0
SWIGLU_ALPHA = 1.702
EPS = 1e-6
NEG = -1e30

COL_HQ, COL_HF_FWD, COL_HF_BWD, COL_HI, COL_HG, COL_AQ, COL_AK, COL_AV = range(8)

TOKEN_TILE = 1024
INPROJ_TILE = 1024
HGRN_TILE = 512
ATT_TILE = 1024
MOE_BLOCK_ROWS = 512
DISPATCH_TOKENS = 2048
COMBINE_TOKENS = 256
LANES = 128
ROW_TILES = D_MODEL // LANES
VMEM_LIMIT = 60 * 1024 * 1024


def _dot(a, b):
    return jnp.dot(a, b, preferred_element_type=F32)


def _dot_nt(a, b):
    return lax.dot_general(a, b, (((1,), (1,)), ((), ())), preferred_element_type=F32)


def _dot_tn(a, b):
    return lax.dot_general(a, b, (((0,), (0,)), ((), ())), preferred_element_type=F32)


def _sigmoid(x):
    return 0.5 * jnp.tanh(0.5 * x) + 0.5


def _abs_max(x):
    return jnp.max(jnp.max(jnp.abs(x), axis=0, keepdims=True), axis=1, keepdims=True)


def _fp8_scale(amax):
    return jnp.where(amax > 0.0, FP8_TARGET / amax, 1.0)


def _rows_to_tiles(dst_ref, x):
    for c in range(ROW_TILES):
        dst_ref[pl.ds(c, x.shape[0], stride=ROW_TILES), :] = x[:, c * LANES:(c + 1) * LANES]


def _tiles_to_rows(src, rows, first_row=0):
    return [src[pl.ds(first_row * ROW_TILES + c, rows, stride=ROW_TILES), :] for c in range(ROW_TILES)]


def _head_norm_rope(p, gain, cos, sin_signed, scale):
    lane = lax.broadcasted_iota(jnp.int32, (p.shape[0], LANES), 1)
    low = lane < ATT_DIM
    first_half = (lane % ATT_DIM) < (ATT_DIM // 2)
    outs = []
    for t in range(ATT_WIDTH // LANES):
        blk = p[:, t * LANES:(t + 1) * LANES]
        sq = blk * blk
        s_low = jnp.sum(jnp.where(low, sq, 0.0), axis=-1, keepdims=True)
        s_high = jnp.sum(jnp.where(low, 0.0, sq), axis=-1, keepdims=True)
        r = jnp.where(low, lax.rsqrt(s_low * (1.0 / ATT_DIM) + EPS),
                      lax.rsqrt(s_high * (1.0 / ATT_DIM) + EPS))
        y = blk * r * gain
        partner = jnp.where(first_half, pltpu.roll(y, LANES - ATT_DIM // 2, axis=1),
                            pltpu.roll(y, ATT_DIM // 2, axis=1))
        outs.append((y * cos + partner * sin_signed) * scale)
    return jnp.concatenate(outs, axis=1)


def _inproj_kernel(x_ref, pos_ref, inv_ref, g1_ref, w_ref, qg_ref, kg_ref,
                   out_ref, d4_ref, d16_ref, stage_ref, stage2_ref):
    x = x_ref[...]
    ms = jnp.mean(x * x, axis=-1, keepdims=True)
    h = (x * lax.rsqrt(ms + EPS) * g1_ref[...]).astype(BF16)
    ang = pos_ref[...].astype(F32) * inv_ref[...]
    lane = lax.broadcasted_iota(jnp.int32, ang.shape, 1)
    cos = jnp.cos(ang)
    sin_signed = jnp.where((lane % ATT_DIM) < (ATT_DIM // 2), -jnp.sin(ang), jnp.sin(ang))
    order = (COL_AQ, COL_AK, COL_AV, COL_HQ, COL_HF_FWD, COL_HF_BWD, COL_HI, COL_HG)
    nxt = _dot(h, w_ref[:, order[0] * 512:(order[0] + 1) * 512])
    for pos, j in enumerate(order):
        p = nxt
        if pos + 1 < len(order):
            jn = order[pos + 1]
            nxt = _dot(h, w_ref[:, jn * 512:(jn + 1) * 512])
        if j == COL_AQ:
            p = _head_norm_rope(p, qg_ref[...], cos, sin_signed, ATT_DIM ** -0.5)
        elif j == COL_AK:
            p = _head_norm_rope(p, kg_ref[...], cos, sin_signed, 1.0)
        out_ref[:, j * 512:(j + 1) * 512] = p.astype(BF16)
        if j >= COL_AQ:
            rows4, rows16 = x.shape[0] // 4, x.shape[0] // 16
            for c in range(ATT_WIDTH // LANES):
                cols = slice((j - COL_AQ) * ATT_WIDTH + c * LANES, (j - COL_AQ) * ATT_WIDTH + (c + 1) * LANES)
                stage_ref[c] = p[:, c * LANES:(c + 1) * LANES]
                for r4 in range(4):
                    group = stage_ref[c, pl.ds(r4, rows4, stride=4), :]
                    d4_ref[0, r4, :, cols] = group.astype(BF16)
                    stage2_ref[c, r4 * rows4:(r4 + 1) * rows4, :] = group
                for r4 in range(4):
                    for m in range(4):
                        d16_ref[0, r4 + 4 * m, :, cols] = (
                            stage2_ref[c, pl.ds(r4 * rows4 + m, rows16, stride=4), :].astype(BF16))


def _inproj(x2d, pos_col, inv_tab, g1, w_in_bf16, qg, kg, batch, seq):
    n = x2d.shape[0]
    t = INPROJ_TILE
    nt = seq // t
    const = lambda i: (0, 0)
    qkv = 3 * ATT_WIDTH

    def residue_major(dil):
        spec = pl.BlockSpec((1, dil, t // dil, qkv), lambda i: (i // nt, 0, i % nt, 0))
        return spec, jax.ShapeDtypeStruct((batch, dil, seq // dil, qkv), BF16)

    spec4, shape4 = residue_major(4)
    spec16, shape16 = residue_major(16)
    return pl.pallas_call(
        _inproj_kernel,
        grid=(n // t,),
        in_specs=[
            pl.BlockSpec((t, D_MODEL), lambda i: (i, 0)),
            pl.BlockSpec((t, 1), lambda i: (i, 0)),
            pl.BlockSpec((1, LANES), const),
            pl.BlockSpec((1, D_MODEL), const),
            pl.BlockSpec((D_MODEL, IN_COLS), const, pipeline_mode=pl.Buffered(1)),
            pl.BlockSpec((1, LANES), const),
            pl.BlockSpec((1, LANES), const),
        ],
        out_specs=[pl.BlockSpec((t, IN_COLS), lambda i: (i, 0)), spec4, spec16],
        out_shape=[jax.ShapeDtypeStruct((n, IN_COLS), BF16), shape4, shape16],
        scratch_shapes=[pltpu.VMEM((ATT_WIDTH // LANES, t, LANES), F32)] * 2,
        compiler_params=pltpu.CompilerParams(
            dimension_semantics=("parallel",), vmem_limit_bytes=VMEM_LIMIT),
        name="inproj",
    )(x2d, pos_col, inv_tab, g1, w_in_bf16, qg, kg)


def _hgrn_direction(q, z, v, lb, state_t, reverse):
    c = HG_CHUNK
    t = q.shape[0]
    n = t // c
    row = lax.broadcasted_iota(jnp.int32, (c, c), 0)
    col = lax.broadcasted_iota(jnp.int32, (c, c), 1)
    mask = (row <= col) if reverse else (row >= col)
    tri = jnp.where(mask, 1.0, 0.0).astype(BF16)
    last_row = 0 if reverse else c - 1

    z = z.astype(F32)
    q = q.astype(F32)
    sg = _sigmoid(z)
    f = lb + (1.0 - lb) * sg
    k = (1.0 - lb) * (1.0 - sg)
    lf = jnp.log(f)
    lf_hi = lf.astype(BF16)
    lf_lo = (lf - lf_hi.astype(F32)).astype(BF16)
    chunks = [slice(j * c, (j + 1) * c) for j in range(n)]
    b = jnp.concatenate([_dot(tri, lf_hi[rs]) + _dot(tri, lf_lo[rs]) for rs in chunks], axis=0)
    b_last = b.reshape(n, c, HG_DIM)[:, last_row:last_row + 1, :]
    decay = jnp.exp(b_last)
    qt = (q * _sigmoid(q) * jnp.exp(b)).astype(BF16)
    kt_f32 = k * jnp.exp(-b)
    kt = kt_f32.astype(BF16)
    kd = (kt_f32.reshape(n, c, HG_DIM) * decay).reshape(t, HG_DIM).astype(BF16)

    outs, updates = [], []
    for rs in chunks:
        a = jnp.where(mask, _dot_nt(qt[rs], kt[rs]), 0.0)
        outs.append(_dot(a.astype(BF16), v[rs]))
        updates.append(_dot_tn(v[rs], kd[rs]))
    for j in (reversed(range(n)) if reverse else range(n)):
        outs[j] = outs[j] + _dot_nt(qt[chunks[j]], state_t.astype(BF16))
        state_t = state_t * decay[j] + updates[j]
    return jnp.concatenate(outs, axis=0), state_t


def _hgrn_kernel(qf_ref, zf_ref, vf_ref, qb_ref, zb_ref, vb_ref, lb_ref,
                 of_ref, ob_ref, sf_ref, sb_ref):
    @pl.when(pl.program_id(1) == 0)
    def _():
        sf_ref[...] = jnp.zeros_like(sf_ref)
        sb_ref[...] = jnp.zeros_like(sb_ref)

    for h in range(HG_HEADS):
        cols = slice(h * HG_DIM, (h + 1) * HG_DIM)
        o, sf = _hgrn_direction(qf_ref[:, cols], zf_ref[:, cols], vf_ref[:, cols],
                                lb_ref[h], sf_ref[h], False)
        of_ref[:, cols] = o.astype(of_ref.dtype)
        sf_ref[h] = sf
        o, sb = _hgrn_direction(qb_ref[:, cols], zb_ref[:, cols], vb_ref[:, cols],
                                lb_ref[HG_HEADS + h], sb_ref[h], True)
        ob_ref[:, cols] = o.astype(ob_ref.dtype)
        sb_ref[h] = sb


def _hgrn(proj, lbs, batch, seq):
    n = proj.shape[0]
    t = HGRN_TILE
    nblk = seq // t

    def fwd(colblk):
        return pl.BlockSpec((t, HG_WIDTH), lambda b, i: (b * nblk + i, colblk))

    def bwd(colblk):
        return pl.BlockSpec((t, HG_WIDTH), lambda b, i: (b * nblk + nblk - 1 - i, colblk))

    return pl.pallas_call(
        _hgrn_kernel,
        grid=(batch, nblk),
        in_specs=[
            fwd(COL_HQ), fwd(COL_HF_FWD), fwd(COL_HI),
            bwd(COL_HQ), bwd(COL_HF_BWD), bwd(COL_HI),
            pl.BlockSpec((2 * HG_HEADS, 1, HG_DIM), lambda b, i: (0, 0, 0)),
        ],
        out_specs=[fwd(0), bwd(0)],
        out_shape=[jax.ShapeDtypeStruct((n, HG_WIDTH), BF16)] * 2,
        scratch_shapes=[pltpu.VMEM((HG_HEADS, HG_DIM, HG_DIM), F32)] * 2,
        compiler_params=pltpu.CompilerParams(
            dimension_semantics=("parallel", "arbitrary"), vmem_limit_bytes=VMEM_LIMIT),
        name="hgrn",
    )(proj, proj, proj, proj, proj, proj, lbs)


def _attn_kernel(q_ref, kc_ref, kp_ref, kn_ref, vc_ref, vp_ref, vn_ref, o_ref, l_ref,
                 kw_ref, vw_ref, *, tq, length):
    n = pl.program_id(2)
    half = ATT_HALF
    kw_ref[0:half, :] = kp_ref[...]
    kw_ref[half:half + tq, :] = kc_ref[...]
    kw_ref[half + tq:, :] = kn_ref[...]
    vw_ref[0:half, :] = vp_ref[...]
    vw_ref[half:half + tq, :] = vc_ref[...]
    vw_ref[half + tq:, :] = vn_ref[...]

    qb_rows = ATT_QBLOCK
    win = qb_rows + 2 * half
    i_idx = lax.broadcasted_iota(jnp.int32, (qb_rows, win), 0)
    j_idx = lax.broadcasted_iota(jnp.int32, (qb_rows, win), 1)
    band = (j_idx >= i_idx) & (j_idx <= i_idx + 2 * half)

    pairs = [slice(p * LANES, (p + 1) * LANES) for p in range(ATT_HEADS // 2)]
    head_lane = lax.broadcasted_iota(jnp.int32, (qb_rows, LANES), 1)
    even_half = head_lane < ATT_DIM
    keep_even = jnp.where(lax.broadcasted_iota(jnp.int32, (1, LANES), 1) < ATT_DIM, 1.0, 0.0).astype(BF16)
    keep_odd = (1.0 - keep_even.astype(F32)).astype(BF16)

    def masked_scores(r0):
        base = n * tq + r0 - half
        valid = band & (j_idx >= -base) & (j_idx < length - base)
        q = q_ref[r0:r0 + qb_rows, :]
        kw = kw_ref[r0:r0 + win, :]
        out = []
        for cs in pairs:
            for keep in (keep_even, keep_odd):
                out.append(jnp.where(valid, _dot_nt(q[:, cs] * keep, kw[:, cs]), NEG))
        return out

    blocks = list(range(0, tq, qb_rows))
    nxt = masked_scores(blocks[0])
    for pos, r0 in enumerate(blocks):
        scores = nxt
        if pos + 1 < len(blocks):
            nxt = masked_scores(blocks[pos + 1])
        vw = vw_ref[r0:r0 + win, :]
        maxes = [jnp.max(s, axis=-1, keepdims=True) for s in scores]
        probs = [jnp.exp(s - m) for s, m in zip(scores, maxes)]
        dens = [jnp.sum(p, axis=-1, keepdims=True) for p in probs]
        outs = []
        for p, cs in enumerate(pairs):
            pv_even = _dot(probs[2 * p].astype(BF16), vw[:, cs])
            pv_odd = _dot(probs[2 * p + 1].astype(BF16), vw[:, cs])
            outs.append(jnp.where(even_half, pv_even, pv_odd)
                        / jnp.where(even_half, dens[2 * p], dens[2 * p + 1]))
        o_ref[r0:r0 + qb_rows, :] = jnp.concatenate(outs, axis=1).astype(o_ref.dtype)
        lse = jnp.zeros((qb_rows, LANES), F32)
        for h, (m, den) in enumerate(zip(maxes, dens)):
            lse = jnp.where(head_lane == h, m + jnp.log(den), lse)
        l_ref[r0:r0 + qb_rows, :] = lse


def _attention(qkv, col0):
    batch, dil, length, _ = qkv.shape
    tq = min(ATT_TILE, length)
    nq = length // tq
    hb = tq // ATT_HALF
    n_hblk = length // ATT_HALF

    def cur(col):
        return pl.BlockSpec((None, None, tq, ATT_WIDTH), lambda b, r, n: (b, r, n, col))

    def prev(col):
        return pl.BlockSpec((None, None, ATT_HALF, ATT_WIDTH),
                            lambda b, r, n: (b, r, jnp.maximum(n * hb - 1, 0), col))

    def nxt(col):
        return pl.BlockSpec((None, None, ATT_HALF, ATT_WIDTH),
                            lambda b, r, n: (b, r, jnp.minimum((n + 1) * hb, n_hblk - 1), col))

    out_spec = pl.BlockSpec((None, None, tq, ATT_WIDTH), lambda b, r, n: (b, r, n, 0))
    return pl.pallas_call(
        functools.partial(_attn_kernel, tq=tq, length=length),
        grid=(batch, dil, nq),
        in_specs=[cur(col0), cur(col0 + 1), prev(col0 + 1), nxt(col0 + 1),
                  cur(col0 + 2), prev(col0 + 2), nxt(col0 + 2)],
        out_specs=[out_spec, pl.BlockSpec((None, None, tq, LANES), lambda b, r, n: (b, r, n, 0))],
        out_shape=[jax.ShapeDtypeStruct((batch, dil, length, ATT_WIDTH), BF16),
                   jax.ShapeDtypeStruct((batch, dil, length, LANES), F32)],
        scratch_shapes=[pltpu.VMEM((tq + 2 * ATT_HALF, ATT_WIDTH), BF16)] * 2,
        compiler_params=pltpu.CompilerParams(
            dimension_semantics=("parallel", "parallel", "parallel"), vmem_limit_bytes=VMEM_LIMIT),
        name=f"attn_d{dil}",
    )(qkv, qkv, qkv, qkv, qkv, qkv, qkv)


def _token_major(src_ref, stage_ref, tmp_ref):
    dil, rows = src_ref.shape[1], src_ref.shape[2]
    nc = src_ref.shape[3] // LANES
    for c in range(nc):
        cols = slice(c * LANES, (c + 1) * LANES)
        if dil == 4:
            for r in range(dil):
                stage_ref[c, pl.ds(r, rows, stride=dil), :] = src_ref[0, r, :, cols].astype(F32)
        else:
            group = 4 * rows
            for r4 in range(4):
                for m in range(4):
                    tmp_ref[c, pl.ds(r4 * group + m, rows, stride=4), :] = (
                        src_ref[0, r4 + 4 * m, :, cols].astype(F32))
            for r4 in range(4):
                stage_ref[c, pl.ds(r4, group, stride=4), :] = tmp_ref[c, r4 * group:(r4 + 1) * group, :]
    return jnp.concatenate([stage_ref[c] for c in range(nc)], axis=1)


def _outproj_kernel(of_ref, ob_ref, hg_ref, o1_ref, o2_ref, o3_ref, l1_ref, l2_ref, l3_ref,
                    x_ref, og_ref, w_ref, g2_ref, wrh_ref, wrl_ref, br_ref,
                    x2_ref, xn_ref, gate_ref, idx_ref, st_o2, st_o3, st_l2, st_l3, st_tmp_o, st_tmp_l):
    o = of_ref[...].astype(F32) + ob_ref[...].astype(F32)
    hg = hg_ref[...].astype(F32)
    parts = []
    for h in range(HG_HEADS):
        blk = o[:, h * HG_DIM:(h + 1) * HG_DIM]
        ms = jnp.mean(blk * blk, axis=-1, keepdims=True)
        parts.append(blk * lax.rsqrt(ms + EPS) * og_ref[...])
    o_hg = jnp.concatenate(parts, axis=1) * (hg * _sigmoid(hg))

    l1 = l1_ref[...]
    l2 = _token_major(l2_ref, st_l2, st_tmp_l)
    l3 = _token_major(l3_ref, st_l3, st_tmp_l)
    mx = jnp.maximum(jnp.maximum(l1, l2), l3)
    e1, e2, e3 = jnp.exp(l1 - mx), jnp.exp(l2 - mx), jnp.exp(l3 - mx)
    den = e1 + e2 + e3
    er = lax.broadcasted_iota(jnp.int32, (LANES, ATT_WIDTH), 0)
    ec = lax.broadcasted_iota(jnp.int32, (LANES, ATT_WIDTH), 1)
    expand = jnp.where(er == ec // ATT_DIM, 1.0, 0.0).astype(BF16)

    def per_lane(w):
        return _dot(w.astype(BF16), expand)

    o_att = (per_lane(e1 / den) * o1_ref[...].astype(F32)
             + per_lane(e2 / den) * _token_major(o2_ref, st_o2, st_tmp_o)
             + per_lane(e3 / den) * _token_major(o3_ref, st_o3, st_tmp_o))

    y = _dot(o_hg.astype(BF16), w_ref[0:HG_WIDTH, :]) + _dot(o_att.astype(BF16), w_ref[HG_WIDTH:, :])
    x2 = x_ref[...] + y
    x2_ref[...] = x2

    ms = jnp.mean(x2 * x2, axis=-1, keepdims=True)
    unit = x2 * lax.rsqrt(ms + EPS)
    xn = unit * g2_ref[0:1, :]
    _rows_to_tiles(xn_ref, unit * g2_ref[1:2, :])
    xn_hi = xn.astype(BF16)
    xn_lo = (xn - xn_hi.astype(F32)).astype(BF16)
    logits = (_dot(xn_hi, wrh_ref[...]) + _dot(xn_lo, wrh_ref[...]) + _dot(xn_hi, wrl_ref[...])
              + br_ref[...])

    lane = lax.broadcasted_iota(jnp.int32, logits.shape, 1)
    lane_f = lane.astype(F32)
    work = jnp.where(lane < N_EXPERTS, logits, -jnp.inf)
    vals, idxs = [], []
    for _ in range(TOP_K):
        m = jnp.max(work, axis=-1, keepdims=True)
        idx = jnp.min(jnp.where(work == m, lane_f, float(LANES)), axis=-1, keepdims=True)
        vals.append(m)
        idxs.append(idx)
        work = jnp.where(lane_f == idx, -jnp.inf, work)
    es = [jnp.exp(v - vals[0]) for v in vals]
    den = es[0] + es[1] + es[2] + es[3]
    gate_out = jnp.zeros(logits.shape, F32)
    idx_out = jnp.zeros(logits.shape, F32)
    for k in range(TOP_K):
        gate_out = jnp.where(lane == k, es[k] / den, gate_out)
        idx_out = jnp.where(lane == k, idxs[k], idx_out)
    gate_ref[...] = gate_out
    idx_ref[...] = idx_out.astype(jnp.int32)


def _outproj(o_f, o_b, proj, atts, lses, x2d, og, w_out_bf16, g2, wr_hi, wr_lo, br, seq):
    n = x2d.shape[0]
    t = TOKEN_TILE
    nt = seq // t
    row = lambda i: (i, 0)
    const = lambda i: (0, 0)
    half = pl.BlockSpec((t, 512), row)

    def residue_major(dil, width):
        return pl.BlockSpec((1, dil, t // dil, width), lambda i: (i // nt, 0, i % nt, 0))

    rm4, rm16 = residue_major(4, ATT_WIDTH), residue_major(16, ATT_WIDTH)
    lse, lse4, lse16 = pl.BlockSpec((t, LANES), row), residue_major(4, LANES), residue_major(16, LANES)
    wide, narrow = pltpu.VMEM((ATT_WIDTH // LANES, t, LANES), F32), pltpu.VMEM((1, t, LANES), F32)
    return pl.pallas_call(
        _outproj_kernel,
        grid=(n // t,),
        in_specs=[
            half, half, pl.BlockSpec((t, 512), lambda i: (i, COL_HG)),
            half, rm4, rm16, lse, lse4, lse16,
            pl.BlockSpec((t, D_MODEL), row),
            pl.BlockSpec((1, HG_DIM), const),
            pl.BlockSpec((D_MODEL, D_MODEL), const),
            pl.BlockSpec((2, D_MODEL), const),
            pl.BlockSpec((D_MODEL, LANES), const),
            pl.BlockSpec((D_MODEL, LANES), const),
            pl.BlockSpec((1, LANES), const),
        ],
        out_specs=[pl.BlockSpec((t, D_MODEL), row), pl.BlockSpec((t * ROW_TILES, LANES), row),
                   pl.BlockSpec((t, LANES), row), pl.BlockSpec((t, LANES), row)],
        out_shape=[jax.ShapeDtypeStruct((n, D_MODEL), F32), jax.ShapeDtypeStruct((n * ROW_TILES, LANES), F32),
                   jax.ShapeDtypeStruct((n, LANES), F32), jax.ShapeDtypeStruct((n, LANES), jnp.int32)],
        scratch_shapes=[wide, wide, narrow, narrow, wide, narrow],
        compiler_params=pltpu.CompilerParams(
            dimension_semantics=("parallel",), vmem_limit_bytes=VMEM_LIMIT),
        name="outproj",
    )(o_f, o_b, proj, *atts, *lses, x2d, og, w_out_bf16, g2, wr_hi, wr_lo, br)


def _moe_kernel(bexp_ref, bfirst_ref, bslot_ref, bnext_ref, bvalid_ref, nused_ref,
                xs_ref, xinv_ref, wu_hbm, bg_ref, bl_ref, wd_hbm, bd_ref,
                y_ref, wu_buf, wd_buf, wg_s, wl_s, wd_s, up_unscale_s, down_unscale_s, sems):
    i = pl.program_id(0)

    def weight_copies(expert, slot):
        return (pltpu.make_async_copy(wu_hbm.at[expert], wu_buf.at[slot], sems.at[0, slot]),
                pltpu.make_async_copy(wd_hbm.at[expert], wd_buf.at[slot], sems.at[1, slot]))

    @pl.when(bfirst_ref[i] == 1)
    def _():
        slot = bslot_ref[i]

        @pl.when(i == 0)
        def _():
            for cp in weight_copies(bexp_ref[0], 0):
                cp.start()

        for cp in weight_copies(bexp_ref[i], slot):
            cp.wait()

        @pl.when(bnext_ref[i] >= 0)
        def _():
            for cp in weight_copies(bnext_ref[i], 1 - slot):
                cp.start()

        su = _fp8_scale(_abs_max(wu_buf[slot]))
        sd = _fp8_scale(_abs_max(wd_buf[slot]))
        up_unscale_s[...] = xinv_ref[...] / su
        down_unscale_s[...] = (1.0 / ACT_FP8_SCALE) / sd
        r = lax.broadcasted_iota(jnp.int32, (2 * LANES, 2 * LANES), 0)
        c = lax.broadcasted_iota(jnp.int32, (2 * LANES, 2 * LANES), 1)
        src = jnp.where(c < LANES, 2 * c, 2 * (c - LANES) + 1)
        perm = jnp.where(r == src, 1.0, 0.0).astype(BF16)
        rows = 256
        for rb in range(D_MODEL // rows):
            rs = slice(rb * rows, (rb + 1) * rows)
            for cb in range(D_EXPERT // LANES):
                w = wu_buf[slot, rs, cb * 2 * LANES:(cb + 1) * 2 * LANES].astype(BF16)
                split = (_dot(w, perm) * su).astype(F8)
                wg_s[rs, cb * LANES:(cb + 1) * LANES] = split[:, :LANES]
                wl_s[rs, cb * LANES:(cb + 1) * LANES] = split[:, LANES:]
        wd_s[...] = (wd_buf[slot] * sd).astype(F8)

    def expert_mlp(rows):
        x8 = jnp.concatenate(_tiles_to_rows(xs_ref, rows), axis=1).astype(F8)
        hglu = _dot(x8, wg_s[...]) * up_unscale_s[...] + bg_ref[0]
        hlin = _dot(x8, wl_s[...]) * up_unscale_s[...] + bl_ref[0]
        glu = jnp.minimum(hglu, SWIGLU_LIMIT)
        lin = jnp.clip(hlin, -SWIGLU_LIMIT, SWIGLU_LIMIT)
        act = glu * _sigmoid(SWIGLU_ALPHA * glu) * (lin + 1.0)
        y = _dot((act * ACT_FP8_SCALE).astype(F8), wd_s[...]) * down_unscale_s[...] + bd_ref[0]
        _rows_to_tiles(y_ref, y)

    valid = bvalid_ref[i]
    half = MOE_BLOCK_ROWS // 2

    @pl.when(valid > half)
    def _():
        expert_mlp(MOE_BLOCK_ROWS)

    @pl.when((valid > 0) & (valid <= half))
    def _():
        expert_mlp(half)
        y_ref[half * ROW_TILES:, :] = jnp.zeros((half * ROW_TILES, LANES), F32)

    @pl.when(valid == 0)
    def _():
        y_ref[...] = jnp.zeros_like(y_ref)


def _moe(block_exp, block_first, block_slot, block_next, block_valid, n_used,
         xs, x_unscale, w_up, b_glu, b_lin, w_down, b_down):
    p_rows = xs.shape[0] // ROW_TILES
    bm = MOE_BLOCK_ROWS
    nb = p_rows // bm
    exp3 = lambda i, be, *_: (be[i], 0, 0)
    grid_spec = pltpu.PrefetchScalarGridSpec(
        num_scalar_prefetch=6,
        grid=(nb,),
        in_specs=[
            pl.BlockSpec((bm * ROW_TILES, LANES), lambda i, be, bf, bs, bn, bv, nu: (jnp.minimum(i, nu[0] - 1), 0)),
            pl.BlockSpec((1, 1), lambda i, *_: (0, 0)),
            pl.BlockSpec(memory_space=pl.ANY),
            pl.BlockSpec((1, 1, D_EXPERT), exp3),
            pl.BlockSpec((1, 1, D_EXPERT), exp3),
            pl.BlockSpec(memory_space=pl.ANY),
            pl.BlockSpec((1, 1, D_MODEL), exp3),
        ],
        out_specs=pl.BlockSpec((bm * ROW_TILES, LANES), lambda i, *_: (i, 0)),
        scratch_shapes=[pltpu.VMEM((2, D_MODEL, 2 * D_EXPERT), F32), pltpu.VMEM((2, D_EXPERT, D_MODEL), F32),
                        pltpu.VMEM((D_MODEL, D_EXPERT), F8), pltpu.VMEM((D_MODEL, D_EXPERT), F8),
                        pltpu.VMEM((D_EXPERT, D_MODEL), F8), pltpu.VMEM((1, 1), F32), pltpu.VMEM((1, 1), F32),
                        pltpu.SemaphoreType.DMA((2, 2))],
    )
    return pl.pallas_call(
        _moe_kernel,
        grid_spec=grid_spec,
        out_shape=jax.ShapeDtypeStruct((p_rows * ROW_TILES, LANES), F32),
        compiler_params=pltpu.CompilerParams(
            dimension_semantics=("arbitrary",), vmem_limit_bytes=VMEM_LIMIT),
        name="moe",
    )(block_exp, block_first, block_slot, block_next, block_valid, n_used,
      xs, x_unscale, w_up, b_glu, b_lin, w_down, b_down)


def _dispatch_kernel(dest_ref, zstart_ref, xn_ref, xs_hbm, zero_ref, sem, zsem):
    i = pl.program_id(0)
    bm = MOE_BLOCK_ROWS
    tokens = xn_ref.shape[0] // ROW_TILES

    def zero_copy(e):
        start = pl.multiple_of(zstart_ref[e] * ROW_TILES, bm * ROW_TILES)
        return pltpu.make_async_copy(zero_ref, xs_hbm.at[pl.ds(start, bm * ROW_TILES)], zsem)

    @pl.when(i == 0)
    def _():
        zero_ref[...] = jnp.zeros_like(zero_ref)
        for e in range(N_EXPERTS):
            @pl.when(zstart_ref[e] >= 0)
            def _():
                zero_copy(e).start()
        for e in range(N_EXPERTS):
            @pl.when(zstart_ref[e] >= 0)
            def _():
                zero_copy(e).wait()

    t0 = i * tokens

    def body(j, carry):
        src = xn_ref.at[pl.ds(pl.multiple_of(j * ROW_TILES, ROW_TILES), ROW_TILES)]
        for k in range(TOP_K):
            d = pl.multiple_of(dest_ref[k * (dest_ref.shape[0] // TOP_K) + t0 + j] * ROW_TILES, ROW_TILES)
            pltpu.make_async_copy(src, xs_hbm.at[pl.ds(d, ROW_TILES)], sem).start(priority=k % 2)
        return carry

    lax.fori_loop(0, tokens, body, 0, unroll=8)
    for k in range(TOP_K):
        pltpu.make_async_copy(xn_ref, xs_hbm.at[pl.ds(0, tokens * ROW_TILES)], sem).wait()


def _dispatch(dest, zstart, xn, p_rows):
    n = xn.shape[0] // ROW_TILES
    t = DISPATCH_TOKENS
    grid_spec = pltpu.PrefetchScalarGridSpec(
        num_scalar_prefetch=2,
        grid=(n // t,),
        in_specs=[pl.BlockSpec((t * ROW_TILES, LANES), lambda i, d, z: (i, 0))],
        out_specs=pl.BlockSpec(memory_space=pl.ANY),
        scratch_shapes=[pltpu.VMEM((MOE_BLOCK_ROWS * ROW_TILES, LANES), F32),
                        pltpu.SemaphoreType.DMA, pltpu.SemaphoreType.DMA],
    )
    return pl.pallas_call(
        _dispatch_kernel,
        grid_spec=grid_spec,
        out_shape=jax.ShapeDtypeStruct((p_rows * ROW_TILES, LANES), F32),
        compiler_params=pltpu.CompilerParams(
            dimension_semantics=("arbitrary",), vmem_limit_bytes=VMEM_LIMIT),
        name="dispatch",
    )(dest, zstart, xn)


def _combine_kernel(dest_ref, y_hbm, x2_ref, gate_ref, out_ref, buf_ref, sems):
    i = pl.program_id(0)
    tc = COMBINE_TOKENS

    def issue(step, slot):
        t0 = step * tc

        group = 8

        def body(g, carry):
            for k in range(TOP_K):
                for u in range(group):
                    j = g * group + u
                    d = pl.multiple_of(dest_ref[k * (dest_ref.shape[0] // TOP_K) + t0 + j] * ROW_TILES, ROW_TILES)
                    r = pl.multiple_of((k * tc + j) * ROW_TILES, ROW_TILES)
                    pltpu.make_async_copy(y_hbm.at[pl.ds(d, ROW_TILES)], buf_ref.at[slot, pl.ds(r, ROW_TILES)],
                                          sems.at[slot]).start(priority=u % 2)
            return carry

        lax.fori_loop(0, tc // group, body, 0)

    @pl.when(i == 0)
    def _():
        issue(0, 0)

    @pl.when(i + 1 < pl.num_programs(0))
    def _():
        issue(i + 1, (i + 1) % 2)

    slot = i % 2
    pltpu.make_async_copy(y_hbm.at[pl.ds(0, TOP_K * tc * ROW_TILES)], buf_ref.at[slot], sems.at[slot]).wait()
    rows = buf_ref.at[slot]
    sub = 64
    for r0 in range(0, tc, sub):
        gate = gate_ref[r0:r0 + sub, :]
        gates = [jnp.broadcast_to(gate[:, k:k + 1], (sub, LANES)) for k in range(TOP_K)]
        for c in range(ROW_TILES):
            acc = x2_ref[r0:r0 + sub, c * LANES:(c + 1) * LANES]
            for k in range(TOP_K):
                acc = acc + gates[k] * rows[pl.ds((k * tc + r0) * ROW_TILES + c, sub, stride=ROW_TILES), :]
            out_ref[r0:r0 + sub, c * LANES:(c + 1) * LANES] = acc


def _combine(dest, y, x2, gates):
    n = x2.shape[0]
    tc = COMBINE_TOKENS
    grid_spec = pltpu.PrefetchScalarGridSpec(
        num_scalar_prefetch=1,
        grid=(n // tc,),
        in_specs=[pl.BlockSpec(memory_space=pl.ANY),
                  pl.BlockSpec((tc, D_MODEL), lambda i, d: (i, 0)),
                  pl.BlockSpec((tc, LANES), lambda i, d: (i, 0))],
        out_specs=pl.BlockSpec((tc, D_MODEL), lambda i, d: (i, 0)),
        scratch_shapes=[pltpu.VMEM((2, TOP_K * tc * ROW_TILES, LANES), F32), pltpu.SemaphoreType.DMA((2,))],
    )
    return pl.pallas_call(
        _combine_kernel,
        grid_spec=grid_spec,
        out_shape=jax.ShapeDtypeStruct((n, D_MODEL), F32),
        compiler_params=pltpu.CompilerParams(
            dimension_semantics=("arbitrary",), vmem_limit_bytes=VMEM_LIMIT),
        name="combine",
    )(dest, y, x2, gates)


def _route(top_idx):
    n = top_idx.shape[0]
    a = n * TOP_K
    bm = MOE_BLOCK_ROWS
    nb = a // bm + N_EXPERTS
    experts = jnp.arange(N_EXPERTS, dtype=jnp.int32)[None, :]
    picked = [top_idx[:, k:k + 1] == experts for k in range(TOP_K)]
    hits = sum(p.astype(jnp.int32) for p in picked)
    csum = jnp.cumsum(hits, axis=0)
    counts = csum[-1]
    padded = ((counts + bm - 1) // bm) * bm
    pad_end = jnp.cumsum(padded)
    pad_start = pad_end - padded
    row_of = csum - hits + pad_start[None, :]
    dest = jnp.concatenate([jnp.sum(jnp.where(p, row_of, 0), axis=1) for p in picked])
    n_used = (pad_end[-1] // bm).astype(jnp.int32)
    blk = jnp.arange(nb, dtype=jnp.int32)
    bexp = jnp.sum((pad_end[None, :] <= (blk * bm)[:, None]).astype(jnp.int32), axis=1)
    bexp = jnp.minimum(bexp, N_EXPERTS - 1)
    bexp = jnp.where(blk < n_used, bexp, bexp[jnp.maximum(n_used - 1, 0)])
    bfirst = jnp.concatenate([jnp.ones((1,), jnp.int32), (bexp[1:] != bexp[:-1]).astype(jnp.int32)])
    bslot = (jnp.cumsum(bfirst) - 1) % 2
    later = jnp.where(bexp[None, :] > bexp[:, None], bexp[None, :], N_EXPERTS)
    bnext = jnp.min(later, axis=1)
    bnext = jnp.where(bnext < N_EXPERTS, bnext, -1).astype(jnp.int32)
    valid_end = jnp.sum(jnp.where(bexp[:, None] == jnp.arange(N_EXPERTS)[None, :], (pad_start + counts)[None, :], 0), axis=1)
    bvalid = jnp.clip(valid_end - blk * bm, 0, bm).astype(jnp.int32)
    zstart = jnp.where(counts > 0, pad_end - bm, -1).astype(jnp.int32)
    return dest.astype(jnp.int32), zstart, (bexp, bfirst, bslot.astype(jnp.int32), bnext, bvalid, n_used.reshape(1))


def kernel(x, positions, norm1_g, w_in, q_norm_g, k_norm_g, hgrn_lower_bounds, hgrn_onorm_g,
           w_out, norm2_g, w_router, b_router, w_up, b_up, w_down, b_down):
    batch, seq, d = x.shape
    n = batch * seq
    depth = norm1_g.shape[0]
    lbs_all = jnp.cumsum(jax.nn.softmax(hgrn_lower_bounds.astype(F32), axis=0), axis=0)
    half = ATT_DIM // 2
    inv = 1.0 / (ROPE_THETA ** (jnp.arange(half, dtype=F32) / half))
    inv_tab = jnp.tile(inv, LANES // half).reshape(1, LANES)
    pos_col = positions.reshape(n, 1)

    x2d = x.reshape(n, d)
    for l in range(depth):
        lbs = lbs_all[l].reshape(2 * HG_HEADS, 1, HG_DIM)
        proj, qkv4, qkv16 = _inproj(
            x2d, pos_col, inv_tab, norm1_g[l].reshape(1, d), w_in[l].astype(BF16),
            jnp.tile(q_norm_g[l], LANES // ATT_DIM).reshape(1, LANES),
            jnp.tile(k_norm_g[l], LANES // ATT_DIM).reshape(1, LANES), batch, seq)
        o_f, o_b = _hgrn(proj, lbs, batch, seq)
        o1, l1 = _attention(proj.reshape(batch, 1, seq, IN_COLS), COL_AQ)
        o4, l4 = _attention(qkv4, 0)
        o16, l16 = _attention(qkv16, 0)
        atts = [o1.reshape(n, ATT_WIDTH), o4, o16]
        lses = [l1.reshape(n, LANES), l4, l16]

        wr = jnp.pad(w_router[l], ((0, 0), (0, LANES - N_EXPERTS)))
        wr_hi = wr.astype(BF16)
        wr_lo = (wr - wr_hi.astype(F32)).astype(BF16)
        br = jnp.pad(b_router[l], (0, LANES - N_EXPERTS)).reshape(1, LANES)
        g2 = norm2_g[l].astype(F32)
        g2_max = jnp.max(jnp.abs(g2))
        x_scale = jnp.where(g2_max > 0.0, ROW_FP8_TARGET / g2_max, 1.0)
        x2, xn, gates, top_idx = _outproj(
            o_f, o_b, proj, atts, lses, x2d, hgrn_onorm_g[l].reshape(1, HG_DIM),
            w_out[l].astype(BF16), jnp.stack([g2, g2 * x_scale]), wr_hi, wr_lo, br, seq)
        dest, zstart, blocks = _route(top_idx[:, :TOP_K])
        xs = _dispatch(dest, zstart, xn, blocks[0].shape[0] * MOE_BLOCK_ROWS)
        y = _moe(*blocks, xs, (1.0 / x_scale).reshape(1, 1), w_up[l],
                 b_up[l][:, 0::2].reshape(N_EXPERTS, 1, D_EXPERT),
                 b_up[l][:, 1::2].reshape(N_EXPERTS, 1, D_EXPERT),
                 w_down[l], b_down[l].reshape(N_EXPERTS, 1, D_MODEL))
        x2d = _combine(dest, y, x2, gates)
    return x2d.reshape(batch, seq, d)
```

```python
import functools

import jax
import jax.numpy as jnp
from jax import lax
from jax.experimental import pallas as pl
from jax.experimental.pallas import tpu as pltpu

F32 = jnp.float32
BF16 = jnp.bfloat16
F8 = jnp.float8_e4m3fn
FP8_TARGET = 240.0
ROW_FP8_TARGET = 8.0
ACT_FP8_SCALE = 4.0

D_MODEL = 1024
HG_HEADS = 4
HG_DIM = 128
HG_WIDTH = HG_HEADS * HG_DIM
HG_CHUNK = 64
ATT_HEADS = 8
ATT_DIM = 64
ATT_WIDTH = ATT_HEADS * ATT_DIM
DILATED_PATTERNS = ((128, 1), (512, 4), (2048, 16))
ATT_HALF = 64
assert all(window // (2 * dil) == ATT_HALF for window, dil in DILATED_PATTERNS)
assert tuple(dil for _, dil in DILATED_PATTERNS) == (1, 4, 16)
ATT_QBLOCK = 128
ROPE_THETA = 10000.0
IN_COLS = 5 * HG_WIDTH + 3 * ATT_WIDTH
N_EXPERTS = 32
TOP_K = 4
D_EXPERT = D_MODEL
SWIGLU_LIMIT = 7.0
SWIGLU_ALPHA = 1.702
EPS = 1e-6
NEG = -1e30

COL_HQ, COL_HF_FWD, COL_HF_BWD, COL_HI, COL_HG, COL_AQ, COL_AK, COL_AV = range(8)

TOKEN_TILE = 1024
INPROJ_TILE = 1024
HGRN_TILE = 512
ATT_TILE = 1024
MOE_BLOCK_ROWS = 512
DISPATCH_TOKENS = 2048
COMBINE_TOKENS = 256
LANES = 128
ROUTE_LANES = 8
ROW_TILES = D_MODEL // LANES
VMEM_LIMIT = 60 * 1024 * 1024


def _dot(a, b):
    return jnp.dot(a, b, preferred_element_type=F32)


def _dot_nt(a, b):
    return lax.dot_general(a, b, (((1,), (1,)), ((), ())), preferred_element_type=F32)


def _dot_tn(a, b):
    return lax.dot_general(a, b, (((0,), (0,)), ((), ())), preferred_element_type=F32)


def _sigmoid(x):
    return 0.5 * jnp.tanh(0.5 * x) + 0.5


def _abs_max(x):
    return jnp.max(jnp.max(jnp.abs(x), axis=0, keepdims=True), axis=1, keepdims=True)


def _fp8_scale(amax):
    return jnp.where(amax > 0.0, FP8_TARGET / amax, 1.0)


def _rows_to_tiles(dst_ref, x):
    for c in range(ROW_TILES):
        dst_ref[pl.ds(c, x.shape[0], stride=ROW_TILES), :] = x[:, c * LANES:(c + 1) * LANES]


def _tiles_to_rows(src, rows, first_row=0):
    return [src[pl.ds(first_row * ROW_TILES + c, rows, stride=ROW_TILES), :] for c in range(ROW_TILES)]


def _head_norm_rope(p, gain, cos, sin_signed, scale):
    lane = lax.broadcasted_iota(jnp.int32, (p.shape[0], LANES), 1)
    low = lane < ATT_DIM
    first_half = (lane % ATT_DIM) < (ATT_DIM // 2)
    outs = []
    for t in range(ATT_WIDTH // LANES):
        blk = p[:, t * LANES:(t + 1) * LANES]
        sq = blk * blk
        s_low = jnp.sum(jnp.where(low, sq, 0.0), axis=-1, keepdims=True)
        s_high = jnp.sum(jnp.where(low, 0.0, sq), axis=-1, keepdims=True)
        r = jnp.where(low, lax.rsqrt(s_low * (1.0 / ATT_DIM) + EPS),
                      lax.rsqrt(s_high * (1.0 / ATT_DIM) + EPS))
        y = blk * r * gain
        partner = jnp.where(first_half, pltpu.roll(y, LANES - ATT_DIM // 2, axis=1),
                            pltpu.roll(y, ATT_DIM // 2, axis=1))
        outs.append((y * cos + partner * sin_signed) * scale)
    return jnp.concatenate(outs, axis=1)


def _inproj_kernel(x_ref, pos_ref, inv_ref, g1_ref, w_ref, qg_ref, kg_ref,
                   out_ref, d4_ref, d16_ref, stage_ref, stage2_ref):
    x = x_ref[...]
    ms = jnp.mean(x * x, axis=-1, keepdims=True)
    h = (x * lax.rsqrt(ms + EPS) * g1_ref[...]).astype(BF16)
    ang = pos_ref[...].astype(F32) * inv_ref[...]
    lane = lax.broadcasted_iota(jnp.int32, ang.shape, 1)
    cos = jnp.cos(ang)
    sin_signed = jnp.where((lane % ATT_DIM) < (ATT_DIM // 2), -jnp.sin(ang), jnp.sin(ang))
    order = (COL_AQ, COL_AK, COL_AV, COL_HQ, COL_HF_FWD, COL_HF_BWD, COL_HI, COL_HG)
    nxt = _dot(h, w_ref[:, order[0] * 512:(order[0] + 1) * 512])
    for pos, j in enumerate(order):
        p = nxt
        if pos + 1 < len(order):
            jn = order[pos + 1]
            nxt = _dot(h, w_ref[:, jn * 512:(jn + 1) * 512])
        if j == COL_AQ:
            p = _head_norm_rope(p, qg_ref[...], cos, sin_signed, ATT_DIM ** -0.5)
        elif j == COL_AK:
            p = _head_norm_rope(p, kg_ref[...], cos, sin_signed, 1.0)
        out_ref[:, j * 512:(j + 1) * 512] = p.astype(BF16)
        if j >= COL_AQ:
            rows4, rows16 = x.shape[0] // 4, x.shape[0] // 16
            for c in range(ATT_WIDTH // LANES):
                cols = slice((j - COL_AQ) * ATT_WIDTH + c * LANES, (j - COL_AQ) * ATT_WIDTH + (c + 1) * LANES)
                stage_ref[c] = p[:, c * LANES:(c + 1) * LANES]
                for r4 in range(4):
                    group = stage_ref[c, pl.ds(r4, rows4, stride=4), :]
                    d4_ref[0, r4, :, cols] = group.astype(BF16)
                    stage2_ref[c, r4 * rows4:(r4 + 1) * rows4, :] = group
                for r4 in range(4):
                    for m in range(4):
                        d16_ref[0, r4 + 4 * m, :, cols] = (
                            stage2_ref[c, pl.ds(r4 * rows4 + m, rows16, stride=4), :].astype(BF16))


def _inproj(x2d, pos_col, inv_tab, g1, w_in_bf16, qg, kg, batch, seq):
    n = x2d.shape[0]
    t = INPROJ_TILE
    nt = seq // t
    const = lambda i: (0, 0)
    qkv = 3 * ATT_WIDTH

    def residue_major(dil):
        spec = pl.BlockSpec((1, dil, t // dil, qkv), lambda i: (i // nt, 0, i % nt, 0))
        return spec, jax.ShapeDtypeStruct((batch, dil, seq // dil, qkv), BF16)

    spec4, shape4 = residue_major(4)
    spec16, shape16 = residue_major(16)
    return pl.pallas_call(
        _inproj_kernel,
        grid=(n // t,),
        in_specs=[
            pl.BlockSpec((t, D_MODEL), lambda i: (i, 0)),
            pl.BlockSpec((t, 1), lambda i: (i, 0)),
            pl.BlockSpec((1, LANES), const),
            pl.BlockSpec((1, D_MODEL), const),
            pl.BlockSpec((D_MODEL, IN_COLS), const, pipeline_mode=pl.Buffered(1)),
            pl.BlockSpec((1, LANES), const),
            pl.BlockSpec((1, LANES), const),
        ],
        out_specs=[pl.BlockSpec((t, IN_COLS), lambda i: (i, 0)), spec4, spec16],
        out_shape=[jax.ShapeDtypeStruct((n, IN_COLS), BF16), shape4, shape16],
        scratch_shapes=[pltpu.VMEM((ATT_WIDTH // LANES, t, LANES), F32)] * 2,
        compiler_params=pltpu.CompilerParams(
            dimension_semantics=("parallel",), vmem_limit_bytes=VMEM_LIMIT),
        name="inproj",
    )(x2d, pos_col, inv_tab, g1, w_in_bf16, qg, kg)


def _hgrn_direction(q, z, v, lb, state_t, reverse):
    c = HG_CHUNK
    t = q.shape[0]
    n = t // c
    row = lax.broadcasted_iota(jnp.int32, (c, c), 0)
    col = lax.broadcasted_iota(jnp.int32, (c, c), 1)
    mask = (row <= col) if reverse else (row >= col)
    tri = jnp.where(mask, 1.0, 0.0).astype(BF16)
    last_row = 0 if reverse else c - 1

    z = z.astype(F32)
    q = q.astype(F32)
    sg = _sigmoid(z)
    f = lb + (1.0 - lb) * sg
    k = (1.0 - lb) * (1.0 - sg)
    lf = jnp.log(f)
    lf_hi = lf.astype(BF16)
    lf_lo = (lf - lf_hi.astype(F32)).astype(BF16)
    chunks = [slice(j * c, (j + 1) * c) for j in range(n)]
    b = jnp.concatenate([_dot(tri, lf_hi[rs]) + _dot(tri, lf_lo[rs]) for rs in chunks], axis=0)
    b_last = b.reshape(n, c, HG_DIM)[:, last_row:last_row + 1, :]
    decay = jnp.exp(b_last)
    qt = (q * _sigmoid(q) * jnp.exp(b)).astype(BF16)
    kt_f32 = k * jnp.exp(-b)
    kt = kt_f32.astype(BF16)
    kd = (kt_f32.reshape(n, c, HG_DIM) * decay).reshape(t, HG_DIM).astype(BF16)

    outs, updates = [], []
    for rs in chunks:
        a = jnp.where(mask, _dot_nt(qt[rs], kt[rs]), 0.0)
        outs.append(_dot(a.astype(BF16), v[rs]))
        updates.append(_dot_tn(v[rs], kd[rs]))
    for j in (reversed(range(n)) if reverse else range(n)):
        outs[j] = outs[j] + _dot_nt(qt[chunks[j]], state_t.astype(BF16))
        state_t = state_t * decay[j] + updates[j]
    return jnp.concatenate(outs, axis=0), state_t


def _hgrn_kernel(qf_ref, zf_ref, vf_ref, qb_ref, zb_ref, vb_ref, lb_ref,
                 of_ref, ob_ref, sf_ref, sb_ref):
    @pl.when(pl.program_id(1) == 0)
    def _():
        sf_ref[...] = jnp.zeros_like(sf_ref)
        sb_ref[...] = jnp.zeros_like(sb_ref)

    for h in range(HG_HEADS):
        cols = slice(h * HG_DIM, (h + 1) * HG_DIM)
        o, sf = _hgrn_direction(qf_ref[:, cols], zf_ref[:, cols], vf_ref[:, cols],
                                lb_ref[h], sf_ref[h], False)
        of_ref[:, cols] = o.astype(of_ref.dtype)
        sf_ref[h] = sf
        o, sb = _hgrn_direction(qb_ref[:, cols], zb_ref[:, cols], vb_ref[:, cols],
                                lb_ref[HG_HEADS + h], sb_ref[h], True)
        ob_ref[:, cols] = o.astype(ob_ref.dtype)
        sb_ref[h] = sb


def _hgrn(proj, lbs, batch, seq):
    n = proj.shape[0]
    t = HGRN_TILE
    nblk = seq // t

    def fwd(colblk):
        return pl.BlockSpec((t, HG_WIDTH), lambda b, i: (b * nblk + i, colblk))

    def bwd(colblk):
        return pl.BlockSpec((t, HG_WIDTH), lambda b, i: (b * nblk + nblk - 1 - i, colblk))

    return pl.pallas_call(
        _hgrn_kernel,
        grid=(batch, nblk),
        in_specs=[
            fwd(COL_HQ), fwd(COL_HF_FWD), fwd(COL_HI),
            bwd(COL_HQ), bwd(COL_HF_BWD), bwd(COL_HI),
            pl.BlockSpec((2 * HG_HEADS, 1, HG_DIM), lambda b, i: (0, 0, 0)),
        ],
        out_specs=[fwd(0), bwd(0)],
        out_shape=[jax.ShapeDtypeStruct((n, HG_WIDTH), BF16)] * 2,
        scratch_shapes=[pltpu.VMEM((HG_HEADS, HG_DIM, HG_DIM), F32)] * 2,
        compiler_params=pltpu.CompilerParams(
            dimension_semantics=("parallel", "arbitrary"), vmem_limit_bytes=VMEM_LIMIT),
        name="hgrn",
    )(proj, proj, proj, proj, proj, proj, lbs)


def _attn_kernel(q_ref, kc_ref, kp_ref, kn_ref, vc_ref, vp_ref, vn_ref, o_ref, l_ref,
                 kw_ref, vw_ref, *, tq, length):
    n = pl.program_id(2)
    half = ATT_HALF
    kw_ref[0:half, :] = kp_ref[...]
    kw_ref[half:half + tq, :] = kc_ref[...]
    kw_ref[half + tq:, :] = kn_ref[...]
    vw_ref[0:half, :] = vp_ref[...]
    vw_ref[half:half + tq, :] = vc_ref[...]
    vw_ref[half + tq:, :] = vn_ref[...]

    qb_rows = ATT_QBLOCK
    win = qb_rows + 2 * half
    i_idx = lax.broadcasted_iota(jnp.int32, (qb_rows, win), 0)
    j_idx = lax.broadcasted_iota(jnp.int32, (qb_rows, win), 1)
    band = (j_idx >= i_idx) & (j_idx <= i_idx + 2 * half)

    pairs = [slice(p * LANES, (p + 1) * LANES) for p in range(ATT_HEADS // 2)]
    head_lane = lax.broadcasted_iota(jnp.int32, (qb_rows, LANES), 1)
    even_half = head_lane < ATT_DIM
    keep_even = jnp.where(lax.broadcasted_iota(jnp.int32, (1, LANES), 1) < ATT_DIM, 1.0, 0.0).astype(BF16)
    keep_odd = (1.0 - keep_even.astype(F32)).astype(BF16)

    def masked_scores(r0):
        base = n * tq + r0 - half
        valid = band & (j_idx >= -base) & (j_idx < length - base)
        q = q_ref[r0:r0 + qb_rows, :]
        kw = kw_ref[r0:r0 + win, :]
        out = []
        for cs in pairs:
            for keep in (keep_even, keep_odd):
                out.append(jnp.where(valid, _dot_nt(q[:, cs] * keep, kw[:, cs]), NEG))
        return out

    blocks = list(range(0, tq, qb_rows))
    nxt = masked_scores(blocks[0])
    for pos, r0 in enumerate(blocks):
        scores = nxt
        if pos + 1 < len(blocks):
            nxt = masked_scores(blocks[pos + 1])
        vw = vw_ref[r0:r0 + win, :]
        maxes = [jnp.max(s, axis=-1, keepdims=True) for s in scores]
        probs = [jnp.exp(s - m) for s, m in zip(scores, maxes)]
        dens = [jnp.sum(p, axis=-1, keepdims=True) for p in probs]
        outs = []
        for p, cs in enumerate(pairs):
            pv_even = _dot(probs[2 * p].astype(BF16), vw[:, cs])
            pv_odd = _dot(probs[2 * p + 1].astype(BF16), vw[:, cs])
            outs.append(jnp.where(even_half, pv_even, pv_odd)
                        / jnp.where(even_half, dens[2 * p], dens[2 * p + 1]))
        o_ref[r0:r0 + qb_rows, :] = jnp.concatenate(outs, axis=1).astype(o_ref.dtype)
        lse = jnp.zeros((qb_rows, LANES), F32)
        for h, (m, den) in enumerate(zip(maxes, dens)):
            lse = jnp.where(head_lane == h, m + jnp.log(den), lse)
        l_ref[r0:r0 + qb_rows, :] = lse


def _attention(qkv, col0):
    batch, dil, length, _ = qkv.shape
    tq = min(ATT_TILE, length)
    nq = length // tq
    hb = tq // ATT_HALF
    n_hblk = length // ATT_HALF

    def cur(col):
        return pl.BlockSpec((None, None, tq, ATT_WIDTH), lambda b, r, n: (b, r, n, col))

    def prev(col):
        return pl.BlockSpec((None, None, ATT_HALF, ATT_WIDTH),
                            lambda b, r, n: (b, r, jnp.maximum(n * hb - 1, 0), col))

    def nxt(col):
        return pl.BlockSpec((None, None, ATT_HALF, ATT_WIDTH),
                            lambda b, r, n: (b, r, jnp.minimum((n + 1) * hb, n_hblk - 1), col))

    out_spec = pl.BlockSpec((None, None, tq, ATT_WIDTH), lambda b, r, n: (b, r, n, 0))
    return pl.pallas_call(
        functools.partial(_attn_kernel, tq=tq, length=length),
        grid=(batch, dil, nq),
        in_specs=[cur(col0), cur(col0 + 1), prev(col0 + 1), nxt(col0 + 1),
                  cur(col0 + 2), prev(col0 + 2), nxt(col0 + 2)],
        out_specs=[out_spec, pl.BlockSpec((None, None, tq, LANES), lambda b, r, n: (b, r, n, 0))],
        out_shape=[jax.ShapeDtypeStruct((batch, dil, length, ATT_WIDTH), BF16),
                   jax.ShapeDtypeStruct((batch, dil, length, LANES), F32)],
        scratch_shapes=[pltpu.VMEM((tq + 2 * ATT_HALF, ATT_WIDTH), BF16)] * 2,
        compiler_params=pltpu.CompilerParams(
            dimension_semantics=("parallel", "parallel", "parallel"), vmem_limit_bytes=VMEM_LIMIT),
        name=f"attn_d{dil}",
    )(qkv, qkv, qkv, qkv, qkv, qkv, qkv)


def _token_major(src_ref, stage_ref, tmp_ref):
    dil, rows = src_ref.shape[1], src_ref.shape[2]
    nc = src_ref.shape[3] // LANES
    for c in range(nc):
        cols = slice(c * LANES, (c + 1) * LANES)
        if dil == 4:
            for r in range(dil):
                stage_ref[c, pl.ds(r, rows, stride=dil), :] = src_ref[0, r, :, cols].astype(F32)
        else:
            group = 4 * rows
            for r4 in range(4):
                for m in range(4):
                    tmp_ref[c, pl.ds(r4 * group + m, rows, stride=4), :] = (
                        src_ref[0, r4 + 4 * m, :, cols].astype(F32))
            for r4 in range(4):
                stage_ref[c, pl.ds(r4, group, stride=4), :] = tmp_ref[c, r4 * group:(r4 + 1) * group, :]
    return jnp.concatenate([stage_ref[c] for c in range(nc)], axis=1)


def _outproj_kernel(of_ref, ob_ref, hg_ref, o1_ref, o2_ref, o3_ref, l1_ref, l2_ref, l3_ref,
                    x_ref, og_ref, w_ref, g2_ref, wrh_ref, wrl_ref, br_ref,
                    x2_ref, xn_ref, gate_ref, idx_ref, st_o2, st_o3, st_l2, st_l3, st_tmp_o, st_tmp_l):
    o = of_ref[...].astype(F32) + ob_ref[...].astype(F32)
    hg = hg_ref[...].astype(F32)
    parts = []
    for h in range(HG_HEADS):
        blk = o[:, h * HG_DIM:(h + 1) * HG_DIM]
        ms = jnp.mean(blk * blk, axis=-1, keepdims=True)
        parts.append(blk * lax.rsqrt(ms + EPS) * og_ref[...])
    o_hg = jnp.concatenate(parts, axis=1) * (hg * _sigmoid(hg))

    l1 = l1_ref[...]
    l2 = _token_major(l2_ref, st_l2, st_tmp_l)
    l3 = _token_major(l3_ref, st_l3, st_tmp_l)
    mx = jnp.maximum(jnp.maximum(l1, l2), l3)
    e1, e2, e3 = jnp.exp(l1 - mx), jnp.exp(l2 - mx), jnp.exp(l3 - mx)
    den = e1 + e2 + e3
    er = lax.broadcasted_iota(jnp.int32, (LANES, ATT_WIDTH), 0)
    ec = lax.broadcasted_iota(jnp.int32, (LANES, ATT_WIDTH), 1)
    expand = jnp.where(er == ec // ATT_DIM, 1.0, 0.0).astype(BF16)

    def per_lane(w):
        return _dot(w.astype(BF16), expand)

    o_att = (per_lane(e1 / den) * o1_ref[...].astype(F32)
             + per_lane(e2 / den) * _token_major(o2_ref, st_o2, st_tmp_o)
             + per_lane(e3 / den) * _token_major(o3_ref, st_o3, st_tmp_o))

    y = _dot(o_hg.astype(BF16), w_ref[0:HG_WIDTH, :]) + _dot(o_att.astype(BF16), w_ref[HG_WIDTH:, :])
    x2 = x_ref[...] + y
    x2_ref[...] = x2

    ms = jnp.mean(x2 * x2, axis=-1, keepdims=True)
    unit = x2 * lax.rsqrt(ms + EPS)
    xn = unit * g2_ref[0:1, :]
    _rows_to_tiles(xn_ref, unit * g2_ref[1:2, :])
    xn_hi = xn.astype(BF16)
    xn_lo = (xn - xn_hi.astype(F32)).astype(BF16)
    logits = (_dot(xn_hi, wrh_ref[...]) + _dot(xn_lo, wrh_ref[...]) + _dot(xn_hi, wrl_ref[...])
              + br_ref[...])

    lane = lax.broadcasted_iota(jnp.int32, logits.shape, 1)
    lane_f = lane.astype(F32)
    work = jnp.where(lane < N_EXPERTS, logits, -jnp.inf)
    vals, idxs = [], []
    for _ in range(TOP_K):
        m = jnp.max(work, axis=-1, keepdims=True)
        idx = jnp.min(jnp.where(work == m, lane_f, float(LANES)), axis=-1, keepdims=True)
        vals.append(m)
        idxs.append(idx)
        work = jnp.where(lane_f == idx, -jnp.inf, work)
    es = [jnp.exp(v - vals[0]) for v in vals]
    den = es[0] + es[1] + es[2] + es[3]
    gate_out = jnp.zeros(logits.shape, F32)
    idx_out = jnp.zeros(logits.shape, F32)
    for k in range(TOP_K):
        gate_out = jnp.where(lane == k, es[k] / den, gate_out)
        idx_out = jnp.where(lane == k, idxs[k], idx_out)
    gate_ref[...] = gate_out[:, :ROUTE_LANES]
    idx_ref[...] = idx_out[:, :ROUTE_LANES].astype(jnp.int32)


def _outproj(o_f, o_b, proj, atts, lses, x2d, og, w_out_bf16, g2, wr_hi, wr_lo, br, seq):
    n = x2d.shape[0]
    t = TOKEN_TILE
    nt = seq // t
    row = lambda i: (i, 0)
    const = lambda i: (0, 0)
    half = pl.BlockSpec((t, 512), row)

    def residue_major(dil, width):
        return pl.BlockSpec((1, dil, t // dil, width), lambda i: (i // nt, 0, i % nt, 0))

    rm4, rm16 = residue_major(4, ATT_WIDTH), residue_major(16, ATT_WIDTH)
    lse, lse4, lse16 = pl.BlockSpec((t, LANES), row), residue_major(4, LANES), residue_major(16, LANES)
    wide, narrow = pltpu.VMEM((ATT_WIDTH // LANES, t, LANES), F32), pltpu.VMEM((1, t, LANES), F32)
    return pl.pallas_call(
        _outproj_kernel,
        grid=(n // t,),
        in_specs=[
            half, half, pl.BlockSpec((t, 512), lambda i: (i, COL_HG)),
            half, rm4, rm16, lse, lse4, lse16,
            pl.BlockSpec((t, D_MODEL), row),
            pl.BlockSpec((1, HG_DIM), const),
            pl.BlockSpec((D_MODEL, D_MODEL), const),
            pl.BlockSpec((2, D_MODEL), const),
            pl.BlockSpec((D_MODEL, LANES), const),
            pl.BlockSpec((D_MODEL, LANES), const),
            pl.BlockSpec((1, LANES), const),
        ],
        out_specs=[pl.BlockSpec((t, D_MODEL), row), pl.BlockSpec((t * ROW_TILES, LANES), row),
                   pl.BlockSpec((t, ROUTE_LANES), row), pl.BlockSpec((t, ROUTE_LANES), row)],
        out_shape=[jax.ShapeDtypeStruct((n, D_MODEL), F32), jax.ShapeDtypeStruct((n * ROW_TILES, LANES), F32),
                   jax.ShapeDtypeStruct((n, ROUTE_LANES), F32), jax.ShapeDtypeStruct((n, ROUTE_LANES), jnp.int32)],
        scratch_shapes=[wide, wide, narrow, narrow, wide, narrow],
        compiler_params=pltpu.CompilerParams(
            dimension_semantics=("parallel",), vmem_limit_bytes=VMEM_LIMIT),
        name="outproj",
    )(o_f, o_b, proj, *atts, *lses, x2d, og, w_out_bf16, g2, wr_hi, wr_lo, br)


def _moe_kernel(bexp_ref, bfirst_ref, bslot_ref, bnext_ref, bvalid_ref, nused_ref,
                xs_ref, xinv_ref, wu_hbm, bg_ref, bl_ref, wd_hbm, bd_ref,
                y_ref, wu_buf, wd_buf, wg_s, wl_s, wd_s, up_unscale_s, down_unscale_s, sems):
    i = pl.program_id(0)

    def weight_copies(expert, slot):
        return (pltpu.make_async_copy(wu_hbm.at[expert], wu_buf.at[slot], sems.at[0, slot]),
                pltpu.make_async_copy(wd_hbm.at[expert], wd_buf.at[slot], sems.at[1, slot]))

    @pl.when(bfirst_ref[i] == 1)
    def _():
        slot = bslot_ref[i]

        @pl.when(i == 0)
        def _():
            for cp in weight_copies(bexp_ref[0], 0):
                cp.start()

        for cp in weight_copies(bexp_ref[i], slot):
            cp.wait()

        @pl.when(bnext_ref[i] >= 0)
        def _():
            for cp in weight_copies(bnext_ref[i], 1 - slot):
                cp.start()

        su = _fp8_scale(_abs_max(wu_buf[slot]))
        sd = _fp8_scale(_abs_max(wd_buf[slot]))
        up_unscale_s[...] = xinv_ref[...] / su
        down_unscale_s[...] = (1.0 / ACT_FP8_SCALE) / sd
        r = lax.broadcasted_iota(jnp.int32, (2 * LANES, 2 * LANES), 0)
        c = lax.broadcasted_iota(jnp.int32, (2 * LANES, 2 * LANES), 1)
        src = jnp.where(c < LANES, 2 * c, 2 * (c - LANES) + 1)
        perm = jnp.where(r == src, 1.0, 0.0).astype(BF16)
        rows = 256
        for rb in range(D_MODEL // rows):
            rs = slice(rb * rows, (rb + 1) * rows)
            for cb in range(D_EXPERT // LANES):
                w = wu_buf[slot, rs, cb * 2 * LANES:(cb + 1) * 2 * LANES].astype(BF16)
                split = (_dot(w, perm) * su).astype(F8)
                wg_s[rs, cb * LANES:(cb + 1) * LANES] = split[:, :LANES]
                wl_s[rs, cb * LANES:(cb + 1) * LANES] = split[:, LANES:]
        wd_s[...] = (wd_buf[slot] * sd).astype(F8)

    def expert_mlp(rows):
        x8 = jnp.concatenate(_tiles_to_rows(xs_ref, rows), axis=1).astype(F8)
        hglu = _dot(x8, wg_s[...]) * up_unscale_s[...] + bg_ref[0]
        hlin = _dot(x8, wl_s[...]) * up_unscale_s[...] + bl_ref[0]
        glu = jnp.minimum(hglu, SWIGLU_LIMIT)
        lin = jnp.clip(hlin, -SWIGLU_LIMIT, SWIGLU_LIMIT)
        act = glu * _sigmoid(SWIGLU_ALPHA * glu) * (lin + 1.0)
        y = _dot((act * ACT_FP8_SCALE).astype(F8), wd_s[...]) * down_unscale_s[...] + bd_ref[0]
        _rows_to_tiles(y_ref, y)

    valid = bvalid_ref[i]
    half = MOE_BLOCK_ROWS // 2

    @pl.when(valid > half)
    def _():
        expert_mlp(MOE_BLOCK_ROWS)

    @pl.when((valid > 0) & (valid <= half))
    def _():
        expert_mlp(half)
        y_ref[half * ROW_TILES:, :] = jnp.zeros((half * ROW_TILES, LANES), F32)

    @pl.when(valid == 0)
    def _():
        y_ref[...] = jnp.zeros_like(y_ref)


def _moe(block_exp, block_first, block_slot, block_next, block_valid, n_used,
         xs, x_unscale, w_up, b_glu, b_lin, w_down, b_down):
    p_rows = xs.shape[0] // ROW_TILES
    bm = MOE_BLOCK_ROWS
    nb = p_rows // bm
    exp3 = lambda i, be, *_: (be[i], 0, 0)
    grid_spec = pltpu.PrefetchScalarGridSpec(
        num_scalar_prefetch=6,
        grid=(nb,),
        in_specs=[
            pl.BlockSpec((bm * ROW_TILES, LANES), lambda i, be, bf, bs, bn, bv, nu: (jnp.minimum(i, nu[0] - 1), 0)),
            pl.BlockSpec((1, 1), lambda i, *_: (0, 0)),
            pl.BlockSpec(memory_space=pl.ANY),
            pl.BlockSpec((1, 1, D_EXPERT), exp3),
            pl.BlockSpec((1, 1, D_EXPERT), exp3),
            pl.BlockSpec(memory_space=pl.ANY),
            pl.BlockSpec((1, 1, D_MODEL), exp3),
        ],
        out_specs=pl.BlockSpec((bm * ROW_TILES, LANES), lambda i, *_: (i, 0)),
        scratch_shapes=[pltpu.VMEM((2, D_MODEL, 2 * D_EXPERT), F32), pltpu.VMEM((2, D_EXPERT, D_MODEL), F32),
                        pltpu.VMEM((D_MODEL, D_EXPERT), F8), pltpu.VMEM((D_MODEL, D_EXPERT), F8),
                        pltpu.VMEM((D_EXPERT, D_MODEL), F8), pltpu.VMEM((1, 1), F32), pltpu.VMEM((1, 1), F32),
                        pltpu.SemaphoreType.DMA((2, 2))],
    )
    return pl.pallas_call(
        _moe_kernel,
        grid_spec=grid_spec,
        out_shape=jax.ShapeDtypeStruct((p_rows * ROW_TILES, LANES), F32),
        compiler_params=pltpu.CompilerParams(
            dimension_semantics=("arbitrary",), vmem_limit_bytes=VMEM_LIMIT),
        name="moe",
    )(block_exp, block_first, block_slot, block_next, block_valid, n_used,
      xs, x_unscale, w_up, b_glu, b_lin, w_down, b_down)


def _dispatch_kernel(dest_ref, zstart_ref, xn_ref, xs_hbm, zero_ref, sem, zsem):
    i = pl.program_id(0)
    bm = MOE_BLOCK_ROWS
    tokens = xn_ref.shape[0] // ROW_TILES

    def zero_copy(e):
        start = pl.multiple_of(zstart_ref[e] * ROW_TILES, bm * ROW_TILES)
        return pltpu.make_async_copy(zero_ref, xs_hbm.at[pl.ds(start, bm * ROW_TILES)], zsem)

    @pl.when(i == 0)
    def _():
        zero_ref[...] = jnp.zeros_like(zero_ref)
        for e in range(N_EXPERTS):
            @pl.when(zstart_ref[e] >= 0)
            def _():
                zero_copy(e).start()
        for e in range(N_EXPERTS):
            @pl.when(zstart_ref[e] >= 0)
            def _():
                zero_copy(e).wait()

    t0 = i * tokens

    def body(j, carry):
        src = xn_ref.at[pl.ds(pl.multiple_of(j * ROW_TILES, ROW_TILES), ROW_TILES)]
        for k in range(TOP_K):
            d = pl.multiple_of(dest_ref[k * (dest_ref.shape[0] // TOP_K) + t0 + j] * ROW_TILES, ROW_TILES)
            pltpu.make_async_copy(src, xs_hbm.at[pl.ds(d, ROW_TILES)], sem).start(priority=k % 2)
        return carry

    lax.fori_loop(0, tokens, body, 0, unroll=8)
    for k in range(TOP_K):
        pltpu.make_async_copy(xn_ref, xs_hbm.at[pl.ds(0, tokens * ROW_TILES)], sem).wait()


def _dispatch(dest, zstart, xn, p_rows):
    n = xn.shape[0] // ROW_TILES
    t = DISPATCH_TOKENS
    grid_spec = pltpu.PrefetchScalarGridSpec(
        num_scalar_prefetch=2,
        grid=(n // t,),
        in_specs=[pl.BlockSpec((t * ROW_TILES, LANES), lambda i, d, z: (i, 0))],
        out_specs=pl.BlockSpec(memory_space=pl.ANY),
        scratch_shapes=[pltpu.VMEM((MOE_BLOCK_ROWS * ROW_TILES, LANES), F32),
                        pltpu.SemaphoreType.DMA, pltpu.SemaphoreType.DMA],
    )
    return pl.pallas_call(
        _dispatch_kernel,
        grid_spec=grid_spec,
        out_shape=jax.ShapeDtypeStruct((p_rows * ROW_TILES, LANES), F32),
        compiler_params=pltpu.CompilerParams(
            dimension_semantics=("arbitrary",), vmem_limit_bytes=VMEM_LIMIT),
        name="dispatch",
    )(dest, zstart, xn)


def _combine_kernel(dest_ref, y_hbm, x2_ref, gate_ref, out_ref, buf_ref, sems):
    i = pl.program_id(0)
    tc = COMBINE_TOKENS

    def issue(step, slot):
        t0 = step * tc

        group = 8

        def body(g, carry):
            for k in range(TOP_K):
                for u in range(group):
                    j = g * group + u
                    d = pl.multiple_of(dest_ref[k * (dest_ref.shape[0] // TOP_K) + t0 + j] * ROW_TILES, ROW_TILES)
                    r = pl.multiple_of((k * tc + j) * ROW_TILES, ROW_TILES)
                    pltpu.make_async_copy(y_hbm.at[pl.ds(d, ROW_TILES)], buf_ref.at[slot, pl.ds(r, ROW_TILES)],
                                          sems.at[slot]).start(priority=u % 2)
            return carry

        lax.fori_loop(0, tc // group, body, 0)

    @pl.when(i == 0)
    def _():
        issue(0, 0)

    @pl.when(i + 1 < pl.num_programs(0))
    def _():
        issue(i + 1, (i + 1) % 2)

    slot = i % 2
    pltpu.make_async_copy(y_hbm.at[pl.ds(0, TOP_K * tc * ROW_TILES)], buf_ref.at[slot], sems.at[slot]).wait()
    rows = buf_ref.at[slot]
    sub = 64
    for r0 in range(0, tc, sub):
        gate = gate_ref[r0:r0 + sub, :]
        gates = [jnp.broadcast_to(gate[:, k:k + 1], (sub, LANES)) for k in range(TOP_K)]
        for c in range(ROW_TILES):
            acc = x2_ref[r0:r0 + sub, c * LANES:(c + 1) * LANES]
            for k in range(TOP_K):
                acc = acc + gates[k] * rows[pl.ds((k * tc + r0) * ROW_TILES + c, sub, stride=ROW_TILES), :]
            out_ref[r0:r0 + sub, c * LANES:(c + 1) * LANES] = acc


def _combine(dest, y, x2, gates):
    n = x2.shape[0]
    tc = COMBINE_TOKENS
    grid_spec = pltpu.PrefetchScalarGridSpec(
        num_scalar_prefetch=1,
        grid=(n // tc,),
        in_specs=[pl.BlockSpec(memory_space=pl.ANY),
                  pl.BlockSpec((tc, D_MODEL), lambda i, d: (i, 0)),
                  pl.BlockSpec((tc, ROUTE_LANES), lambda i, d: (i, 0))],
        out_specs=pl.BlockSpec((tc, D_MODEL), lambda i, d: (i, 0)),
        scratch_shapes=[pltpu.VMEM((2, TOP_K * tc * ROW_TILES, LANES), F32), pltpu.SemaphoreType.DMA((2,))],
    )
    return pl.pallas_call(
        _combine_kernel,
        grid_spec=grid_spec,
        out_shape=jax.ShapeDtypeStruct((n, D_MODEL), F32),
        compiler_params=pltpu.CompilerParams(
            dimension_semantics=("arbitrary",), vmem_limit_bytes=VMEM_LIMIT),
        name="combine",
    )(dest, y, x2, gates)


def _route(top_idx):
    n = top_idx.shape[0]
    a = n * TOP_K
    bm = MOE_BLOCK_ROWS
    nb = a // bm + N_EXPERTS
    experts = jnp.arange(N_EXPERTS, dtype=jnp.int32)[None, :]
    picked = [top_idx[:, k:k + 1] == experts for k in range(TOP_K)]
    hits = sum(p.astype(jnp.int32) for p in picked)
    csum = jnp.cumsum(hits, axis=0)
    counts = csum[-1]
    padded = ((counts + bm - 1) // bm) * bm
    pad_end = jnp.cumsum(padded)
    pad_start = pad_end - padded
    row_of = csum - hits + pad_start[None, :]
    dest = jnp.concatenate([jnp.sum(jnp.where(p, row_of, 0), axis=1) for p in picked])
    n_used = (pad_end[-1] // bm).astype(jnp.int32)
    blk = jnp.arange(nb, dtype=jnp.int32)
    bexp = jnp.sum((pad_end[None, :] <= (blk * bm)[:, None]).astype(jnp.int32), axis=1)
    bexp = jnp.minimum(bexp, N_EXPERTS - 1)
    bexp = jnp.where(blk < n_used, bexp, bexp[jnp.maximum(n_used - 1, 0)])
    bfirst = jnp.concatenate([jnp.ones((1,), jnp.int32), (bexp[1:] != bexp[:-1]).astype(jnp.int32)])
    bslot = (jnp.cumsum(bfirst) - 1) % 2
    later = jnp.where(bexp[None, :] > bexp[:, None], bexp[None, :], N_EXPERTS)
    bnext = jnp.min(later, axis=1)
    bnext = jnp.where(bnext < N_EXPERTS, bnext, -1).astype(jnp.int32)
    valid_end = jnp.sum(jnp.where(bexp[:, None] == jnp.arange(N_EXPERTS)[None, :], (pad_start + counts)[None, :], 0), axis=1)
    bvalid = jnp.clip(valid_end - blk * bm, 0, bm).astype(jnp.int32)
    zstart = jnp.where(counts > 0, pad_end - bm, -1).astype(jnp.int32)
    return dest.astype(jnp.int32), zstart, (bexp, bfirst, bslot.astype(jnp.int32), bnext, bvalid, n_used.reshape(1))


def kernel(x, positions, norm1_g, w_in, q_norm_g, k_norm_g, hgrn_lower_bounds, hgrn_onorm_g,
           w_out, norm2_g, w_router, b_router, w_up, b_up, w_down, b_down):
    batch, seq, d = x.shape
    n = batch * seq
    depth = norm1_g.shape[0]
    lbs_all = jnp.cumsum(jax.nn.softmax(hgrn_lower_bounds.astype(F32), axis=0), axis=0)
    half = ATT_DIM // 2
    inv = 1.0 / (ROPE_THETA ** (jnp.arange(half, dtype=F32) / half))
    inv_tab = jnp.tile(inv, LANES // half).reshape(1, LANES)
    pos_col = positions.reshape(n, 1)

    x2d = x.reshape(n, d)
    for l in range(depth):
        lbs = lbs_all[l].reshape(2 * HG_HEADS, 1, HG_DIM)
        proj, qkv4, qkv16 = _inproj(
            x2d, pos_col, inv_tab, norm1_g[l].reshape(1, d), w_in[l].astype(BF16),
            jnp.tile(q_norm_g[l], LANES // ATT_DIM).reshape(1, LANES),
            jnp.tile(k_norm_g[l], LANES // ATT_DIM).reshape(1, LANES), batch, seq)
        o_f, o_b = _hgrn(proj, lbs, batch, seq)
        o1, l1 = _attention(proj.reshape(batch, 1, seq, IN_COLS), COL_AQ)
        o4, l4 = _attention(qkv4, 0)
        o16, l16 = _attention(qkv16, 0)
        atts = [o1.reshape(n, ATT_WIDTH), o4, o16]
        lses = [l1.reshape(n, LANES), l4, l16]

        wr = jnp.pad(w_router[l], ((0, 0), (0, LANES - N_EXPERTS)))
        wr_hi = wr.astype(BF16)
        wr_lo = (wr - wr_hi.astype(F32)).astype(BF16)
        br = jnp.pad(b_router[l], (0, LANES - N_EXPERTS)).reshape(1, LANES)
        g2 = norm2_g[l].astype(F32)
        g2_max = jnp.max(jnp.abs(g2))
        x_scale = jnp.where(g2_max > 0.0, ROW_FP8_TARGET / g2_max, 1.0)
        x2, xn, gates, top_idx = _outproj(
            o_f, o_b, proj, atts, lses, x2d, hgrn_onorm_g[l].reshape(1, HG_DIM),
            w_out[l].astype(BF16), jnp.stack([g2, g2 * x_scale]), wr_hi, wr_lo, br, seq)
        dest, zstart, blocks = _route(top_idx[:, :TOP_K])
        xs = _dispatch(dest, zstart, xn, blocks[0].shape[0] * MOE_BLOCK_ROWS)
        y = _moe(*blocks, xs, (1.0 / x_scale).reshape(1, 1), w_up[l],
                 b_up[l][:, 0::2].reshape(N_EXPERTS, 1, D_EXPERT),
                 b_up[l][:, 1::2].reshape(N_EXPERTS, 1, D_EXPERT),
                 w_down[l], b_down[l].reshape(N_EXPERTS, 1, D_MODEL))
        x2d = _combine(dest, y, x2, gates)
    return x2d.reshape(batch, seq, d)
```

```python
import functools

import jax
import jax.numpy as jnp
from jax import lax
from jax.experimental import pallas as pl
from jax.experimental.pallas import tpu as pltpu

F32 = jnp.float32
BF16 = jnp.bfloat16
F8 = jnp.float8_e4m3fn
FP8_TARGET = 240.0
ROW_FP8_TARGET = 8.0
ACT_FP8_SCALE = 4.0

D_MODEL = 1024
HG_HEADS = 4
HG_DIM = 128
HG_WIDTH = HG_HEADS * HG_DIM
HG_CHUNK = 64
ATT_HEADS = 8
ATT_DIM = 64
ATT_WIDTH = ATT_HEADS * ATT_DIM
DILATED_PATTERNS = ((128, 1), (512, 4), (2048, 16))
ATT_HALF = 64
assert all(window // (2 * dil) == ATT_HALF for window, dil in DILATED_PATTERNS)
assert tuple(dil for _, dil in DILATED_PATTERNS) == (1, 4, 16)
ATT_QBLOCK = 128
ROPE_THETA = 10000.0
IN_COLS = 5 * HG_WIDTH + 3 * ATT_WIDTH
N_EXPERTS = 32
TOP_K = 4
D_EXPERT = D_MODEL
SWIGLU_LIMIT = 7.0
SWIGLU_ALPHA = 1.702
EPS = 1e-6
NEG = -1e30

COL_HQ, COL_HF_FWD, COL_HF_BWD, COL_HI, COL_HG, COL_AQ, COL_AK, COL_AV = range(8)

TOKEN_TILE = 1024
INPROJ_TILE = 1024
HGRN_TILE = 512
ATT_TILE = 1024
MOE_BLOCK_ROWS = 512
DISPATCH_TOKENS = 2048
COMBINE_TOKENS = 256
LANES = 128
ROW_TILES = D_MODEL // LANES
VMEM_LIMIT = 60 * 1024 * 1024


def _dot(a, b):
    return jnp.dot(a, b, preferred_element_type=F32)


def _dot_nt(a, b):
    return lax.dot_general(a, b, (((1,), (1,)), ((), ())), preferred_element_type=F32)


def _dot_tn(a, b):
    return lax.dot_general(a, b, (((0,), (0,)), ((), ())), preferred_element_type=F32)


def _sigmoid(x):
    return 0.5 * jnp.tanh(0.5 * x) + 0.5


def _abs_max(x):
    return jnp.max(jnp.max(jnp.abs(x), axis=0, keepdims=True), axis=1, keepdims=True)


def _fp8_scale(amax):
    return jnp.where(amax > 0.0, FP8_TARGET / amax, 1.0)


def _rows_to_tiles(dst_ref, x):
    for c in range(ROW_TILES):
        dst_ref[pl.ds(c, x.shape[0], stride=ROW_TILES), :] = x[:, c * LANES:(c + 1) * LANES]


def _tiles_to_rows(src, rows, first_row=0):
    return [src[pl.ds(first_row * ROW_TILES + c, rows, stride=ROW_TILES), :] for c in range(ROW_TILES)]


def _head_norm_rope(p, gain, cos, sin_signed, scale):
    lane = lax.broadcasted_iota(jnp.int32, (p.shape[0], LANES), 1)
    low = lane < ATT_DIM
    first_half = (lane % ATT_DIM) < (ATT_DIM // 2)
    outs = []
    for t in range(ATT_WIDTH // LANES):
        blk = p[:, t * LANES:(t + 1) * LANES]
        sq = blk * blk
        s_low = jnp.sum(jnp.where(low, sq, 0.0), axis=-1, keepdims=True)
        s_high = jnp.sum(jnp.where(low, 0.0, sq), axis=-1, keepdims=True)
        r = jnp.where(low, lax.rsqrt(s_low * (1.0 / ATT_DIM) + EPS),
                      lax.rsqrt(s_high * (1.0 / ATT_DIM) + EPS))
        y = blk * r * gain
        partner = jnp.where(first_half, pltpu.roll(y, LANES - ATT_DIM // 2, axis=1),
                            pltpu.roll(y, ATT_DIM // 2, axis=1))
        outs.append((y * cos + partner * sin_signed) * scale)
    return jnp.concatenate(outs, axis=1)


def _inproj_kernel(x_ref, pos_ref, inv_ref, g1_ref, w_ref, qg_ref, kg_ref,
                   out_ref, d4_ref, d16_ref, stage_ref, stage2_ref):
    x = x_ref[...]
    ms = jnp.mean(x * x, axis=-1, keepdims=True)
    h = (x * lax.rsqrt(ms + EPS) * g1_ref[...]).astype(BF16)
    ang = pos_ref[...].astype(F32) * inv_ref[...]
    lane = lax.broadcasted_iota(jnp.int32, ang.shape, 1)
    cos = jnp.cos(ang)
    sin_signed = jnp.where((lane % ATT_DIM) < (ATT_DIM // 2), -jnp.sin(ang), jnp.sin(ang))
    order = (COL_AQ, COL_AK, COL_AV, COL_HQ, COL_HF_FWD, COL_HF_BWD, COL_HI, COL_HG)
    nxt = _dot(h, w_ref[:, order[0] * 512:(order[0] + 1) * 512])
    for pos, j in enumerate(order):
        p = nxt
        if pos + 1 < len(order):
            jn = order[pos + 1]
            nxt = _dot(h, w_ref[:, jn * 512:(jn + 1) * 512])
        if j == COL_AQ:
            p = _head_norm_rope(p, qg_ref[...], cos, sin_signed, ATT_DIM ** -0.5)
        elif j == COL_AK:
            p = _head_norm_rope(p, kg_ref[...], cos, sin_signed, 1.0)
        out_ref[:, j * 512:(j + 1) * 512] = p.astype(BF16)
        if j >= COL_AQ:
            rows4, rows16 = x.shape[0] // 4, x.shape[0] // 16
            for c in range(ATT_WIDTH // LANES):
                cols = slice((j - COL_AQ) * ATT_WIDTH + c * LANES, (j - COL_AQ) * ATT_WIDTH + (c + 1) * LANES)
                stage_ref[c] = p[:, c * LANES:(c + 1) * LANES]
                for r4 in range(4):
                    group = stage_ref[c, pl.ds(r4, rows4, stride=4), :]
                    d4_ref[0, r4, :, cols] = group.astype(BF16)
                    stage2_ref[c, r4 * rows4:(r4 + 1) * rows4, :] = group
                for r4 in range(4):
                    for m in range(4):
                        d16_ref[0, r4 + 4 * m, :, cols] = (
                            stage2_ref[c, pl.ds(r4 * rows4 + m, rows16, stride=4), :].astype(BF16))


def _inproj(x2d, pos_col, inv_tab, g1, w_in_bf16, qg, kg, batch, seq):
    n = x2d.shape[0]
    t = INPROJ_TILE
    nt = seq // t
    const = lambda i: (0, 0)
    qkv = 3 * ATT_WIDTH

    def residue_major(dil):
        spec = pl.BlockSpec((1, dil, t // dil, qkv), lambda i: (i // nt, 0, i % nt, 0))
        return spec, jax.ShapeDtypeStruct((batch, dil, seq // dil, qkv), BF16)

    spec4, shape4 = residue_major(4)
    spec16, shape16 = residue_major(16)
    return pl.pallas_call(
        _inproj_kernel,
        grid=(n // t,),
        in_specs=[
            pl.BlockSpec((t, D_MODEL), lambda i: (i, 0)),
            pl.BlockSpec((t, 1), lambda i: (i, 0)),
            pl.BlockSpec((1, LANES), const),
            pl.BlockSpec((1, D_MODEL), const),
            pl.BlockSpec((D_MODEL, IN_COLS), const, pipeline_mode=pl.Buffered(1)),
            pl.BlockSpec((1, LANES), const),
            pl.BlockSpec((1, LANES), const),
        ],
        out_specs=[pl.BlockSpec((t, IN_COLS), lambda i: (i, 0)), spec4, spec16],
        out_shape=[jax.ShapeDtypeStruct((n, IN_COLS), BF16), shape4, shape16],
        scratch_shapes=[pltpu.VMEM((ATT_WIDTH // LANES, t, LANES), F32)] * 2,
        compiler_params=pltpu.CompilerParams(
            dimension_semantics=("parallel",), vmem_limit_bytes=VMEM_LIMIT),
        name="inproj",
    )(x2d, pos_col, inv_tab, g1, w_in_bf16, qg, kg)


def _hgrn_direction(q, z, v, lb, state_t, reverse):
    c = HG_CHUNK
    t = q.shape[0]
    n = t // c
    row = lax.broadcasted_iota(jnp.int32, (c, c), 0)
    col = lax.broadcasted_iota(jnp.int32, (c, c), 1)
    mask = (row <= col) if reverse else (row >= col)
    tri = jnp.where(mask, 1.0, 0.0).astype(BF16)
    last_row = 0 if reverse else c - 1

    z = z.astype(F32)
    q = q.astype(F32)
    sg = _sigmoid(z)
    f = lb + (1.0 - lb) * sg
    k = (1.0 - lb) * (1.0 - sg)
    lf = jnp.log(f)
    lf_hi = lf.astype(BF16)
    lf_lo = (lf - lf_hi.astype(F32)).astype(BF16)
    chunks = [slice(j * c, (j + 1) * c) for j in range(n)]
    b = jnp.concatenate([_dot(tri, lf_hi[rs]) + _dot(tri, lf_lo[rs]) for rs in chunks], axis=0)
    b_last = b.reshape(n, c, HG_DIM)[:, last_row:last_row + 1, :]
    decay = jnp.exp(b_last)
    qt = (q * _sigmoid(q) * jnp.exp(b)).astype(BF16)
    kt_f32 = k * jnp.exp(-b)
    kt = kt_f32.astype(BF16)
    kd = (kt_f32.reshape(n, c, HG_DIM) * decay).reshape(t, HG_DIM).astype(BF16)

    outs, updates = [], []
    for rs in chunks:
        a = jnp.where(mask, _dot_nt(qt[rs], kt[rs]), 0.0)
        outs.append(_dot(a.astype(BF16), v[rs]))
        updates.append(_dot_tn(v[rs], kd[rs]))
    for j in (reversed(range(n)) if reverse else range(n)):
        outs[j] = outs[j] + _dot_nt(qt[chunks[j]], state_t.astype(BF16))
        state_t = state_t * decay[j] + updates[j]
    return jnp.concatenate(outs, axis=0), state_t


def _hgrn_kernel(qf_ref, zf_ref, vf_ref, qb_ref, zb_ref, vb_ref, lb_ref,
                 of_ref, ob_ref, sf_ref, sb_ref):
    @pl.when(pl.program_id(1) == 0)
    def _():
        sf_ref[...] = jnp.zeros_like(sf_ref)
        sb_ref[...] = jnp.zeros_like(sb_ref)

    for h in range(HG_HEADS):
        cols = slice(h * HG_DIM, (h + 1) * HG_DIM)
        o, sf = _hgrn_direction(qf_ref[:, cols], zf_ref[:, cols], vf_ref[:, cols],
                                lb_ref[h], sf_ref[h], False)
        of_ref[:, cols] = o.astype(of_ref.dtype)
        sf_ref[h] = sf
        o, sb = _hgrn_direction(qb_ref[:, cols], zb_ref[:, cols], vb_ref[:, cols],
                                lb_ref[HG_HEADS + h], sb_ref[h], True)
        ob_ref[:, cols] = o.astype(ob_ref.dtype)
        sb_ref[h] = sb


def _hgrn(proj, lbs, batch, seq):
    n = proj.shape[0]
    t = HGRN_TILE
    nblk = seq // t

    def fwd(colblk):
        return pl.BlockSpec((t, HG_WIDTH), lambda b, i: (b * nblk + i, colblk))

    def bwd(colblk):
        return pl.BlockSpec((t, HG_WIDTH), lambda b, i: (b * nblk + nblk - 1 - i, colblk))

    return pl.pallas_call(
        _hgrn_kernel,
        grid=(batch, nblk),
        in_specs=[
            fwd(COL_HQ), fwd(COL_HF_FWD), fwd(COL_HI),
            bwd(COL_HQ), bwd(COL_HF_BWD), bwd(COL_HI),
            pl.BlockSpec((2 * HG_HEADS, 1, HG_DIM), lambda b, i: (0, 0, 0)),
        ],
        out_specs=[fwd(0), bwd(0)],
        out_shape=[jax.ShapeDtypeStruct((n, HG_WIDTH), BF16)] * 2,
        scratch_shapes=[pltpu.VMEM((HG_HEADS, HG_DIM, HG_DIM), F32)] * 2,
        compiler_params=pltpu.CompilerParams(
            dimension_semantics=("parallel", "arbitrary"), vmem_limit_bytes=VMEM_LIMIT),
        name="hgrn",
    )(proj, proj, proj, proj, proj, proj, lbs)


def _attn_kernel(q_ref, kc_ref, kp_ref, kn_ref, vc_ref, vp_ref, vn_ref, o_ref, l_ref,
                 kw_ref, vw_ref, *, tq, length):
    n = pl.program_id(2)
    half = ATT_HALF
    kw_ref[0:half, :] = kp_ref[...]
    kw_ref[half:half + tq, :] = kc_ref[...]
    kw_ref[half + tq:, :] = kn_ref[...]
    vw_ref[0:half, :] = vp_ref[...]
    vw_ref[half:half + tq, :] = vc_ref[...]
    vw_ref[half + tq:, :] = vn_ref[...]

    qb_rows = ATT_QBLOCK
    win = qb_rows + 2 * half
    i_idx = lax.broadcasted_iota(jnp.int32, (qb_rows, win), 0)
    j_idx = lax.broadcasted_iota(jnp.int32, (qb_rows, win), 1)
    band = (j_idx >= i_idx) & (j_idx <= i_idx + 2 * half)

    pairs = [slice(p * LANES, (p + 1) * LANES) for p in range(ATT_HEADS // 2)]
    head_lane = lax.broadcasted_iota(jnp.int32, (qb_rows, LANES), 1)
    even_half = head_lane < ATT_DIM
    keep_even = jnp.where(lax.broadcasted_iota(jnp.int32, (1, LANES), 1) < ATT_DIM, 1.0, 0.0).astype(BF16)
    keep_odd = (1.0 - keep_even.astype(F32)).astype(BF16)

    def masked_scores(r0):
        base = n * tq + r0 - half
        valid = band & (j_idx >= -base) & (j_idx < length - base)
        q = q_ref[r0:r0 + qb_rows, :]
        kw = kw_ref[r0:r0 + win, :]
        out = []
        for cs in pairs:
            for keep in (keep_even, keep_odd):
                out.append(jnp.where(valid, _dot_nt(q[:, cs] * keep, kw[:, cs]), NEG))
        return out

    blocks = list(range(0, tq, qb_rows))
    nxt = masked_scores(blocks[0])
    for pos, r0 in enumerate(blocks):
        scores = nxt
        if pos + 1 < len(blocks):
            nxt = masked_scores(blocks[pos + 1])
        vw = vw_ref[r0:r0 + win, :]
        maxes = [jnp.max(s, axis=-1, keepdims=True) for s in scores]
        probs = [jnp.exp(s - m) for s, m in zip(scores, maxes)]
        dens = [jnp.sum(p, axis=-1, keepdims=True) for p in probs]
        outs = []
        for p, cs in enumerate(pairs):
            pv_even = _dot(probs[2 * p].astype(BF16), vw[:, cs])
            pv_odd = _dot(probs[2 * p + 1].astype(BF16), vw[:, cs])
            outs.append(jnp.where(even_half, pv_even, pv_odd)
                        / jnp.where(even_half, dens[2 * p], dens[2 * p + 1]))
        o_ref[r0:r0 + qb_rows, :] = jnp.concatenate(outs, axis=1).astype(o_ref.dtype)
        m_all = jnp.zeros((qb_rows, LANES), F32)
        den_all = jnp.ones((qb_rows, LANES), F32)
        for h, (m, den) in enumerate(zip(maxes, dens)):
            m_all = jnp.where(head_lane == h, m, m_all)
            den_all = jnp.where(head_lane == h, den, den_all)
        l_ref[r0:r0 + qb_rows, :] = m_all + jnp.log(den_all)


def _attention(qkv, col0):
    batch, dil, length, _ = qkv.shape
    tq = min(ATT_TILE, length)
    nq = length // tq
    hb = tq // ATT_HALF
    n_hblk = length // ATT_HALF

    def cur(col):
        return pl.BlockSpec((None, None, tq, ATT_WIDTH), lambda b, r, n: (b, r, n, col))

    def prev(col):
        return pl.BlockSpec((None, None, ATT_HALF, ATT_WIDTH),
                            lambda b, r, n: (b, r, jnp.maximum(n * hb - 1, 0), col))

    def nxt(col):
        return pl.BlockSpec((None, None, ATT_HALF, ATT_WIDTH),
                            lambda b, r, n: (b, r, jnp.minimum((n + 1) * hb, n_hblk - 1), col))

    out_spec = pl.BlockSpec((None, None, tq, ATT_WIDTH), lambda b, r, n: (b, r, n, 0))
    return pl.pallas_call(
        functools.partial(_attn_kernel, tq=tq, length=length),
        grid=(batch, dil, nq),
        in_specs=[cur(col0), cur(col0 + 1), prev(col0 + 1), nxt(col0 + 1),
                  cur(col0 + 2), prev(col0 + 2), nxt(col0 + 2)],
        out_specs=[out_spec, pl.BlockSpec((None, None, tq, LANES), lambda b, r, n: (b, r, n, 0))],
        out_shape=[jax.ShapeDtypeStruct((batch, dil, length, ATT_WIDTH), BF16),
                   jax.ShapeDtypeStruct((batch, dil, length, LANES), F32)],
        scratch_shapes=[pltpu.VMEM((tq + 2 * ATT_HALF, ATT_WIDTH), BF16)] * 2,
        compiler_params=pltpu.CompilerParams(
            dimension_semantics=("parallel", "parallel", "parallel"), vmem_limit_bytes=VMEM_LIMIT),
        name=f"attn_d{dil}",
    )(qkv, qkv, qkv, qkv, qkv, qkv, qkv)


def _token_major(src_ref, stage_ref, tmp_ref):
    dil, rows = src_ref.shape[1], src_ref.shape[2]
    nc = src_ref.shape[3] // LANES
    for c in range(nc):
        cols = slice(c * LANES, (c + 1) * LANES)
        if dil == 4:
            for r in range(dil):
                stage_ref[c, pl.ds(r, rows, stride=dil), :] = src_ref[0, r, :, cols].astype(F32)
        else:
            group = 4 * rows
            for r4 in range(4):
                for m in range(4):
                    tmp_ref[c, pl.ds(r4 * group + m, rows, stride=4), :] = (
                        src_ref[0, r4 + 4 * m, :, cols].astype(F32))
            for r4 in range(4):
                stage_ref[c, pl.ds(r4, group, stride=4), :] = tmp_ref[c, r4 * group:(r4 + 1) * group, :]
    return jnp.concatenate([stage_ref[c] for c in range(nc)], axis=1)


def _outproj_kernel(of_ref, ob_ref, hg_ref, o1_ref, o2_ref, o3_ref, l1_ref, l2_ref, l3_ref,
                    x_ref, og_ref, w_ref, g2_ref, wrh_ref, wrl_ref, br_ref,
                    x2_ref, xn_ref, gate_ref, idx_ref, st_o2, st_o3, st_l2, st_l3, st_tmp_o, st_tmp_l):
    o = of_ref[...].astype(F32) + ob_ref[...].astype(F32)
    hg = hg_ref[...].astype(F32)
    parts = []
    for h in range(HG_HEADS):
        blk = o[:, h * HG_DIM:(h + 1) * HG_DIM]
        ms = jnp.mean(blk * blk, axis=-1, keepdims=True)
        parts.append(blk * lax.rsqrt(ms + EPS) * og_ref[...])
    o_hg = jnp.concatenate(parts, axis=1) * (hg * _sigmoid(hg))

    l1 = l1_ref[...]
    l2 = _token_major(l2_ref, st_l2, st_tmp_l)
    l3 = _token_major(l3_ref, st_l3, st_tmp_l)
    mx = jnp.maximum(jnp.maximum(l1, l2), l3)
    e1, e2, e3 = jnp.exp(l1 - mx), jnp.exp(l2 - mx), jnp.exp(l3 - mx)
    den = e1 + e2 + e3
    er = lax.broadcasted_iota(jnp.int32, (LANES, ATT_WIDTH), 0)
    ec = lax.broadcasted_iota(jnp.int32, (LANES, ATT_WIDTH), 1)
    expand = jnp.where(er == ec // ATT_DIM, 1.0, 0.0).astype(BF16)

    def per_lane(w):
        return _dot(w.astype(BF16), expand)

    o_att = (per_lane(e1 / den) * o1_ref[...].astype(F32)
             + per_lane(e2 / den) * _token_major(o2_ref, st_o2, st_tmp_o)
             + per_lane(e3 / den) * _token_major(o3_ref, st_o3, st_tmp_o))

    y = _dot(o_hg.astype(BF16), w_ref[0:HG_WIDTH, :]) + _dot(o_att.astype(BF16), w_ref[HG_WIDTH:, :])
    x2 = x_ref[...] + y
    x2_ref[...] = x2

    ms = jnp.mean(x2 * x2, axis=-1, keepdims=True)
    unit = x2 * lax.rsqrt(ms + EPS)
    xn = unit * g2_ref[0:1, :]
    _rows_to_tiles(xn_ref, unit * g2_ref[1:2, :])
    xn_hi = xn.astype(BF16)
    xn_lo = (xn - xn_hi.astype(F32)).astype(BF16)
    logits = (_dot(xn_hi, wrh_ref[...]) + _dot(xn_lo, wrh_ref[...]) + _dot(xn_hi, wrl_ref[...])
              + br_ref[...])

    lane = lax.broadcasted_iota(jnp.int32, logits.shape, 1)
    lane_f = lane.astype(F32)
    work = jnp.where(lane < N_EXPERTS, logits, -jnp.inf)
    vals, idxs = [], []
    for _ in range(TOP_K):
        m = jnp.max(work, axis=-1, keepdims=True)
        idx = jnp.min(jnp.where(work == m, lane_f, float(LANES)), axis=-1, keepdims=True)
        vals.append(m)
        idxs.append(idx)
        work = jnp.where(lane_f == idx, -jnp.inf, work)
    es = [jnp.exp(v - vals[0]) for v in vals]
    den = es[0] + es[1] + es[2] + es[3]
    gate_out = jnp.zeros(logits.shape, F32)
    idx_out = jnp.zeros(logits.shape, F32)
    for k in range(TOP_K):
        gate_out = jnp.where(lane == k, es[k] / den, gate_out)
        idx_out = jnp.where(lane == k, idxs[k], idx_out)
    gate_ref[...] = gate_out
    idx_ref[...] = idx_out.astype(jnp.int32)


def _outproj(o_f, o_b, proj, atts, lses, x2d, og, w_out_bf16, g2, wr_hi, wr_lo, br, seq):
    n = x2d.shape[0]
    t = TOKEN_TILE
    nt = seq // t
    row = lambda i: (i, 0)
    const = lambda i: (0, 0)
    half = pl.BlockSpec((t, 512), row)

    def residue_major(dil, width):
        return pl.BlockSpec((1, dil, t // dil, width), lambda i: (i // nt, 0, i % nt, 0))

    rm4, rm16 = residue_major(4, ATT_WIDTH), residue_major(16, ATT_WIDTH)
    lse, lse4, lse16 = pl.BlockSpec((t, LANES), row), residue_major(4, LANES), residue_major(16, LANES)
    wide, narrow = pltpu.VMEM((ATT_WIDTH // LANES, t, LANES), F32), pltpu.VMEM((1, t, LANES), F32)
    return pl.pallas_call(
        _outproj_kernel,
        grid=(n // t,),
        in_specs=[
            half, half, pl.BlockSpec((t, 512), lambda i: (i, COL_HG)),
            half, rm4, rm16, lse, lse4, lse16,
            pl.BlockSpec((t, D_MODEL), row),
            pl.BlockSpec((1, HG_DIM), const),
            pl.BlockSpec((D_MODEL, D_MODEL), const),
            pl.BlockSpec((2, D_MODEL), const),
            pl.BlockSpec((D_MODEL, LANES), const),
            pl.BlockSpec((D_MODEL, LANES), const),
            pl.BlockSpec((1, LANES), const),
        ],
        out_specs=[pl.BlockSpec((t, D_MODEL), row), pl.BlockSpec((t * ROW_TILES, LANES), row),
                   pl.BlockSpec((t, LANES), row), pl.BlockSpec((t, LANES), row)],
        out_shape=[jax.ShapeDtypeStruct((n, D_MODEL), F32), jax.ShapeDtypeStruct((n * ROW_TILES, LANES), F32),
                   jax.ShapeDtypeStruct((n, LANES), F32), jax.ShapeDtypeStruct((n, LANES), jnp.int32)],
        scratch_shapes=[wide, wide, narrow, narrow, wide, narrow],
        compiler_params=pltpu.CompilerParams(
            dimension_semantics=("parallel",), vmem_limit_bytes=VMEM_LIMIT),
        name="outproj",
    )(o_f, o_b, proj, *atts, *lses, x2d, og, w_out_bf16, g2, wr_hi, wr_lo, br)


def _moe_kernel(bexp_ref, bfirst_ref, bslot_ref, bnext_ref, bvalid_ref, nused_ref,
                xs_ref, xinv_ref, wu_hbm, bg_ref, bl_ref, wd_hbm, bd_ref,
                y_ref, wu_buf, wd_buf, wg_s, wl_s, wd_s, up_unscale_s, down_unscale_s, sems):
    i = pl.program_id(0)

    def weight_copies(expert, slot):
        return (pltpu.make_async_copy(wu_hbm.at[expert], wu_buf.at[slot], sems.at[0, slot]),
                pltpu.make_async_copy(wd_hbm.at[expert], wd_buf.at[slot], sems.at[1, slot]))

    @pl.when(bfirst_ref[i] == 1)
    def _():
        slot = bslot_ref[i]

        @pl.when(i == 0)
        def _():
            for cp in weight_copies(bexp_ref[0], 0):
                cp.start()

        for cp in weight_copies(bexp_ref[i], slot):
            cp.wait()

        @pl.when(bnext_ref[i] >= 0)
        def _():
            for cp in weight_copies(bnext_ref[i], 1 - slot):
                cp.start()

        su = _fp8_scale(_abs_max(wu_buf[slot]))
        sd = _fp8_scale(_abs_max(wd_buf[slot]))
        up_unscale_s[...] = xinv_ref[...] / su
        down_unscale_s[...] = (1.0 / ACT_FP8_SCALE) / sd
        r = lax.broadcasted_iota(jnp.int32, (2 * LANES, 2 * LANES), 0)
        c = lax.broadcasted_iota(jnp.int32, (2 * LANES, 2 * LANES), 1)
        src = jnp.where(c < LANES, 2 * c, 2 * (c - LANES) + 1)
        perm = jnp.where(r == src, 1.0, 0.0).astype(BF16)
        rows = 256
        for rb in range(D_MODEL // rows):
            rs = slice(rb * rows, (rb + 1) * rows)
            for cb in range(D_EXPERT // LANES):
                w = wu_buf[slot, rs, cb * 2 * LANES:(cb + 1) * 2 * LANES].astype(BF16)
                split = (_dot(w, perm) * su).astype(F8)
                wg_s[rs, cb * LANES:(cb + 1) * LANES] = split[:, :LANES]
                wl_s[rs, cb * LANES:(cb + 1) * LANES] = split[:, LANES:]
        wd_s[...] = (wd_buf[slot] * sd).astype(F8)

    def expert_mlp(rows):
        x8 = jnp.concatenate(_tiles_to_rows(xs_ref, rows), axis=1).astype(F8)
        hglu = _dot(x8, wg_s[...]) * up_unscale_s[...] + bg_ref[0]
        hlin = _dot(x8, wl_s[...]) * up_unscale_s[...] + bl_ref[0]
        glu = jnp.minimum(hglu, SWIGLU_LIMIT)
        lin = jnp.clip(hlin, -SWIGLU_LIMIT, SWIGLU_LIMIT)
        act = glu * _sigmoid(SWIGLU_ALPHA * glu) * (lin + 1.0)
        y = _dot((act * ACT_FP8_SCALE).astype(F8), wd_s[...]) * down_unscale_s[...] + bd_ref[0]
        _rows_to_tiles(y_ref, y)

    valid = bvalid_ref[i]
    half = MOE_BLOCK_ROWS // 2

    @pl.when(valid > half)
    def _():
        expert_mlp(MOE_BLOCK_ROWS)

    @pl.when((valid > 0) & (valid <= half))
    def _():
        expert_mlp(half)
        y_ref[half * ROW_TILES:, :] = jnp.zeros((half * ROW_TILES, LANES), F32)

    @pl.when(valid == 0)
    def _():
        y_ref[...] = jnp.zeros_like(y_ref)


def _moe(block_exp, block_first, block_slot, block_next, block_valid, n_used,
         xs, x_unscale, w_up, b_glu, b_lin, w_down, b_down):
    p_rows = xs.shape[0] // ROW_TILES
    bm = MOE_BLOCK_ROWS
    nb = p_rows // bm
    exp3 = lambda i, be, *_: (be[i], 0, 0)
    grid_spec = pltpu.PrefetchScalarGridSpec(
        num_scalar_prefetch=6,
        grid=(nb,),
        in_specs=[
            pl.BlockSpec((bm * ROW_TILES, LANES), lambda i, be, bf, bs, bn, bv, nu: (jnp.minimum(i, nu[0] - 1), 0)),
            pl.BlockSpec((1, 1), lambda i, *_: (0, 0)),
            pl.BlockSpec(memory_space=pl.ANY),
            pl.BlockSpec((1, 1, D_EXPERT), exp3),
            pl.BlockSpec((1, 1, D_EXPERT), exp3),
            pl.BlockSpec(memory_space=pl.ANY),
            pl.BlockSpec((1, 1, D_MODEL), exp3),
        ],
        out_specs=pl.BlockSpec((bm * ROW_TILES, LANES), lambda i, *_: (i, 0)),
        scratch_shapes=[pltpu.VMEM((2, D_MODEL, 2 * D_EXPERT), F32), pltpu.VMEM((2, D_EXPERT, D_MODEL), F32),
                        pltpu.VMEM((D_MODEL, D_EXPERT), F8), pltpu.VMEM((D_MODEL, D_EXPERT), F8),
                        pltpu.VMEM((D_EXPERT, D_MODEL), F8), pltpu.VMEM((1, 1), F32), pltpu.VMEM((1, 1), F32),
                        pltpu.SemaphoreType.DMA((2, 2))],
    )
    return pl.pallas_call(
        _moe_kernel,
        grid_spec=grid_spec,
        out_shape=jax.ShapeDtypeStruct((p_rows * ROW_TILES, LANES), F32),
        compiler_params=pltpu.CompilerParams(
            dimension_semantics=("arbitrary",), vmem_limit_bytes=VMEM_LIMIT),
        name="moe",
    )(block_exp, block_first, block_slot, block_next, block_valid, n_used,
      xs, x_unscale, w_up, b_glu, b_lin, w_down, b_down)


def _dispatch_kernel(dest_ref, zstart_ref, xn_ref, xs_hbm, zero_ref, sem, zsem):
    i = pl.program_id(0)
    bm = MOE_BLOCK_ROWS
    tokens = xn_ref.shape[0] // ROW_TILES

    def zero_copy(e):
        start = pl.multiple_of(zstart_ref[e] * ROW_TILES, bm * ROW_TILES)
        return pltpu.make_async_copy(zero_ref, xs_hbm.at[pl.ds(start, bm * ROW_TILES)], zsem)

    @pl.when(i == 0)
    def _():
        zero_ref[...] = jnp.zeros_like(zero_ref)
        for e in range(N_EXPERTS):
            @pl.when(zstart_ref[e] >= 0)
            def _():
                zero_copy(e).start()
        for e in range(N_EXPERTS):
            @pl.when(zstart_ref[e] >= 0)
            def _():
                zero_copy(e).wait()

    t0 = i * tokens

    def body(j, carry):
        src = xn_ref.at[pl.ds(pl.multiple_of(j * ROW_TILES, ROW_TILES), ROW_TILES)]
        for k in range(TOP_K):
            d = pl.multiple_of(dest_ref[k * (dest_ref.shape[0] // TOP_K) + t0 + j] * ROW_TILES, ROW_TILES)
            pltpu.make_async_copy(src, xs_hbm.at[pl.ds(d, ROW_TILES)], sem).start(priority=k % 2)
        return carry

    lax.fori_loop(0, tokens, body, 0, unroll=8)
    for k in range(TOP_K):
        pltpu.make_async_copy(xn_ref, xs_hbm.at[pl.ds(0, tokens * ROW_TILES)], sem).wait()


def _dispatch(dest, zstart, xn, p_rows):
    n = xn.shape[0] // ROW_TILES
    t = DISPATCH_TOKENS
    grid_spec = pltpu.PrefetchScalarGridSpec(
        num_scalar_prefetch=2,
        grid=(n // t,),
        in_specs=[pl.BlockSpec((t * ROW_TILES, LANES), lambda i, d, z: (i, 0))],
        out_specs=pl.BlockSpec(memory_space=pl.ANY),
        scratch_shapes=[pltpu.VMEM((MOE_BLOCK_ROWS * ROW_TILES, LANES), F32),
                        pltpu.SemaphoreType.DMA, pltpu.SemaphoreType.DMA],
    )
    return pl.pallas_call(
        _dispatch_kernel,
        grid_spec=grid_spec,
        out_shape=jax.ShapeDtypeStruct((p_rows * ROW_TILES, LANES), F32),
        compiler_params=pltpu.CompilerParams(
            dimension_semantics=("arbitrary",), vmem_limit_bytes=VMEM_LIMIT),
        name="dispatch",
    )(dest, zstart, xn)


def _combine_kernel(dest_ref, y_hbm, x2_ref, gate_ref, out_ref, buf_ref, sems):
    i = pl.program_id(0)
    tc = COMBINE_TOKENS

    def issue(step, slot):
        t0 = step * tc

        group = 8

        def body(g, carry):
            for k in range(TOP_K):
                for u in range(group):
                    j = g * group + u
                    d = pl.multiple_of(dest_ref[k * (dest_ref.shape[0] // TOP_K) + t0 + j] * ROW_TILES, ROW_TILES)
                    r = pl.multiple_of((k * tc + j) * ROW_TILES, ROW_TILES)
                    pltpu.make_async_copy(y_hbm.at[pl.ds(d, ROW_TILES)], buf_ref.at[slot, pl.ds(r, ROW_TILES)],
                                          sems.at[slot]).start(priority=u % 2)
            return carry

        lax.fori_loop(0, tc // group, body, 0)

    @pl.when(i == 0)
    def _():
        issue(0, 0)

    @pl.when(i + 1 < pl.num_programs(0))
    def _():
        issue(i + 1, (i + 1) % 2)

    slot = i % 2
    pltpu.make_async_copy(y_hbm.at[pl.ds(0, TOP_K * tc * ROW_TILES)], buf_ref.at[slot], sems.at[slot]).wait()
    rows = buf_ref.at[slot]
    sub = 64
    for r0 in range(0, tc, sub):
        gate = gate_ref[r0:r0 + sub, :]
        gates = [jnp.broadcast_to(gate[:, k:k + 1], (sub, LANES)) for k in range(TOP_K)]
        for c in range(ROW_TILES):
            acc = x2_ref[r0:r0 + sub, c * LANES:(c + 1) * LANES]
            for k in range(TOP_K):
                acc = acc + gates[k] * rows[pl.ds((k * tc + r0) * ROW_TILES + c, sub, stride=ROW_TILES), :]
            out_ref[r0:r0 + sub, c * LANES:(c + 1) * LANES] = acc


def _combine(dest, y, x2, gates):
    n = x2.shape[0]
    tc = COMBINE_TOKENS
    grid_spec = pltpu.PrefetchScalarGridSpec(
        num_scalar_prefetch=1,
        grid=(n // tc,),
        in_specs=[pl.BlockSpec(memory_space=pl.ANY),
                  pl.BlockSpec((tc, D_MODEL), lambda i, d: (i, 0)),
                  pl.BlockSpec((tc, LANES), lambda i, d: (i, 0))],
        out_specs=pl.BlockSpec((tc, D_MODEL), lambda i, d: (i, 0)),
        scratch_shapes=[pltpu.VMEM((2, TOP_K * tc * ROW_TILES, LANES), F32), pltpu.SemaphoreType.DMA((2,))],
    )
    return pl.pallas_call(
        _combine_kernel,
        grid_spec=grid_spec,
        out_shape=jax.ShapeDtypeStruct((n, D_MODEL), F32),
        compiler_params=pltpu.CompilerParams(
            dimension_semantics=("arbitrary",), vmem_limit_bytes=VMEM_LIMIT),
        name="combine",
    )(dest, y, x2, gates)


def _route(top_idx):
    n = top_idx.shape[0]
    a = n * TOP_K
    bm = MOE_BLOCK_ROWS
    nb = a // bm + N_EXPERTS
    experts = jnp.arange(N_EXPERTS, dtype=jnp.int32)[None, :]
    picked = [top_idx[:, k:k + 1] == experts for k in range(TOP_K)]
    hits = sum(p.astype(jnp.int32) for p in picked)
    csum = jnp.cumsum(hits, axis=0)
    counts = csum[-1]
    padded = ((counts + bm - 1) // bm) * bm
    pad_end = jnp.cumsum(padded)
    pad_start = pad_end - padded
    row_of = csum - hits + pad_start[None, :]
    dest = jnp.concatenate([jnp.sum(jnp.where(p, row_of, 0), axis=1) for p in picked])
    n_used = (pad_end[-1] // bm).astype(jnp.int32)
    blk = jnp.arange(nb, dtype=jnp.int32)
    bexp = jnp.sum((pad_end[None, :] <= (blk * bm)[:, None]).astype(jnp.int32), axis=1)
    bexp = jnp.minimum(bexp, N_EXPERTS - 1)
    bexp = jnp.where(blk < n_used, bexp, bexp[jnp.maximum(n_used - 1, 0)])
    bfirst = jnp.concatenate([jnp.ones((1,), jnp.int32), (bexp[1:] != bexp[:-1]).astype(jnp.int32)])
    bslot = (jnp.cumsum(bfirst) - 1) % 2
    later = jnp.where(bexp[None, :] > bexp[:, None], bexp[None, :], N_EXPERTS)
    bnext = jnp.min(later, axis=1)
    bnext = jnp.where(bnext < N_EXPERTS, bnext, -1).astype(jnp.int32)
    valid_end = jnp.sum(jnp.where(bexp[:, None] == jnp.arange(N_EXPERTS)[None, :], (pad_start + counts)[None, :], 0), axis=1)
    bvalid = jnp.clip(valid_end - blk * bm, 0, bm).astype(jnp.int32)
    zstart = jnp.where(counts > 0, pad_end - bm, -1).astype(jnp.int32)
    return dest.astype(jnp.int32), zstart, (bexp, bfirst, bslot.astype(jnp.int32), bnext, bvalid, n_used.reshape(1))


def kernel(x, positions, norm1_g, w_in, q_norm_g, k_norm_g, hgrn_lower_bounds, hgrn_onorm_g,
           w_out, norm2_g, w_router, b_router, w_up, b_up, w_down, b_down):
    batch, seq, d = x.shape
    n = batch * seq
    depth = norm1_g.shape[0]
    lbs_all = jnp.cumsum(jax.nn.softmax(hgrn_lower_bounds.astype(F32), axis=0), axis=0)
    half = ATT_DIM // 2
    inv = 1.0 / (ROPE_THETA ** (jnp.arange(half, dtype=F32) / half))
    inv_tab = jnp.tile(inv, LANES // half).reshape(1, LANES)
    pos_col = positions.reshape(n, 1)

    x2d = x.reshape(n, d)
    for l in range(depth):
        lbs = lbs_all[l].reshape(2 * HG_HEADS, 1, HG_DIM)
        proj, qkv4, qkv16 = _inproj(
            x2d, pos_col, inv_tab, norm1_g[l].reshape(1, d), w_in[l].astype(BF16),
            jnp.tile(q_norm_g[l], LANES // ATT_DIM).reshape(1, LANES),
            jnp.tile(k_norm_g[l], LANES // ATT_DIM).reshape(1, LANES), batch, seq)
        o_f, o_b = _hgrn(proj, lbs, batch, seq)
        o1, l1 = _attention(proj.reshape(batch, 1, seq, IN_COLS), COL_AQ)
        o4, l4 = _attention(qkv4, 0)
        o16, l16 = _attention(qkv16, 0)
        atts = [o1.reshape(n, ATT_WIDTH), o4, o16]
        lses = [l1.reshape(n, LANES), l4, l16]

        wr = jnp.pad(w_router[l], ((0, 0), (0, LANES - N_EXPERTS)))
        wr_hi = wr.astype(BF16)
        wr_lo = (wr - wr_hi.astype(F32)).astype(BF16)
        br = jnp.pad(b_router[l], (0, LANES - N_EXPERTS)).reshape(1, LANES)
        g2 = norm2_g[l].astype(F32)
        g2_max = jnp.max(jnp.abs(g2))
        x_scale = jnp.where(g2_max > 0.0, ROW_FP8_TARGET / g2_max, 1.0)
        x2, xn, gates, top_idx = _outproj(
            o_f, o_b, proj, atts, lses, x2d, hgrn_onorm_g[l].reshape(1, HG_DIM),
            w_out[l].astype(BF16), jnp.stack([g2, g2 * x_scale]), wr_hi, wr_lo, br, seq)
        dest, zstart, blocks = _route(top_idx[:, :TOP_K])
        xs = _dispatch(dest, zstart, xn, blocks[0].shape[0] * MOE_BLOCK_ROWS)
        y = _moe(*blocks, xs, (1.0 / x_scale).reshape(1, 1), w_up[l],
                 b_up[l][:, 0::2].reshape(N_EXPERTS, 1, D_EXPERT),
                 b_up[l][:, 1::2].reshape(N_EXPERTS, 1, D_EXPERT),
                 w_down[l], b_down[l].reshape(N_EXPERTS, 1, D_MODEL))
        x2d = _combine(dest, y, x2, gates)
    return x2d.reshape(batch, seq, d)
```
